```python
import math
import jax, jax.numpy as jnp
from jax import lax
import numpy as np

D_MODEL = 1024
BATCH = 4
SEQ = 4096
DEPTH = 1
DEC_BATCH = 32
DEC_SEQ = 8
PAST_LEN = 16384
PAGE_SIZE = 128

D_PLE = 256
HEAD_DIM = 64
D_ATTN = D_MODEL // 2
N_ATTN_HEADS = D_ATTN // HEAD_DIM
DILATED_BRANCHES = ((128, 1), (512, 4), (2048, 16))
N_STEPS = 128
MAX_WINDOW = 2048
BLK = 128
N_BUCKETS = 32
MAX_DISTANCE = 2048
D_SSM = D_MODEL // 2
SSM_HEAD_DIM = 64
N_SSM_HEADS = D_SSM // SSM_HEAD_DIM
D_STATE = 128
N_SSM_GROUPS = 2
HEADS_PER_GROUP = N_SSM_HEADS // N_SSM_GROUPS
SSM_CONV = 4
CONV_DIM = D_SSM + 2 * N_SSM_GROUPS * D_STATE
SSD_CHUNK = 128
D_MIX = D_ATTN + D_SSM
IN_DIM = 3 * D_ATTN + D_SSM + CONV_DIM + N_SSM_HEADS
D_FF = ((8 * D_MODEL // 3 + 127) // 128) * 128
FFN_CONV = 3
EPS = 1e-6

kernel_name = "hymba_dilated_ssd_convffn_step"

F32 = jnp.float32


def rms_norm(x, g):
    xf = x.astype(F32)
    y = xf * lax.rsqrt(jnp.mean(xf * xf, axis=-1, keepdims=True) + EPS)
    return (y * g.astype(F32)).astype(x.dtype)


def rel_bucket(dist):
    dist = jnp.asarray(dist, jnp.int32)
    max_exact = N_BUCKETS // 2
    d = jnp.maximum(dist, 1).astype(F32)
    large = max_exact + (jnp.log(d / max_exact) / math.log(MAX_DISTANCE / max_exact)
                         * (N_BUCKETS - max_exact)).astype(jnp.int32)
    large = jnp.minimum(large, N_BUCKETS - 1)
    return jnp.where(dist < max_exact, dist, large)


def causal_dwconv(x, prefix, w, b):
    K = w.shape[0]
    L = x.shape[1]
    xp = jnp.concatenate([prefix.astype(x.dtype), x], axis=1)
    y = b + w[0] * xp[:, 0:L]
    for k in range(1, K):
        y = y + w[k] * xp[:, k:k + L]
    return y, xp[:, xp.shape[1] - (K - 1):]


def combine_branches(outs, lses):
    w = jax.nn.softmax(jnp.stack(lses, axis=-1), axis=-1)
    acc = w[..., 0, None] * outs[0].astype(F32)
    for n in range(1, len(outs)):
        acc = acc + w[..., n, None] * outs[n].astype(F32)
    return acc.astype(outs[0].dtype)


def dilated_attn_prompt(q, k, v, rel_bias):
    Bsz, S, H, E = q.shape
    scale = E ** -0.5
    i = jnp.arange(BLK)[:, None]
    c = jnp.arange(2 * BLK)[None, :]
    j = BLK + i - c
    band = (j >= 0) & (j <= N_STEPS)
    outs, lses = [], []
    for window, dil in DILATED_BRANCHES:
        L = S // dil
        nb = -(-L // BLK)
        Lp = nb * BLK

        def to_sub(t):
            t = t.reshape(Bsz, L, dil, H, E).transpose(0, 2, 1, 3, 4)
            t = jnp.pad(t, ((0, 0), (0, 0), (0, Lp - L), (0, 0), (0, 0)))
            return t.reshape(Bsz, dil, nb, BLK, H, E)

        def with_prev(t):
            prev = jnp.pad(t, ((0, 0), (0, 0), (1, 0), (0, 0), (0, 0), (0, 0)))[:, :, :-1]
            return jnp.concatenate([prev, t], axis=3)

        qs = to_sub(q)
        kk = with_prev(to_sub(k))
        vv = with_prev(to_sub(v))
        l_q = jnp.arange(nb)[:, None, None] * BLK + i[None]
        valid = band[None] & (l_q - j[None] >= 0)
        bias = rel_bias[rel_bucket(jnp.clip(j, 0, N_STEPS) * dil)]
        s = jnp.einsum('bdnqhe,bdnkhe->bdnhqk', qs, kk).astype(F32) * scale
        s = s + bias.astype(F32).transpose(2, 0, 1)[None, None, None]
        s = jnp.where(valid[None, None, :, None], s, -jnp.inf)
        m = jnp.max(s, axis=-1, keepdims=True)
        pe = jnp.exp(s - m)
        den = jnp.sum(pe, axis=-1, keepdims=True)
        o = jnp.einsum('bdnhqk,bdnkhe->bdnqhe', (pe / den).astype(v.dtype), vv)
        lse = (m + jnp.log(den))[..., 0]
        o = o.reshape(Bsz, dil, Lp, H, E)[:, :, :L].transpose(0, 2, 1, 3, 4).reshape(Bsz, S, H, E)
        lse = lse.transpose(0, 1, 2, 4, 3).reshape(Bsz, dil, Lp, H)[:, :, :L]
        lse = lse.transpose(0, 2, 1, 3).reshape(Bsz, S, H)
        outs.append(o)
        lses.append(lse)
    return combine_branches(outs, lses)


def dilated_attn_sample(q, k_new, v_new, k_buf, v_buf, rel_bias):
    T = q.shape[1]
    Wb = k_buf.shape[1]
    scale = q.shape[-1] ** -0.5
    kc = jnp.concatenate([k_buf.astype(k_new.dtype), k_new], axis=1)
    vc = jnp.concatenate([v_buf.astype(v_new.dtype), v_new], axis=1)
    steps = jnp.arange(N_STEPS + 1)
    outs, lses = [], []
    for window, dil in DILATED_BRANCHES:
        idx = Wb + jnp.arange(T)[:, None] - steps[None] * dil
        valid = idx >= 0
        idxc = jnp.maximum(idx, 0)
        kg = kc[:, idxc]
        vg = vc[:, idxc]
        bias = rel_bias[rel_bucket(steps * dil)]
        s = jnp.einsum('bthe,btjhe->bthj', q, kg).astype(F32) * scale
        s = s + bias.astype(F32).T[None, None]
        s = jnp.where(valid[None, :, None, :], s, -jnp.inf)
        m = jnp.max(s, axis=-1, keepdims=True)
        pe = jnp.exp(s - m)
        den = jnp.sum(pe, axis=-1, keepdims=True)
        o = jnp.einsum('bthj,btjhe->bthe', (pe / den).astype(vg.dtype), vg)
        outs.append(o)
        lses.append((m + jnp.log(den))[..., 0])
    return combine_branches(outs, lses)


def ssd_scan(x, dt, A, Bm, Cm, state0, chunk):
    b, L = x.shape[:2]
    nc = L // chunk
    G, HG, P, N = N_SSM_GROUPS, HEADS_PER_GROUP, SSM_HEAD_DIM, D_STATE
    xg = x.astype(F32).reshape(b, nc, chunk, G, HG, P)
    dtg = dt.reshape(b, nc, chunk, G, HG)
    Bc = Bm.astype(F32).reshape(b, nc, chunk, G, N)
    Cc = Cm.astype(F32).reshape(b, nc, chunk, G, N)
    acum = jnp.cumsum(dtg * A.reshape(G, HG), axis=2)
    tri = jnp.tril(jnp.ones((chunk, chunk), bool))[:, :, None, None]
    seg = acum[:, :, :, None] - acum[:, :, None]
    decay = jnp.where(tri, jnp.exp(jnp.where(tri, seg, 0.0)), 0.0)
    cb = jnp.einsum('bctgn,bcsgn->bctsg', Cc, Bc)
    y_diag = jnp.einsum('bctsgh,bcsghp->bctghp', cb[..., None] * decay * dtg[:, :, None], xg)
    decay_end = jnp.exp(acum[:, :, -1:] - acum) * dtg
    chunk_states = jnp.einsum('bclgn,bclgh,bclghp->bcghpn', Bc, decay_end, xg)
    chunk_decay = jnp.exp(acum[:, :, -1])
    s0 = state0.astype(F32).reshape(b, G, HG, P, N)

    def step(s, inp):
        dec, st = inp
        return s * dec[..., None, None] + st, s

    final, prev = lax.scan(step, s0, (chunk_decay.transpose(1, 0, 2, 3),
                                      chunk_states.transpose(1, 0, 2, 3, 4, 5)))
    prev = prev.transpose(1, 0, 2, 3, 4, 5)
    y_off = jnp.einsum('bctgn,bctgh,bcghpn->bctghp', Cc, jnp.exp(acum), prev)
    y = (y_diag + y_off).reshape(b, L, N_SSM_HEADS, P)
    return y, final.reshape(b, N_SSM_HEADS, P, N)


def gated_group_rmsnorm(y, z, g):
    yf = y.astype(F32) * jax.nn.silu(z.astype(F32))
    yg = yf.reshape(*yf.shape[:-1], N_SSM_GROUPS, D_SSM // N_SSM_GROUPS)
    yg = yg * lax.rsqrt(jnp.mean(yg * yg, axis=-1, keepdims=True) + EPS)
    return (yg.reshape(yf.shape) * g.astype(F32)).astype(z.dtype)


def layer(h, p_i, attn_fn, conv_prefix, ssm_state0, ffn_prefix, chunk,
          g_mix, w_in, conv_w, conv_b, dt_bias, a_log, d_skip, g_ssm, w_out,
          g_ffn, w_up, ffn_conv_w, ffn_conv_b, w_down, w_ple_proj, g_ple, w_ple_gate):
    b, L, _ = h.shape
    xn = rms_norm(h, g_mix)
    proj = xn @ w_in
    o1, o2, o3 = D_ATTN, 2 * D_ATTN, 3 * D_ATTN
    o4 = o3 + D_SSM
    o5 = o4 + CONV_DIM
    q = proj[..., :o1].reshape(b, L, N_ATTN_HEADS, HEAD_DIM)
    k = proj[..., o1:o2].reshape(b, L, N_ATTN_HEADS, HEAD_DIM)
    v = proj[..., o2:o3].reshape(b, L, N_ATTN_HEADS, HEAD_DIM)
    z = proj[..., o3:o4]
    xbc = proj[..., o4:o5]
    dt_raw = proj[..., o5:]
    attn_o = attn_fn(q, k, v).reshape(b, L, D_ATTN)
    xbc_c, conv_state = causal_dwconv(xbc, conv_prefix, conv_w, conv_b)
    xbc_c = jax.nn.silu(xbc_c)
    GN = N_SSM_GROUPS * D_STATE
    xs = xbc_c[..., :D_SSM].reshape(b, L, N_SSM_HEADS, SSM_HEAD_DIM)
    Bm = xbc_c[..., D_SSM:D_SSM + GN].reshape(b, L, N_SSM_GROUPS, D_STATE)
    Cm = xbc_c[..., D_SSM + GN:].reshape(b, L, N_SSM_GROUPS, D_STATE)
    dt = jax.nn.softplus(dt_raw.astype(F32) + dt_bias.astype(F32))
    A = -jnp.exp(a_log.astype(F32))
    y, ssm_state = ssd_scan(xs, dt, A, Bm, Cm, ssm_state0, chunk)
    y = y + d_skip.astype(F32)[:, None] * xs.astype(F32)
    y = gated_group_rmsnorm(y.reshape(b, L, D_SSM), z, g_ssm)
    h = h + jnp.concatenate([attn_o, y.astype(attn_o.dtype)], axis=-1) @ w_out
    hn = rms_norm(h, g_ffn)
    u = hn @ w_up
    u_c, ffn_state = causal_dwconv(u, ffn_prefix, ffn_conv_w, ffn_conv_b)
    h = h + (jax.nn.silu(u_c[..., :D_FF]) * u_c[..., D_FF:]) @ w_down
    e = rms_norm(p_i @ w_ple_proj, g_ple)
    h = h + jax.nn.sigmoid((h @ w_ple_gate).astype(F32)).astype(h.dtype) * e
    return h, k, v, ssm_state, conv_state, ffn_state


def setup_inputs(seed: int = 0) -> dict:
    key = jax.random.key(seed)
    ks = jax.random.split(key, 32)

    def nrm(k, shape, scale):
        return jax.random.normal(k, shape, F32) * scale

    w_buf = min(MAX_WINDOW, PAST_LEN)
    dt0 = jnp.exp(jax.random.uniform(ks[10], (DEPTH, N_SSM_HEADS), F32,
                                     math.log(1e-3), math.log(1e-1)))
    return {
        "x_prompt": nrm(ks[0], (BATCH, SEQ, D_MODEL), 1.0),
        "x_sample": nrm(ks[1], (DEC_BATCH, DEC_SEQ, D_MODEL), 1.0),
        "p_prompt": nrm(ks[2], (DEPTH, BATCH, SEQ, D_PLE), 1.0),
        "p_sample": nrm(ks[3], (DEPTH, DEC_BATCH, DEC_SEQ, D_PLE), 1.0),
        "cache_k": nrm(ks[4], (DEPTH, DEC_BATCH, w_buf, N_ATTN_HEADS, HEAD_DIM), 1.0),
        "cache_v": nrm(ks[5], (DEPTH, DEC_BATCH, w_buf, N_ATTN_HEADS, HEAD_DIM), 1.0),
        "state_ssm": nrm(ks[6], (DEPTH, DEC_BATCH, N_SSM_HEADS, SSM_HEAD_DIM, D_STATE), 0.1),
        "state_conv": nrm(ks[7], (DEPTH, DEC_BATCH, SSM_CONV - 1, CONV_DIM), 1.0),
        "state_ffn_conv": nrm(ks[8], (DEPTH, DEC_BATCH, FFN_CONV - 1, 2 * D_FF), 1.0),
        "rel_bias": nrm(ks[9], (N_BUCKETS, N_ATTN_HEADS), 0.1),
        "g_mix": 1.0 + nrm(ks[11], (DEPTH, D_MODEL), 0.02),
        "w_in": nrm(ks[12], (DEPTH, D_MODEL, IN_DIM), D_MODEL ** -0.5),
        "conv_w": nrm(ks[13], (DEPTH, SSM_CONV, CONV_DIM), SSM_CONV ** -0.5),
        "conv_b": nrm(ks[14], (DEPTH, CONV_DIM), 0.01),
        "dt_bias": dt0 + jnp.log(-jnp.expm1(-dt0)),
        "a_log": jnp.log(jax.random.uniform(ks[15], (DEPTH, N_SSM_HEADS), F32, 1.0, 16.0)),
        "d_skip": 1.0 + nrm(ks[16], (DEPTH, N_SSM_HEADS), 0.02),
        "g_ssm": 1.0 + nrm(ks[17], (DEPTH, D_SSM), 0.02),
        "w_out": nrm(ks[18], (DEPTH, D_MIX, D_MODEL), D_MIX ** -0.5),
        "g_ffn": 1.0 + nrm(ks[19], (DEPTH, D_MODEL), 0.02),
        "w_up": nrm(ks[20], (DEPTH, D_MODEL, 2 * D_FF), D_MODEL ** -0.5),
        "ffn_conv_w": nrm(ks[21], (DEPTH, FFN_CONV, 2 * D_FF), FFN_CONV ** -0.5),
        "ffn_conv_b": nrm(ks[22], (DEPTH, 2 * D_FF), 0.01),
        "w_down": nrm(ks[23], (DEPTH, D_FF, D_MODEL), D_FF ** -0.5),
        "w_ple_proj": nrm(ks[24], (DEPTH, D_PLE, D_MODEL), D_PLE ** -0.5),
        "g_ple": 1.0 + nrm(ks[25], (DEPTH, D_MODEL), 0.02),
        "w_ple_gate": nrm(ks[26], (DEPTH, D_MODEL, D_MODEL), D_MODEL ** -0.5),
        "g_final": 1.0 + nrm(ks[27], (D_MODEL,), 0.02),
    }


def reference(x_prompt, x_sample, p_prompt, p_sample, cache_k, cache_v, state_ssm,
              state_conv, state_ffn_conv, rel_bias, g_mix, w_in, conv_w, conv_b,
              dt_bias, a_log, d_skip, g_ssm, w_out, g_ffn, w_up, ffn_conv_w,
              ffn_conv_b, w_down, w_ple_proj, g_ple, w_ple_gate, g_final):
    bp, S, _ = x_prompt.shape
    T = x_sample.shape[1]
    n_keep = min(MAX_WINDOW, S)
    hp, hs = x_prompt, x_sample
    kp_l, vp_l, ks_l, vs_l = [], [], [], []
    sp_l, ss_l, cp_l, cs_l, fp_l, fs_l = [], [], [], [], [], []
    for i in range(DEPTH):
        wi = (g_mix[i], w_in[i], conv_w[i], conv_b[i], dt_bias[i], a_log[i], d_skip[i],
              g_ssm[i], w_out[i], g_ffn[i], w_up[i], ffn_conv_w[i], ffn_conv_b[i],
              w_down[i], w_ple_proj[i], g_ple[i], w_ple_gate[i])
        hp, kp, vp, sp, cp, fp = layer(
            hp, p_prompt[i],
            lambda q, k, v: dilated_attn_prompt(q, k, v, rel_bias),
            jnp.zeros((bp, SSM_CONV - 1, CONV_DIM), x_prompt.dtype),
            jnp.zeros((bp, N_SSM_HEADS, SSM_HEAD_DIM, D_STATE), F32),
            jnp.zeros((bp, FFN_CONV - 1, 2 * D_FF), x_prompt.dtype),
            SSD_CHUNK, *wi)
        hs, ksn, vsn, ssn, csn, fsn = layer(
            hs, p_sample[i],
            lambda q, k, v, kb=cache_k[i], vb=cache_v[i]: dilated_attn_sample(q, k, v, kb, vb, rel_bias),
            state_conv[i], state_ssm[i], state_ffn_conv[i],
            T, *wi)
        kp_l.append(kp[:, S - n_keep:])
        vp_l.append(vp[:, S - n_keep:])
        ks_l.append(ksn)
        vs_l.append(vsn)
        sp_l.append(sp)
        ss_l.append(ssn)
        cp_l.append(cp)
        cs_l.append(csn)
        fp_l.append(fp)
        fs_l.append(fsn)
    y_prompt = rms_norm(hp, g_final)
    y_sample = rms_norm(hs, g_final)
    return (y_prompt, y_sample,
            jnp.stack(kp_l), jnp.stack(vp_l), jnp.stack(ks_l), jnp.stack(vs_l),
            jnp.stack(sp_l), jnp.stack(ss_l), jnp.stack(cp_l), jnp.stack(cs_l),
            jnp.stack(fp_l), jnp.stack(fs_l))
```

```python
import functools
import math

import numpy as np
import jax
import jax.numpy as jnp
from jax import lax
from jax.experimental import pallas as pl
from jax.experimental.pallas import tpu as pltpu

F32 = jnp.float32
BF16 = jnp.bfloat16

D_MODEL = 1024
HEAD_DIM = 64
N_HEADS = 8
D_ATTN = N_HEADS * HEAD_DIM
N_PAIRS = N_HEADS // 2
DILATED_BRANCHES = ((128, 1), (512, 4), (2048, 16))
N_STEPS = 128
BLK = 128
N_BUCKETS = 32
MAX_DISTANCE = 2048
D_SSM = 512
N_SSM_HEADS = 8
SSM_HEAD_DIM = 64
D_STATE = 128
N_SSM_GROUPS = 2
SSM_CONV = 4
CONV_DIM = D_SSM + 2 * N_SSM_GROUPS * D_STATE
SSD_CHUNK = 128
D_FF = 2816
FFN_CONV = 3
D_PLE = 256
EPS = 1e-6
NEG = -1e30

LANE = 128
SUBLANE = 8
FF_CHUNK = 256
VMEM_LIMIT = 56 * 1024 * 1024

O_Q, O_K, O_V, O_Z, O_XBC, O_DT = 0, 512, 1024, 1536, 2048, 3072
IN_MAIN = O_DT + LANE


def _rel_bucket_np(dist):
    dist = np.asarray(dist, np.int32)
    max_exact = N_BUCKETS // 2
    d = np.maximum(dist, 1).astype(np.float32)
    large = max_exact + (np.log(d / np.float32(max_exact)) / np.float32(math.log(MAX_DISTANCE / max_exact))
                         * np.float32(N_BUCKETS - max_exact)).astype(np.int32)
    large = np.minimum(large, N_BUCKETS - 1)
    return np.where(dist < max_exact, dist, large)


def _nt_dot(a, b):
    return lax.dot_general(a, b, (((1,), (1,)), ((), ())), preferred_element_type=F32)


def _dot(a, b):
    return jnp.dot(a, b, preferred_element_type=F32)


def _silu(x):
    return x * jax.nn.sigmoid(x)


def _softplus(x):
    return jnp.maximum(x, 0.0) + jnp.log1p(jnp.exp(-jnp.abs(x)))


def _rms(x, g):
    return x * lax.rsqrt(jnp.mean(x * x, axis=-1, keepdims=True) + EPS) * g


def _lane_lt64(shape):
    return lax.broadcasted_iota(jnp.int32, shape, len(shape) - 1) < HEAD_DIM


def _inproj_kernel(x_ref, g_ref, w_ref, qb_ref, kb_ref, vb_ref, kf_ref, vf_ref, z_ref, xbc_ref, dt_ref):
    xn = _rms(x_ref[...], g_ref[...]).astype(BF16)

    def proj(lo, hi):
        return _dot(xn, w_ref[:, lo:hi])

    q = proj(O_Q, O_K)
    qb_ref[...] = (q * (HEAD_DIM ** -0.5)).astype(BF16)
    k = proj(O_K, O_V)
    kf_ref[...] = k
    kb_ref[...] = k.astype(BF16)
    v = proj(O_V, O_Z)
    vf_ref[...] = v
    vb_ref[...] = v.astype(BF16)
    z_ref[...] = proj(O_Z, O_XBC)
    xbc_ref[...] = proj(O_XBC, O_DT)
    dt_ref[...] = proj(O_DT, IN_MAIN)


def _inproj(x2d, g_mix, w_main, tm):
    rows = x2d.shape[0]
    row = lambda i: (i, 0)
    fixed = lambda i: (0, 0)
    widths = (D_ATTN, D_ATTN, D_ATTN, D_ATTN, D_ATTN, D_SSM, CONV_DIM, LANE)
    dtypes = (BF16, BF16, BF16, F32, F32, F32, F32, F32)
    return pl.pallas_call(
        _inproj_kernel,
        grid=(rows // tm,),
        in_specs=[pl.BlockSpec((tm, D_MODEL), row),
                  pl.BlockSpec((1, D_MODEL), fixed),
                  pl.BlockSpec((D_MODEL, IN_MAIN), fixed)],
        out_specs=[pl.BlockSpec((tm, w), row) for w in widths],
        out_shape=[jax.ShapeDtypeStruct((rows, w), dt) for w, dt in zip(widths, dtypes)],
        compiler_params=pltpu.CompilerParams(dimension_semantics=("parallel",),
                                             vmem_limit_bytes=VMEM_LIMIT),
        name="inproj",
    )(x2d, g_mix, w_main)


def _ssd_kernel(xbc_ref, z_ref, dt_ref, prefix_ref, state0_ref, cw_ref, cb_ref, dtb_ref, alog_ref,
                dskip_ref, gssm_ref, y_ref, state_ref, xp_ref, st_ref, *, n_chunks, valid_len):
    L = SSD_CHUNK
    c = pl.program_id(1)

    @pl.when(c == 0)
    def _init():
        xp_ref[0:SUBLANE, :] = jnp.zeros((SUBLANE, CONV_DIM), F32)
        xp_ref[SUBLANE - (SSM_CONV - 1):SUBLANE, :] = prefix_ref[...]
        for jp in range(N_PAIRS):
            st_ref[jp] = state0_ref[2 * jp:2 * jp + 2].reshape(2 * SSM_HEAD_DIM, D_STATE).T

    xp_ref[SUBLANE:SUBLANE + L, :] = xbc_ref[...]
    conv = cb_ref[...]
    for k in range(SSM_CONV):
        lo = SUBLANE - (SSM_CONV - 1) + k
        conv = conv + cw_ref[k:k + 1, :] * xp_ref[lo:lo + L, :]
    xp_ref[0:SUBLANE, :] = xp_ref[L:L + SUBLANE, :]
    xc = _silu(conv)
    xs = xc[:, :D_SSM]
    gn = N_SSM_GROUPS * D_STATE

    row = lax.broadcasted_iota(jnp.int32, (L, L), 0)
    col = lax.broadcasted_iota(jnp.int32, (L, L), 1)
    tri = row >= col
    lt64 = col < SSM_HEAD_DIM

    dt = _softplus(dt_ref[...] + dtb_ref[...])
    if valid_len < L:
        dt = jnp.where(row < valid_len, dt, 0.0)
    a = dt * (-jnp.exp(alog_ref[...]))
    acum = a
    shift = 1
    while shift < L:
        acum = acum + jnp.where(row >= shift, pltpu.roll(acum, shift, 0), 0.0)
        shift *= 2
    acum_t = acum.T
    dt_t = dt.T

    def head_terms(h, cb):
        colb = jnp.broadcast_to(acum[:, h:h + 1], (L, L))
        rowb = jnp.broadcast_to(acum_t[h:h + 1, :], (L, L))
        decay = jnp.exp(jnp.where(tri, colb - rowb, NEG))
        m = (cb * decay * jnp.broadcast_to(dt_t[h:h + 1, :], (L, L))).astype(BF16)
        lastb = jnp.sum(jnp.where(row == L - 1, colb, 0.0), axis=0, keepdims=True)
        dtcol = jnp.broadcast_to(dt[:, h:h + 1], (L, L))
        return m, jnp.exp(colb), jnp.exp(lastb - colb) * dtcol, jnp.exp(lastb)

    pairs = []
    for g in range(N_SSM_GROUPS):
        bg = xc[:, D_SSM + g * D_STATE:D_SSM + (g + 1) * D_STATE]
        cg = xc[:, D_SSM + gn + g * D_STATE:D_SSM + gn + (g + 1) * D_STATE].astype(BF16)
        cb = _nt_dot(cg, bg.astype(BF16))
        bg_t = bg.T.astype(BF16)
        for i in range(N_PAIRS // N_SSM_GROUPS):
            jp = g * (N_PAIRS // N_SSM_GROUPS) + i
            xpair = xs[:, jp * LANE:(jp + 1) * LANE]
            xpair_b = xpair.astype(BF16)
            m_a, e_a, de_a, el_a = head_terms(2 * jp, cb)
            m_b, e_b, de_b, el_b = head_terms(2 * jp + 1, cb)
            st = st_ref[jp]
            y_diag = jnp.where(lt64, _dot(m_a, xpair_b), _dot(m_b, xpair_b))
            y_off = _dot(cg, st.astype(BF16)) * jnp.where(lt64, e_a, e_b)
            xd = (xpair * jnp.where(lt64, de_a, de_b)).astype(BF16)
            st_ref[jp] = st * jnp.where(lt64[0:1, :], el_a, el_b) + _dot(bg_t, xd)
            pairs.append(y_diag + y_off + dskip_ref[:, jp * LANE:(jp + 1) * LANE] * xpair)
    y = jnp.concatenate(pairs, axis=1)

    yf = y * _silu(z_ref[...])
    gw = D_SSM // N_SSM_GROUPS
    normed = []
    for g in range(N_SSM_GROUPS):
        seg = yf[:, g * gw:(g + 1) * gw]
        normed.append(seg * lax.rsqrt(jnp.mean(seg * seg, axis=-1, keepdims=True) + EPS))
    y_ref[...] = (jnp.concatenate(normed, axis=1) * gssm_ref[...]).astype(y_ref.dtype)

    @pl.when(c == n_chunks - 1)
    def _fin():
        for jp in range(N_PAIRS):
            state_ref[2 * jp:2 * jp + 2] = st_ref[jp].T.reshape(2, SSM_HEAD_DIM, D_STATE)


def _ssd(xbc, z, dt, prefix, state0, cw, cb, dtb, alog, dskip, gssm, *, valid_len):
    ns, lp, _ = xbc.shape
    n_chunks = lp // SSD_CHUNK
    chunk = lambda s, c: (s, c, 0)
    seq = lambda s, c: (s, 0, 0)
    fixed = lambda s, c: (0, 0)
    kern = functools.partial(_ssd_kernel, n_chunks=n_chunks, valid_len=valid_len)
    return pl.pallas_call(
        kern,
        grid=(ns, n_chunks),
        in_specs=[pl.BlockSpec((None, SSD_CHUNK, CONV_DIM), chunk),
                  pl.BlockSpec((None, SSD_CHUNK, D_SSM), chunk),
                  pl.BlockSpec((None, SSD_CHUNK, LANE), chunk),
                  pl.BlockSpec((None, SSM_CONV - 1, CONV_DIM), seq),
                  pl.BlockSpec((None, N_SSM_HEADS, SSM_HEAD_DIM, D_STATE), lambda s, c: (s, 0, 0, 0)),
                  pl.BlockSpec((SSM_CONV, CONV_DIM), fixed),
                  pl.BlockSpec((1, CONV_DIM), fixed),
                  pl.BlockSpec((1, LANE), fixed),
                  pl.BlockSpec((1, LANE), fixed),
                  pl.BlockSpec((1, D_SSM), fixed),
                  pl.BlockSpec((1, D_SSM), fixed)],
        out_specs=[pl.BlockSpec((None, SSD_CHUNK, D_SSM), chunk),
                   pl.BlockSpec((None, N_SSM_HEADS, SSM_HEAD_DIM, D_STATE), lambda s, c: (s, 0, 0, 0))],
        out_shape=[jax.ShapeDtypeStruct((ns, lp, D_SSM), BF16),
                   jax.ShapeDtypeStruct((ns, N_SSM_HEADS, SSM_HEAD_DIM, D_STATE), F32)],
        scratch_shapes=[pltpu.VMEM((SUBLANE + SSD_CHUNK, CONV_DIM), F32),
                        pltpu.VMEM((N_PAIRS, D_STATE, LANE), F32)],
        compiler_params=pltpu.CompilerParams(dimension_semantics=("parallel", "arbitrary"),
                                             vmem_limit_bytes=VMEM_LIMIT),
        name="ssd",
    )(xbc, z, dt, prefix, state0, cw, cb, dtb, alog, dskip, gssm)


def _attn_prompt_kernel(q_ref, kp_ref, kc_ref, vp_ref, vc_ref, bias_ref, o_ref, lse_ref):
    first = (pl.program_id(2) == 0).astype(jnp.int32)
    lt64 = _lane_lt64((BLK, LANE))
    lane = lax.broadcasted_iota(jnp.int32, (BLK, LANE), 1)
    lse_slab = jnp.zeros((BLK, LANE), F32)
    zero = jnp.zeros((BLK, LANE), BF16)
    for jp in range(N_PAIRS):
        sl = slice(jp * LANE, (jp + 1) * LANE)
        qp = q_ref[:, sl]
        kp, kc, vp, vc = kp_ref[:, sl], kc_ref[:, sl], vp_ref[:, sl], vc_ref[:, sl]
        outs = []
        for half in range(2):
            h = 2 * jp + half
            qm = jnp.where(lt64 if half == 0 else ~lt64, qp, zero)
            s_p = _nt_dot(qm, kp) + bias_ref[first, h, :, 0:BLK]
            s_c = _nt_dot(qm, kc) + bias_ref[first, h, :, BLK:2 * BLK]
            m = jnp.maximum(jnp.max(s_p, axis=-1, keepdims=True), jnp.max(s_c, axis=-1, keepdims=True))
            p_p = jnp.exp(s_p - m)
            p_c = jnp.exp(s_c - m)
            den = jnp.sum(p_p, axis=-1, keepdims=True) + jnp.sum(p_c, axis=-1, keepdims=True)
            pv = _dot(p_p.astype(BF16), vp) + _dot(p_c.astype(BF16), vc)
            outs.append(pv / den)
            lse_slab = jnp.where(lane == h, m + jnp.log(den), lse_slab)
        o_ref[:, sl] = jnp.where(lt64, outs[0], outs[1]).astype(o_ref.dtype)
    lse_ref[...] = lse_slab


def _attn_prompt_branch(qb, kb, vb, bias, dil):
    bsz, s, _ = qb.shape
    l = s // dil
    nb = l // BLK
    view = lambda t: t.reshape(bsz, l, dil * D_ATTN)
    cur = lambda b, r, n: (b, n, r)
    prev = lambda b, r, n: (b, jnp.maximum(n - 1, 0), r)
    blk = pl.BlockSpec((None, BLK, D_ATTN), cur)
    blk_prev = pl.BlockSpec((None, BLK, D_ATTN), prev)
    o, lse = pl.pallas_call(
        _attn_prompt_kernel,
        grid=(bsz, dil, nb),
        in_specs=[blk, blk_prev, blk, blk_prev, blk,
                  pl.BlockSpec((2, N_HEADS, BLK, 2 * BLK), lambda b, r, n: (0, 0, 0, 0))],
        out_specs=[blk, pl.BlockSpec((None, BLK, LANE), cur)],
        out_shape=[jax.ShapeDtypeStruct((bsz, l, dil * D_ATTN), BF16),
                   jax.ShapeDtypeStruct((bsz, l, dil * LANE), F32)],
        compiler_params=pltpu.CompilerParams(dimension_semantics=("parallel", "parallel", "parallel"),
                                             vmem_limit_bytes=VMEM_LIMIT),
        name=f"attn_prompt_d{dil}",
    )(view(qb), view(kb), view(kb), view(vb), view(vb), bias)
    return o.reshape(bsz, s, D_ATTN), lse.reshape(bsz, s, LANE)


def _prompt_bias_tables(rel_bias):
    i = np.arange(BLK)[:, None]
    c = np.arange(2 * BLK)[None, :]
    j = BLK + i - c
    band = (j >= 0) & (j <= N_STEPS)
    tables = []
    for _, dil in DILATED_BRANCHES:
        bucket = _rel_bucket_np(np.clip(j, 0, N_STEPS) * dil)
        vals = jnp.transpose(rel_bias[bucket], (2, 0, 1))
        t0 = jnp.where(band[None], vals, NEG)
        t1 = jnp.where((band & (c >= BLK))[None], vals, NEG)
        tables.append(jnp.stack([t0, t1]).astype(F32))
    return tables


def _attn_sample_kernel(q_ref, kn_ref, vn_ref, k16_ref, v16_ref, k4_ref, v4_ref, k1_ref, v1_ref,
                        bias_ref, o_ref, knp_ref, vnp_ref):
    t_len = q_ref.shape[0]
    rows = t_len * N_HEADS
    rid = lax.broadcasted_iota(jnp.int32, (rows, D_ATTN), 0)
    cid = lax.broadcasted_iota(jnp.int32, (rows, D_ATTN), 1)
    head_of_row = rid & (N_HEADS - 1)
    q = q_ref[...]
    qt = jnp.concatenate([jnp.broadcast_to(q[t:t + 1, :], (N_HEADS, D_ATTN)) for t in range(t_len)], axis=0)
    in_head = (cid >= head_of_row * HEAD_DIM) & (cid < (head_of_row + 1) * HEAD_DIM)
    qb = jnp.where(in_head, qt, jnp.zeros_like(qt))

    knp_ref[...] = jnp.zeros(knp_ref.shape, BF16)
    vnp_ref[...] = jnp.zeros(vnp_ref.shape, BF16)
    knp_ref[0:t_len, :] = kn_ref[...]
    vnp_ref[0:t_len, :] = vn_ref[...]

    t_of_row = lax.broadcasted_iota(jnp.int32, (rows, LANE), 0) >> 3

    groups = []
    for br, (k_ref, v_ref, dil) in enumerate(((k1_ref, v1_ref, 1), (k4_ref, v4_ref, 4), (k16_ref, v16_ref, 16))):
        nres = min(dil, t_len)
        s = None
        sels = []
        for r in range(nres):
            kr = k_ref[:, r * D_ATTN:(r + 1) * D_ATTN].astype(BF16)
            sr = _nt_dot(qb, kr)
            sel = None if nres == 1 else (t_of_row & (dil - 1)) == r
            s = sr if sel is None else (jnp.where(sel, sr, 0.0) if s is None else jnp.where(sel, sr, s))
            sels.append(sel)
        groups.append((s + bias_ref[br], v_ref, sels))
    s_new = _nt_dot(qb, knp_ref[...])
    new_tiles = [s_new + bias_ref[3 + br] for br in range(3)]

    m = None
    for tile in [g[0] for g in groups] + new_tiles:
        tm = jnp.max(tile, axis=-1, keepdims=True)
        m = tm if m is None else jnp.maximum(m, tm)
    den = jnp.zeros((rows, 1), F32)
    acc = jnp.zeros((rows, D_ATTN), F32)
    for s, v_ref, sels in groups:
        p = jnp.exp(s - m)
        den = den + jnp.sum(p, axis=-1, keepdims=True)
        for r, sel in enumerate(sels):
            pr = p if sel is None else jnp.where(sel, p, 0.0)
            acc = acc + _dot(pr.astype(BF16), v_ref[:, r * D_ATTN:(r + 1) * D_ATTN].astype(BF16))
    p_new = None
    for tile in new_tiles:
        p = jnp.exp(tile - m)
        den = den + jnp.sum(p, axis=-1, keepdims=True)
        p_new = p if p_new is None else p_new + p
    acc = acc + _dot(p_new.astype(BF16), vnp_ref[...])
    acc = jnp.where(in_head, acc / den, 0.0)
    o_ref[...] = jnp.sum(acc.reshape(t_len, N_HEADS, D_ATTN), axis=1).astype(o_ref.dtype)


def _sample_bias_tables(rel_bias, t_len):
    t = np.arange(t_len)[:, None]
    g = np.arange(LANE)[None, :]
    specs = []
    specs.append((BLK + t - g, 1))
    specs.append((BLK + t // 4 - g, 4))
    specs.append((BLK - g + 0 * t, 16))
    for dil in (1, 4, 16):
        d = t - g
        j = np.where((d >= 0) & (d % dil == 0) & (g < t_len), d // dil, -1)
        specs.append((j, dil))
    tables = []
    for j, dil in specs:
        valid = (j >= 0) & (j <= N_STEPS)
        bucket = _rel_bucket_np(np.clip(j, 0, N_STEPS) * dil)
        vals = rel_bias[bucket]
        tab = jnp.where(valid[:, :, None], vals, NEG)
        tables.append(jnp.transpose(tab, (0, 2, 1)).reshape(t_len * N_HEADS, LANE))
    return jnp.stack(tables).astype(F32)


def _attn_sample(qb, kb, vb, cache_k, cache_v, bias):
    ns, t_len, _ = qb.shape
    w = cache_k.shape[1]
    assert w == MAX_DISTANCE and t_len == SUBLANE
    seq = lambda s: (s, 0, 0)
    v16 = lambda c: c.reshape(ns, w // 16, 16 * D_ATTN)
    v4 = lambda c: c.reshape(ns, w // 4, 4 * D_ATTN)
    spec16 = pl.BlockSpec((None, BLK, t_len * D_ATTN), seq)
    spec4 = pl.BlockSpec((None, BLK, 4 * D_ATTN), lambda s: (s, w // 4 // BLK - 1, 0))
    spec1 = pl.BlockSpec((None, BLK, D_ATTN), lambda s: (s, w // BLK - 1, 0))
    tok = pl.BlockSpec((None, t_len, D_ATTN), seq)
    return pl.pallas_call(
        _attn_sample_kernel,
        grid=(ns,),
        in_specs=[tok, tok, tok, spec16, spec16, spec4, spec4, spec1, spec1,
                  pl.BlockSpec(bias.shape, lambda s: (0, 0, 0))],
        out_specs=tok,
        out_shape=jax.ShapeDtypeStruct((ns, t_len, D_ATTN), BF16),
        scratch_shapes=[pltpu.VMEM((BLK, D_ATTN), BF16), pltpu.VMEM((BLK, D_ATTN), BF16)],
        compiler_params=pltpu.CompilerParams(dimension_semantics=("parallel",),
                                             vmem_limit_bytes=VMEM_LIMIT),
        name="attn_sample",
    )(qb, kb, vb, v16(cache_k), v16(cache_v), v4(cache_k), v4(cache_v), cache_k, cache_v, bias)


def _mix_ffn_kernel(*refs, n_parts, seg_in_tile, n_tiles):
    it = iter(refs)
    x_ref = next(it)
    parts = [(next(it), next(it)) for _ in range(n_parts)] if n_parts > 1 else [(next(it), None)]
    yssm_ref, p_ref, fpre_ref = next(it), next(it), next(it)
    (expand_ref, wout_ref, gffn_ref, wup_ref, fcw_ref, fcb_ref, wdown_ref, wple_ref, gple_ref,
     wgate_ref, gfin_ref) = (next(it) for _ in range(11))
    y_ref, uout_ref = next(it), next(it)
    act_ref, uext_ref, tail_ref = next(it), next(it), next(it)
    tm = x_ref.shape[0]

    if n_parts > 1:
        lses = [lse_ref[...] for _, lse_ref in parts]
        mx = functools.reduce(jnp.maximum, lses)
        ws = [jnp.exp(l - mx) for l in lses]
        den = functools.reduce(lambda a, b: a + b, ws)
        attn = None
        for (o_ref, _), w in zip(parts, ws):
            wn = w / den
            hi = wn.astype(BF16)
            lo = (wn - hi.astype(F32)).astype(BF16)
            wexp = _dot(hi, expand_ref[...]) + _dot(lo, expand_ref[...])
            term = wexp * o_ref[...].astype(F32)
            attn = term if attn is None else attn + term
        attn = attn.astype(BF16)
    else:
        attn = parts[0][0][...]
    h1 = x_ref[...] + _dot(attn, wout_ref[0:D_ATTN, :]) + _dot(yssm_ref[...], wout_ref[D_ATTN:, :])

    hn = _rms(h1, gffn_ref[...]).astype(BF16)

    if not seg_in_tile:
        t = pl.program_id(1)

        @pl.when(t == 0)
        def _init():
            tail_ref[...] = jnp.zeros(tail_ref.shape, F32)
            tail_ref[SUBLANE - (FFN_CONV - 1):SUBLANE, :] = fpre_ref[...]
    else:
        rin = lax.broadcasted_iota(jnp.int32, (tm, FF_CHUNK), 0) & (SUBLANE - 1)

    def conv(u, c0):
        cs = slice(c0, c0 + FF_CHUNK)
        if seg_in_tile:
            pre = fpre_ref[:, cs]
            um1 = jnp.where(rin == 0, pltpu.roll(pre, tm - 1, 0), pltpu.roll(u, 1, 0))
            um2 = jnp.where(rin < 2, pre, pltpu.roll(u, 2, 0))
            uout_ref[:, cs] = u
        else:
            uext_ref[0:SUBLANE, :] = tail_ref[:, cs]
            uext_ref[SUBLANE:SUBLANE + tm, :] = u
            tail_ref[:, cs] = u[tm - SUBLANE:tm, :]
            um2 = uext_ref[SUBLANE - 2:SUBLANE - 2 + tm, :]
            um1 = uext_ref[SUBLANE - 1:SUBLANE - 1 + tm, :]
        return fcb_ref[:, cs] + fcw_ref[0:1, cs] * um2 + fcw_ref[1:2, cs] * um1 + fcw_ref[2:3, cs] * u

    for j in range(D_FF // FF_CHUNK):
        c0 = j * FF_CHUNK
        u_gate = conv(_dot(hn, wup_ref[:, c0:c0 + FF_CHUNK]), c0)
        u_lin = conv(_dot(hn, wup_ref[:, D_FF + c0:D_FF + c0 + FF_CHUNK]), D_FF + c0)
        act_ref[:, c0:c0 + FF_CHUNK] = (_silu(u_gate) * u_lin).astype(BF16)
    h2 = h1 + _dot(act_ref[...], wdown_ref[...])

    e = _rms(_dot(p_ref[...].astype(BF16), wple_ref[...]), gple_ref[...])
    h3 = h2 + jax.nn.sigmoid(_dot(h2.astype(BF16), wgate_ref[...])) * e
    y_ref[...] = _rms(h3, gfin_ref[...])

    if not seg_in_tile:
        @pl.when(pl.program_id(1) == n_tiles - 1)
        def _fin():
            uout_ref[...] = tail_ref[...]


def _mix_ffn(x, parts, yssm, p, fpre, weights, *, tm, seg_in_tile):
    ns, l, _ = x.shape
    n_tiles = l // tm
    n_parts = len(parts)
    tile = lambda s, t: (s, t, 0)
    seq = lambda s, t: (s, 0, 0)
    fixed = lambda s, t: (0, 0)
    in_specs = [pl.BlockSpec((None, tm, D_MODEL), tile)]
    args = [x]
    for o, lse in parts:
        in_specs.append(pl.BlockSpec((None, tm, D_ATTN), tile))
        args.append(o)
        if n_parts > 1:
            in_specs.append(pl.BlockSpec((None, tm, LANE), tile))
            args.append(lse)
    in_specs += [pl.BlockSpec((None, tm, D_SSM), tile), pl.BlockSpec((None, tm, D_PLE), tile)]
    args += [yssm, p]
    if seg_in_tile:
        in_specs.append(pl.BlockSpec((None, tm, 2 * D_FF), tile))
        uout_spec = pl.BlockSpec((None, tm, 2 * D_FF), tile)
        uout_shape = jax.ShapeDtypeStruct((ns, l, 2 * D_FF), F32)
    else:
        in_specs.append(pl.BlockSpec((None, FFN_CONV - 1, 2 * D_FF), seq))
        uout_spec = pl.BlockSpec((None, SUBLANE, 2 * D_FF), seq)
        uout_shape = jax.ShapeDtypeStruct((ns, SUBLANE, 2 * D_FF), F32)
    args.append(fpre)
    for w in weights:
        in_specs.append(pl.BlockSpec(w.shape, fixed))
        args.append(w)
    kern = functools.partial(_mix_ffn_kernel, n_parts=n_parts, seg_in_tile=seg_in_tile, n_tiles=n_tiles)
    return pl.pallas_call(
        kern,
        grid=(ns, n_tiles),
        in_specs=in_specs,
        out_specs=[pl.BlockSpec((None, tm, D_MODEL), tile), uout_spec],
        out_shape=[jax.ShapeDtypeStruct((ns, l, D_MODEL), F32), uout_shape],
        scratch_shapes=[pltpu.VMEM((tm, D_FF), BF16),
                        pltpu.VMEM((SUBLANE + tm, FF_CHUNK), F32),
                        pltpu.VMEM((SUBLANE, 2 * D_FF), F32)],
        compiler_params=pltpu.CompilerParams(dimension_semantics=("parallel", "arbitrary"),
                                             vmem_limit_bytes=VMEM_LIMIT),
        name="mix_ffn_seg" if seg_in_tile else "mix_ffn",
    )(*args)


def _pad_lanes(v, width=LANE):
    return jnp.pad(v.astype(F32), (0, width - v.shape[0]))[None, :]


def kernel(x_prompt, x_sample, p_prompt, p_sample, cache_k, cache_v, state_ssm, state_conv, state_ffn_conv,
           rel_bias, g_mix, w_in, conv_w, conv_b, dt_bias, a_log, d_skip, g_ssm, w_out, g_ffn, w_up,
           ffn_conv_w, ffn_conv_b, w_down, w_ple_proj, g_ple, w_ple_gate, g_final):
    assert w_in.shape[0] == 1, "one layer"
    bp, s, _ = x_prompt.shape
    nsamp, t_len, _ = x_sample.shape
    n_keep = min(MAX_DISTANCE, s)

    w_main = jnp.pad(w_in[0], ((0, 0), (0, IN_MAIN - w_in.shape[2]))).astype(BF16)
    gmix = g_mix[0][None, :]
    ssd_params = (conv_w[0], conv_b[0][None, :], _pad_lanes(dt_bias[0]), _pad_lanes(a_log[0]),
                  jnp.repeat(d_skip[0], SSM_HEAD_DIM)[None, :], g_ssm[0][None, :])
    expand = (np.arange(LANE)[:, None] == (np.arange(D_ATTN)[None, :] // HEAD_DIM)).astype(np.float32)
    ffn_weights = (jnp.asarray(expand, BF16), w_out[0].astype(BF16), g_ffn[0][None, :], w_up[0].astype(BF16),
                   ffn_conv_w[0], ffn_conv_b[0][None, :], w_down[0].astype(BF16), w_ple_proj[0].astype(BF16),
                   g_ple[0][None, :], w_ple_gate[0].astype(BF16), g_final[None, :])

    qb, kb, vb, kf, vf, z, xbc, dt = _inproj(x_prompt.reshape(bp * s, D_MODEL), gmix, w_main, tm=512)
    r3 = lambda a: a.reshape(bp, s, a.shape[-1])
    yssm_p, ssm_p = _ssd(r3(xbc), r3(z), r3(dt),
                         jnp.zeros((bp, SSM_CONV - 1, CONV_DIM), F32),
                         jnp.zeros((bp, N_SSM_HEADS, SSM_HEAD_DIM, D_STATE), F32),
                         *ssd_params, valid_len=SSD_CHUNK)
    parts = [_attn_prompt_branch(r3(qb), r3(kb), r3(vb), bias, dil)
             for bias, (_, dil) in zip(_prompt_bias_tables(rel_bias), DILATED_BRANCHES)]
    y_prompt, tail_p = _mix_ffn(x_prompt, parts, yssm_p, p_prompt[0],
                                jnp.zeros((bp, FFN_CONV - 1, 2 * D_FF), F32), ffn_weights,
                                tm=512, seg_in_tile=False)
    k_prompt = r3(kf)[:, s - n_keep:].reshape(1, bp, n_keep, N_HEADS, HEAD_DIM)
    v_prompt = r3(vf)[:, s - n_keep:].reshape(1, bp, n_keep, N_HEADS, HEAD_DIM)
    conv_prompt = r3(xbc)[:, s - (SSM_CONV - 1):][None]
    ffn_conv_prompt = tail_p[:, SUBLANE - (FFN_CONV - 1):][None]

    rows = nsamp * t_len
    qs, ks, vs, kfs, vfs, zs, xbcs, dts = _inproj(x_sample.reshape(rows, D_MODEL), gmix, w_main, tm=rows)
    s3 = lambda a: a.reshape(nsamp, t_len, a.shape[-1])
    pad_t = lambda a: jnp.pad(s3(a), ((0, 0), (0, SSD_CHUNK - t_len), (0, 0)))
    yssm_s, ssm_s = _ssd(pad_t(xbcs), pad_t(zs), pad_t(dts), state_conv[0], state_ssm[0],
                         *ssd_params, valid_len=t_len)
    attn_s = _attn_sample(s3(qs), s3(ks), s3(vs),
                          cache_k[0].reshape(nsamp, -1, D_ATTN), cache_v[0].reshape(nsamp, -1, D_ATTN),
                          _sample_bias_tables(rel_bias, t_len))
    fpre = jnp.pad(state_ffn_conv[0], ((0, 0), (0, t_len - (FFN_CONV - 1)), (0, 0))).reshape(1, rows, 2 * D_FF)
    y_s, u_s = _mix_ffn(x_sample.reshape(1, rows, D_MODEL), [(attn_s.reshape(1, rows, D_ATTN), None)],
                        yssm_s[:, :t_len].reshape(1, rows, D_SSM), p_sample[0].reshape(1, rows, D_PLE),
                        fpre, ffn_weights, tm=rows, seg_in_tile=True)
    y_sample = y_s.reshape(nsamp, t_len, D_MODEL)
    k_sample = kfs.reshape(1, nsamp, t_len, N_HEADS, HEAD_DIM)
    v_sample = vfs.reshape(1, nsamp, t_len, N_HEADS, HEAD_DIM)
    conv_sample = s3(xbcs)[:, t_len - (SSM_CONV - 1):][None]
    ffn_conv_sample = u_s.reshape(nsamp, t_len, 2 * D_FF)[:, t_len - (FFN_CONV - 1):][None]

    return (y_prompt, y_sample, k_prompt, v_prompt, k_sample, v_sample,
            ssm_p[None], ssm_s[None], conv_prompt, conv_sample, ffn_conv_prompt, ffn_conv_sample)
```

```python
import functools
import math

import numpy as np
import jax
import jax.numpy as jnp
from jax import lax
from jax.experimental import pallas as pl
from jax.experimental.pallas import tpu as pltpu

F32 = jnp.float32
BF16 = jnp.bfloat16

D_MODEL = 1024
HEAD_DIM = 64
N_HEADS = 8
D_ATTN = N_HEADS * HEAD_DIM
N_PAIRS = N_HEADS // 2
DILATIONS = (1, 4, 16)
N_STEPS = 128
BLK = 128
N_BUCKETS = 32
MAX_DISTANCE = 2048
D_SSM = 512
N_SSM_HEADS = 8
SSM_HEAD_DIM = 64
D_STATE = 128
N_SSM_GROUPS = 2
SSM_CONV = 4
CONV_DIM = D_SSM + 2 * N_SSM_GROUPS * D_STATE
SSD_CHUNK = 128
D_FF = 2816
FFN_CONV = 3
D_PLE = 256
EPS = 1e-6
NEG = -1e30

LANE = 128
SUBLANE = 8
FF_CHUNK = 256
VMEM_LIMIT = 56 * 1024 * 1024

O_Q, O_K, O_V, O_Z, O_XBC, O_DT = 0, 512, 1024, 1536, 2048, 3072
IN_MAIN = O_DT + LANE


def _rel_bucket_np(dist):
    dist = np.asarray(dist, np.int32)
    max_exact = N_BUCKETS // 2
    d = np.maximum(dist, 1).astype(np.float32)
    large = max_exact + (np.log(d / np.float32(max_exact)) / np.float32(math.log(MAX_DISTANCE / max_exact))
                         * np.float32(N_BUCKETS - max_exact)).astype(np.int32)
    large = np.minimum(large, N_BUCKETS - 1)
    return np.where(dist < max_exact, dist, large)


def _nt_dot(a, b):
    return lax.dot_general(a, b, (((1,), (1,)), ((), ())), preferred_element_type=F32)


def _dot(a, b):
    return jnp.dot(a, b, preferred_element_type=F32)


def _silu(x):
    return x * jax.nn.sigmoid(x)


def _softplus(x):
    return jnp.maximum(x, 0.0) + jnp.log1p(jnp.exp(-jnp.abs(x)))


def _rms(x, g):
    return x * lax.rsqrt(jnp.mean(x * x, axis=-1, keepdims=True) + EPS) * g


def _lane_lt64(shape):
    return lax.broadcasted_iota(jnp.int32, shape, len(shape) - 1) < HEAD_DIM


def _params(*sem):
    return pltpu.CompilerParams(dimension_semantics=sem, vmem_limit_bytes=VMEM_LIMIT)


def _bias_kernel(rb_ref, ip_ref, i16_ref, i4_ref, i1_ref, in_ref, tp_ref, t16_ref, t4_ref, t1_ref, tn_ref):
    def lookup(idx, h):
        def body(b, acc):
            return jnp.where(idx == b, rb_ref[b, h], acc)
        return lax.fori_loop(0, N_BUCKETS, body, jnp.full(idx.shape, NEG, F32))

    for br in range(len(DILATIONS)):
        for half in range(2):
            cs = slice(half * BLK, (half + 1) * BLK)
            idx = ip_ref[br, :, cs]
            for h in range(N_HEADS):
                tp_ref[br, h, :, cs] = lookup(idx, h)
    for h in range(N_HEADS):
        jp, half = divmod(h, 2)
        rs = slice(half * SUBLANE, (half + 1) * SUBLANE)
        t16_ref[jp, rs, :] = lookup(i16_ref[...], h)
        t4_ref[jp, rs, :] = lookup(i4_ref[...], h)
        t1_ref[jp, rs, :] = lookup(i1_ref[...], h)
        for br in range(len(DILATIONS)):
            tn_ref[br, jp, rs, :] = lookup(in_ref[br], h)


def _bucket_maps(t_len):
    i = np.arange(BLK)[:, None]
    c = np.arange(2 * BLK)[None, :]
    j = BLK + i - c
    band = (j >= 0) & (j <= N_STEPS)
    prompt = np.stack([np.where(band, _rel_bucket_np(np.clip(j, 0, N_STEPS) * d), -1) for d in DILATIONS])

    t = np.arange(t_len)[:, None]

    def sample_map(diff, dil):
        ok = (diff >= 0) & (diff % dil == 0) & (diff // dil <= N_STEPS)
        return np.where(ok, _rel_bucket_np(np.maximum(diff, 0)), -1).astype(np.int32)

    w = np.arange(MAX_DISTANCE)[None, :]
    cache = {d: sample_map(MAX_DISTANCE + t - w, d) for d in DILATIONS}
    g = np.arange(LANE)[None, :]
    new = np.stack([np.where(g < t_len, sample_map(t - g, d), -1) for d in DILATIONS])
    return (prompt.astype(np.int32), cache[16], cache[4][:, MAX_DISTANCE - 4 * BLK:],
            cache[1][:, MAX_DISTANCE - BLK:], new.astype(np.int32))


def _bias_tables(rel_bias, t_len):
    maps = _bucket_maps(t_len)
    nb = len(DILATIONS)
    shapes = [(nb, N_HEADS, BLK, 2 * BLK), (N_PAIRS, 2 * t_len, MAX_DISTANCE), (N_PAIRS, 2 * t_len, 4 * BLK),
              (N_PAIRS, 2 * t_len, BLK), (nb, N_PAIRS, 2 * t_len, LANE)]
    return pl.pallas_call(
        _bias_kernel,
        in_specs=[pl.BlockSpec(memory_space=pltpu.SMEM)] + [pl.BlockSpec(memory_space=pltpu.VMEM)] * 5,
        out_specs=[pl.BlockSpec(memory_space=pltpu.VMEM)] * 5,
        out_shape=[jax.ShapeDtypeStruct(s, F32) for s in shapes],
        compiler_params=pltpu.CompilerParams(vmem_limit_bytes=VMEM_LIMIT),
        name="bias_tables",
    )(rel_bias, *[jnp.asarray(m) for m in maps])


def _inproj_kernel(x_ref, g_ref, w_ref, *refs, prompt, keep_from):
    tm = x_ref.shape[0]
    xn = _rms(x_ref[...], g_ref[...]).astype(BF16)

    def proj(lo, hi):
        return _dot(xn, w_ref[:, lo:hi])

    q = proj(O_Q, O_K) * (HEAD_DIM ** -0.5)
    k = proj(O_K, O_V)
    v = proj(O_V, O_Z)
    if prompt:
        (q1, k1, v1, q4, k4, v4, q16, k16, v16, kt_ref, vt_ref, z_ref, xbc_ref, dt_ref, perm_ref) = refs
        for val, nat, r4, r16 in ((q, q1, q4, q16), (k, k1, k4, k16), (v, v1, v4, v16)):
            nat[...] = val.astype(BF16)
            for jp in range(N_PAIRS):
                perm_ref[jp] = val[:, jp * LANE:(jp + 1) * LANE]
            for out_ref, dil in ((r4, 4), (r16, 16)):
                for r in range(dil):
                    rows = [perm_ref[jp, pl.ds(r, tm // dil, stride=dil), :] for jp in range(N_PAIRS)]
                    out_ref[r] = jnp.concatenate(rows, axis=1).astype(BF16)

        @pl.when(pl.program_id(1) >= keep_from)
        def _keep():
            kt_ref[...] = k.T.reshape(N_HEADS, HEAD_DIM, tm)
            vt_ref[...] = v.T.reshape(N_HEADS, HEAD_DIM, tm)
    else:
        qf_ref, kf_ref, vf_ref, z_ref, xbc_ref, dt_ref = refs
        qf_ref[...] = q
        kf_ref[...] = k
        vf_ref[...] = v
    z_ref[...] = proj(O_Z, O_XBC)
    xbc_ref[...] = proj(O_XBC, O_DT)
    dt_ref[...] = proj(O_DT, IN_MAIN)


def _inproj(x, g_mix, w_main, *, tm, n_keep):
    ns, l, _ = x.shape
    nt = l // tm
    prompt = n_keep > 0
    tile = lambda s, t: (s, t, 0)
    fixed = lambda s, t: (0, 0)
    nat = lambda w, dt: (pl.BlockSpec((None, tm, w), tile), jax.ShapeDtypeStruct((ns, l, w), dt))
    outs = []
    if prompt:
        keep_from = (l - n_keep) // tm
        outs += [nat(D_ATTN, BF16)] * 3
        for dil in (4, 16):
            outs += [(pl.BlockSpec((None, dil, tm // dil, D_ATTN), lambda s, t: (s, 0, t, 0)),
                      jax.ShapeDtypeStruct((ns, dil, l // dil, D_ATTN), BF16))] * 3
        outs += [(pl.BlockSpec((None, N_HEADS, HEAD_DIM, tm), lambda s, t: (s, 0, 0, jnp.maximum(t - keep_from, 0))),
                  jax.ShapeDtypeStruct((ns, N_HEADS, HEAD_DIM, n_keep), F32))] * 2
        scratch = [pltpu.VMEM((N_PAIRS, tm, LANE), F32)]
    else:
        keep_from = 0
        outs += [nat(D_ATTN, F32)] * 3
        scratch = []
    outs += [nat(D_SSM, F32), nat(CONV_DIM, F32), nat(LANE, F32)]
    return pl.pallas_call(
        functools.partial(_inproj_kernel, prompt=prompt, keep_from=keep_from),
        grid=(ns, nt),
        in_specs=[pl.BlockSpec((None, tm, D_MODEL), tile),
                  pl.BlockSpec((1, D_MODEL), fixed),
                  pl.BlockSpec((D_MODEL, IN_MAIN), fixed)],
        out_specs=[o[0] for o in outs],
        out_shape=[o[1] for o in outs],
        scratch_shapes=scratch,
        compiler_params=_params("parallel", "arbitrary"),
        name="inproj" if prompt else "inproj_sample",
    )(x, g_mix, w_main)


def _ssd_kernel(xbc_ref, z_ref, dt_ref, prefix_ref, state0_ref, cw_ref, cb_ref, dtb_ref, alog_ref,
                dskip_ref, gssm_ref, y_ref, state_ref, xp_ref, st_ref, *, n_chunks, valid_len):
    L = SSD_CHUNK
    c = pl.program_id(1)

    @pl.when(c == 0)
    def _init():
        xp_ref[0:SUBLANE, :] = jnp.zeros((SUBLANE, CONV_DIM), F32)
        xp_ref[SUBLANE - (SSM_CONV - 1):SUBLANE, :] = prefix_ref[...]
        for jp in range(N_PAIRS):
            st_ref[jp] = state0_ref[2 * jp:2 * jp + 2].reshape(2 * SSM_HEAD_DIM, D_STATE).T

    xp_ref[SUBLANE:SUBLANE + L, :] = xbc_ref[...]
    conv = cb_ref[...]
    for k in range(SSM_CONV):
        lo = SUBLANE - (SSM_CONV - 1) + k
        conv = conv + cw_ref[k:k + 1, :] * xp_ref[lo:lo + L, :]
    xp_ref[0:SUBLANE, :] = xp_ref[L:L + SUBLANE, :]
    xc = _silu(conv)
    xs = xc[:, :D_SSM]
    gn = N_SSM_GROUPS * D_STATE

    row = lax.broadcasted_iota(jnp.int32, (L, L), 0)
    col = lax.broadcasted_iota(jnp.int32, (L, L), 1)
    tri = row >= col
    lt64 = col < SSM_HEAD_DIM

    dt = _softplus(dt_ref[...] + dtb_ref[...])
    if valid_len < L:
        dt = jnp.where(row < valid_len, dt, 0.0)
    a = dt * (-jnp.exp(alog_ref[...]))
    acum = a
    shift = 1
    while shift < L:
        acum = acum + jnp.where(row >= shift, pltpu.roll(acum, shift, 0), 0.0)
        shift *= 2
    acum_t = acum.T
    dt_t = dt.T

    def head_terms(h, cb):
        colb = jnp.broadcast_to(acum[:, h:h + 1], (L, L))
        rowb = jnp.broadcast_to(acum_t[h:h + 1, :], (L, L))
        decay = jnp.exp(jnp.where(tri, colb - rowb, NEG))
        m = (cb * decay * jnp.broadcast_to(dt_t[h:h + 1, :], (L, L))).astype(BF16)
        lastb = jnp.sum(jnp.where(row == L - 1, colb, 0.0), axis=0, keepdims=True)
        dtcol = jnp.broadcast_to(dt[:, h:h + 1], (L, L))
        return m, jnp.exp(colb), jnp.exp(lastb - colb) * dtcol, jnp.exp(lastb)

    pairs = []
    for g in range(N_SSM_GROUPS):
        bg = xc[:, D_SSM + g * D_STATE:D_SSM + (g + 1) * D_STATE]
        cg = xc[:, D_SSM + gn + g * D_STATE:D_SSM + gn + (g + 1) * D_STATE].astype(BF16)
        cb = _nt_dot(cg, bg.astype(BF16))
        bg_t = bg.T.astype(BF16)
        for i in range(N_PAIRS // N_SSM_GROUPS):
            jp = g * (N_PAIRS // N_SSM_GROUPS) + i
            xpair = xs[:, jp * LANE:(jp + 1) * LANE]
            xpair_b = xpair.astype(BF16)
            m_a, e_a, de_a, el_a = head_terms(2 * jp, cb)
            m_b, e_b, de_b, el_b = head_terms(2 * jp + 1, cb)
            st = st_ref[jp]
            y_diag = jnp.where(lt64, _dot(m_a, xpair_b), _dot(m_b, xpair_b))
            y_off = _dot(cg, st.astype(BF16)) * jnp.where(lt64, e_a, e_b)
            xd = (xpair * jnp.where(lt64, de_a, de_b)).astype(BF16)
            st_ref[jp] = st * jnp.where(lt64[0:1, :], el_a, el_b) + _dot(bg_t, xd)
            pairs.append(y_diag + y_off + dskip_ref[:, jp * LANE:(jp + 1) * LANE] * xpair)
    y = jnp.concatenate(pairs, axis=1)

    yf = y * _silu(z_ref[...])
    gw = D_SSM // N_SSM_GROUPS
    normed = []
    for g in range(N_SSM_GROUPS):
        seg = yf[:, g * gw:(g + 1) * gw]
        normed.append(seg * lax.rsqrt(jnp.mean(seg * seg, axis=-1, keepdims=True) + EPS))
    y_ref[...] = (jnp.concatenate(normed, axis=1) * gssm_ref[...]).astype(y_ref.dtype)

    @pl.when(c == n_chunks - 1)
    def _fin():
        for jp in range(N_PAIRS):
            state_ref[2 * jp:2 * jp + 2] = st_ref[jp].T.reshape(2, SSM_HEAD_DIM, D_STATE)


def _ssd(xbc, z, dt, prefix, state0, cw, cb, dtb, alog, dskip, gssm, *, valid_len):
    ns, lp, _ = xbc.shape
    n_chunks = lp // SSD_CHUNK
    chunk = lambda s, c: (s, c, 0)
    seq = lambda s, c: (s, 0, 0)
    fixed = lambda s, c: (0, 0)
    kern = functools.partial(_ssd_kernel, n_chunks=n_chunks, valid_len=valid_len)
    return pl.pallas_call(
        kern,
        grid=(ns, n_chunks),
        in_specs=[pl.BlockSpec((None, SSD_CHUNK, CONV_DIM), chunk),
                  pl.BlockSpec((None, SSD_CHUNK, D_SSM), chunk),
                  pl.BlockSpec((None, SSD_CHUNK, LANE), chunk),
                  pl.BlockSpec((None, SSM_CONV - 1, CONV_DIM), seq),
                  pl.BlockSpec((None, N_SSM_HEADS, SSM_HEAD_DIM, D_STATE), lambda s, c: (s, 0, 0, 0)),
                  pl.BlockSpec((SSM_CONV, CONV_DIM), fixed),
                  pl.BlockSpec((1, CONV_DIM), fixed),
                  pl.BlockSpec((1, LANE), fixed),
                  pl.BlockSpec((1, LANE), fixed),
                  pl.BlockSpec((1, D_SSM), fixed),
                  pl.BlockSpec((1, D_SSM), fixed)],
        out_specs=[pl.BlockSpec((None, SSD_CHUNK, D_SSM), chunk),
                   pl.BlockSpec((None, N_SSM_HEADS, SSM_HEAD_DIM, D_STATE), lambda s, c: (s, 0, 0, 0))],
        out_shape=[jax.ShapeDtypeStruct((ns, lp, D_SSM), BF16),
                   jax.ShapeDtypeStruct((ns, N_SSM_HEADS, SSM_HEAD_DIM, D_STATE), F32)],
        scratch_shapes=[pltpu.VMEM((SUBLANE + SSD_CHUNK, CONV_DIM), F32),
                        pltpu.VMEM((N_PAIRS, D_STATE, LANE), F32)],
        compiler_params=_params("parallel", "arbitrary"),
        name="ssd",
    )(xbc, z, dt, prefix, state0, cw, cb, dtb, alog, dskip, gssm)


def _attn_prompt_kernel(q_ref, kp_ref, kc_ref, vp_ref, vc_ref, bias_ref, o_ref, lse_ref, *, dil):
    first = pl.program_id(1) == 0
    r = pl.program_id(2)
    lt64 = _lane_lt64((BLK, LANE))
    lane = lax.broadcasted_iota(jnp.int32, (BLK, LANE), 1)
    lse_slab = jnp.zeros((BLK, LANE), F32)
    zero = jnp.zeros((BLK, LANE), BF16)
    rows = pl.ds(r, BLK, stride=dil) if dil > 1 else slice(None)
    for jp in range(N_PAIRS):
        sl = slice(jp * LANE, (jp + 1) * LANE)
        qp = q_ref[:, sl]
        kp, kc, vp, vc = kp_ref[:, sl], kc_ref[:, sl], vp_ref[:, sl], vc_ref[:, sl]
        outs = []
        for half in range(2):
            h = 2 * jp + half
            qm = jnp.where(lt64 if half == 0 else ~lt64, qp, zero)
            s_p = _nt_dot(qm, kp) + jnp.where(first, NEG, bias_ref[h, :, 0:BLK])
            s_c = _nt_dot(qm, kc) + bias_ref[h, :, BLK:2 * BLK]
            m = jnp.maximum(jnp.max(s_p, axis=-1, keepdims=True), jnp.max(s_c, axis=-1, keepdims=True))
            p_p = jnp.exp(s_p - m)
            p_c = jnp.exp(s_c - m)
            den = jnp.sum(p_p, axis=-1, keepdims=True) + jnp.sum(p_c, axis=-1, keepdims=True)
            pv = _dot(p_p.astype(BF16), vp) + _dot(p_c.astype(BF16), vc)
            outs.append(pv / den)
            lse_slab = jnp.where(lane == h, m + jnp.log(den), lse_slab)
        o_ref[jp, rows, :] = jnp.where(lt64, outs[0], outs[1])
    lse_ref[rows, :] = lse_slab


def _attn_prompt_branch(q, k, v, bias, dil):
    bsz, _, l, _ = q.shape
    s = l * dil
    nb = l // BLK
    cur = lambda b, n, r: (b, r, n, 0)
    prev = lambda b, n, r: (b, r, jnp.maximum(n - 1, 0), 0)
    blk = pl.BlockSpec((None, None, BLK, D_ATTN), cur)
    blk_prev = pl.BlockSpec((None, None, BLK, D_ATTN), prev)
    return pl.pallas_call(
        functools.partial(_attn_prompt_kernel, dil=dil),
        grid=(bsz, nb, dil),
        in_specs=[blk, blk_prev, blk, blk_prev, blk,
                  pl.BlockSpec((N_HEADS, BLK, 2 * BLK), lambda b, n, r: (0, 0, 0))],
        out_specs=[pl.BlockSpec((None, N_PAIRS, BLK * dil, LANE), lambda b, n, r: (b, 0, n, 0)),
                   pl.BlockSpec((None, BLK * dil, LANE), lambda b, n, r: (b, n, 0))],
        out_shape=[jax.ShapeDtypeStruct((bsz, N_PAIRS, s, LANE), F32),
                   jax.ShapeDtypeStruct((bsz, s, LANE), F32)],
        compiler_params=_params("parallel", "parallel", "arbitrary"),
        name=f"attn_prompt_d{dil}",
    )(q, k, k, v, v, bias)


def _attn_sample_kernel(q_ref, kn_ref, vn_ref, kc_ref, vc_ref, b16_ref, b4_ref, b1_ref, bn_ref, o_ref):
    t_len = q_ref.shape[0]
    w = kc_ref.shape[-1]
    lt64 = _lane_lt64((t_len, LANE))
    pad = jnp.zeros((BLK - t_len, D_ATTN), F32)
    kn = jnp.concatenate([kn_ref[...], pad], axis=0).astype(BF16)
    vn = jnp.concatenate([vn_ref[...], pad], axis=0).astype(BF16)
    outs = []
    for jp in range(N_PAIRS):
        sl = slice(jp * LANE, (jp + 1) * LANE)
        qp = q_ref[:, sl]
        qm = jnp.concatenate([jnp.where(lt64, qp, 0.0), jnp.where(lt64, 0.0, qp)], axis=0).astype(BF16)
        kt = kc_ref[2 * jp:2 * jp + 2].reshape(2 * HEAD_DIM, w).astype(BF16)
        vt = vc_ref[2 * jp:2 * jp + 2].reshape(2 * HEAD_DIM, w).astype(BF16)
        s = _dot(qm, kt)
        s_new = _nt_dot(qm, kn[:, sl])
        w4, w1 = w - 4 * BLK, w - BLK
        tiles = [s + b16_ref[jp], s[:, w4:] + b4_ref[jp], s[:, w1:] + b1_ref[jp]]
        tiles += [s_new + bn_ref[br, jp] for br in range(len(DILATIONS))]
        m = functools.reduce(jnp.maximum, [jnp.max(t, axis=-1, keepdims=True) for t in tiles])
        ps = [jnp.exp(t - m) for t in tiles]
        den = functools.reduce(lambda a, b: a + b, [jnp.sum(p, axis=-1, keepdims=True) for p in ps])
        p16, p4, p1 = ps[0], ps[1], ps[2]
        p_cache = jnp.concatenate([p16[:, :w4], p16[:, w4:w1] + p4[:, :w1 - w4],
                                   p16[:, w1:] + p4[:, w1 - w4:] + p1], axis=1)
        p_new = ps[3] + ps[4] + ps[5]
        o2 = (_nt_dot(p_cache.astype(BF16), vt) + _dot(p_new.astype(BF16), vn[:, sl])) / den
        outs.append(jnp.where(lt64, o2[0:t_len], o2[t_len:2 * t_len]))
    o_ref[...] = jnp.concatenate(outs, axis=1).astype(o_ref.dtype)


def _attn_sample(q, kn, vn, kc, vc, b16, b4, b1, bn):
    ns, t_len, _ = q.shape
    w = kc.shape[-1]
    assert w == MAX_DISTANCE and t_len == SUBLANE
    tok = pl.BlockSpec((None, t_len, D_ATTN), lambda s: (s, 0, 0))
    cache = pl.BlockSpec((None, N_HEADS, HEAD_DIM, w), lambda s: (s, 0, 0, 0))
    full = lambda a: pl.BlockSpec(a.shape, lambda s: (0,) * a.ndim)
    return pl.pallas_call(
        _attn_sample_kernel,
        grid=(ns,),
        in_specs=[tok, tok, tok, cache, cache, full(b16), full(b4), full(b1), full(bn)],
        out_specs=tok,
        out_shape=jax.ShapeDtypeStruct((ns, t_len, D_ATTN), BF16),
        compiler_params=_params("parallel"),
        name="attn_sample",
    )(q, kn, vn, kc, vc, b16, b4, b1, bn)


def _mix_ffn_kernel(*refs, n_parts, seg_in_tile, n_tiles):
    it = iter(refs)
    x_ref = next(it)
    parts = [(next(it), next(it)) for _ in range(n_parts)] if n_parts > 1 else [(next(it), None)]
    yssm_ref, p_ref, fpre_ref = next(it), next(it), next(it)
    (expand_ref, wout_ref, gffn_ref, wup_ref, fcw_ref, fcb_ref, wdown_ref, wple_ref, gple_ref,
     wgate_ref, gfin_ref) = (next(it) for _ in range(11))
    y_ref, uout_ref = next(it), next(it)
    act_ref, uext_ref, tail_ref = next(it), next(it), next(it)
    tm = x_ref.shape[0]

    if n_parts > 1:
        lses = [lse_ref[...] for _, lse_ref in parts]
        mx = functools.reduce(jnp.maximum, lses)
        ws = [jnp.exp(l - mx) for l in lses]
        den = functools.reduce(lambda a, b: a + b, ws)
        attn = None
        for (o_ref, _), w in zip(parts, ws):
            wn = w / den
            hi = wn.astype(BF16)
            lo = (wn - hi.astype(F32)).astype(BF16)
            wexp = _dot(hi, expand_ref[...]) + _dot(lo, expand_ref[...])
            o = jnp.concatenate([o_ref[jp] for jp in range(N_PAIRS)], axis=1)
            attn = wexp * o if attn is None else attn + wexp * o
        attn = attn.astype(BF16)
    else:
        attn = parts[0][0][...]
    h1 = x_ref[...] + _dot(attn, wout_ref[0:D_ATTN, :]) + _dot(yssm_ref[...], wout_ref[D_ATTN:, :])

    hn = _rms(h1, gffn_ref[...]).astype(BF16)

    if not seg_in_tile:
        t = pl.program_id(1)

        @pl.when(t == 0)
        def _init():
            tail_ref[...] = jnp.zeros(tail_ref.shape, F32)
            tail_ref[SUBLANE - (FFN_CONV - 1):SUBLANE, :] = fpre_ref[...]
    else:
        rin = lax.broadcasted_iota(jnp.int32, (tm, FF_CHUNK), 0) & (SUBLANE - 1)

    def conv(u, c0):
        cs = slice(c0, c0 + FF_CHUNK)
        if seg_in_tile:
            pre = fpre_ref[:, cs]
            um1 = jnp.where(rin == 0, pltpu.roll(pre, tm - 1, 0), pltpu.roll(u, 1, 0))
            um2 = jnp.where(rin < 2, pre, pltpu.roll(u, 2, 0))
            uout_ref[:, cs] = u
        else:
            uext_ref[0:SUBLANE, :] = tail_ref[:, cs]
            uext_ref[SUBLANE:SUBLANE + tm, :] = u
            tail_ref[:, cs] = u[tm - SUBLANE:tm, :]
            um2 = uext_ref[SUBLANE - 2:SUBLANE - 2 + tm, :]
            um1 = uext_ref[SUBLANE - 1:SUBLANE - 1 + tm, :]
        return fcb_ref[:, cs] + fcw_ref[0:1, cs] * um2 + fcw_ref[1:2, cs] * um1 + fcw_ref[2:3, cs] * u

    for j in range(D_FF // FF_CHUNK):
        c0 = j * FF_CHUNK
        u_gate = conv(_dot(hn, wup_ref[:, c0:c0 + FF_CHUNK]), c0)
        u_lin = conv(_dot(hn, wup_ref[:, D_FF + c0:D_FF + c0 + FF_CHUNK]), D_FF + c0)
        act_ref[:, c0:c0 + FF_CHUNK] = (_silu(u_gate) * u_lin).astype(BF16)
    h2 = h1 + _dot(act_ref[...], wdown_ref[...])

    e = _rms(_dot(p_ref[...].astype(BF16), wple_ref[...]), gple_ref[...])
    h3 = h2 + jax.nn.sigmoid(_dot(h2.astype(BF16), wgate_ref[...])) * e
    y_ref[...] = _rms(h3, gfin_ref[...])

    if not seg_in_tile:
        @pl.when(pl.program_id(1) == n_tiles - 1)
        def _fin():
            uout_ref[...] = tail_ref[...]


def _mix_ffn(x, parts, yssm, p, fpre, weights, *, tm, seg_in_tile):
    ns, l, _ = x.shape
    n_tiles = l // tm
    n_parts = len(parts)
    tile = lambda s, t: (s, t, 0)
    seq = lambda s, t: (s, 0, 0)
    fixed = lambda s, t: (0, 0)
    in_specs = [pl.BlockSpec((None, tm, D_MODEL), tile)]
    args = [x]
    for o, lse in parts:
        if n_parts > 1:
            in_specs += [pl.BlockSpec((None, N_PAIRS, tm, LANE), lambda s, t: (s, 0, t, 0)),
                         pl.BlockSpec((None, tm, LANE), tile)]
            args += [o, lse]
        else:
            in_specs.append(pl.BlockSpec((None, tm, D_ATTN), tile))
            args.append(o)
    in_specs += [pl.BlockSpec((None, tm, D_SSM), tile), pl.BlockSpec((None, tm, D_PLE), tile)]
    args += [yssm, p]
    if seg_in_tile:
        in_specs.append(pl.BlockSpec((None, tm, 2 * D_FF), tile))
        uout_spec = pl.BlockSpec((None, tm, 2 * D_FF), tile)
        uout_shape = jax.ShapeDtypeStruct((ns, l, 2 * D_FF), F32)
    else:
        in_specs.append(pl.BlockSpec((None, FFN_CONV - 1, 2 * D_FF), seq))
        uout_spec = pl.BlockSpec((None, SUBLANE, 2 * D_FF), seq)
        uout_shape = jax.ShapeDtypeStruct((ns, SUBLANE, 2 * D_FF), F32)
    args.append(fpre)
    for w in weights:
        in_specs.append(pl.BlockSpec(w.shape, fixed))
        args.append(w)
    kern = functools.partial(_mix_ffn_kernel, n_parts=n_parts, seg_in_tile=seg_in_tile, n_tiles=n_tiles)
    return pl.pallas_call(
        kern,
        grid=(ns, n_tiles),
        in_specs=in_specs,
        out_specs=[pl.BlockSpec((None, tm, D_MODEL), tile), uout_spec],
        out_shape=[jax.ShapeDtypeStruct((ns, l, D_MODEL), F32), uout_shape],
        scratch_shapes=[pltpu.VMEM((tm, D_FF), BF16),
                        pltpu.VMEM((SUBLANE + tm, FF_CHUNK), F32),
                        pltpu.VMEM((SUBLANE, 2 * D_FF), F32)],
        compiler_params=_params("parallel", "arbitrary"),
        name="mix_ffn_seg" if seg_in_tile else "mix_ffn",
    )(*args)


def _pad_lanes(v, width=LANE):
    return jnp.pad(v.astype(F32), (0, width - v.shape[0]))[None, :]


def kernel(x_prompt, x_sample, p_prompt, p_sample, cache_k, cache_v, state_ssm, state_conv, state_ffn_conv,
           rel_bias, g_mix, w_in, conv_w, conv_b, dt_bias, a_log, d_skip, g_ssm, w_out, g_ffn, w_up,
           ffn_conv_w, ffn_conv_b, w_down, w_ple_proj, g_ple, w_ple_gate, g_final):
    assert w_in.shape[0] == 1, "one layer"
    bp, s, _ = x_prompt.shape
    nsamp, t_len, _ = x_sample.shape
    n_keep = min(MAX_DISTANCE, s)

    w_main = jnp.pad(w_in[0], ((0, 0), (0, IN_MAIN - w_in.shape[2]))).astype(BF16)
    gmix = g_mix[0][None, :]
    ssd_params = (conv_w[0], conv_b[0][None, :], _pad_lanes(dt_bias[0]), _pad_lanes(a_log[0]),
                  jnp.repeat(d_skip[0], SSM_HEAD_DIM)[None, :], g_ssm[0][None, :])
    expand = (np.arange(LANE)[:, None] == (np.arange(D_ATTN)[None, :] // HEAD_DIM)).astype(np.float32)
    ffn_weights = (jnp.asarray(expand, BF16), w_out[0].astype(BF16), g_ffn[0][None, :], w_up[0].astype(BF16),
                   ffn_conv_w[0], ffn_conv_b[0][None, :], w_down[0].astype(BF16), w_ple_proj[0].astype(BF16),
                   g_ple[0][None, :], w_ple_gate[0].astype(BF16), g_final[None, :])
    bias_p, b16, b4, b1, bn = _bias_tables(rel_bias, t_len)

    (q1, k1, v1, q4, k4, v4, q16, k16, v16, kt, vt, z, xbc, dt) = _inproj(x_prompt, gmix, w_main, tm=512, n_keep=n_keep)
    yssm_p, ssm_p = _ssd(xbc, z, dt,
                         jnp.zeros((bp, SSM_CONV - 1, CONV_DIM), F32),
                         jnp.zeros((bp, N_SSM_HEADS, SSM_HEAD_DIM, D_STATE), F32),
                         *ssd_params, valid_len=SSD_CHUNK)
    parts = [_attn_prompt_branch(q1[:, None], k1[:, None], v1[:, None], bias_p[0], 1),
             _attn_prompt_branch(q4, k4, v4, bias_p[1], 4),
             _attn_prompt_branch(q16, k16, v16, bias_p[2], 16)]
    y_prompt, tail_p = _mix_ffn(x_prompt, parts, yssm_p, p_prompt[0],
                                jnp.zeros((bp, FFN_CONV - 1, 2 * D_FF), F32), ffn_weights,
                                tm=512, seg_in_tile=False)
    k_prompt = jnp.transpose(kt, (0, 3, 1, 2))[None]
    v_prompt = jnp.transpose(vt, (0, 3, 1, 2))[None]
    conv_prompt = xbc[:, s - (SSM_CONV - 1):][None]
    ffn_conv_prompt = tail_p[:, SUBLANE - (FFN_CONV - 1):][None]

    rows = nsamp * t_len
    qs, ks, vs, zs, xbcs, dts = _inproj(x_sample.reshape(1, rows, D_MODEL), gmix, w_main, tm=rows, n_keep=0)
    s3 = lambda a: a.reshape(nsamp, t_len, a.shape[-1])
    pad_t = lambda a: jnp.pad(s3(a), ((0, 0), (0, SSD_CHUNK - t_len), (0, 0)))
    yssm_s, ssm_s = _ssd(pad_t(xbcs), pad_t(zs), pad_t(dts), state_conv[0], state_ssm[0],
                         *ssd_params, valid_len=t_len)
    attn_s = _attn_sample(s3(qs), s3(ks), s3(vs),
                          jnp.transpose(cache_k[0], (0, 2, 3, 1)), jnp.transpose(cache_v[0], (0, 2, 3, 1)),
                          b16, b4, b1, bn)
    fpre = jnp.pad(state_ffn_conv[0], ((0, 0), (0, t_len - (FFN_CONV - 1)), (0, 0))).reshape(1, rows, 2 * D_FF)
    y_s, u_s = _mix_ffn(x_sample.reshape(1, rows, D_MODEL), [(attn_s.reshape(1, rows, D_ATTN), None)],
                        yssm_s[:, :t_len].reshape(1, rows, D_SSM), p_sample[0].reshape(1, rows, D_PLE),
                        fpre, ffn_weights, tm=rows, seg_in_tile=True)
    y_sample = y_s.reshape(nsamp, t_len, D_MODEL)
    k_sample = ks.reshape(1, nsamp, t_len, N_HEADS, HEAD_DIM)
    v_sample = vs.reshape(1, nsamp, t_len, N_HEADS, HEAD_DIM)
    conv_sample = s3(xbcs)[:, t_len - (SSM_CONV - 1):][None]
    ffn_conv_sample = u_s.reshape(nsamp, t_len, 2 * D_FF)[:, t_len - (FFN_CONV - 1):][None]

    return (y_prompt, y_sample, k_prompt, v_prompt, k_sample, v_sample,
            ssm_p[None], ssm_s[None], conv_prompt, conv_sample, ffn_conv_prompt, ffn_conv_sample)
```

```python
import functools
import math

import numpy as np
import jax
import jax.numpy as jnp
from jax import lax
from jax.experimental import pallas as pl
from jax.experimental.pallas import tpu as pltpu

F32 = jnp.float32
BF16 = jnp.bfloat16

D_MODEL = 1024
HEAD_DIM = 64
N_HEADS = 8
D_ATTN = N_HEADS * HEAD_DIM
N_PAIRS = N_HEADS // 2
DILATIONS = (1, 4, 16)
N_STEPS = 128
BLK = 128
N_BUCKETS = 32
MAX_DISTANCE = 2048
D_SSM = 512
N_SSM_HEADS = 8
SSM_HEAD_DIM = 64
D_STATE = 128
N_SSM_GROUPS = 2
SSM_CONV = 4
CONV_DIM = D_SSM + 2 * N_SSM_GROUPS * D_STATE
SSD_CHUNK = 128
D_FF = 2816
FFN_CONV = 3
D_PLE = 256
EPS = 1e-6
NEG = -1e30

LANE = 128
SUBLANE = 8
FF_CHUNK = 256
VMEM_LIMIT = 56 * 1024 * 1024

O_Q, O_K, O_V, O_Z, O_XBC, O_DT = 0, 512, 1024, 1536, 2048, 3072
IN_MAIN = O_DT + LANE


def _rel_bucket_np(dist):
    dist = np.asarray(dist, np.int32)
    max_exact = N_BUCKETS // 2
    d = np.maximum(dist, 1).astype(np.float32)
    large = max_exact + (np.log(d / np.float32(max_exact)) / np.float32(math.log(MAX_DISTANCE / max_exact))
                         * np.float32(N_BUCKETS - max_exact)).astype(np.int32)
    large = np.minimum(large, N_BUCKETS - 1)
    return np.where(dist < max_exact, dist, large)


def _nt_dot(a, b):
    return lax.dot_general(a, b, (((1,), (1,)), ((), ())), preferred_element_type=F32)


def _dot(a, b):
    return jnp.dot(a, b, preferred_element_type=F32)


def _silu(x):
    return x * jax.nn.sigmoid(x)


def _softplus(x):
    return jnp.maximum(x, 0.0) + jnp.log1p(jnp.exp(-jnp.abs(x)))


def _rms(x, g):
    return x * lax.rsqrt(jnp.mean(x * x, axis=-1, keepdims=True) + EPS) * g


def _lane_lt64(shape):
    return lax.broadcasted_iota(jnp.int32, shape, len(shape) - 1) < HEAD_DIM


def _params(*sem):
    return pltpu.CompilerParams(dimension_semantics=sem, vmem_limit_bytes=VMEM_LIMIT)


def _bias_kernel(rb_ref, rbt_ref, ig_ref, i16_ref, i4_ref, i1_ref, in_ref, tp_ref, t16_ref, t4_ref, t1_ref, tn_ref):
    def lookup(idx, h):
        def body(b, acc):
            return jnp.where(idx == b, rb_ref[b, h], acc)
        return lax.fori_loop(0, N_BUCKETS, body, jnp.full(idx.shape, NEG, F32))

    for br in range(len(DILATIONS)):
        idx = jnp.broadcast_to(ig_ref[br], (N_HEADS, 2 * BLK))
        gen = jnp.full((N_HEADS, 2 * BLK), NEG, F32)
        for b in range(N_BUCKETS):
            gen = jnp.where(idx == b, jnp.broadcast_to(rbt_ref[:, b:b + 1], (N_HEADS, 2 * BLK)), gen)
        for h in range(N_HEADS):
            rows = jnp.broadcast_to(gen[h:h + 1, :], (BLK, 2 * BLK))
            tp_ref[br, h] = pltpu.roll(rows, 0, 1, stride=1, stride_axis=0)
    for h in range(N_HEADS):
        jp, half = divmod(h, 2)
        rs = slice(half * SUBLANE, (half + 1) * SUBLANE)
        t16_ref[jp, rs, :] = lookup(i16_ref[...], h)
        t4_ref[jp, rs, :] = lookup(i4_ref[...], h)
        t1_ref[jp, rs, :] = lookup(i1_ref[...], h)
        for br in range(len(DILATIONS)):
            tn_ref[br, jp, rs, :] = lookup(in_ref[br], h)


def _bucket_maps(t_len):
    j = BLK - np.arange(2 * BLK)[None, :]
    prompt = np.stack([np.where(j >= 0, _rel_bucket_np(np.clip(j, 0, N_STEPS) * d), -1) for d in DILATIONS])

    t = np.arange(t_len)[:, None]

    def sample_map(diff, dil):
        ok = (diff >= 0) & (diff % dil == 0) & (diff // dil <= N_STEPS)
        return np.where(ok, _rel_bucket_np(np.maximum(diff, 0)), -1).astype(np.int32)

    w = np.arange(MAX_DISTANCE)[None, :]
    cache = {d: sample_map(MAX_DISTANCE + t - w, d) for d in DILATIONS}
    g = np.arange(LANE)[None, :]
    new = np.stack([np.where(g < t_len, sample_map(t - g, d), -1) for d in DILATIONS])
    return (prompt.astype(np.int32), cache[16], cache[4][:, MAX_DISTANCE - 4 * BLK:],
            cache[1][:, MAX_DISTANCE - BLK:], new.astype(np.int32))


def _bias_tables(rel_bias, t_len):
    maps = _bucket_maps(t_len)
    nb = len(DILATIONS)
    shapes = [(nb, N_HEADS, BLK, 2 * BLK), (N_PAIRS, 2 * t_len, MAX_DISTANCE), (N_PAIRS, 2 * t_len, 4 * BLK),
              (N_PAIRS, 2 * t_len, BLK), (nb, N_PAIRS, 2 * t_len, LANE)]
    return pl.pallas_call(
        _bias_kernel,
        in_specs=[pl.BlockSpec(memory_space=pltpu.SMEM)] + [pl.BlockSpec(memory_space=pltpu.VMEM)] * 6,
        out_specs=[pl.BlockSpec(memory_space=pltpu.VMEM)] * 5,
        out_shape=[jax.ShapeDtypeStruct(s, F32) for s in shapes],
        compiler_params=pltpu.CompilerParams(vmem_limit_bytes=VMEM_LIMIT),
        name="bias_tables",
    )(rel_bias, rel_bias.T, *[jnp.asarray(m) for m in maps])


def _inproj_kernel(x_ref, g_ref, w_ref, *refs, prompt, keep_from):
    tm = x_ref.shape[0]
    xn = _rms(x_ref[...], g_ref[...]).astype(BF16)

    def proj(lo, hi):
        return _dot(xn, w_ref[:, lo:hi])

    q = proj(O_Q, O_K) * (HEAD_DIM ** -0.5)
    k = proj(O_K, O_V)
    v = proj(O_V, O_Z)
    if prompt:
        (q1, k1, v1, q4, k4, v4, q16, k16, v16, kt_ref, vt_ref, z_ref, xbc_ref, dt_ref, perm_ref, mid_ref) = refs
        for i, (val, nat, r4, r16) in enumerate(((q, q1, q4, q16), (k, k1, k4, k16), (v, v1, v4, v16))):
            nat[...] = val.astype(BF16)
            for jp in range(N_PAIRS):
                perm_ref[i, jp] = val[:, jp * LANE:(jp + 1) * LANE]
            for ra in range(4):
                for jp in range(N_PAIRS):
                    mid_ref[i, ra, jp] = perm_ref[i, jp, pl.ds(ra, tm // 4, stride=4), :]
                r4[ra] = jnp.concatenate([mid_ref[i, ra, jp] for jp in range(N_PAIRS)], axis=1).astype(BF16)
                for rb in range(4):
                    rows = [mid_ref[i, ra, jp, pl.ds(rb, tm // 16, stride=4), :] for jp in range(N_PAIRS)]
                    r16[ra + 4 * rb] = jnp.concatenate(rows, axis=1).astype(BF16)

        @pl.when(pl.program_id(1) >= keep_from)
        def _keep():
            kt_ref[...] = k.T.reshape(N_HEADS, HEAD_DIM, tm)
            vt_ref[...] = v.T.reshape(N_HEADS, HEAD_DIM, tm)
    else:
        qf_ref, kf_ref, vf_ref, z_ref, xbc_ref, dt_ref = refs
        qf_ref[...] = q
        kf_ref[...] = k
        vf_ref[...] = v
    z_ref[...] = proj(O_Z, O_XBC)
    xbc_ref[...] = proj(O_XBC, O_DT)
    dt_ref[...] = proj(O_DT, IN_MAIN)


def _inproj(x, g_mix, w_main, *, tm, n_keep):
    ns, l, _ = x.shape
    nt = l // tm
    prompt = n_keep > 0
    tile = lambda s, t: (s, t, 0)
    fixed = lambda s, t: (0, 0)
    nat = lambda w, dt: (pl.BlockSpec((None, tm, w), tile), jax.ShapeDtypeStruct((ns, l, w), dt))
    outs = []
    if prompt:
        keep_from = (l - n_keep) // tm
        outs += [nat(D_ATTN, BF16)] * 3
        for dil in (4, 16):
            outs += [(pl.BlockSpec((None, dil, tm // dil, D_ATTN), lambda s, t: (s, 0, t, 0)),
                      jax.ShapeDtypeStruct((ns, dil, l // dil, D_ATTN), BF16))] * 3
        outs += [(pl.BlockSpec((None, N_HEADS, HEAD_DIM, tm), lambda s, t: (s, 0, 0, jnp.maximum(t - keep_from, 0))),
                  jax.ShapeDtypeStruct((ns, N_HEADS, HEAD_DIM, n_keep), F32))] * 2
        scratch = [pltpu.VMEM((3, N_PAIRS, tm, LANE), F32), pltpu.VMEM((3, 4, N_PAIRS, tm // 4, LANE), F32)]
    else:
        keep_from = 0
        outs += [nat(D_ATTN, F32)] * 3
        scratch = []
    outs += [nat(D_SSM, F32), nat(CONV_DIM, F32), nat(LANE, F32)]
    return pl.pallas_call(
        functools.partial(_inproj_kernel, prompt=prompt, keep_from=keep_from),
        grid=(ns, nt),
        in_specs=[pl.BlockSpec((None, tm, D_MODEL), tile),
                  pl.BlockSpec((1, D_MODEL), fixed),
                  pl.BlockSpec((D_MODEL, IN_MAIN), fixed)],
        out_specs=[o[0] for o in outs],
        out_shape=[o[1] for o in outs],
        scratch_shapes=scratch,
        compiler_params=_params("parallel", "arbitrary"),
        name="inproj" if prompt else "inproj_sample",
    )(x, g_mix, w_main)


def _ssd_kernel(xbc_ref, z_ref, dt_ref, prefix_ref, state0_ref, cw_ref, cb_ref, dtb_ref, alog_ref,
                dskip_ref, gssm_ref, y_ref, state_ref, xp_ref, st_ref, *, n_chunks, valid_len):
    L = SSD_CHUNK
    c = pl.program_id(1)

    @pl.when(c == 0)
    def _init():
        xp_ref[0:SUBLANE, :] = jnp.zeros((SUBLANE, CONV_DIM), F32)
        xp_ref[SUBLANE - (SSM_CONV - 1):SUBLANE, :] = prefix_ref[...]
        for jp in range(N_PAIRS):
            st_ref[jp] = state0_ref[2 * jp:2 * jp + 2].reshape(2 * SSM_HEAD_DIM, D_STATE).T

    xp_ref[SUBLANE:SUBLANE + L, :] = xbc_ref[...]
    conv = cb_ref[...]
    for k in range(SSM_CONV):
        lo = SUBLANE - (SSM_CONV - 1) + k
        conv = conv + cw_ref[k:k + 1, :] * xp_ref[lo:lo + L, :]
    xp_ref[0:SUBLANE, :] = xp_ref[L:L + SUBLANE, :]
    xc = _silu(conv)
    xs = xc[:, :D_SSM]
    gn = N_SSM_GROUPS * D_STATE

    row = lax.broadcasted_iota(jnp.int32, (L, L), 0)
    col = lax.broadcasted_iota(jnp.int32, (L, L), 1)
    tri = row >= col
    lt64 = col < SSM_HEAD_DIM

    dt = _softplus(dt_ref[...] + dtb_ref[...])
    if valid_len < L:
        dt = jnp.where(row < valid_len, dt, 0.0)
    a = dt * (-jnp.exp(alog_ref[...]))
    acum = a
    shift = 1
    while shift < L:
        acum = acum + jnp.where(row >= shift, pltpu.roll(acum, shift, 0), 0.0)
        shift *= 2
    acum_t = acum.T
    dt_t = dt.T
    e_slab = jnp.exp(acum)
    de_slab = jnp.exp(acum[L - 1:L, :] - acum) * dt

    def head_terms(h, cb):
        colb = jnp.broadcast_to(acum[:, h:h + 1], (L, L))
        rowb = jnp.broadcast_to(acum_t[h:h + 1, :], (L, L))
        decay = jnp.exp(jnp.where(tri, colb - rowb, NEG))
        m = (cb * decay * jnp.broadcast_to(dt_t[h:h + 1, :], (L, L))).astype(BF16)
        e = jnp.broadcast_to(e_slab[:, h:h + 1], (L, L))
        return m, e, jnp.broadcast_to(de_slab[:, h:h + 1], (L, L)), e[L - 1:L, :]

    pairs = []
    for g in range(N_SSM_GROUPS):
        bg = xc[:, D_SSM + g * D_STATE:D_SSM + (g + 1) * D_STATE]
        cg = xc[:, D_SSM + gn + g * D_STATE:D_SSM + gn + (g + 1) * D_STATE].astype(BF16)
        cb = _nt_dot(cg, bg.astype(BF16))
        bg_t = bg.T.astype(BF16)
        for i in range(N_PAIRS // N_SSM_GROUPS):
            jp = g * (N_PAIRS // N_SSM_GROUPS) + i
            xpair = xs[:, jp * LANE:(jp + 1) * LANE]
            xpair_b = xpair.astype(BF16)
            m_a, e_a, de_a, el_a = head_terms(2 * jp, cb)
            m_b, e_b, de_b, el_b = head_terms(2 * jp + 1, cb)
            st = st_ref[jp]
            y_diag = jnp.where(lt64, _dot(m_a, xpair_b), _dot(m_b, xpair_b))
            y_off = _dot(cg, st.astype(BF16)) * jnp.where(lt64, e_a, e_b)
            xd = (xpair * jnp.where(lt64, de_a, de_b)).astype(BF16)
            st_ref[jp] = st * jnp.where(lt64[0:1, :], el_a, el_b) + _dot(bg_t, xd)
            pairs.append(y_diag + y_off + dskip_ref[:, jp * LANE:(jp + 1) * LANE] * xpair)
    y = jnp.concatenate(pairs, axis=1)

    yf = y * _silu(z_ref[...])
    gw = D_SSM // N_SSM_GROUPS
    normed = []
    for g in range(N_SSM_GROUPS):
        seg = yf[:, g * gw:(g + 1) * gw]
        normed.append(seg * lax.rsqrt(jnp.mean(seg * seg, axis=-1, keepdims=True) + EPS))
    y_ref[...] = (jnp.concatenate(normed, axis=1) * gssm_ref[...]).astype(y_ref.dtype)

    @pl.when(c == n_chunks - 1)
    def _fin():
        for jp in range(N_PAIRS):
            state_ref[2 * jp:2 * jp + 2] = st_ref[jp].T.reshape(2, SSM_HEAD_DIM, D_STATE)


def _ssd(xbc, z, dt, prefix, state0, cw, cb, dtb, alog, dskip, gssm, *, valid_len):
    ns, lp, _ = xbc.shape
    n_chunks = lp // SSD_CHUNK
    chunk = lambda s, c: (s, c, 0)
    seq = lambda s, c: (s, 0, 0)
    fixed = lambda s, c: (0, 0)
    kern = functools.partial(_ssd_kernel, n_chunks=n_chunks, valid_len=valid_len)
    return pl.pallas_call(
        kern,
        grid=(ns, n_chunks),
        in_specs=[pl.BlockSpec((None, SSD_CHUNK, CONV_DIM), chunk),
                  pl.BlockSpec((None, SSD_CHUNK, D_SSM), chunk),
                  pl.BlockSpec((None, SSD_CHUNK, LANE), chunk),
                  pl.BlockSpec((None, SSM_CONV - 1, CONV_DIM), seq),
                  pl.BlockSpec((None, N_SSM_HEADS, SSM_HEAD_DIM, D_STATE), lambda s, c: (s, 0, 0, 0)),
                  pl.BlockSpec((SSM_CONV, CONV_DIM), fixed),
                  pl.BlockSpec((1, CONV_DIM), fixed),
                  pl.BlockSpec((1, LANE), fixed),
                  pl.BlockSpec((1, LANE), fixed),
                  pl.BlockSpec((1, D_SSM), fixed),
                  pl.BlockSpec((1, D_SSM), fixed)],
        out_specs=[pl.BlockSpec((None, SSD_CHUNK, D_SSM), chunk),
                   pl.BlockSpec((None, N_SSM_HEADS, SSM_HEAD_DIM, D_STATE), lambda s, c: (s, 0, 0, 0))],
        out_shape=[jax.ShapeDtypeStruct((ns, lp, D_SSM), BF16),
                   jax.ShapeDtypeStruct((ns, N_SSM_HEADS, SSM_HEAD_DIM, D_STATE), F32)],
        scratch_shapes=[pltpu.VMEM((SUBLANE + SSD_CHUNK, CONV_DIM), F32),
                        pltpu.VMEM((N_PAIRS, D_STATE, LANE), F32)],
        compiler_params=_params("parallel", "arbitrary"),
        name="ssd",
    )(xbc, z, dt, prefix, state0, cw, cb, dtb, alog, dskip, gssm)


def _attn_prompt_kernel(q_ref, kp_ref, kc_ref, vp_ref, vc_ref, bias_ref, o_ref, lse_ref, s_ref, p_ref, *, dil):
    first = pl.program_id(1) == 0
    r = pl.program_id(2)
    lt64 = _lane_lt64((BLK, LANE))
    lane = lax.broadcasted_iota(jnp.int32, (BLK, LANE), 1)
    zero = jnp.zeros((BLK, LANE), BF16)
    rows = pl.ds(r, BLK, stride=dil) if dil > 1 else slice(None)
    for jp in range(N_PAIRS):
        sl = slice(jp * LANE, (jp + 1) * LANE)
        qp = q_ref[:, sl]
        for half in range(2):
            h = 2 * jp + half
            qm = jnp.where(lt64 if half == 0 else ~lt64, qp, zero)
            s_ref[h, :, 0:BLK] = _nt_dot(qm, kp_ref[:, sl]) + jnp.where(first, NEG, bias_ref[h, :, 0:BLK])
            s_ref[h, :, BLK:2 * BLK] = _nt_dot(qm, kc_ref[:, sl]) + bias_ref[h, :, BLK:2 * BLK]
    s = s_ref[...]
    m = jnp.max(s, axis=-1, keepdims=True)
    p = jnp.exp(s - m)
    den = jnp.sum(p, axis=-1, keepdims=True)
    p_ref[...] = p.astype(BF16)
    lse = m + jnp.log(den)
    lse_slab = jnp.zeros((BLK, LANE), F32)
    for h in range(N_HEADS):
        lse_slab = jnp.where(lane == h, lse[h], lse_slab)
    lse_ref[rows, :] = lse_slab
    for jp in range(N_PAIRS):
        sl = slice(jp * LANE, (jp + 1) * LANE)
        vp, vc = vp_ref[:, sl], vc_ref[:, sl]
        outs = []
        for half in range(2):
            h = 2 * jp + half
            pv = _dot(p_ref[h, :, 0:BLK], vp) + _dot(p_ref[h, :, BLK:2 * BLK], vc)
            outs.append(pv / den[h])
        o_ref[jp, rows, :] = jnp.where(lt64, outs[0], outs[1])


def _attn_prompt_branch(q, k, v, bias, dil):
    bsz, _, l, _ = q.shape
    s = l * dil
    nb = l // BLK
    cur = lambda b, n, r: (b, r, n, 0)
    prev = lambda b, n, r: (b, r, jnp.maximum(n - 1, 0), 0)
    blk = pl.BlockSpec((None, None, BLK, D_ATTN), cur)
    blk_prev = pl.BlockSpec((None, None, BLK, D_ATTN), prev)
    return pl.pallas_call(
        functools.partial(_attn_prompt_kernel, dil=dil),
        grid=(bsz, nb, dil),
        in_specs=[blk, blk_prev, blk, blk_prev, blk,
                  pl.BlockSpec((N_HEADS, BLK, 2 * BLK), lambda b, n, r: (0, 0, 0))],
        out_specs=[pl.BlockSpec((None, N_PAIRS, BLK * dil, LANE), lambda b, n, r: (b, 0, n, 0)),
                   pl.BlockSpec((None, BLK * dil, LANE), lambda b, n, r: (b, n, 0))],
        out_shape=[jax.ShapeDtypeStruct((bsz, N_PAIRS, s, LANE), F32),
                   jax.ShapeDtypeStruct((bsz, s, LANE), F32)],
        scratch_shapes=[pltpu.VMEM((N_HEADS, BLK, 2 * BLK), F32), pltpu.VMEM((N_HEADS, BLK, 2 * BLK), BF16)],
        compiler_params=_params("parallel", "parallel", "arbitrary"),
        name=f"attn_prompt_d{dil}",
    )(q, k, k, v, v, bias)


def _attn_sample_kernel(q_ref, kn_ref, vn_ref, kc_ref, vc_ref, b16_ref, b4_ref, b1_ref, bn_ref, o_ref):
    t_len = q_ref.shape[0]
    w = kc_ref.shape[-1]
    lt64 = _lane_lt64((t_len, LANE))
    pad = jnp.zeros((BLK - t_len, D_ATTN), F32)
    kn = jnp.concatenate([kn_ref[...], pad], axis=0).astype(BF16)
    vn = jnp.concatenate([vn_ref[...], pad], axis=0).astype(BF16)
    outs = []
    for jp in range(N_PAIRS):
        sl = slice(jp * LANE, (jp + 1) * LANE)
        qp = q_ref[:, sl]
        qm = jnp.concatenate([jnp.where(lt64, qp, 0.0), jnp.where(lt64, 0.0, qp)], axis=0).astype(BF16)
        kt = kc_ref[2 * jp:2 * jp + 2].reshape(2 * HEAD_DIM, w).astype(BF16)
        vt = vc_ref[2 * jp:2 * jp + 2].reshape(2 * HEAD_DIM, w).astype(BF16)
        s = _dot(qm, kt)
        s_new = _nt_dot(qm, kn[:, sl])
        w4, w1 = w - 4 * BLK, w - BLK
        tiles = [s + b16_ref[jp], s[:, w4:] + b4_ref[jp], s[:, w1:] + b1_ref[jp]]
        tiles += [s_new + bn_ref[br, jp] for br in range(len(DILATIONS))]
        m = functools.reduce(jnp.maximum, [jnp.max(t, axis=-1, keepdims=True) for t in tiles])
        ps = [jnp.exp(t - m) for t in tiles]
        den = functools.reduce(lambda a, b: a + b, [jnp.sum(p, axis=-1, keepdims=True) for p in ps])
        p16, p4, p1 = ps[0], ps[1], ps[2]
        p_cache = jnp.concatenate([p16[:, :w4], p16[:, w4:w1] + p4[:, :w1 - w4],
                                   p16[:, w1:] + p4[:, w1 - w4:] + p1], axis=1)
        p_new = ps[3] + ps[4] + ps[5]
        o2 = (_nt_dot(p_cache.astype(BF16), vt) + _dot(p_new.astype(BF16), vn[:, sl])) / den
        outs.append(jnp.where(lt64, o2[0:t_len], o2[t_len:2 * t_len]))
    o_ref[...] = jnp.concatenate(outs, axis=1).astype(o_ref.dtype)


def _attn_sample(q, kn, vn, kc, vc, b16, b4, b1, bn):
    ns, t_len, _ = q.shape
    w = kc.shape[-1]
    assert w == MAX_DISTANCE and t_len == SUBLANE
    tok = pl.BlockSpec((None, t_len, D_ATTN), lambda s: (s, 0, 0))
    cache = pl.BlockSpec((None, N_HEADS, HEAD_DIM, w), lambda s: (s, 0, 0, 0))
    full = lambda a: pl.BlockSpec(a.shape, lambda s: (0,) * a.ndim)
    return pl.pallas_call(
        _attn_sample_kernel,
        grid=(ns,),
        in_specs=[tok, tok, tok, cache, cache, full(b16), full(b4), full(b1), full(bn)],
        out_specs=tok,
        out_shape=jax.ShapeDtypeStruct((ns, t_len, D_ATTN), BF16),
        compiler_params=_params("parallel"),
        name="attn_sample",
    )(q, kn, vn, kc, vc, b16, b4, b1, bn)


def _mix_ffn_kernel(*refs, n_parts, seg_in_tile, n_tiles):
    it = iter(refs)
    x_ref = next(it)
    parts = [(next(it), next(it)) for _ in range(n_parts)] if n_parts > 1 else [(next(it), None)]
    yssm_ref, p_ref, fpre_ref = next(it), next(it), next(it)
    (expand_ref, wout_ref, gffn_ref, wup_ref, fcw_ref, fcb_ref, wdown_ref, wple_ref, gple_ref,
     wgate_ref, gfin_ref) = (next(it) for _ in range(11))
    y_ref, uout_ref = next(it), next(it)
    act_ref, tail_ref = next(it), next(it)
    tm = x_ref.shape[0]

    if n_parts > 1:
        lses = [lse_ref[...] for _, lse_ref in parts]
        mx = functools.reduce(jnp.maximum, lses)
        ws = [jnp.exp(l - mx) for l in lses]
        den = functools.reduce(lambda a, b: a + b, ws)
        attn = None
        for (o_ref, _), w in zip(parts, ws):
            wn = w / den
            hi = wn.astype(BF16)
            lo = (wn - hi.astype(F32)).astype(BF16)
            wexp = _dot(hi, expand_ref[...]) + _dot(lo, expand_ref[...])
            o = jnp.concatenate([o_ref[jp] for jp in range(N_PAIRS)], axis=1)
            attn = wexp * o if attn is None else attn + wexp * o
        attn = attn.astype(BF16)
    else:
        attn = parts[0][0][...]
    h1 = x_ref[...] + _dot(attn, wout_ref[0:D_ATTN, :]) + _dot(yssm_ref[...], wout_ref[D_ATTN:, :])

    hn = _rms(h1, gffn_ref[...]).astype(BF16)

    if not seg_in_tile:
        t = pl.program_id(1)

        @pl.when(t == 0)
        def _init():
            tail_ref[...] = jnp.zeros(tail_ref.shape, F32)
            tail_ref[SUBLANE - (FFN_CONV - 1):SUBLANE, :] = fpre_ref[...]

        rin8 = lax.broadcasted_iota(jnp.int32, (SUBLANE, FF_CHUNK), 0)
    else:
        rin = lax.broadcasted_iota(jnp.int32, (tm, FF_CHUNK), 0) & (SUBLANE - 1)

    def conv(u, c0):
        cs = slice(c0, c0 + FF_CHUNK)
        if seg_in_tile:
            pre = fpre_ref[:, cs]
            um1 = jnp.where(rin == 0, pltpu.roll(pre, tm - 1, 0), pltpu.roll(u, 1, 0))
            um2 = jnp.where(rin < 2, pre, pltpu.roll(u, 2, 0))
            uout_ref[:, cs] = u
        else:
            tail = tail_ref[:, cs]
            tail_ref[:, cs] = u[tm - SUBLANE:tm, :]
            r1, r2 = pltpu.roll(u, 1, 0), pltpu.roll(u, 2, 0)
            um1 = jnp.concatenate([jnp.where(rin8 < 1, pltpu.roll(tail, 1, 0), r1[0:SUBLANE]), r1[SUBLANE:]], axis=0)
            um2 = jnp.concatenate([jnp.where(rin8 < 2, pltpu.roll(tail, 2, 0), r2[0:SUBLANE]), r2[SUBLANE:]], axis=0)
        return fcb_ref[:, cs] + fcw_ref[0:1, cs] * um2 + fcw_ref[1:2, cs] * um1 + fcw_ref[2:3, cs] * u

    for j in range(D_FF // FF_CHUNK):
        c0 = j * FF_CHUNK
        u_gate = conv(_dot(hn, wup_ref[:, c0:c0 + FF_CHUNK]), c0)
        u_lin = conv(_dot(hn, wup_ref[:, D_FF + c0:D_FF + c0 + FF_CHUNK]), D_FF + c0)
        act_ref[:, c0:c0 + FF_CHUNK] = (_silu(u_gate) * u_lin).astype(BF16)
    h2 = h1 + _dot(act_ref[...], wdown_ref[...])

    e = _rms(_dot(p_ref[...].astype(BF16), wple_ref[...]), gple_ref[...])
    h3 = h2 + jax.nn.sigmoid(_dot(h2.astype(BF16), wgate_ref[...])) * e
    y_ref[...] = _rms(h3, gfin_ref[...])

    if not seg_in_tile:
        @pl.when(pl.program_id(1) == n_tiles - 1)
        def _fin():
            uout_ref[...] = tail_ref[...]


def _mix_ffn(x, parts, yssm, p, fpre, weights, *, tm, seg_in_tile):
    ns, l, _ = x.shape
    n_tiles = l // tm
    n_parts = len(parts)
    tile = lambda s, t: (s, t, 0)
    seq = lambda s, t: (s, 0, 0)
    fixed = lambda s, t: (0, 0)
    in_specs = [pl.BlockSpec((None, tm, D_MODEL), tile)]
    args = [x]
    for o, lse in parts:
        if n_parts > 1:
            in_specs += [pl.BlockSpec((None, N_PAIRS, tm, LANE), lambda s, t: (s, 0, t, 0)),
                         pl.BlockSpec((None, tm, LANE), tile)]
            args += [o, lse]
        else:
            in_specs.append(pl.BlockSpec((None, tm, D_ATTN), tile))
            args.append(o)
    in_specs += [pl.BlockSpec((None, tm, D_SSM), tile), pl.BlockSpec((None, tm, D_PLE), tile)]
    args += [yssm, p]
    if seg_in_tile:
        in_specs.append(pl.BlockSpec((None, tm, 2 * D_FF), tile))
        uout_spec = pl.BlockSpec((None, tm, 2 * D_FF), tile)
        uout_shape = jax.ShapeDtypeStruct((ns, l, 2 * D_FF), F32)
    else:
        in_specs.append(pl.BlockSpec((None, FFN_CONV - 1, 2 * D_FF), seq))
        uout_spec = pl.BlockSpec((None, SUBLANE, 2 * D_FF), seq)
        uout_shape = jax.ShapeDtypeStruct((ns, SUBLANE, 2 * D_FF), F32)
    args.append(fpre)
    for w in weights:
        in_specs.append(pl.BlockSpec(w.shape, fixed))
        args.append(w)
    kern = functools.partial(_mix_ffn_kernel, n_parts=n_parts, seg_in_tile=seg_in_tile, n_tiles=n_tiles)
    return pl.pallas_call(
        kern,
        grid=(ns, n_tiles),
        in_specs=in_specs,
        out_specs=[pl.BlockSpec((None, tm, D_MODEL), tile), uout_spec],
        out_shape=[jax.ShapeDtypeStruct((ns, l, D_MODEL), F32), uout_shape],
        scratch_shapes=[pltpu.VMEM((tm, D_FF), BF16),
                        pltpu.VMEM((SUBLANE, 2 * D_FF), F32)],
        compiler_params=_params("parallel", "arbitrary"),
        name="mix_ffn_seg" if seg_in_tile else "mix_ffn",
    )(*args)


def _pad_lanes(v, width=LANE):
    return jnp.pad(v.astype(F32), (0, width - v.shape[0]))[None, :]


def kernel(x_prompt, x_sample, p_prompt, p_sample, cache_k, cache_v, state_ssm, state_conv, state_ffn_conv,
           rel_bias, g_mix, w_in, conv_w, conv_b, dt_bias, a_log, d_skip, g_ssm, w_out, g_ffn, w_up,
           ffn_conv_w, ffn_conv_b, w_down, w_ple_proj, g_ple, w_ple_gate, g_final):
    assert w_in.shape[0] == 1, "one layer"
    bp, s, _ = x_prompt.shape
    nsamp, t_len, _ = x_sample.shape
    n_keep = min(MAX_DISTANCE, s)

    w_main = jnp.pad(w_in[0], ((0, 0), (0, IN_MAIN - w_in.shape[2]))).astype(BF16)
    gmix = g_mix[0][None, :]
    ssd_params = (conv_w[0], conv_b[0][None, :], _pad_lanes(dt_bias[0]), _pad_lanes(a_log[0]),
                  jnp.repeat(d_skip[0], SSM_HEAD_DIM)[None, :], g_ssm[0][None, :])
    expand = (np.arange(LANE)[:, None] == (np.arange(D_ATTN)[None, :] // HEAD_DIM)).astype(np.float32)
    ffn_weights = (jnp.asarray(expand, BF16), w_out[0].astype(BF16), g_ffn[0][None, :], w_up[0].astype(BF16),
                   ffn_conv_w[0], ffn_conv_b[0][None, :], w_down[0].astype(BF16), w_ple_proj[0].astype(BF16),
                   g_ple[0][None, :], w_ple_gate[0].astype(BF16), g_final[None, :])
    bias_p, b16, b4, b1, bn = _bias_tables(rel_bias, t_len)

    (q1, k1, v1, q4, k4, v4, q16, k16, v16, kt, vt, z, xbc, dt) = _inproj(x_prompt, gmix, w_main, tm=512, n_keep=n_keep)
    yssm_p, ssm_p = _ssd(xbc, z, dt,
                         jnp.zeros((bp, SSM_CONV - 1, CONV_DIM), F32),
                         jnp.zeros((bp, N_SSM_HEADS, SSM_HEAD_DIM, D_STATE), F32),
                         *ssd_params, valid_len=SSD_CHUNK)
    parts = [_attn_prompt_branch(q1[:, None], k1[:, None], v1[:, None], bias_p[0], 1),
             _attn_prompt_branch(q4, k4, v4, bias_p[1], 4),
             _attn_prompt_branch(q16, k16, v16, bias_p[2], 16)]
    y_prompt, tail_p = _mix_ffn(x_prompt, parts, yssm_p, p_prompt[0],
                                jnp.zeros((bp, FFN_CONV - 1, 2 * D_FF), F32), ffn_weights,
                                tm=512, seg_in_tile=False)
    k_prompt = jnp.transpose(kt, (0, 3, 1, 2))[None]
    v_prompt = jnp.transpose(vt, (0, 3, 1, 2))[None]
    conv_prompt = xbc[:, s - (SSM_CONV - 1):][None]
    ffn_conv_prompt = tail_p[:, SUBLANE - (FFN_CONV - 1):][None]

    rows = nsamp * t_len
    qs, ks, vs, zs, xbcs, dts = _inproj(x_sample.reshape(1, rows, D_MODEL), gmix, w_main, tm=rows, n_keep=0)
    s3 = lambda a: a.reshape(nsamp, t_len, a.shape[-1])
    pad_t = lambda a: jnp.pad(s3(a), ((0, 0), (0, SSD_CHUNK - t_len), (0, 0)))
    yssm_s, ssm_s = _ssd(pad_t(xbcs), pad_t(zs), pad_t(dts), state_conv[0], state_ssm[0],
                         *ssd_params, valid_len=t_len)
    attn_s = _attn_sample(s3(qs), s3(ks), s3(vs),
                          jnp.transpose(cache_k[0], (0, 2, 3, 1)), jnp.transpose(cache_v[0], (0, 2, 3, 1)),
                          b16, b4, b1, bn)
    fpre = jnp.pad(state_ffn_conv[0], ((0, 0), (0, t_len - (FFN_CONV - 1)), (0, 0))).reshape(1, rows, 2 * D_FF)
    y_s, u_s = _mix_ffn(x_sample.reshape(1, rows, D_MODEL), [(attn_s.reshape(1, rows, D_ATTN), None)],
                        yssm_s[:, :t_len].reshape(1, rows, D_SSM), p_sample[0].reshape(1, rows, D_PLE),
                        fpre, ffn_weights, tm=rows, seg_in_tile=True)
    y_sample = y_s.reshape(nsamp, t_len, D_MODEL)
    k_sample = ks.reshape(1, nsamp, t_len, N_HEADS, HEAD_DIM)
    v_sample = vs.reshape(1, nsamp, t_len, N_HEADS, HEAD_DIM)
    conv_sample = s3(xbcs)[:, t_len - (SSM_CONV - 1):][None]
    ffn_conv_sample = u_s.reshape(nsamp, t_len, 2 * D_FF)[:, t_len - (FFN_CONV - 1):][None]

    return (y_prompt, y_sample, k_prompt, v_prompt, k_sample, v_sample,
            ssm_p[None], ssm_s[None], conv_prompt, conv_sample, ffn_conv_prompt, ffn_conv_sample)
```

```python
import functools
import math

import numpy as np
import jax
import jax.numpy as jnp
from jax import lax
from jax.experimental import pallas as pl
from jax.experimental.pallas import tpu as pltpu

F32 = jnp.float32
BF16 = jnp.bfloat16

D_MODEL = 1024
HEAD_DIM = 64
N_HEADS = 8
D_ATTN = N_HEADS * HEAD_DIM
N_PAIRS = N_HEADS // 2
DILATIONS = (1, 4, 16)
N_STEPS = 128
BLK = 128
N_BUCKETS = 32
MAX_DISTANCE = 2048
D_SSM = 512
N_SSM_HEADS = 8
SSM_HEAD_DIM = 64
D_STATE = 128
N_SSM_GROUPS = 2
SSM_CONV = 4
CONV_DIM = D_SSM + 2 * N_SSM_GROUPS * D_STATE
SSD_CHUNK = 128
D_FF = 2816
FFN_CONV = 3
D_PLE = 256
EPS = 1e-6
NEG = -1e30

LANE = 128
SUBLANE = 8
FF_CHUNK = 256
VMEM_LIMIT = 56 * 1024 * 1024

O_Q, O_K, O_V, O_Z, O_XBC, O_DT = 0, 512, 1024, 1536, 2048, 3072
IN_MAIN = O_DT + LANE


def _rel_bucket_np(dist):
    dist = np.asarray(dist, np.int32)
    max_exact = N_BUCKETS // 2
    d = np.maximum(dist, 1).astype(np.float32)
    large = max_exact + (np.log(d / np.float32(max_exact)) / np.float32(math.log(MAX_DISTANCE / max_exact))
                         * np.float32(N_BUCKETS - max_exact)).astype(np.int32)
    large = np.minimum(large, N_BUCKETS - 1)
    return np.where(dist < max_exact, dist, large)


def _nt_dot(a, b):
    return lax.dot_general(a, b, (((1,), (1,)), ((), ())), preferred_element_type=F32)


def _dot(a, b):
    return jnp.dot(a, b, preferred_element_type=F32)


def _silu(x):
    return x * jax.nn.sigmoid(x)


def _softplus(x):
    return jnp.maximum(x, 0.0) + jnp.log1p(jnp.exp(-jnp.abs(x)))


def _rms(x, g):
    return x * lax.rsqrt(jnp.mean(x * x, axis=-1, keepdims=True) + EPS) * g


def _lane_lt64(shape):
    return lax.broadcasted_iota(jnp.int32, shape, len(shape) - 1) < HEAD_DIM


def _params(*sem):
    return pltpu.CompilerParams(dimension_semantics=sem, vmem_limit_bytes=VMEM_LIMIT)


def _bias_kernel(rb_ref, rbt_ref, ig_ref, i16_ref, i4_ref, i1_ref, in_ref, tp_ref, t16_ref, t4_ref, t1_ref, tn_ref):
    def lookup(idx, h):
        def body(b, acc):
            return jnp.where(idx == b, rb_ref[b, h], acc)
        return lax.fori_loop(0, N_BUCKETS, body, jnp.full(idx.shape, NEG, F32))

    for br in range(len(DILATIONS)):
        idx = jnp.broadcast_to(ig_ref[br], (N_HEADS, 2 * BLK))
        gen = jnp.full((N_HEADS, 2 * BLK), NEG, F32)
        for b in range(N_BUCKETS):
            gen = jnp.where(idx == b, jnp.broadcast_to(rbt_ref[:, b:b + 1], (N_HEADS, 2 * BLK)), gen)
        for h in range(N_HEADS):
            rows = jnp.broadcast_to(gen[h:h + 1, :], (BLK, 2 * BLK))
            tp_ref[br, h] = pltpu.roll(rows, 0, 1, stride=1, stride_axis=0)
    for h in range(N_HEADS):
        jp, half = divmod(h, 2)
        rs = slice(half * SUBLANE, (half + 1) * SUBLANE)
        t16_ref[jp, rs, :] = lookup(i16_ref[...], h)
        t4_ref[jp, rs, :] = lookup(i4_ref[...], h)
        t1_ref[jp, rs, :] = lookup(i1_ref[...], h)
        for br in range(len(DILATIONS)):
            tn_ref[br, jp, rs, :] = lookup(in_ref[br], h)


def _bucket_maps(t_len):
    j = BLK - np.arange(2 * BLK)[None, :]
    prompt = np.stack([np.where(j >= 0, _rel_bucket_np(np.clip(j, 0, N_STEPS) * d), -1) for d in DILATIONS])

    t = np.arange(t_len)[:, None]

    def sample_map(diff, dil):
        ok = (diff >= 0) & (diff % dil == 0) & (diff // dil <= N_STEPS)
        return np.where(ok, _rel_bucket_np(np.maximum(diff, 0)), -1).astype(np.int32)

    w = np.arange(MAX_DISTANCE)[None, :]
    cache = {d: sample_map(MAX_DISTANCE + t - w, d) for d in DILATIONS}
    g = np.arange(LANE)[None, :]
    new = np.stack([np.where(g < t_len, sample_map(t - g, d), -1) for d in DILATIONS])
    return (prompt.astype(np.int32), cache[16], cache[4][:, MAX_DISTANCE - 4 * BLK:],
            cache[1][:, MAX_DISTANCE - BLK:], new.astype(np.int32))


def _bias_tables(rel_bias, t_len):
    maps = _bucket_maps(t_len)
    nb = len(DILATIONS)
    shapes = [(nb, N_HEADS, BLK, 2 * BLK), (N_PAIRS, 2 * t_len, MAX_DISTANCE), (N_PAIRS, 2 * t_len, 4 * BLK),
              (N_PAIRS, 2 * t_len, BLK), (nb, N_PAIRS, 2 * t_len, LANE)]
    return pl.pallas_call(
        _bias_kernel,
        in_specs=[pl.BlockSpec(memory_space=pltpu.SMEM)] + [pl.BlockSpec(memory_space=pltpu.VMEM)] * 6,
        out_specs=[pl.BlockSpec(memory_space=pltpu.VMEM)] * 5,
        out_shape=[jax.ShapeDtypeStruct(s, F32) for s in shapes],
        compiler_params=pltpu.CompilerParams(vmem_limit_bytes=VMEM_LIMIT),
        name="bias_tables",
    )(rel_bias, rel_bias.T, *[jnp.asarray(m) for m in maps])


def _inproj_kernel(x_ref, g_ref, w_ref, *refs, prompt, keep_from):
    tm = x_ref.shape[0]
    xn = _rms(x_ref[...], g_ref[...]).astype(BF16)

    def proj(lo, hi):
        return _dot(xn, w_ref[:, lo:hi])

    q = proj(O_Q, O_K) * (HEAD_DIM ** -0.5)
    k = proj(O_K, O_V)
    v = proj(O_V, O_Z)
    if prompt:
        (q1, k1, v1, q4, k4, v4, q16, k16, v16, kt_ref, vt_ref, z_ref, xbc_ref, dt_ref, perm_ref, mid_ref) = refs
        for i, (val, nat, r4, r16) in enumerate(((q, q1, q4, q16), (k, k1, k4, k16), (v, v1, v4, v16))):
            nat[...] = val.astype(BF16)
            for jp in range(N_PAIRS):
                perm_ref[i, jp] = val[:, jp * LANE:(jp + 1) * LANE]
            for ra in range(4):
                for jp in range(N_PAIRS):
                    mid_ref[i, ra, jp] = perm_ref[i, jp, pl.ds(ra, tm // 4, stride=4), :]
                r4[ra] = jnp.concatenate([mid_ref[i, ra, jp] for jp in range(N_PAIRS)], axis=1).astype(BF16)
                for rb in range(4):
                    rows = [mid_ref[i, ra, jp, pl.ds(rb, tm // 16, stride=4), :] for jp in range(N_PAIRS)]
                    r16[ra + 4 * rb] = jnp.concatenate(rows, axis=1).astype(BF16)
    else:
        qf_ref, kf_ref, vf_ref, z_ref, xbc_ref, dt_ref = refs
        qf_ref[...] = q
        kf_ref[...] = k
        vf_ref[...] = v
    z_ref[...] = proj(O_Z, O_XBC)
    xbc_ref[...] = proj(O_XBC, O_DT)
    dt_ref[...] = proj(O_DT, IN_MAIN)

    if prompt:
        @pl.when(pl.program_id(1) >= keep_from)
        def _keep():
            for i, out_ref in ((1, kt_ref), (2, vt_ref)):
                for jp in range(N_PAIRS):
                    out_ref[2 * jp:2 * jp + 2] = perm_ref[i, jp].T.reshape(2, HEAD_DIM, tm)


def _inproj(x, g_mix, w_main, *, tm, n_keep):
    ns, l, _ = x.shape
    nt = l // tm
    prompt = n_keep > 0
    tile = lambda s, t: (s, t, 0)
    fixed = lambda s, t: (0, 0)
    nat = lambda w, dt: (pl.BlockSpec((None, tm, w), tile), jax.ShapeDtypeStruct((ns, l, w), dt))
    outs = []
    if prompt:
        keep_from = (l - n_keep) // tm
        outs += [nat(D_ATTN, BF16)] * 3
        for dil in (4, 16):
            outs += [(pl.BlockSpec((None, dil, tm // dil, D_ATTN), lambda s, t: (s, 0, t, 0)),
                      jax.ShapeDtypeStruct((ns, dil, l // dil, D_ATTN), BF16))] * 3
        outs += [(pl.BlockSpec((None, N_HEADS, HEAD_DIM, tm), lambda s, t: (s, 0, 0, jnp.maximum(t - keep_from, 0))),
                  jax.ShapeDtypeStruct((ns, N_HEADS, HEAD_DIM, n_keep), F32))] * 2
        scratch = [pltpu.VMEM((3, N_PAIRS, tm, LANE), F32), pltpu.VMEM((3, 4, N_PAIRS, tm // 4, LANE), F32)]
    else:
        keep_from = 0
        outs += [nat(D_ATTN, F32)] * 3
        scratch = []
    outs += [nat(D_SSM, F32), nat(CONV_DIM, F32), nat(LANE, F32)]
    return pl.pallas_call(
        functools.partial(_inproj_kernel, prompt=prompt, keep_from=keep_from),
        grid=(ns, nt),
        in_specs=[pl.BlockSpec((None, tm, D_MODEL), tile),
                  pl.BlockSpec((1, D_MODEL), fixed),
                  pl.BlockSpec((D_MODEL, IN_MAIN), fixed)],
        out_specs=[o[0] for o in outs],
        out_shape=[o[1] for o in outs],
        scratch_shapes=scratch,
        compiler_params=_params("parallel", "arbitrary"),
        name="inproj" if prompt else "inproj_sample",
    )(x, g_mix, w_main)


def _ssd_kernel(xbc_ref, z_ref, dt_ref, prefix_ref, state0_ref, cw_ref, cb_ref, dtb_ref, alog_ref,
                dskip_ref, gssm_ref, y_ref, state_ref, xp_ref, st_ref, *, n_chunks, valid_len):
    L = SSD_CHUNK
    c = pl.program_id(1)

    @pl.when(c == 0)
    def _init():
        xp_ref[0:SUBLANE, :] = jnp.zeros((SUBLANE, CONV_DIM), F32)
        xp_ref[SUBLANE - (SSM_CONV - 1):SUBLANE, :] = prefix_ref[...]
        for jp in range(N_PAIRS):
            st_ref[jp] = state0_ref[2 * jp:2 * jp + 2].reshape(2 * SSM_HEAD_DIM, D_STATE).T

    xp_ref[SUBLANE:SUBLANE + L, :] = xbc_ref[...]
    conv = cb_ref[...]
    for k in range(SSM_CONV):
        lo = SUBLANE - (SSM_CONV - 1) + k
        conv = conv + cw_ref[k:k + 1, :] * xp_ref[lo:lo + L, :]
    xp_ref[0:SUBLANE, :] = xp_ref[L:L + SUBLANE, :]
    xc = _silu(conv)
    xs = xc[:, :D_SSM]
    gn = N_SSM_GROUPS * D_STATE

    row = lax.broadcasted_iota(jnp.int32, (L, L), 0)
    col = lax.broadcasted_iota(jnp.int32, (L, L), 1)
    tri = row >= col
    lt64 = col < SSM_HEAD_DIM

    dt = _softplus(dt_ref[...] + dtb_ref[...])
    if valid_len < L:
        dt = jnp.where(row < valid_len, dt, 0.0)
    a = dt * (-jnp.exp(alog_ref[...]))
    acum = a
    shift = 1
    while shift < L:
        acum = acum + jnp.where(row >= shift, pltpu.roll(acum, shift, 0), 0.0)
        shift *= 2
    acum_t = acum.T
    dt_t = dt.T
    e_slab = jnp.exp(acum)
    de_slab = jnp.exp(acum[L - 1:L, :] - acum) * dt

    def head_terms(h, cb):
        colb = jnp.broadcast_to(acum[:, h:h + 1], (L, L))
        rowb = jnp.broadcast_to(acum_t[h:h + 1, :], (L, L))
        decay = jnp.exp(jnp.where(tri, colb - rowb, NEG))
        m = (cb * decay * jnp.broadcast_to(dt_t[h:h + 1, :], (L, L))).astype(BF16)
        e = jnp.broadcast_to(e_slab[:, h:h + 1], (L, L))
        return m, e, jnp.broadcast_to(de_slab[:, h:h + 1], (L, L)), e[L - 1:L, :]

    pairs = []
    for g in range(N_SSM_GROUPS):
        bg = xc[:, D_SSM + g * D_STATE:D_SSM + (g + 1) * D_STATE]
        cg = xc[:, D_SSM + gn + g * D_STATE:D_SSM + gn + (g + 1) * D_STATE].astype(BF16)
        cb = _nt_dot(cg, bg.astype(BF16))
        bg_t = bg.T.astype(BF16)
        for i in range(N_PAIRS // N_SSM_GROUPS):
            jp = g * (N_PAIRS // N_SSM_GROUPS) + i
            xpair = xs[:, jp * LANE:(jp + 1) * LANE]
            xpair_b = xpair.astype(BF16)
            m_a, e_a, de_a, el_a = head_terms(2 * jp, cb)
            m_b, e_b, de_b, el_b = head_terms(2 * jp + 1, cb)
            st = st_ref[jp]
            y_diag = jnp.where(lt64, _dot(m_a, xpair_b), _dot(m_b, xpair_b))
            y_off = _dot(cg, st.astype(BF16)) * jnp.where(lt64, e_a, e_b)
            xd = (xpair * jnp.where(lt64, de_a, de_b)).astype(BF16)
            st_ref[jp] = st * jnp.where(lt64[0:1, :], el_a, el_b) + _dot(bg_t, xd)
            pairs.append(y_diag + y_off + dskip_ref[:, jp * LANE:(jp + 1) * LANE] * xpair)
    y = jnp.concatenate(pairs, axis=1)

    yf = y * _silu(z_ref[...])
    gw = D_SSM // N_SSM_GROUPS
    normed = []
    for g in range(N_SSM_GROUPS):
        seg = yf[:, g * gw:(g + 1) * gw]
        normed.append(seg * lax.rsqrt(jnp.mean(seg * seg, axis=-1, keepdims=True) + EPS))
    y_ref[...] = (jnp.concatenate(normed, axis=1) * gssm_ref[...]).astype(y_ref.dtype)

    @pl.when(c == n_chunks - 1)
    def _fin():
        for jp in range(N_PAIRS):
            state_ref[2 * jp:2 * jp + 2] = st_ref[jp].T.reshape(2, SSM_HEAD_DIM, D_STATE)


def _ssd(xbc, z, dt, prefix, state0, cw, cb, dtb, alog, dskip, gssm, *, valid_len):
    ns, lp, _ = xbc.shape
    n_chunks = lp // SSD_CHUNK
    chunk = lambda s, c: (s, c, 0)
    seq = lambda s, c: (s, 0, 0)
    fixed = lambda s, c: (0, 0)
    kern = functools.partial(_ssd_kernel, n_chunks=n_chunks, valid_len=valid_len)
    return pl.pallas_call(
        kern,
        grid=(ns, n_chunks),
        in_specs=[pl.BlockSpec((None, SSD_CHUNK, CONV_DIM), chunk),
                  pl.BlockSpec((None, SSD_CHUNK, D_SSM), chunk),
                  pl.BlockSpec((None, SSD_CHUNK, LANE), chunk),
                  pl.BlockSpec((None, SSM_CONV - 1, CONV_DIM), seq),
                  pl.BlockSpec((None, N_SSM_HEADS, SSM_HEAD_DIM, D_STATE), lambda s, c: (s, 0, 0, 0)),
                  pl.BlockSpec((SSM_CONV, CONV_DIM), fixed),
                  pl.BlockSpec((1, CONV_DIM), fixed),
                  pl.BlockSpec((1, LANE), fixed),
                  pl.BlockSpec((1, LANE), fixed),
                  pl.BlockSpec((1, D_SSM), fixed),
                  pl.BlockSpec((1, D_SSM), fixed)],
        out_specs=[pl.BlockSpec((None, SSD_CHUNK, D_SSM), chunk),
                   pl.BlockSpec((None, N_SSM_HEADS, SSM_HEAD_DIM, D_STATE), lambda s, c: (s, 0, 0, 0))],
        out_shape=[jax.ShapeDtypeStruct((ns, lp, D_SSM), BF16),
                   jax.ShapeDtypeStruct((ns, N_SSM_HEADS, SSM_HEAD_DIM, D_STATE), F32)],
        scratch_shapes=[pltpu.VMEM((SUBLANE + SSD_CHUNK, CONV_DIM), F32),
                        pltpu.VMEM((N_PAIRS, D_STATE, LANE), F32)],
        compiler_params=_params("parallel", "arbitrary"),
        name="ssd",
    )(xbc, z, dt, prefix, state0, cw, cb, dtb, alog, dskip, gssm)


ATTN_UNITS = 4


def _attn_prompt_kernel(q_ref, kp_ref, kc_ref, vp_ref, vc_ref, bias_ref, o_ref, lse_ref, s_ref, p_ref, *, dil):
    units = q_ref.shape[0]
    blk0 = pl.program_id(1) == 0
    lt64 = _lane_lt64((BLK, LANE))
    lane = lax.broadcasted_iota(jnp.int32, (BLK, LANE), 1)
    zero = jnp.zeros((BLK, LANE), BF16)

    def prev_of(g, p_ref_, c_ref_):
        if dil == 1:
            return (p_ref_.at[0], blk0) if g == 0 else (c_ref_.at[g - 1], None)
        return p_ref_.at[g], blk0

    def rows_of(g):
        if dil == 1:
            return slice(g * BLK, (g + 1) * BLK)
        return pl.ds(pl.program_id(2) * units + g, BLK, stride=dil)

    def scores(g, slot):
        kprev, masked = prev_of(g, kp_ref, kc_ref)
        for jp in range(N_PAIRS):
            sl = slice(jp * LANE, (jp + 1) * LANE)
            qp = q_ref[g, :, sl]
            for half in range(2):
                h = 2 * jp + half
                qm = jnp.where(lt64 if half == 0 else ~lt64, qp, zero)
                bias_prev = bias_ref[h, :, 0:BLK]
                if masked is not None:
                    bias_prev = jnp.where(masked, NEG, bias_prev)
                s_ref[slot, h, :, 0:BLK] = _nt_dot(qm, kprev[:, sl]) + bias_prev
                s_ref[slot, h, :, BLK:2 * BLK] = _nt_dot(qm, kc_ref[g, :, sl]) + bias_ref[h, :, BLK:2 * BLK]

    def softmax(g, slot):
        s = s_ref[slot]
        m = jnp.max(s, axis=-1, keepdims=True)
        p = jnp.exp(s - m)
        den = jnp.sum(p, axis=-1, keepdims=True)
        p_ref[slot] = p.astype(BF16)
        lse = m + jnp.log(den)
        lse_slab = jnp.zeros((BLK, LANE), F32)
        for h in range(N_HEADS):
            lse_slab = jnp.where(lane == h, lse[h], lse_slab)
        lse_ref[rows_of(g), :] = lse_slab
        return den

    def values(g, slot, den):
        vprev, _ = prev_of(g, vp_ref, vc_ref)
        for jp in range(N_PAIRS):
            sl = slice(jp * LANE, (jp + 1) * LANE)
            outs = []
            for half in range(2):
                h = 2 * jp + half
                pv = _dot(p_ref[slot, h, :, 0:BLK], vprev[:, sl]) + _dot(p_ref[slot, h, :, BLK:2 * BLK], vc_ref[g, :, sl])
                outs.append(pv / den[h])
            o_ref[jp, rows_of(g), :] = jnp.where(lt64, outs[0], outs[1])

    scores(0, 0)
    for g in range(units):
        den = softmax(g, g % 2)
        if g + 1 < units:
            scores(g + 1, (g + 1) % 2)
        values(g, g % 2, den)


def _attn_prompt_branch(q, k, v, bias, dil):
    bsz, _, l, _ = q.shape
    s = l * dil
    u = ATTN_UNITS
    if dil == 1:
        q, k, v = (t.reshape(bsz, l // BLK, BLK, D_ATTN) for t in (q, k, v))
        grid = (bsz, l // BLK // u, 1)
        cur = pl.BlockSpec((None, u, BLK, D_ATTN), lambda b, n, r: (b, n, 0, 0))
        prev = pl.BlockSpec((None, 1, BLK, D_ATTN), lambda b, n, r: (b, jnp.maximum(n * u - 1, 0), 0, 0))
        rows = u * BLK
    else:
        grid = (bsz, l // BLK, dil // u)
        cur = pl.BlockSpec((None, u, BLK, D_ATTN), lambda b, n, r: (b, r, n, 0))
        prev = pl.BlockSpec((None, u, BLK, D_ATTN), lambda b, n, r: (b, r, jnp.maximum(n - 1, 0), 0))
        rows = dil * BLK
    return pl.pallas_call(
        functools.partial(_attn_prompt_kernel, dil=dil),
        grid=grid,
        in_specs=[cur, prev, cur, prev, cur,
                  pl.BlockSpec((N_HEADS, BLK, 2 * BLK), lambda b, n, r: (0, 0, 0))],
        out_specs=[pl.BlockSpec((None, N_PAIRS, rows, LANE), lambda b, n, r: (b, 0, n, 0)),
                   pl.BlockSpec((None, rows, LANE), lambda b, n, r: (b, n, 0))],
        out_shape=[jax.ShapeDtypeStruct((bsz, N_PAIRS, s, LANE), F32),
                   jax.ShapeDtypeStruct((bsz, s, LANE), F32)],
        scratch_shapes=[pltpu.VMEM((2, N_HEADS, BLK, 2 * BLK), F32), pltpu.VMEM((2, N_HEADS, BLK, 2 * BLK), BF16)],
        compiler_params=_params("parallel", "parallel", "arbitrary"),
        name=f"attn_prompt_d{dil}",
    )(q, k, k, v, v, bias)


def _attn_sample_kernel(q_ref, kn_ref, vn_ref, kc_ref, vc_ref, b16_ref, b4_ref, b1_ref, bn_ref, o_ref):
    t_len = q_ref.shape[0]
    w = kc_ref.shape[-1]
    lt64 = _lane_lt64((t_len, LANE))
    pad = jnp.zeros((BLK - t_len, D_ATTN), F32)
    kn = jnp.concatenate([kn_ref[...], pad], axis=0).astype(BF16)
    vn = jnp.concatenate([vn_ref[...], pad], axis=0).astype(BF16)
    outs = []
    for jp in range(N_PAIRS):
        sl = slice(jp * LANE, (jp + 1) * LANE)
        qp = q_ref[:, sl]
        qm = jnp.concatenate([jnp.where(lt64, qp, 0.0), jnp.where(lt64, 0.0, qp)], axis=0).astype(BF16)
        kt = kc_ref[2 * jp:2 * jp + 2].reshape(2 * HEAD_DIM, w).astype(BF16)
        vt = vc_ref[2 * jp:2 * jp + 2].reshape(2 * HEAD_DIM, w).astype(BF16)
        s = _dot(qm, kt)
        s_new = _nt_dot(qm, kn[:, sl])
        w4, w1 = w - 4 * BLK, w - BLK
        tiles = [s + b16_ref[jp], s[:, w4:] + b4_ref[jp], s[:, w1:] + b1_ref[jp]]
        tiles += [s_new + bn_ref[br, jp] for br in range(len(DILATIONS))]
        m = functools.reduce(jnp.maximum, [jnp.max(t, axis=-1, keepdims=True) for t in tiles])
        ps = [jnp.exp(t - m) for t in tiles]
        den = functools.reduce(lambda a, b: a + b, [jnp.sum(p, axis=-1, keepdims=True) for p in ps])
        p16, p4, p1 = ps[0], ps[1], ps[2]
        p_cache = jnp.concatenate([p16[:, :w4], p16[:, w4:w1] + p4[:, :w1 - w4],
                                   p16[:, w1:] + p4[:, w1 - w4:] + p1], axis=1)
        p_new = ps[3] + ps[4] + ps[5]
        o2 = (_nt_dot(p_cache.astype(BF16), vt) + _dot(p_new.astype(BF16), vn[:, sl])) / den
        outs.append(jnp.where(lt64, o2[0:t_len], o2[t_len:2 * t_len]))
    o_ref[...] = jnp.concatenate(outs, axis=1).astype(o_ref.dtype)


def _attn_sample(q, kn, vn, kc, vc, b16, b4, b1, bn):
    ns, t_len, _ = q.shape
    w = kc.shape[-1]
    assert w == MAX_DISTANCE and t_len == SUBLANE
    tok = pl.BlockSpec((None, t_len, D_ATTN), lambda s: (s, 0, 0))
    cache = pl.BlockSpec((None, N_HEADS, HEAD_DIM, w), lambda s: (s, 0, 0, 0))
    full = lambda a: pl.BlockSpec(a.shape, lambda s: (0,) * a.ndim)
    return pl.pallas_call(
        _attn_sample_kernel,
        grid=(ns,),
        in_specs=[tok, tok, tok, cache, cache, full(b16), full(b4), full(b1), full(bn)],
        out_specs=tok,
        out_shape=jax.ShapeDtypeStruct((ns, t_len, D_ATTN), BF16),
        compiler_params=_params("parallel"),
        name="attn_sample",
    )(q, kn, vn, kc, vc, b16, b4, b1, bn)


def _mix_ffn_kernel(*refs, n_parts, seg_in_tile, n_tiles):
    it = iter(refs)
    x_ref = next(it)
    parts = [(next(it), next(it)) for _ in range(n_parts)] if n_parts > 1 else [(next(it), None)]
    yssm_ref, p_ref, fpre_ref = next(it), next(it), next(it)
    (expand_ref, wout_ref, gffn_ref, wup_ref, fcw_ref, fcb_ref, wdown_ref, wple_ref, gple_ref,
     wgate_ref, gfin_ref) = (next(it) for _ in range(11))
    y_ref, uout_ref = next(it), next(it)
    act_ref, tail_ref = next(it), next(it)
    tm = x_ref.shape[0]

    if n_parts > 1:
        lses = [lse_ref[...] for _, lse_ref in parts]
        mx = functools.reduce(jnp.maximum, lses)
        ws = [jnp.exp(l - mx) for l in lses]
        den = functools.reduce(lambda a, b: a + b, ws)
        attn = None
        for (o_ref, _), w in zip(parts, ws):
            wn = w / den
            hi = wn.astype(BF16)
            lo = (wn - hi.astype(F32)).astype(BF16)
            wexp = _dot(jnp.concatenate([hi, lo], axis=1), expand_ref[...])
            o = jnp.concatenate([o_ref[jp] for jp in range(N_PAIRS)], axis=1)
            attn = wexp * o if attn is None else attn + wexp * o
        attn = attn.astype(BF16)
    else:
        attn = parts[0][0][...]
    h1 = x_ref[...] + _dot(attn, wout_ref[0:D_ATTN, :]) + _dot(yssm_ref[...], wout_ref[D_ATTN:, :])

    hn = _rms(h1, gffn_ref[...]).astype(BF16)

    if not seg_in_tile:
        t = pl.program_id(1)

        @pl.when(t == 0)
        def _init():
            tail_ref[...] = jnp.zeros(tail_ref.shape, F32)
            tail_ref[SUBLANE - (FFN_CONV - 1):SUBLANE, :] = fpre_ref[...]

        rin8 = lax.broadcasted_iota(jnp.int32, (SUBLANE, FF_CHUNK), 0)
    else:
        rin = lax.broadcasted_iota(jnp.int32, (tm, FF_CHUNK), 0) & (SUBLANE - 1)

    def conv(u, c0):
        cs = slice(c0, c0 + FF_CHUNK)
        if seg_in_tile:
            pre = fpre_ref[:, cs]
            um1 = jnp.where(rin == 0, pltpu.roll(pre, tm - 1, 0), pltpu.roll(u, 1, 0))
            um2 = jnp.where(rin < 2, pre, pltpu.roll(u, 2, 0))
            uout_ref[:, cs] = u
        else:
            tail = tail_ref[:, cs]
            tail_ref[:, cs] = u[tm - SUBLANE:tm, :]
            r1, r2 = pltpu.roll(u, 1, 0), pltpu.roll(u, 2, 0)
            um1 = jnp.concatenate([jnp.where(rin8 < 1, pltpu.roll(tail, 1, 0), r1[0:SUBLANE]), r1[SUBLANE:]], axis=0)
            um2 = jnp.concatenate([jnp.where(rin8 < 2, pltpu.roll(tail, 2, 0), r2[0:SUBLANE]), r2[SUBLANE:]], axis=0)
        return fcb_ref[:, cs] + fcw_ref[0:1, cs] * um2 + fcw_ref[1:2, cs] * um1 + fcw_ref[2:3, cs] * u

    for j in range(D_FF // FF_CHUNK):
        c0 = j * FF_CHUNK
        u_gate = conv(_dot(hn, wup_ref[:, c0:c0 + FF_CHUNK]), c0)
        u_lin = conv(_dot(hn, wup_ref[:, D_FF + c0:D_FF + c0 + FF_CHUNK]), D_FF + c0)
        act_ref[:, c0:c0 + FF_CHUNK] = (_silu(u_gate) * u_lin).astype(BF16)
    h2 = h1 + _dot(act_ref[...], wdown_ref[...])

    e = _rms(_dot(p_ref[...].astype(BF16), wple_ref[...]), gple_ref[...])
    h3 = h2 + jax.nn.sigmoid(_dot(h2.astype(BF16), wgate_ref[...])) * e
    y_ref[...] = _rms(h3, gfin_ref[...])

    if not seg_in_tile:
        @pl.when(pl.program_id(1) == n_tiles - 1)
        def _fin():
            uout_ref[...] = tail_ref[...]


def _mix_ffn(x, parts, yssm, p, fpre, weights, *, tm, seg_in_tile):
    ns, l, _ = x.shape
    n_tiles = l // tm
    n_parts = len(parts)
    tile = lambda s, t: (s, t, 0)
    seq = lambda s, t: (s, 0, 0)
    fixed = lambda s, t: (0, 0)
    in_specs = [pl.BlockSpec((None, tm, D_MODEL), tile)]
    args = [x]
    for o, lse in parts:
        if n_parts > 1:
            in_specs += [pl.BlockSpec((None, N_PAIRS, tm, LANE), lambda s, t: (s, 0, t, 0)),
                         pl.BlockSpec((None, tm, LANE), tile)]
            args += [o, lse]
        else:
            in_specs.append(pl.BlockSpec((None, tm, D_ATTN), tile))
            args.append(o)
    in_specs += [pl.BlockSpec((None, tm, D_SSM), tile), pl.BlockSpec((None, tm, D_PLE), tile)]
    args += [yssm, p]
    if seg_in_tile:
        in_specs.append(pl.BlockSpec((None, tm, 2 * D_FF), tile))
        uout_spec = pl.BlockSpec((None, tm, 2 * D_FF), tile)
        uout_shape = jax.ShapeDtypeStruct((ns, l, 2 * D_FF), F32)
    else:
        in_specs.append(pl.BlockSpec((None, FFN_CONV - 1, 2 * D_FF), seq))
        uout_spec = pl.BlockSpec((None, SUBLANE, 2 * D_FF), seq)
        uout_shape = jax.ShapeDtypeStruct((ns, SUBLANE, 2 * D_FF), F32)
    args.append(fpre)
    for w in weights:
        in_specs.append(pl.BlockSpec(w.shape, fixed))
        args.append(w)
    kern = functools.partial(_mix_ffn_kernel, n_parts=n_parts, seg_in_tile=seg_in_tile, n_tiles=n_tiles)
    return pl.pallas_call(
        kern,
        grid=(ns, n_tiles),
        in_specs=in_specs,
        out_specs=[pl.BlockSpec((None, tm, D_MODEL), tile), uout_spec],
        out_shape=[jax.ShapeDtypeStruct((ns, l, D_MODEL), F32), uout_shape],
        scratch_shapes=[pltpu.VMEM((tm, D_FF), BF16),
                        pltpu.VMEM((SUBLANE, 2 * D_FF), F32)],
        compiler_params=_params("parallel", "arbitrary"),
        name="mix_ffn_seg" if seg_in_tile else "mix_ffn",
    )(*args)


def _pad_lanes(v, width=LANE):
    return jnp.pad(v.astype(F32), (0, width - v.shape[0]))[None, :]


def kernel(x_prompt, x_sample, p_prompt, p_sample, cache_k, cache_v, state_ssm, state_conv, state_ffn_conv,
           rel_bias, g_mix, w_in, conv_w, conv_b, dt_bias, a_log, d_skip, g_ssm, w_out, g_ffn, w_up,
           ffn_conv_w, ffn_conv_b, w_down, w_ple_proj, g_ple, w_ple_gate, g_final):
    assert w_in.shape[0] == 1, "one layer"
    bp, s, _ = x_prompt.shape
    nsamp, t_len, _ = x_sample.shape
    n_keep = min(MAX_DISTANCE, s)

    w_main = jnp.pad(w_in[0], ((0, 0), (0, IN_MAIN - w_in.shape[2]))).astype(BF16)
    gmix = g_mix[0][None, :]
    ssd_params = (conv_w[0], conv_b[0][None, :], _pad_lanes(dt_bias[0]), _pad_lanes(a_log[0]),
                  jnp.repeat(d_skip[0], SSM_HEAD_DIM)[None, :], g_ssm[0][None, :])
    expand = (np.arange(2 * LANE)[:, None] % LANE == (np.arange(D_ATTN)[None, :] // HEAD_DIM)).astype(np.float32)
    ffn_weights = (jnp.asarray(expand, BF16), w_out[0].astype(BF16), g_ffn[0][None, :], w_up[0].astype(BF16),
                   ffn_conv_w[0], ffn_conv_b[0][None, :], w_down[0].astype(BF16), w_ple_proj[0].astype(BF16),
                   g_ple[0][None, :], w_ple_gate[0].astype(BF16), g_final[None, :])
    bias_p, b16, b4, b1, bn = _bias_tables(rel_bias, t_len)

    (q1, k1, v1, q4, k4, v4, q16, k16, v16, kt, vt, z, xbc, dt) = _inproj(x_prompt, gmix, w_main, tm=512, n_keep=n_keep)
    yssm_p, ssm_p = _ssd(xbc, z, dt,
                         jnp.zeros((bp, SSM_CONV - 1, CONV_DIM), F32),
                         jnp.zeros((bp, N_SSM_HEADS, SSM_HEAD_DIM, D_STATE), F32),
                         *ssd_params, valid_len=SSD_CHUNK)
    parts = [_attn_prompt_branch(q1[:, None], k1[:, None], v1[:, None], bias_p[0], 1),
             _attn_prompt_branch(q4, k4, v4, bias_p[1], 4),
             _attn_prompt_branch(q16, k16, v16, bias_p[2], 16)]
    y_prompt, tail_p = _mix_ffn(x_prompt, parts, yssm_p, p_prompt[0],
                                jnp.zeros((bp, FFN_CONV - 1, 2 * D_FF), F32), ffn_weights,
                                tm=512, seg_in_tile=False)
    k_prompt = jnp.transpose(kt, (0, 3, 1, 2))[None]
    v_prompt = jnp.transpose(vt, (0, 3, 1, 2))[None]
    conv_prompt = xbc[:, s - (SSM_CONV - 1):][None]
    ffn_conv_prompt = tail_p[:, SUBLANE - (FFN_CONV - 1):][None]

    rows = nsamp * t_len
    qs, ks, vs, zs, xbcs, dts = _inproj(x_sample.reshape(1, rows, D_MODEL), gmix, w_main, tm=rows, n_keep=0)
    s3 = lambda a: a.reshape(nsamp, t_len, a.shape[-1])
    pad_t = lambda a: jnp.pad(s3(a), ((0, 0), (0, SSD_CHUNK - t_len), (0, 0)))
    yssm_s, ssm_s = _ssd(pad_t(xbcs), pad_t(zs), pad_t(dts), state_conv[0], state_ssm[0],
                         *ssd_params, valid_len=t_len)
    attn_s = _attn_sample(s3(qs), s3(ks), s3(vs),
                          jnp.transpose(cache_k[0], (0, 2, 3, 1)), jnp.transpose(cache_v[0], (0, 2, 3, 1)),
                          b16, b4, b1, bn)
    fpre = jnp.pad(state_ffn_conv[0], ((0, 0), (0, t_len - (FFN_CONV - 1)), (0, 0))).reshape(1, rows, 2 * D_FF)
    y_s, u_s = _mix_ffn(x_sample.reshape(1, rows, D_MODEL), [(attn_s.reshape(1, rows, D_ATTN), None)],
                        yssm_s[:, :t_len].reshape(1, rows, D_SSM), p_sample[0].reshape(1, rows, D_PLE),
                        fpre, ffn_weights, tm=rows, seg_in_tile=True)
    y_sample = y_s.reshape(nsamp, t_len, D_MODEL)
    k_sample = ks.reshape(1, nsamp, t_len, N_HEADS, HEAD_DIM)
    v_sample = vs.reshape(1, nsamp, t_len, N_HEADS, HEAD_DIM)
    conv_sample = s3(xbcs)[:, t_len - (SSM_CONV - 1):][None]
    ffn_conv_sample = u_s.reshape(nsamp, t_len, 2 * D_FF)[:, t_len - (FFN_CONV - 1):][None]

    return (y_prompt, y_sample, k_prompt, v_prompt, k_sample, v_sample,
            ssm_p[None], ssm_s[None], conv_prompt, conv_sample, ffn_conv_prompt, ffn_conv_sample)
```

```python
import functools
import math

import numpy as np
import jax
import jax.numpy as jnp
from jax import lax
from jax.experimental import pallas as pl
from jax.experimental.pallas import tpu as pltpu

F32 = jnp.float32
BF16 = jnp.bfloat16

D_MODEL = 1024
HEAD_DIM = 64
N_HEADS = 8
D_ATTN = N_HEADS * HEAD_DIM
N_PAIRS = N_HEADS // 2
DILATIONS = (1, 4, 16)
N_STEPS = 128
BLK = 128
N_BUCKETS = 32
MAX_DISTANCE = 2048
D_SSM = 512
N_SSM_HEADS = 8
SSM_HEAD_DIM = 64
D_STATE = 128
N_SSM_GROUPS = 2
SSM_CONV = 4
CONV_DIM = D_SSM + 2 * N_SSM_GROUPS * D_STATE
SSD_CHUNK = 128
D_FF = 2816
FFN_CONV = 3
D_PLE = 256
EPS = 1e-6
NEG = -1e30

LANE = 128
SUBLANE = 8
FF_CHUNK = 256
VMEM_LIMIT = 56 * 1024 * 1024

O_Q, O_K, O_V, O_Z, O_XBC, O_DT = 0, 512, 1024, 1536, 2048, 3072
IN_MAIN = O_DT + LANE


def _rel_bucket_np(dist):
    dist = np.asarray(dist, np.int32)
    max_exact = N_BUCKETS // 2
    d = np.maximum(dist, 1).astype(np.float32)
    large = max_exact + (np.log(d / np.float32(max_exact)) / np.float32(math.log(MAX_DISTANCE / max_exact))
                         * np.float32(N_BUCKETS - max_exact)).astype(np.int32)
    large = np.minimum(large, N_BUCKETS - 1)
    return np.where(dist < max_exact, dist, large)


def _nt_dot(a, b):
    return lax.dot_general(a, b, (((1,), (1,)), ((), ())), preferred_element_type=F32)


def _dot(a, b):
    return jnp.dot(a, b, preferred_element_type=F32)


def _silu(x):
    return x * jax.nn.sigmoid(x)


def _softplus(x):
    return jnp.maximum(x, 0.0) + jnp.log1p(jnp.exp(-jnp.abs(x)))


def _rms(x, g):
    return x * lax.rsqrt(jnp.mean(x * x, axis=-1, keepdims=True) + EPS) * g


def _lane_lt64(shape):
    return lax.broadcasted_iota(jnp.int32, shape, len(shape) - 1) < HEAD_DIM


def _params(*sem):
    return pltpu.CompilerParams(dimension_semantics=sem, vmem_limit_bytes=VMEM_LIMIT)


def _bias_kernel(rb_ref, rbt_ref, ig_ref, i16_ref, i4_ref, i1_ref, in_ref, tp_ref, t16_ref, t4_ref, t1_ref, tn_ref):
    def lookup(idx, h):
        def body(b, acc):
            return jnp.where(idx == b, rb_ref[b, h], acc)
        return lax.fori_loop(0, N_BUCKETS, body, jnp.full(idx.shape, NEG, F32))

    for br in range(len(DILATIONS)):
        idx = jnp.broadcast_to(ig_ref[br], (N_HEADS, 2 * BLK))
        gen = jnp.full((N_HEADS, 2 * BLK), NEG, F32)
        for b in range(N_BUCKETS):
            gen = jnp.where(idx == b, jnp.broadcast_to(rbt_ref[:, b:b + 1], (N_HEADS, 2 * BLK)), gen)
        for h in range(N_HEADS):
            rows = jnp.broadcast_to(gen[h:h + 1, :], (BLK, 2 * BLK))
            tp_ref[br, h] = pltpu.roll(rows, 0, 1, stride=1, stride_axis=0)
    for h in range(N_HEADS):
        jp, half = divmod(h, 2)
        rs = slice(half * SUBLANE, (half + 1) * SUBLANE)
        t16_ref[jp, rs, :] = lookup(i16_ref[...], h)
        t4_ref[jp, rs, :] = lookup(i4_ref[...], h)
        t1_ref[jp, rs, :] = lookup(i1_ref[...], h)
        for br in range(len(DILATIONS)):
            tn_ref[br, jp, rs, :] = lookup(in_ref[br], h)


def _bucket_maps(t_len):
    j = BLK - np.arange(2 * BLK)[None, :]
    prompt = np.stack([np.where(j >= 0, _rel_bucket_np(np.clip(j, 0, N_STEPS) * d), -1) for d in DILATIONS])

    t = np.arange(t_len)[:, None]

    def sample_map(diff, dil):
        ok = (diff >= 0) & (diff % dil == 0) & (diff // dil <= N_STEPS)
        return np.where(ok, _rel_bucket_np(np.maximum(diff, 0)), -1).astype(np.int32)

    w = np.arange(MAX_DISTANCE)[None, :]
    cache = {d: sample_map(MAX_DISTANCE + t - w, d) for d in DILATIONS}
    g = np.arange(LANE)[None, :]
    new = np.stack([np.where(g < t_len, sample_map(t - g, d), -1) for d in DILATIONS])
    return (prompt.astype(np.int32), cache[16], cache[4][:, MAX_DISTANCE - 4 * BLK:],
            cache[1][:, MAX_DISTANCE - BLK:], new.astype(np.int32))


def _bias_tables(rel_bias, t_len):
    maps = _bucket_maps(t_len)
    nb = len(DILATIONS)
    shapes = [(nb, N_HEADS, BLK, 2 * BLK), (N_PAIRS, 2 * t_len, MAX_DISTANCE), (N_PAIRS, 2 * t_len, 4 * BLK),
              (N_PAIRS, 2 * t_len, BLK), (nb, N_PAIRS, 2 * t_len, LANE)]
    return pl.pallas_call(
        _bias_kernel,
        in_specs=[pl.BlockSpec(memory_space=pltpu.SMEM)] + [pl.BlockSpec(memory_space=pltpu.VMEM)] * 6,
        out_specs=[pl.BlockSpec(memory_space=pltpu.VMEM)] * 5,
        out_shape=[jax.ShapeDtypeStruct(s, F32) for s in shapes],
        compiler_params=pltpu.CompilerParams(vmem_limit_bytes=VMEM_LIMIT),
        name="bias_tables",
    )(rel_bias, rel_bias.T, *[jnp.asarray(m) for m in maps])


def _inproj_kernel(x_ref, g_ref, w_ref, *refs, prompt, keep_from):
    tm = x_ref.shape[0]
    xn = _rms(x_ref[...], g_ref[...]).astype(BF16)

    def proj(lo, hi):
        return _dot(xn, w_ref[:, lo:hi])

    q = proj(O_Q, O_K) * (HEAD_DIM ** -0.5)
    k = proj(O_K, O_V)
    v = proj(O_V, O_Z)
    if prompt:
        (q1, k1, v1, q4, k4, v4, q16, k16, v16, kt_ref, vt_ref, z_ref, xbc_ref, dt_ref, perm_ref, mid_ref) = refs
        for i, (val, nat, r4, r16) in enumerate(((q, q1, q4, q16), (k, k1, k4, k16), (v, v1, v4, v16))):
            nat[...] = val.astype(BF16)
            for jp in range(N_PAIRS):
                perm_ref[i, jp] = val[:, jp * LANE:(jp + 1) * LANE]
            for ra in range(4):
                for jp in range(N_PAIRS):
                    mid_ref[i, ra, jp] = perm_ref[i, jp, pl.ds(ra, tm // 4, stride=4), :]
                r4[ra] = jnp.concatenate([mid_ref[i, ra, jp] for jp in range(N_PAIRS)], axis=1).astype(BF16)
                for rb in range(4):
                    rows = [mid_ref[i, ra, jp, pl.ds(rb, tm // 16, stride=4), :] for jp in range(N_PAIRS)]
                    r16[ra + 4 * rb] = jnp.concatenate(rows, axis=1).astype(BF16)
    else:
        qf_ref, kf_ref, vf_ref, z_ref, xbc_ref, dt_ref = refs
        qf_ref[...] = q
        kf_ref[...] = k
        vf_ref[...] = v
    z_ref[...] = proj(O_Z, O_XBC)
    xbc_ref[...] = proj(O_XBC, O_DT)
    dt_ref[...] = proj(O_DT, IN_MAIN)

    if prompt:
        @pl.when(pl.program_id(1) >= keep_from)
        def _keep():
            for i, out_ref in ((1, kt_ref), (2, vt_ref)):
                for jp in range(N_PAIRS):
                    out_ref[2 * jp:2 * jp + 2] = perm_ref[i, jp].T.reshape(2, HEAD_DIM, tm)


def _inproj(x, g_mix, w_main, *, tm, n_keep):
    ns, l, _ = x.shape
    nt = l // tm
    prompt = n_keep > 0
    tile = lambda s, t: (s, t, 0)
    fixed = lambda s, t: (0, 0)
    nat = lambda w, dt: (pl.BlockSpec((None, tm, w), tile), jax.ShapeDtypeStruct((ns, l, w), dt))
    outs = []
    if prompt:
        keep_from = (l - n_keep) // tm
        outs += [nat(D_ATTN, BF16)] * 3
        for dil in (4, 16):
            outs += [(pl.BlockSpec((None, dil, tm // dil, D_ATTN), lambda s, t: (s, 0, t, 0)),
                      jax.ShapeDtypeStruct((ns, dil, l // dil, D_ATTN), BF16))] * 3
        outs += [(pl.BlockSpec((None, N_HEADS, HEAD_DIM, tm), lambda s, t: (s, 0, 0, jnp.maximum(t - keep_from, 0))),
                  jax.ShapeDtypeStruct((ns, N_HEADS, HEAD_DIM, n_keep), F32))] * 2
        scratch = [pltpu.VMEM((3, N_PAIRS, tm, LANE), F32), pltpu.VMEM((3, 4, N_PAIRS, tm // 4, LANE), F32)]
    else:
        keep_from = 0
        outs += [nat(D_ATTN, F32)] * 3
        scratch = []
    outs += [nat(D_SSM, F32), nat(CONV_DIM, F32), nat(LANE, F32)]
    return pl.pallas_call(
        functools.partial(_inproj_kernel, prompt=prompt, keep_from=keep_from),
        grid=(ns, nt),
        in_specs=[pl.BlockSpec((None, tm, D_MODEL), tile),
                  pl.BlockSpec((1, D_MODEL), fixed),
                  pl.BlockSpec((D_MODEL, IN_MAIN), fixed)],
        out_specs=[o[0] for o in outs],
        out_shape=[o[1] for o in outs],
        scratch_shapes=scratch,
        compiler_params=_params("parallel", "arbitrary"),
        name="inproj" if prompt else "inproj_sample",
    )(x, g_mix, w_main)


def _ssd_kernel(xbc_ref, z_ref, dt_ref, prefix_ref, state0_ref, cw_ref, cb_ref, dtb_ref, alog_ref,
                dskip_ref, gssm_ref, y_ref, state_ref, xp_ref, st_ref, *, n_chunks, valid_len):
    L = SSD_CHUNK
    c = pl.program_id(1)

    @pl.when(c == 0)
    def _init():
        xp_ref[0:SUBLANE, :] = jnp.zeros((SUBLANE, CONV_DIM), F32)
        xp_ref[SUBLANE - (SSM_CONV - 1):SUBLANE, :] = prefix_ref[...]
        for jp in range(N_PAIRS):
            st_ref[jp] = state0_ref[2 * jp:2 * jp + 2].reshape(2 * SSM_HEAD_DIM, D_STATE).T

    xp_ref[SUBLANE:SUBLANE + L, :] = xbc_ref[...]
    conv = cb_ref[...]
    for k in range(SSM_CONV):
        lo = SUBLANE - (SSM_CONV - 1) + k
        conv = conv + cw_ref[k:k + 1, :] * xp_ref[lo:lo + L, :]
    xp_ref[0:SUBLANE, :] = xp_ref[L:L + SUBLANE, :]
    xc = _silu(conv)
    xs = xc[:, :D_SSM]
    gn = N_SSM_GROUPS * D_STATE

    row = lax.broadcasted_iota(jnp.int32, (L, L), 0)
    col = lax.broadcasted_iota(jnp.int32, (L, L), 1)
    tri = row >= col
    lt64 = col < SSM_HEAD_DIM

    dt = _softplus(dt_ref[...] + dtb_ref[...])
    if valid_len < L:
        dt = jnp.where(row < valid_len, dt, 0.0)
    a = dt * (-jnp.exp(alog_ref[...]))
    acum = a
    shift = 1
    while shift < L:
        acum = acum + jnp.where(row >= shift, pltpu.roll(acum, shift, 0), 0.0)
        shift *= 2
    acum_t = acum.T
    dt_t = dt.T
    e_slab = jnp.exp(acum)
    de_slab = jnp.exp(acum[L - 1:L, :] - acum) * dt

    def head_terms(h, cb):
        colb = jnp.broadcast_to(acum[:, h:h + 1], (L, L))
        rowb = jnp.broadcast_to(acum_t[h:h + 1, :], (L, L))
        decay = jnp.exp(jnp.where(tri, colb - rowb, NEG))
        m = (cb * decay * jnp.broadcast_to(dt_t[h:h + 1, :], (L, L))).astype(BF16)
        e = jnp.broadcast_to(e_slab[:, h:h + 1], (L, L))
        return m, e, jnp.broadcast_to(de_slab[:, h:h + 1], (L, L)), e[L - 1:L, :]

    pairs = []
    for g in range(N_SSM_GROUPS):
        bg = xc[:, D_SSM + g * D_STATE:D_SSM + (g + 1) * D_STATE]
        cg = xc[:, D_SSM + gn + g * D_STATE:D_SSM + gn + (g + 1) * D_STATE].astype(BF16)
        cb = _nt_dot(cg, bg.astype(BF16))
        bg_t = bg.T.astype(BF16)
        for i in range(N_PAIRS // N_SSM_GROUPS):
            jp = g * (N_PAIRS // N_SSM_GROUPS) + i
            xpair = xs[:, jp * LANE:(jp + 1) * LANE]
            xpair_b = xpair.astype(BF16)
            m_a, e_a, de_a, el_a = head_terms(2 * jp, cb)
            m_b, e_b, de_b, el_b = head_terms(2 * jp + 1, cb)
            st = st_ref[jp]
            y_diag = jnp.where(lt64, _dot(m_a, xpair_b), _dot(m_b, xpair_b))
            y_off = _dot(cg, st.astype(BF16)) * jnp.where(lt64, e_a, e_b)
            xd = (xpair * jnp.where(lt64, de_a, de_b)).astype(BF16)
            st_ref[jp] = st * jnp.where(lt64[0:1, :], el_a, el_b) + _dot(bg_t, xd)
            pairs.append(y_diag + y_off + dskip_ref[:, jp * LANE:(jp + 1) * LANE] * xpair)
    y = jnp.concatenate(pairs, axis=1)

    yf = y * _silu(z_ref[...])
    gw = D_SSM // N_SSM_GROUPS
    normed = []
    for g in range(N_SSM_GROUPS):
        seg = yf[:, g * gw:(g + 1) * gw]
        normed.append(seg * lax.rsqrt(jnp.mean(seg * seg, axis=-1, keepdims=True) + EPS))
    y_ref[...] = (jnp.concatenate(normed, axis=1) * gssm_ref[...]).astype(y_ref.dtype)

    @pl.when(c == n_chunks - 1)
    def _fin():
        for jp in range(N_PAIRS):
            state_ref[2 * jp:2 * jp + 2] = st_ref[jp].T.reshape(2, SSM_HEAD_DIM, D_STATE)


def _ssd(xbc, z, dt, prefix, state0, cw, cb, dtb, alog, dskip, gssm, *, valid_len):
    ns, lp, _ = xbc.shape
    n_chunks = lp // SSD_CHUNK
    chunk = lambda s, c: (s, c, 0)
    seq = lambda s, c: (s, 0, 0)
    fixed = lambda s, c: (0, 0)
    kern = functools.partial(_ssd_kernel, n_chunks=n_chunks, valid_len=valid_len)
    return pl.pallas_call(
        kern,
        grid=(ns, n_chunks),
        in_specs=[pl.BlockSpec((None, SSD_CHUNK, CONV_DIM), chunk),
                  pl.BlockSpec((None, SSD_CHUNK, D_SSM), chunk),
                  pl.BlockSpec((None, SSD_CHUNK, LANE), chunk),
                  pl.BlockSpec((None, SSM_CONV - 1, CONV_DIM), seq),
                  pl.BlockSpec((None, N_SSM_HEADS, SSM_HEAD_DIM, D_STATE), lambda s, c: (s, 0, 0, 0)),
                  pl.BlockSpec((SSM_CONV, CONV_DIM), fixed),
                  pl.BlockSpec((1, CONV_DIM), fixed),
                  pl.BlockSpec((1, LANE), fixed),
                  pl.BlockSpec((1, LANE), fixed),
                  pl.BlockSpec((1, D_SSM), fixed),
                  pl.BlockSpec((1, D_SSM), fixed)],
        out_specs=[pl.BlockSpec((None, SSD_CHUNK, D_SSM), chunk),
                   pl.BlockSpec((None, N_SSM_HEADS, SSM_HEAD_DIM, D_STATE), lambda s, c: (s, 0, 0, 0))],
        out_shape=[jax.ShapeDtypeStruct((ns, lp, D_SSM), BF16),
                   jax.ShapeDtypeStruct((ns, N_SSM_HEADS, SSM_HEAD_DIM, D_STATE), F32)],
        scratch_shapes=[pltpu.VMEM((SUBLANE + SSD_CHUNK, CONV_DIM), F32),
                        pltpu.VMEM((N_PAIRS, D_STATE, LANE), F32)],
        compiler_params=_params("parallel", "arbitrary"),
        name="ssd",
    )(xbc, z, dt, prefix, state0, cw, cb, dtb, alog, dskip, gssm)


ATTN_UNITS = 4


def _attn_prompt_kernel(q_ref, kp_ref, kc_ref, vp_ref, vc_ref, bias_ref, o_ref, max_ref, den_ref, s_ref, p_ref, *, dil):
    units = q_ref.shape[0]
    blk0 = pl.program_id(1) == 0
    lt64 = _lane_lt64((BLK, LANE))
    lane = lax.broadcasted_iota(jnp.int32, (BLK, LANE), 1)
    zero = jnp.zeros((BLK, LANE), BF16)

    def prev_of(g, p_ref_, c_ref_):
        if dil == 1:
            return (p_ref_.at[0], blk0) if g == 0 else (c_ref_.at[g - 1], None)
        return p_ref_.at[g], blk0

    def rows_of(g):
        if dil == 1:
            return slice(g * BLK, (g + 1) * BLK)
        return pl.ds(pl.program_id(2) * units + g, BLK, stride=dil)

    def scores(g, slot):
        kprev, masked = prev_of(g, kp_ref, kc_ref)
        for jp in range(N_PAIRS):
            sl = slice(jp * LANE, (jp + 1) * LANE)
            qp = q_ref[g, :, sl]
            kcat = jnp.concatenate([kprev[:, sl], kc_ref[g, :, sl]], axis=0)
            for half in range(2):
                h = 2 * jp + half
                qm = jnp.where(lt64 if half == 0 else ~lt64, qp, zero)
                bias = bias_ref[h]
                if masked is not None:
                    bias = jnp.concatenate([jnp.where(masked, NEG, bias[:, 0:BLK]), bias[:, BLK:]], axis=1)
                s_ref[slot, h] = _nt_dot(qm, kcat) + bias

    def softmax(g, slot):
        s = s_ref[slot]
        m = jnp.max(s, axis=-1, keepdims=True)
        p = jnp.exp(s - m)
        den = jnp.sum(p, axis=-1, keepdims=True)
        p_ref[slot] = p.astype(BF16)
        m_slab = jnp.zeros((BLK, LANE), F32)
        den_slab = jnp.ones((BLK, LANE), F32)
        for h in range(N_HEADS):
            m_slab = jnp.where(lane == h, m[h], m_slab)
            den_slab = jnp.where(lane == h, den[h], den_slab)
        max_ref[rows_of(g), :] = m_slab
        den_ref[rows_of(g), :] = den_slab

    def values(g, slot):
        vprev, _ = prev_of(g, vp_ref, vc_ref)
        for jp in range(N_PAIRS):
            sl = slice(jp * LANE, (jp + 1) * LANE)
            vcat = jnp.concatenate([vprev[:, sl], vc_ref[g, :, sl]], axis=0)
            pv = [_dot(p_ref[slot, 2 * jp + half], vcat) for half in range(2)]
            o_ref[jp, rows_of(g), :] = jnp.where(lt64, pv[0], pv[1])

    scores(0, 0)
    for g in range(units):
        softmax(g, g % 2)
        if g + 1 < units:
            scores(g + 1, (g + 1) % 2)
        values(g, g % 2)


def _attn_prompt_branch(q, k, v, bias, dil):
    bsz, _, l, _ = q.shape
    s = l * dil
    u = ATTN_UNITS
    if dil == 1:
        q, k, v = (t.reshape(bsz, l // BLK, BLK, D_ATTN) for t in (q, k, v))
        grid = (bsz, l // BLK // u, 1)
        cur = pl.BlockSpec((None, u, BLK, D_ATTN), lambda b, n, r: (b, n, 0, 0))
        prev = pl.BlockSpec((None, 1, BLK, D_ATTN), lambda b, n, r: (b, jnp.maximum(n * u - 1, 0), 0, 0))
        rows = u * BLK
    else:
        grid = (bsz, l // BLK, dil // u)
        cur = pl.BlockSpec((None, u, BLK, D_ATTN), lambda b, n, r: (b, r, n, 0))
        prev = pl.BlockSpec((None, u, BLK, D_ATTN), lambda b, n, r: (b, r, jnp.maximum(n - 1, 0), 0))
        rows = dil * BLK
    return pl.pallas_call(
        functools.partial(_attn_prompt_kernel, dil=dil),
        grid=grid,
        in_specs=[cur, prev, cur, prev, cur,
                  pl.BlockSpec((N_HEADS, BLK, 2 * BLK), lambda b, n, r: (0, 0, 0))],
        out_specs=[pl.BlockSpec((None, N_PAIRS, rows, LANE), lambda b, n, r: (b, 0, n, 0)),
                   pl.BlockSpec((None, rows, LANE), lambda b, n, r: (b, n, 0)),
                   pl.BlockSpec((None, rows, LANE), lambda b, n, r: (b, n, 0))],
        out_shape=[jax.ShapeDtypeStruct((bsz, N_PAIRS, s, LANE), F32),
                   jax.ShapeDtypeStruct((bsz, s, LANE), F32),
                   jax.ShapeDtypeStruct((bsz, s, LANE), F32)],
        scratch_shapes=[pltpu.VMEM((2, N_HEADS, BLK, 2 * BLK), F32), pltpu.VMEM((2, N_HEADS, BLK, 2 * BLK), BF16)],
        compiler_params=_params("parallel", "parallel", "arbitrary"),
        name=f"attn_prompt_d{dil}",
    )(q, k, k, v, v, bias)


def _attn_sample_kernel(q_ref, kn_ref, vn_ref, kc_ref, vc_ref, b16_ref, b4_ref, b1_ref, bn_ref, o_ref):
    t_len = q_ref.shape[0]
    w = kc_ref.shape[-1]
    lt64 = _lane_lt64((t_len, LANE))
    pad = jnp.zeros((BLK - t_len, D_ATTN), F32)
    kn = jnp.concatenate([kn_ref[...], pad], axis=0).astype(BF16)
    vn = jnp.concatenate([vn_ref[...], pad], axis=0).astype(BF16)
    outs = []
    for jp in range(N_PAIRS):
        sl = slice(jp * LANE, (jp + 1) * LANE)
        qp = q_ref[:, sl]
        qm = jnp.concatenate([jnp.where(lt64, qp, 0.0), jnp.where(lt64, 0.0, qp)], axis=0).astype(BF16)
        kt = kc_ref[2 * jp:2 * jp + 2].reshape(2 * HEAD_DIM, w).astype(BF16)
        vt = vc_ref[2 * jp:2 * jp + 2].reshape(2 * HEAD_DIM, w).astype(BF16)
        s = _dot(qm, kt)
        s_new = _nt_dot(qm, kn[:, sl])
        w4, w1 = w - 4 * BLK, w - BLK
        tiles = [s + b16_ref[jp], s[:, w4:] + b4_ref[jp], s[:, w1:] + b1_ref[jp]]
        tiles += [s_new + bn_ref[br, jp] for br in range(len(DILATIONS))]
        m = functools.reduce(jnp.maximum, [jnp.max(t, axis=-1, keepdims=True) for t in tiles])
        ps = [jnp.exp(t - m) for t in tiles]
        den = functools.reduce(lambda a, b: a + b, [jnp.sum(p, axis=-1, keepdims=True) for p in ps])
        p16, p4, p1 = ps[0], ps[1], ps[2]
        p_cache = jnp.concatenate([p16[:, :w4], p16[:, w4:w1] + p4[:, :w1 - w4],
                                   p16[:, w1:] + p4[:, w1 - w4:] + p1], axis=1)
        p_new = ps[3] + ps[4] + ps[5]
        o2 = (_nt_dot(p_cache.astype(BF16), vt) + _dot(p_new.astype(BF16), vn[:, sl])) / den
        outs.append(jnp.where(lt64, o2[0:t_len], o2[t_len:2 * t_len]))
    o_ref[...] = jnp.concatenate(outs, axis=1).astype(o_ref.dtype)


def _attn_sample(q, kn, vn, kc, vc, b16, b4, b1, bn):
    ns, t_len, _ = q.shape
    w = kc.shape[-1]
    assert w == MAX_DISTANCE and t_len == SUBLANE
    tok = pl.BlockSpec((None, t_len, D_ATTN), lambda s: (s, 0, 0))
    cache = pl.BlockSpec((None, N_HEADS, HEAD_DIM, w), lambda s: (s, 0, 0, 0))
    full = lambda a: pl.BlockSpec(a.shape, lambda s: (0,) * a.ndim)
    return pl.pallas_call(
        _attn_sample_kernel,
        grid=(ns,),
        in_specs=[tok, tok, tok, cache, cache, full(b16), full(b4), full(b1), full(bn)],
        out_specs=tok,
        out_shape=jax.ShapeDtypeStruct((ns, t_len, D_ATTN), BF16),
        compiler_params=_params("parallel"),
        name="attn_sample",
    )(q, kn, vn, kc, vc, b16, b4, b1, bn)


def _mix_ffn_kernel(*refs, n_parts, seg_in_tile, n_tiles):
    it = iter(refs)
    x_ref = next(it)
    parts = [(next(it), next(it), next(it)) for _ in range(n_parts)] if n_parts > 1 else [(next(it),)]
    yssm_ref, p_ref, fpre_ref = next(it), next(it), next(it)
    (expand_ref, wout_ref, gffn_ref, wup_ref, fcw_ref, fcb_ref, wdown_ref, wple_ref, gple_ref,
     wgate_ref, gfin_ref) = (next(it) for _ in range(11))
    y_ref, uout_ref = next(it), next(it)
    act_ref, tail_ref = next(it), next(it)
    tm = x_ref.shape[0]

    if n_parts > 1:
        maxes = [max_ref[...] for _, max_ref, _ in parts]
        mx = functools.reduce(jnp.maximum, maxes)
        ws = [jnp.exp(m - mx) for m in maxes]
        den = functools.reduce(lambda a, b: a + b, [w * den_ref[...] for w, (_, _, den_ref) in zip(ws, parts)])
        attn = None
        for (o_ref, _, _), w in zip(parts, ws):
            wn = w / den
            hi = wn.astype(BF16)
            lo = (wn - hi.astype(F32)).astype(BF16)
            wexp = _dot(jnp.concatenate([hi, lo], axis=1), expand_ref[...])
            o = jnp.concatenate([o_ref[jp] for jp in range(N_PAIRS)], axis=1)
            attn = wexp * o if attn is None else attn + wexp * o
        attn = attn.astype(BF16)
    else:
        attn = parts[0][0][...]
    h1 = x_ref[...] + _dot(attn, wout_ref[0:D_ATTN, :]) + _dot(yssm_ref[...], wout_ref[D_ATTN:, :])

    hn = _rms(h1, gffn_ref[...]).astype(BF16)

    if not seg_in_tile:
        t = pl.program_id(1)

        @pl.when(t == 0)
        def _init():
            tail_ref[...] = jnp.zeros(tail_ref.shape, F32)
            tail_ref[SUBLANE - (FFN_CONV - 1):SUBLANE, :] = fpre_ref[...]

        rin8 = lax.broadcasted_iota(jnp.int32, (SUBLANE, FF_CHUNK), 0)
    else:
        rin = lax.broadcasted_iota(jnp.int32, (tm, FF_CHUNK), 0) & (SUBLANE - 1)

    def conv(u, c0):
        cs = slice(c0, c0 + FF_CHUNK)
        if seg_in_tile:
            pre = fpre_ref[:, cs]
            um1 = jnp.where(rin == 0, pltpu.roll(pre, tm - 1, 0), pltpu.roll(u, 1, 0))
            um2 = jnp.where(rin < 2, pre, pltpu.roll(u, 2, 0))
            uout_ref[:, cs] = u
        else:
            tail = tail_ref[:, cs]
            tail_ref[:, cs] = u[tm - SUBLANE:tm, :]
            r1, r2 = pltpu.roll(u, 1, 0), pltpu.roll(u, 2, 0)
            um1 = jnp.concatenate([jnp.where(rin8 < 1, pltpu.roll(tail, 1, 0), r1[0:SUBLANE]), r1[SUBLANE:]], axis=0)
            um2 = jnp.concatenate([jnp.where(rin8 < 2, pltpu.roll(tail, 2, 0), r2[0:SUBLANE]), r2[SUBLANE:]], axis=0)
        return fcb_ref[:, cs] + fcw_ref[0:1, cs] * um2 + fcw_ref[1:2, cs] * um1 + fcw_ref[2:3, cs] * u

    for j in range(D_FF // FF_CHUNK):
        c0 = j * FF_CHUNK
        u_gate = conv(_dot(hn, wup_ref[:, c0:c0 + FF_CHUNK]), c0)
        u_lin = conv(_dot(hn, wup_ref[:, D_FF + c0:D_FF + c0 + FF_CHUNK]), D_FF + c0)
        act_ref[:, c0:c0 + FF_CHUNK] = (_silu(u_gate) * u_lin).astype(BF16)
    h2 = h1 + _dot(act_ref[...], wdown_ref[...])

    e = _rms(_dot(p_ref[...].astype(BF16), wple_ref[...]), gple_ref[...])
    h3 = h2 + jax.nn.sigmoid(_dot(h2.astype(BF16), wgate_ref[...])) * e
    y_ref[...] = _rms(h3, gfin_ref[...])

    if not seg_in_tile:
        @pl.when(pl.program_id(1) == n_tiles - 1)
        def _fin():
            uout_ref[...] = tail_ref[...]


def _mix_ffn(x, parts, yssm, p, fpre, weights, *, tm, seg_in_tile):
    ns, l, _ = x.shape
    n_tiles = l // tm
    n_parts = len(parts)
    tile = lambda s, t: (s, t, 0)
    seq = lambda s, t: (s, 0, 0)
    fixed = lambda s, t: (0, 0)
    in_specs = [pl.BlockSpec((None, tm, D_MODEL), tile)]
    args = [x]
    for part in parts:
        if n_parts > 1:
            in_specs += [pl.BlockSpec((None, N_PAIRS, tm, LANE), lambda s, t: (s, 0, t, 0)),
                         pl.BlockSpec((None, tm, LANE), tile), pl.BlockSpec((None, tm, LANE), tile)]
            args += list(part)
        else:
            in_specs.append(pl.BlockSpec((None, tm, D_ATTN), tile))
            args.append(part[0])
    in_specs += [pl.BlockSpec((None, tm, D_SSM), tile), pl.BlockSpec((None, tm, D_PLE), tile)]
    args += [yssm, p]
    if seg_in_tile:
        in_specs.append(pl.BlockSpec((None, tm, 2 * D_FF), tile))
        uout_spec = pl.BlockSpec((None, tm, 2 * D_FF), tile)
        uout_shape = jax.ShapeDtypeStruct((ns, l, 2 * D_FF), F32)
    else:
        in_specs.append(pl.BlockSpec((None, FFN_CONV - 1, 2 * D_FF), seq))
        uout_spec = pl.BlockSpec((None, SUBLANE, 2 * D_FF), seq)
        uout_shape = jax.ShapeDtypeStruct((ns, SUBLANE, 2 * D_FF), F32)
    args.append(fpre)
    for w in weights:
        in_specs.append(pl.BlockSpec(w.shape, fixed))
        args.append(w)
    kern = functools.partial(_mix_ffn_kernel, n_parts=n_parts, seg_in_tile=seg_in_tile, n_tiles=n_tiles)
    return pl.pallas_call(
        kern,
        grid=(ns, n_tiles),
        in_specs=in_specs,
        out_specs=[pl.BlockSpec((None, tm, D_MODEL), tile), uout_spec],
        out_shape=[jax.ShapeDtypeStruct((ns, l, D_MODEL), F32), uout_shape],
        scratch_shapes=[pltpu.VMEM((tm, D_FF), BF16),
                        pltpu.VMEM((SUBLANE, 2 * D_FF), F32)],
        compiler_params=_params("parallel", "arbitrary"),
        name="mix_ffn_seg" if seg_in_tile else "mix_ffn",
    )(*args)


def _pad_lanes(v, width=LANE):
    return jnp.pad(v.astype(F32), (0, width - v.shape[0]))[None, :]


def kernel(x_prompt, x_sample, p_prompt, p_sample, cache_k, cache_v, state_ssm, state_conv, state_ffn_conv,
           rel_bias, g_mix, w_in, conv_w, conv_b, dt_bias, a_log, d_skip, g_ssm, w_out, g_ffn, w_up,
           ffn_conv_w, ffn_conv_b, w_down, w_ple_proj, g_ple, w_ple_gate, g_final):
    assert w_in.shape[0] == 1, "one layer"
    bp, s, _ = x_prompt.shape
    nsamp, t_len, _ = x_sample.shape
    n_keep = min(MAX_DISTANCE, s)

    w_main = jnp.pad(w_in[0], ((0, 0), (0, IN_MAIN - w_in.shape[2]))).astype(BF16)
    gmix = g_mix[0][None, :]
    ssd_params = (conv_w[0], conv_b[0][None, :], _pad_lanes(dt_bias[0]), _pad_lanes(a_log[0]),
                  jnp.repeat(d_skip[0], SSM_HEAD_DIM)[None, :], g_ssm[0][None, :])
    expand = (np.arange(2 * LANE)[:, None] % LANE == (np.arange(D_ATTN)[None, :] // HEAD_DIM)).astype(np.float32)
    ffn_weights = (jnp.asarray(expand, BF16), w_out[0].astype(BF16), g_ffn[0][None, :], w_up[0].astype(BF16),
                   ffn_conv_w[0], ffn_conv_b[0][None, :], w_down[0].astype(BF16), w_ple_proj[0].astype(BF16),
                   g_ple[0][None, :], w_ple_gate[0].astype(BF16), g_final[None, :])
    bias_p, b16, b4, b1, bn = _bias_tables(rel_bias, t_len)

    (q1, k1, v1, q4, k4, v4, q16, k16, v16, kt, vt, z, xbc, dt) = _inproj(x_prompt, gmix, w_main, tm=512, n_keep=n_keep)
    yssm_p, ssm_p = _ssd(xbc, z, dt,
                         jnp.zeros((bp, SSM_CONV - 1, CONV_DIM), F32),
                         jnp.zeros((bp, N_SSM_HEADS, SSM_HEAD_DIM, D_STATE), F32),
                         *ssd_params, valid_len=SSD_CHUNK)
    parts = [_attn_prompt_branch(q1[:, None], k1[:, None], v1[:, None], bias_p[0], 1),
             _attn_prompt_branch(q4, k4, v4, bias_p[1], 4),
             _attn_prompt_branch(q16, k16, v16, bias_p[2], 16)]
    y_prompt, tail_p = _mix_ffn(x_prompt, parts, yssm_p, p_prompt[0],
                                jnp.zeros((bp, FFN_CONV - 1, 2 * D_FF), F32), ffn_weights,
                                tm=512, seg_in_tile=False)
    k_prompt = jnp.transpose(kt, (0, 3, 1, 2))[None]
    v_prompt = jnp.transpose(vt, (0, 3, 1, 2))[None]
    conv_prompt = xbc[:, s - (SSM_CONV - 1):][None]
    ffn_conv_prompt = tail_p[:, SUBLANE - (FFN_CONV - 1):][None]

    rows = nsamp * t_len
    qs, ks, vs, zs, xbcs, dts = _inproj(x_sample.reshape(1, rows, D_MODEL), gmix, w_main, tm=rows, n_keep=0)
    s3 = lambda a: a.reshape(nsamp, t_len, a.shape[-1])
    pad_t = lambda a: jnp.pad(s3(a), ((0, 0), (0, SSD_CHUNK - t_len), (0, 0)))
    yssm_s, ssm_s = _ssd(pad_t(xbcs), pad_t(zs), pad_t(dts), state_conv[0], state_ssm[0],
                         *ssd_params, valid_len=t_len)
    attn_s = _attn_sample(s3(qs), s3(ks), s3(vs),
                          jnp.transpose(cache_k[0], (0, 2, 3, 1)), jnp.transpose(cache_v[0], (0, 2, 3, 1)),
                          b16, b4, b1, bn)
    fpre = jnp.pad(state_ffn_conv[0], ((0, 0), (0, t_len - (FFN_CONV - 1)), (0, 0))).reshape(1, rows, 2 * D_FF)
    y_s, u_s = _mix_ffn(x_sample.reshape(1, rows, D_MODEL), [(attn_s.reshape(1, rows, D_ATTN),)],
                        yssm_s[:, :t_len].reshape(1, rows, D_SSM), p_sample[0].reshape(1, rows, D_PLE),
                        fpre, ffn_weights, tm=rows, seg_in_tile=True)
    y_sample = y_s.reshape(nsamp, t_len, D_MODEL)
    k_sample = ks.reshape(1, nsamp, t_len, N_HEADS, HEAD_DIM)
    v_sample = vs.reshape(1, nsamp, t_len, N_HEADS, HEAD_DIM)
    conv_sample = s3(xbcs)[:, t_len - (SSM_CONV - 1):][None]
    ffn_conv_sample = u_s.reshape(nsamp, t_len, 2 * D_FF)[:, t_len - (FFN_CONV - 1):][None]

    return (y_prompt, y_sample, k_prompt, v_prompt, k_sample, v_sample,
            ssm_p[None], ssm_s[None], conv_prompt, conv_sample, ffn_conv_prompt, ffn_conv_sample)
```

```python
import functools
import math

import numpy as np
import jax
import jax.numpy as jnp
from jax import lax
from jax.experimental import pallas as pl
from jax.experimental.pallas import tpu as pltpu

F32 = jnp.float32
BF16 = jnp.bfloat16

D_MODEL = 1024
HEAD_DIM = 64
N_HEADS = 8
D_ATTN = N_HEADS * HEAD_DIM
N_PAIRS = N_HEADS // 2
DILATIONS = (1, 4, 16)
N_STEPS = 128
BLK = 128
N_BUCKETS = 32
MAX_DISTANCE = 2048
D_SSM = 512
N_SSM_HEADS = 8
SSM_HEAD_DIM = 64
D_STATE = 128
N_SSM_GROUPS = 2
SSM_CONV = 4
CONV_DIM = D_SSM + 2 * N_SSM_GROUPS * D_STATE
SSD_CHUNK = 128
D_FF = 2816
FFN_CONV = 3
D_PLE = 256
EPS = 1e-6
NEG = -1e30

LANE = 128
SUBLANE = 8
FF_CHUNK = 256
VMEM_LIMIT = 56 * 1024 * 1024

O_Q, O_K, O_V, O_Z, O_XBC, O_DT = 0, 512, 1024, 1536, 2048, 3072
IN_MAIN = O_DT + LANE


def _rel_bucket_np(dist):
    dist = np.asarray(dist, np.int32)
    max_exact = N_BUCKETS // 2
    d = np.maximum(dist, 1).astype(np.float32)
    large = max_exact + (np.log(d / np.float32(max_exact)) / np.float32(math.log(MAX_DISTANCE / max_exact))
                         * np.float32(N_BUCKETS - max_exact)).astype(np.int32)
    large = np.minimum(large, N_BUCKETS - 1)
    return np.where(dist < max_exact, dist, large)


def _nt_dot(a, b):
    return lax.dot_general(a, b, (((1,), (1,)), ((), ())), preferred_element_type=F32)


def _dot(a, b):
    return jnp.dot(a, b, preferred_element_type=F32)


def _silu(x):
    return x * jax.nn.sigmoid(x)


def _softplus(x):
    return jnp.maximum(x, 0.0) + jnp.log1p(jnp.exp(-jnp.abs(x)))


def _rms(x, g):
    return x * lax.rsqrt(jnp.mean(x * x, axis=-1, keepdims=True) + EPS) * g


def _lane_lt64(shape):
    return lax.broadcasted_iota(jnp.int32, shape, len(shape) - 1) < HEAD_DIM


def _params(*sem):
    return pltpu.CompilerParams(dimension_semantics=sem, vmem_limit_bytes=VMEM_LIMIT)


def _bias_kernel(rb_ref, rbt_ref, ig_ref, i16_ref, i4_ref, i1_ref, in_ref, tp_ref, t16_ref, t4_ref, t1_ref, tn_ref):
    def lookup(idx, h):
        def body(b, acc):
            return jnp.where(idx == b, rb_ref[b, h], acc)
        return lax.fori_loop(0, N_BUCKETS, body, jnp.full(idx.shape, NEG, F32))

    for br in range(len(DILATIONS)):
        idx = jnp.broadcast_to(ig_ref[br], (N_HEADS, 2 * BLK))
        gen = jnp.full((N_HEADS, 2 * BLK), NEG, F32)
        for b in range(N_BUCKETS):
            gen = jnp.where(idx == b, jnp.broadcast_to(rbt_ref[:, b:b + 1], (N_HEADS, 2 * BLK)), gen)
        for h in range(N_HEADS):
            rows = jnp.broadcast_to(gen[h:h + 1, :], (BLK, 2 * BLK))
            tp_ref[br, h] = pltpu.roll(rows, 0, 1, stride=1, stride_axis=0)
    for h in range(N_HEADS):
        jp, half = divmod(h, 2)
        rs = slice(half * SUBLANE, (half + 1) * SUBLANE)
        t16_ref[jp, rs, :] = lookup(i16_ref[...], h)
        t4_ref[jp, rs, :] = lookup(i4_ref[...], h)
        t1_ref[jp, rs, :] = lookup(i1_ref[...], h)
        for br in range(len(DILATIONS)):
            tn_ref[br, jp, rs, :] = lookup(in_ref[br], h)


def _bucket_maps(t_len):
    j = BLK - np.arange(2 * BLK)[None, :]
    prompt = np.stack([np.where(j >= 0, _rel_bucket_np(np.clip(j, 0, N_STEPS) * d), -1) for d in DILATIONS])

    t = np.arange(t_len)[:, None]

    def sample_map(diff, dil):
        ok = (diff >= 0) & (diff % dil == 0) & (diff // dil <= N_STEPS)
        return np.where(ok, _rel_bucket_np(np.maximum(diff, 0)), -1).astype(np.int32)

    w = np.arange(MAX_DISTANCE)[None, :]
    cache = {d: sample_map(MAX_DISTANCE + t - w, d) for d in DILATIONS}
    g = np.arange(LANE)[None, :]
    new = np.stack([np.where(g < t_len, sample_map(t - g, d), -1) for d in DILATIONS])
    return (prompt.astype(np.int32), cache[16], cache[4][:, MAX_DISTANCE - 4 * BLK:],
            cache[1][:, MAX_DISTANCE - BLK:], new.astype(np.int32))


def _bias_tables(rel_bias, t_len):
    maps = _bucket_maps(t_len)
    nb = len(DILATIONS)
    shapes = [(nb, N_HEADS, BLK, 2 * BLK), (N_PAIRS, 2 * t_len, MAX_DISTANCE), (N_PAIRS, 2 * t_len, 4 * BLK),
              (N_PAIRS, 2 * t_len, BLK), (nb, N_PAIRS, 2 * t_len, LANE)]
    return pl.pallas_call(
        _bias_kernel,
        in_specs=[pl.BlockSpec(memory_space=pltpu.SMEM)] + [pl.BlockSpec(memory_space=pltpu.VMEM)] * 6,
        out_specs=[pl.BlockSpec(memory_space=pltpu.VMEM)] * 5,
        out_shape=[jax.ShapeDtypeStruct(s, F32) for s in shapes],
        compiler_params=pltpu.CompilerParams(vmem_limit_bytes=VMEM_LIMIT),
        name="bias_tables",
    )(rel_bias, rel_bias.T, *[jnp.asarray(m) for m in maps])


def _inproj_kernel(x_ref, g_ref, w_ref, q_ref, k_ref, v_ref, z_ref, xbc_ref, dt_ref):
    xn = _rms(x_ref[...], g_ref[...]).astype(BF16)

    def proj(lo, hi):
        return _dot(xn, w_ref[:, lo:hi])

    q_ref[...] = proj(O_Q, O_K) * (HEAD_DIM ** -0.5)
    k_ref[...] = proj(O_K, O_V)
    v_ref[...] = proj(O_V, O_Z)
    z_ref[...] = proj(O_Z, O_XBC)
    xbc_ref[...] = proj(O_XBC, O_DT)
    dt_ref[...] = proj(O_DT, IN_MAIN)


def _inproj(x, g_mix, w_main):
    rows = x.shape[0]
    full = lambda a: pl.BlockSpec(a.shape, lambda i: (0, 0))
    widths = (D_ATTN, D_ATTN, D_ATTN, D_SSM, CONV_DIM, LANE)
    return pl.pallas_call(
        _inproj_kernel,
        grid=(1,),
        in_specs=[full(x), full(g_mix), full(w_main)],
        out_specs=[pl.BlockSpec((rows, w), lambda i: (0, 0)) for w in widths],
        out_shape=[jax.ShapeDtypeStruct((rows, w), F32) for w in widths],
        compiler_params=_params("arbitrary"),
        name="inproj_sample",
    )(x, g_mix, w_main)


def _inproj_ssd_kernel(x_ref, g_ref, w_ref, prefix_ref, state0_ref, cw_ref, cb_ref, dtb_ref, alog_ref, dskip_ref,
                       gssm_ref, q1, k1, v1, q4, k4, v4, q16, k16, v16, kt_ref, vt_ref, y_ref, state_ref, ctail_ref,
                       perm_ref, mid_ref, xp_ref, z_scr, dt_scr, st_ref, *, keep_from, n_tiles):
    tm = x_ref.shape[0]
    L = SSD_CHUNK
    t = pl.program_id(1)

    @pl.when(t == 0)
    def _init():
        xp_ref[0:SUBLANE, :] = jnp.zeros((SUBLANE, CONV_DIM), F32)
        xp_ref[SUBLANE - (SSM_CONV - 1):SUBLANE, :] = prefix_ref[...]
        _load_state(state0_ref, st_ref)

    xn = _rms(x_ref[...], g_ref[...]).astype(BF16)

    def proj(lo, hi):
        return _dot(xn, w_ref[:, lo:hi])

    xp_ref[SUBLANE:SUBLANE + tm, :] = proj(O_XBC, O_DT)
    dt_scr[...] = proj(O_DT, IN_MAIN)
    z_scr[...] = proj(O_Z, O_XBC)

    def project_attn(i):
        lo, scale, nat, r4, r16 = ((O_Q, HEAD_DIM ** -0.5, q1, q4, q16), (O_K, None, k1, k4, k16),
                                   (O_V, None, v1, v4, v16))[i]
        val = proj(lo, lo + D_ATTN)
        if scale is not None:
            val = val * scale
        nat[...] = val.astype(BF16)
        for jp in range(N_PAIRS):
            perm_ref[i, jp] = val[:, jp * LANE:(jp + 1) * LANE]
        for ra in range(4):
            for jp in range(N_PAIRS):
                mid_ref[i, ra, jp] = perm_ref[i, jp, pl.ds(ra, tm // 4, stride=4), :]
            r4[ra] = jnp.concatenate([mid_ref[i, ra, jp] for jp in range(N_PAIRS)], axis=1).astype(BF16)
            for rb in range(4):
                rows = [mid_ref[i, ra, jp, pl.ds(rb, tm // 16, stride=4), :] for jp in range(N_PAIRS)]
                r16[ra + 4 * rb] = jnp.concatenate(rows, axis=1).astype(BF16)

    prm = (cw_ref, cb_ref, dtb_ref, alog_ref, dskip_ref, gssm_ref)
    n_chunks = tm // L
    assert n_chunks >= 3
    for c in range(n_chunks):
        base = SUBLANE + c * L - (SSM_CONV - 1)
        y = _ssd_chunk(lambda k, base=base: xp_ref[base + k:base + k + L, :],
                       z_scr[c * L:(c + 1) * L, :], dt_scr[c * L:(c + 1) * L, :], prm, st_ref, L)
        y_ref[c * L:(c + 1) * L, :] = y.astype(y_ref.dtype)
        if c < 3:
            project_attn(c)
    xp_ref[0:SUBLANE, :] = xp_ref[tm:tm + SUBLANE, :]

    @pl.when(t == n_tiles - 1)
    def _fin():
        _store_state(st_ref, state_ref)
        ctail_ref[...] = xp_ref[0:SUBLANE, :]

    @pl.when(t >= keep_from)
    def _keep():
        for i, out_ref in ((1, kt_ref), (2, vt_ref)):
            for jp in range(N_PAIRS):
                out_ref[2 * jp:2 * jp + 2] = perm_ref[i, jp].T.reshape(2, HEAD_DIM, tm)


def _inproj_ssd(x, g_mix, w_main, prefix, state0, ssd_params, *, tm, n_keep):
    ns, l, _ = x.shape
    nt = l // tm
    keep_from = (l - n_keep) // tm
    tile = lambda s, t: (s, t, 0)
    seq = lambda s, t: (s, 0, 0)
    fixed = lambda s, t: (0, 0)
    state_spec = pl.BlockSpec((None, N_SSM_HEADS, SSM_HEAD_DIM, D_STATE), lambda s, t: (s, 0, 0, 0))
    nat = lambda w: (pl.BlockSpec((None, tm, w), tile), jax.ShapeDtypeStruct((ns, l, w), BF16))
    outs = [nat(D_ATTN)] * 3
    for dil in (4, 16):
        outs += [(pl.BlockSpec((None, dil, tm // dil, D_ATTN), lambda s, t: (s, 0, t, 0)),
                  jax.ShapeDtypeStruct((ns, dil, l // dil, D_ATTN), BF16))] * 3
    outs += [(pl.BlockSpec((None, N_HEADS, HEAD_DIM, tm), lambda s, t: (s, 0, 0, jnp.maximum(t - keep_from, 0))),
              jax.ShapeDtypeStruct((ns, N_HEADS, HEAD_DIM, n_keep), F32))] * 2
    outs += [nat(D_SSM),
             (state_spec, jax.ShapeDtypeStruct((ns, N_SSM_HEADS, SSM_HEAD_DIM, D_STATE), F32)),
             (pl.BlockSpec((None, SUBLANE, CONV_DIM), seq), jax.ShapeDtypeStruct((ns, SUBLANE, CONV_DIM), F32))]
    return pl.pallas_call(
        functools.partial(_inproj_ssd_kernel, keep_from=keep_from, n_tiles=nt),
        grid=(ns, nt),
        in_specs=[pl.BlockSpec((None, tm, D_MODEL), tile),
                  pl.BlockSpec((1, D_MODEL), fixed),
                  pl.BlockSpec((D_MODEL, IN_MAIN), fixed),
                  pl.BlockSpec((None, SSM_CONV - 1, CONV_DIM), seq),
                  state_spec] + [pl.BlockSpec(p.shape, fixed) for p in ssd_params],
        out_specs=[o[0] for o in outs],
        out_shape=[o[1] for o in outs],
        scratch_shapes=[pltpu.VMEM((3, N_PAIRS, tm, LANE), F32),
                        pltpu.VMEM((3, 4, N_PAIRS, tm // 4, LANE), F32),
                        pltpu.VMEM((SUBLANE + tm, CONV_DIM), F32),
                        pltpu.VMEM((tm, D_SSM), F32),
                        pltpu.VMEM((tm, LANE), F32),
                        pltpu.VMEM((N_PAIRS, D_STATE, LANE), F32)],
        compiler_params=_params("parallel", "arbitrary"),
        name="inproj_ssd",
    )(x, g_mix, w_main, prefix, state0, *ssd_params)


def _load_state(state0_ref, st_ref):
    for jp in range(N_PAIRS):
        st_ref[jp] = state0_ref[2 * jp:2 * jp + 2].reshape(2 * SSM_HEAD_DIM, D_STATE).T


def _store_state(st_ref, state_ref):
    for jp in range(N_PAIRS):
        state_ref[2 * jp:2 * jp + 2] = st_ref[jp].T.reshape(2, SSM_HEAD_DIM, D_STATE)


def _ssd_chunk(xwin, z, dt_raw, prm, st_ref, valid_len):
    cw_ref, cb_ref, dtb_ref, alog_ref, dskip_ref, gssm_ref = prm
    L = SSD_CHUNK
    conv = cb_ref[...]
    for k in range(SSM_CONV):
        conv = conv + cw_ref[k:k + 1, :] * xwin(k)
    xc = _silu(conv)
    xs = xc[:, :D_SSM]
    gn = N_SSM_GROUPS * D_STATE

    row = lax.broadcasted_iota(jnp.int32, (L, L), 0)
    col = lax.broadcasted_iota(jnp.int32, (L, L), 1)
    tri = row >= col
    lt64 = col < SSM_HEAD_DIM

    dt = _softplus(dt_raw + dtb_ref[...])
    if valid_len < L:
        dt = jnp.where(row < valid_len, dt, 0.0)
    a = dt * (-jnp.exp(alog_ref[...]))
    acum = a
    shift = 1
    while shift < L:
        acum = acum + jnp.where(row >= shift, pltpu.roll(acum, shift, 0), 0.0)
        shift *= 2
    acum_t = acum.T
    dt_t = dt.T
    e_slab = jnp.exp(acum)
    de_slab = jnp.exp(acum[L - 1:L, :] - acum) * dt

    def head_terms(h, cb):
        colb = jnp.broadcast_to(acum[:, h:h + 1], (L, L))
        rowb = jnp.broadcast_to(acum_t[h:h + 1, :], (L, L))
        decay = jnp.exp(jnp.where(tri, colb - rowb, NEG))
        m = (cb * decay * jnp.broadcast_to(dt_t[h:h + 1, :], (L, L))).astype(BF16)
        e = jnp.broadcast_to(e_slab[:, h:h + 1], (L, L))
        return m, e, jnp.broadcast_to(de_slab[:, h:h + 1], (L, L)), e[L - 1:L, :]

    pairs = []
    for g in range(N_SSM_GROUPS):
        bg = xc[:, D_SSM + g * D_STATE:D_SSM + (g + 1) * D_STATE]
        cg = xc[:, D_SSM + gn + g * D_STATE:D_SSM + gn + (g + 1) * D_STATE].astype(BF16)
        cb = _nt_dot(cg, bg.astype(BF16))
        bg_t = bg.T.astype(BF16)
        for i in range(N_PAIRS // N_SSM_GROUPS):
            jp = g * (N_PAIRS // N_SSM_GROUPS) + i
            xpair = xs[:, jp * LANE:(jp + 1) * LANE]
            xpair_b = xpair.astype(BF16)
            m_a, e_a, de_a, el_a = head_terms(2 * jp, cb)
            m_b, e_b, de_b, el_b = head_terms(2 * jp + 1, cb)
            st = st_ref[jp]
            y_diag = jnp.where(lt64, _dot(m_a, xpair_b), _dot(m_b, xpair_b))
            y_off = _dot(cg, st.astype(BF16)) * jnp.where(lt64, e_a, e_b)
            xd = (xpair * jnp.where(lt64, de_a, de_b)).astype(BF16)
            st_ref[jp] = st * jnp.where(lt64[0:1, :], el_a, el_b) + _dot(bg_t, xd)
            pairs.append(y_diag + y_off + dskip_ref[:, jp * LANE:(jp + 1) * LANE] * xpair)
    y = jnp.concatenate(pairs, axis=1)

    yf = y * _silu(z)
    gw = D_SSM // N_SSM_GROUPS
    normed = []
    for g in range(N_SSM_GROUPS):
        seg = yf[:, g * gw:(g + 1) * gw]
        normed.append(seg * lax.rsqrt(jnp.mean(seg * seg, axis=-1, keepdims=True) + EPS))
    return jnp.concatenate(normed, axis=1) * gssm_ref[...]


def _ssd_short_kernel(xbc_ref, z_ref, dt_ref, prefix_ref, state0_ref, cw_ref, cb_ref, dtb_ref, alog_ref,
                      dskip_ref, gssm_ref, y_ref, state_ref, xp_ref, st_ref):
    t_len = xbc_ref.shape[0]
    L = SSD_CHUNK
    lead = SUBLANE - (SSM_CONV - 1)
    xp_ref[0:lead, :] = jnp.zeros((lead, CONV_DIM), F32)
    xp_ref[lead:SUBLANE, :] = prefix_ref[...]
    xp_ref[SUBLANE:SUBLANE + t_len, :] = xbc_ref[...]
    xp_ref[SUBLANE + t_len:SUBLANE + L, :] = jnp.zeros((L - t_len, CONV_DIM), F32)
    _load_state(state0_ref, st_ref)
    pad = lambda a: jnp.concatenate([a, jnp.zeros((L - t_len, a.shape[1]), F32)], axis=0)
    y = _ssd_chunk(lambda k: xp_ref[lead + k:lead + k + L, :], pad(z_ref[...]), pad(dt_ref[...]),
                   (cw_ref, cb_ref, dtb_ref, alog_ref, dskip_ref, gssm_ref), st_ref, t_len)
    y_ref[...] = y[0:t_len].astype(y_ref.dtype)
    _store_state(st_ref, state_ref)


def _ssd_short(xbc, z, dt, prefix, state0, ssd_params):
    ns, t_len, _ = xbc.shape
    seq = lambda s: (s, 0, 0)
    state_spec = pl.BlockSpec((None, N_SSM_HEADS, SSM_HEAD_DIM, D_STATE), lambda s: (s, 0, 0, 0))
    return pl.pallas_call(
        _ssd_short_kernel,
        grid=(ns,),
        in_specs=[pl.BlockSpec((None, t_len, CONV_DIM), seq),
                  pl.BlockSpec((None, t_len, D_SSM), seq),
                  pl.BlockSpec((None, t_len, LANE), seq),
                  pl.BlockSpec((None, SSM_CONV - 1, CONV_DIM), seq),
                  state_spec] + [pl.BlockSpec(p.shape, lambda s: (0, 0)) for p in ssd_params],
        out_specs=[pl.BlockSpec((None, t_len, D_SSM), seq), state_spec],
        out_shape=[jax.ShapeDtypeStruct((ns, t_len, D_SSM), BF16),
                   jax.ShapeDtypeStruct((ns, N_SSM_HEADS, SSM_HEAD_DIM, D_STATE), F32)],
        scratch_shapes=[pltpu.VMEM((SUBLANE + SSD_CHUNK, CONV_DIM), F32),
                        pltpu.VMEM((N_PAIRS, D_STATE, LANE), F32)],
        compiler_params=_params("parallel"),
        name="ssd_short",
    )(xbc, z, dt, prefix, state0, *ssd_params)


ATTN_UNITS = 4


def _attn_prompt_kernel(q_ref, kp_ref, kc_ref, vp_ref, vc_ref, bias_ref, o_ref, max_ref, den_ref, s_ref, p_ref, *, dil):
    units = q_ref.shape[0]
    blk0 = pl.program_id(1) == 0
    lt64 = _lane_lt64((BLK, LANE))
    lane = lax.broadcasted_iota(jnp.int32, (BLK, LANE), 1)
    zero = jnp.zeros((BLK, LANE), BF16)

    def prev_of(g, p_ref_, c_ref_):
        if dil == 1:
            return (p_ref_.at[0], blk0) if g == 0 else (c_ref_.at[g - 1], None)
        return p_ref_.at[g], blk0

    def rows_of(g):
        if dil == 1:
            return slice(g * BLK, (g + 1) * BLK)
        return pl.ds(pl.program_id(2) * units + g, BLK, stride=dil)

    def scores(g, slot):
        kprev, masked = prev_of(g, kp_ref, kc_ref)
        for jp in range(N_PAIRS):
            sl = slice(jp * LANE, (jp + 1) * LANE)
            qp = q_ref[g, :, sl]
            kcat = jnp.concatenate([kprev[:, sl], kc_ref[g, :, sl]], axis=0)
            for half in range(2):
                h = 2 * jp + half
                qm = jnp.where(lt64 if half == 0 else ~lt64, qp, zero)
                bias = bias_ref[h]
                if masked is not None:
                    bias = jnp.concatenate([jnp.where(masked, NEG, bias[:, 0:BLK]), bias[:, BLK:]], axis=1)
                s_ref[slot, h] = _nt_dot(qm, kcat) + bias

    def softmax(g, slot):
        s = s_ref[slot]
        m = jnp.max(s, axis=-1, keepdims=True)
        p = jnp.exp(s - m)
        den = jnp.sum(p, axis=-1, keepdims=True)
        p_ref[slot] = p.astype(BF16)
        m_slab = jnp.zeros((BLK, LANE), F32)
        den_slab = jnp.ones((BLK, LANE), F32)
        for h in range(N_HEADS):
            m_slab = jnp.where(lane == h, m[h], m_slab)
            den_slab = jnp.where(lane == h, den[h], den_slab)
        max_ref[rows_of(g), :] = m_slab
        den_ref[rows_of(g), :] = den_slab

    def values(g, slot):
        vprev, _ = prev_of(g, vp_ref, vc_ref)
        for jp in range(N_PAIRS):
            sl = slice(jp * LANE, (jp + 1) * LANE)
            vcat = jnp.concatenate([vprev[:, sl], vc_ref[g, :, sl]], axis=0)
            pv = [_dot(p_ref[slot, 2 * jp + half], vcat) for half in range(2)]
            o_ref[jp, rows_of(g), :] = jnp.where(lt64, pv[0], pv[1])

    scores(0, 0)
    for g in range(units):
        softmax(g, g % 2)
        if g + 1 < units:
            scores(g + 1, (g + 1) % 2)
        values(g, g % 2)


def _attn_prompt_branch(q, k, v, bias, dil):
    bsz, _, l, _ = q.shape
    s = l * dil
    u = ATTN_UNITS
    if dil == 1:
        q, k, v = (t.reshape(bsz, l // BLK, BLK, D_ATTN) for t in (q, k, v))
        grid = (bsz, l // BLK // u, 1)
        cur = pl.BlockSpec((None, u, BLK, D_ATTN), lambda b, n, r: (b, n, 0, 0))
        prev = pl.BlockSpec((None, 1, BLK, D_ATTN), lambda b, n, r: (b, jnp.maximum(n * u - 1, 0), 0, 0))
        rows = u * BLK
    else:
        grid = (bsz, l // BLK, dil // u)
        cur = pl.BlockSpec((None, u, BLK, D_ATTN), lambda b, n, r: (b, r, n, 0))
        prev = pl.BlockSpec((None, u, BLK, D_ATTN), lambda b, n, r: (b, r, jnp.maximum(n - 1, 0), 0))
        rows = dil * BLK
    return pl.pallas_call(
        functools.partial(_attn_prompt_kernel, dil=dil),
        grid=grid,
        in_specs=[cur, prev, cur, prev, cur,
                  pl.BlockSpec((N_HEADS, BLK, 2 * BLK), lambda b, n, r: (0, 0, 0))],
        out_specs=[pl.BlockSpec((None, N_PAIRS, rows, LANE), lambda b, n, r: (b, 0, n, 0)),
                   pl.BlockSpec((None, rows, LANE), lambda b, n, r: (b, n, 0)),
                   pl.BlockSpec((None, rows, LANE), lambda b, n, r: (b, n, 0))],
        out_shape=[jax.ShapeDtypeStruct((bsz, N_PAIRS, s, LANE), F32),
                   jax.ShapeDtypeStruct((bsz, s, LANE), F32),
                   jax.ShapeDtypeStruct((bsz, s, LANE), F32)],
        scratch_shapes=[pltpu.VMEM((2, N_HEADS, BLK, 2 * BLK), F32), pltpu.VMEM((2, N_HEADS, BLK, 2 * BLK), BF16)],
        compiler_params=_params("parallel", "parallel", "arbitrary"),
        name=f"attn_prompt_d{dil}",
    )(q, k, k, v, v, bias)


def _attn_sample_kernel(q_ref, kn_ref, vn_ref, kc_ref, vc_ref, b16_ref, b4_ref, b1_ref, bn_ref, o_ref):
    t_len = q_ref.shape[0]
    w = kc_ref.shape[-1]
    lt64 = _lane_lt64((t_len, LANE))
    pad = jnp.zeros((BLK - t_len, D_ATTN), F32)
    kn = jnp.concatenate([kn_ref[...], pad], axis=0).astype(BF16)
    vn = jnp.concatenate([vn_ref[...], pad], axis=0).astype(BF16)
    outs = []
    for jp in range(N_PAIRS):
        sl = slice(jp * LANE, (jp + 1) * LANE)
        qp = q_ref[:, sl]
        qm = jnp.concatenate([jnp.where(lt64, qp, 0.0), jnp.where(lt64, 0.0, qp)], axis=0).astype(BF16)
        kt = kc_ref[2 * jp:2 * jp + 2].reshape(2 * HEAD_DIM, w).astype(BF16)
        vt = vc_ref[2 * jp:2 * jp + 2].reshape(2 * HEAD_DIM, w).astype(BF16)
        s = _dot(qm, kt)
        s_new = _nt_dot(qm, kn[:, sl])
        w4, w1 = w - 4 * BLK, w - BLK
        tiles = [s + b16_ref[jp], s[:, w4:] + b4_ref[jp], s[:, w1:] + b1_ref[jp]]
        tiles += [s_new + bn_ref[br, jp] for br in range(len(DILATIONS))]
        m = functools.reduce(jnp.maximum, [jnp.max(t, axis=-1, keepdims=True) for t in tiles])
        ps = [jnp.exp(t - m) for t in tiles]
        den = functools.reduce(lambda a, b: a + b, [jnp.sum(p, axis=-1, keepdims=True) for p in ps])
        p16, p4, p1 = ps[0], ps[1], ps[2]
        p_cache = jnp.concatenate([p16[:, :w4], p16[:, w4:w1] + p4[:, :w1 - w4],
                                   p16[:, w1:] + p4[:, w1 - w4:] + p1], axis=1)
        p_new = ps[3] + ps[4] + ps[5]
        o2 = (_nt_dot(p_cache.astype(BF16), vt) + _dot(p_new.astype(BF16), vn[:, sl])) / den
        outs.append(jnp.where(lt64, o2[0:t_len], o2[t_len:2 * t_len]))
    o_ref[...] = jnp.concatenate(outs, axis=1).astype(o_ref.dtype)


def _attn_sample(q, kn, vn, kc, vc, b16, b4, b1, bn):
    ns, t_len, _ = q.shape
    w = kc.shape[-1]
    assert w == MAX_DISTANCE and t_len == SUBLANE
    tok = pl.BlockSpec((None, t_len, D_ATTN), lambda s: (s, 0, 0))
    cache = pl.BlockSpec((None, N_HEADS, HEAD_DIM, w), lambda s: (s, 0, 0, 0))
    full = lambda a: pl.BlockSpec(a.shape, lambda s: (0,) * a.ndim)
    return pl.pallas_call(
        _attn_sample_kernel,
        grid=(ns,),
        in_specs=[tok, tok, tok, cache, cache, full(b16), full(b4), full(b1), full(bn)],
        out_specs=tok,
        out_shape=jax.ShapeDtypeStruct((ns, t_len, D_ATTN), BF16),
        compiler_params=_params("parallel"),
        name="attn_sample",
    )(q, kn, vn, kc, vc, b16, b4, b1, bn)


def _mix_ffn_kernel(*refs, n_parts, seg_in_tile, n_tiles):
    it = iter(refs)
    x_ref = next(it)
    parts = [(next(it), next(it), next(it)) for _ in range(n_parts)] if n_parts > 1 else [(next(it),)]
    yssm_ref, p_ref, fpre_ref = next(it), next(it), next(it)
    (expand_ref, wout_ref, gffn_ref, wup_ref, fcw_ref, fcb_ref, wdown_ref, wple_ref, gple_ref,
     wgate_ref, gfin_ref) = (next(it) for _ in range(11))
    y_ref, uout_ref = next(it), next(it)
    act_ref, tail_ref = next(it), next(it)
    tm = x_ref.shape[0]

    if n_parts > 1:
        maxes = [max_ref[...] for _, max_ref, _ in parts]
        mx = functools.reduce(jnp.maximum, maxes)
        ws = [jnp.exp(m - mx) for m in maxes]
        den = functools.reduce(lambda a, b: a + b, [w * den_ref[...] for w, (_, _, den_ref) in zip(ws, parts)])
        attn = None
        for (o_ref, _, _), w in zip(parts, ws):
            wn = w / den
            hi = wn.astype(BF16)
            lo = (wn - hi.astype(F32)).astype(BF16)
            wexp = _dot(jnp.concatenate([hi, lo], axis=1), expand_ref[...])
            o = jnp.concatenate([o_ref[jp] for jp in range(N_PAIRS)], axis=1)
            attn = wexp * o if attn is None else attn + wexp * o
        attn = attn.astype(BF16)
    else:
        attn = parts[0][0][...]
    h1 = x_ref[...] + _dot(attn, wout_ref[0:D_ATTN, :]) + _dot(yssm_ref[...], wout_ref[D_ATTN:, :])

    hn = _rms(h1, gffn_ref[...]).astype(BF16)

    if not seg_in_tile:
        t = pl.program_id(1)

        @pl.when(t == 0)
        def _init():
            tail_ref[...] = jnp.zeros(tail_ref.shape, F32)
            tail_ref[SUBLANE - (FFN_CONV - 1):SUBLANE, :] = fpre_ref[...]

        rin8 = lax.broadcasted_iota(jnp.int32, (SUBLANE, FF_CHUNK), 0)
    else:
        rin = lax.broadcasted_iota(jnp.int32, (tm, FF_CHUNK), 0) & (SUBLANE - 1)

    def conv(u, c0):
        cs = slice(c0, c0 + FF_CHUNK)
        if seg_in_tile:
            pre = fpre_ref[:, cs]
            um1 = jnp.where(rin == 0, pltpu.roll(pre, tm - 1, 0), pltpu.roll(u, 1, 0))
            um2 = jnp.where(rin < 2, pre, pltpu.roll(u, 2, 0))
            uout_ref[:, cs] = u
        else:
            tail = tail_ref[:, cs]
            tail_ref[:, cs] = u[tm - SUBLANE:tm, :]
            r1, r2 = pltpu.roll(u, 1, 0), pltpu.roll(u, 2, 0)
            um1 = jnp.concatenate([jnp.where(rin8 < 1, pltpu.roll(tail, 1, 0), r1[0:SUBLANE]), r1[SUBLANE:]], axis=0)
            um2 = jnp.concatenate([jnp.where(rin8 < 2, pltpu.roll(tail, 2, 0), r2[0:SUBLANE]), r2[SUBLANE:]], axis=0)
        return fcb_ref[:, cs] + fcw_ref[0:1, cs] * um2 + fcw_ref[1:2, cs] * um1 + fcw_ref[2:3, cs] * u

    for j in range(D_FF // FF_CHUNK):
        c0 = j * FF_CHUNK
        u_gate = conv(_dot(hn, wup_ref[:, c0:c0 + FF_CHUNK]), c0)
        u_lin = conv(_dot(hn, wup_ref[:, D_FF + c0:D_FF + c0 + FF_CHUNK]), D_FF + c0)
        act_ref[:, c0:c0 + FF_CHUNK] = (_silu(u_gate) * u_lin).astype(BF16)
    h2 = h1 + _dot(act_ref[...], wdown_ref[...])

    e = _rms(_dot(p_ref[...].astype(BF16), wple_ref[...]), gple_ref[...])
    h3 = h2 + jax.nn.sigmoid(_dot(h2.astype(BF16), wgate_ref[...])) * e
    y_ref[...] = _rms(h3, gfin_ref[...])

    if not seg_in_tile:
        @pl.when(pl.program_id(1) == n_tiles - 1)
        def _fin():
            uout_ref[...] = tail_ref[...]


def _mix_ffn(x, parts, yssm, p, fpre, weights, *, tm, seg_in_tile):
    ns, l, _ = x.shape
    n_tiles = l // tm
    n_parts = len(parts)
    tile = lambda s, t: (s, t, 0)
    seq = lambda s, t: (s, 0, 0)
    fixed = lambda s, t: (0, 0)
    in_specs = [pl.BlockSpec((None, tm, D_MODEL), tile)]
    args = [x]
    for part in parts:
        if n_parts > 1:
            in_specs += [pl.BlockSpec((None, N_PAIRS, tm, LANE), lambda s, t: (s, 0, t, 0)),
                         pl.BlockSpec((None, tm, LANE), tile), pl.BlockSpec((None, tm, LANE), tile)]
            args += list(part)
        else:
            in_specs.append(pl.BlockSpec((None, tm, D_ATTN), tile))
            args.append(part[0])
    in_specs += [pl.BlockSpec((None, tm, D_SSM), tile), pl.BlockSpec((None, tm, D_PLE), tile)]
    args += [yssm, p]
    if seg_in_tile:
        in_specs.append(pl.BlockSpec((None, tm, 2 * D_FF), tile))
        uout_spec = pl.BlockSpec((None, tm, 2 * D_FF), tile)
        uout_shape = jax.ShapeDtypeStruct((ns, l, 2 * D_FF), F32)
    else:
        in_specs.append(pl.BlockSpec((None, FFN_CONV - 1, 2 * D_FF), seq))
        uout_spec = pl.BlockSpec((None, SUBLANE, 2 * D_FF), seq)
        uout_shape = jax.ShapeDtypeStruct((ns, SUBLANE, 2 * D_FF), F32)
    args.append(fpre)
    for w in weights:
        in_specs.append(pl.BlockSpec(w.shape, fixed))
        args.append(w)
    kern = functools.partial(_mix_ffn_kernel, n_parts=n_parts, seg_in_tile=seg_in_tile, n_tiles=n_tiles)
    return pl.pallas_call(
        kern,
        grid=(ns, n_tiles),
        in_specs=in_specs,
        out_specs=[pl.BlockSpec((None, tm, D_MODEL), tile), uout_spec],
        out_shape=[jax.ShapeDtypeStruct((ns, l, D_MODEL), F32), uout_shape],
        scratch_shapes=[pltpu.VMEM((tm, D_FF), BF16),
                        pltpu.VMEM((SUBLANE, 2 * D_FF), F32)],
        compiler_params=_params("parallel", "arbitrary"),
        name="mix_ffn_seg" if seg_in_tile else "mix_ffn",
    )(*args)


def _pad_lanes(v, width=LANE):
    return jnp.pad(v.astype(F32), (0, width - v.shape[0]))[None, :]


def kernel(x_prompt, x_sample, p_prompt, p_sample, cache_k, cache_v, state_ssm, state_conv, state_ffn_conv,
           rel_bias, g_mix, w_in, conv_w, conv_b, dt_bias, a_log, d_skip, g_ssm, w_out, g_ffn, w_up,
           ffn_conv_w, ffn_conv_b, w_down, w_ple_proj, g_ple, w_ple_gate, g_final):
    assert w_in.shape[0] == 1, "one layer"
    bp, s, _ = x_prompt.shape
    nsamp, t_len, _ = x_sample.shape
    n_keep = min(MAX_DISTANCE, s)

    w_main = jnp.pad(w_in[0], ((0, 0), (0, IN_MAIN - w_in.shape[2]))).astype(BF16)
    gmix = g_mix[0][None, :]
    ssd_params = (conv_w[0], conv_b[0][None, :], _pad_lanes(dt_bias[0]), _pad_lanes(a_log[0]),
                  jnp.repeat(d_skip[0], SSM_HEAD_DIM)[None, :], g_ssm[0][None, :])
    expand = (np.arange(2 * LANE)[:, None] % LANE == (np.arange(D_ATTN)[None, :] // HEAD_DIM)).astype(np.float32)
    ffn_weights = (jnp.asarray(expand, BF16), w_out[0].astype(BF16), g_ffn[0][None, :], w_up[0].astype(BF16),
                   ffn_conv_w[0], ffn_conv_b[0][None, :], w_down[0].astype(BF16), w_ple_proj[0].astype(BF16),
                   g_ple[0][None, :], w_ple_gate[0].astype(BF16), g_final[None, :])
    bias_p, b16, b4, b1, bn = _bias_tables(rel_bias, t_len)

    (q1, k1, v1, q4, k4, v4, q16, k16, v16, kt, vt, yssm_p, ssm_p, ctail_p) = _inproj_ssd(
        x_prompt, gmix, w_main,
        jnp.zeros((bp, SSM_CONV - 1, CONV_DIM), F32),
        jnp.zeros((bp, N_SSM_HEADS, SSM_HEAD_DIM, D_STATE), F32),
        ssd_params, tm=512, n_keep=n_keep)
    parts = [_attn_prompt_branch(q1[:, None], k1[:, None], v1[:, None], bias_p[0], 1),
             _attn_prompt_branch(q4, k4, v4, bias_p[1], 4),
             _attn_prompt_branch(q16, k16, v16, bias_p[2], 16)]
    y_prompt, tail_p = _mix_ffn(x_prompt, parts, yssm_p, p_prompt[0],
                                jnp.zeros((bp, FFN_CONV - 1, 2 * D_FF), F32), ffn_weights,
                                tm=512, seg_in_tile=False)
    k_prompt = jnp.transpose(kt, (0, 3, 1, 2))[None]
    v_prompt = jnp.transpose(vt, (0, 3, 1, 2))[None]
    conv_prompt = ctail_p[:, SUBLANE - (SSM_CONV - 1):][None]
    ffn_conv_prompt = tail_p[:, SUBLANE - (FFN_CONV - 1):][None]

    rows = nsamp * t_len
    qs, ks, vs, zs, xbcs, dts = _inproj(x_sample.reshape(rows, D_MODEL), gmix, w_main)
    s3 = lambda a: a.reshape(nsamp, t_len, a.shape[-1])
    yssm_s, ssm_s = _ssd_short(s3(xbcs), s3(zs), s3(dts), state_conv[0], state_ssm[0], ssd_params)
    attn_s = _attn_sample(s3(qs), s3(ks), s3(vs),
                          jnp.transpose(cache_k[0], (0, 2, 3, 1)), jnp.transpose(cache_v[0], (0, 2, 3, 1)),
                          b16, b4, b1, bn)
    fpre = jnp.pad(state_ffn_conv[0], ((0, 0), (0, t_len - (FFN_CONV - 1)), (0, 0))).reshape(1, rows, 2 * D_FF)
    y_s, u_s = _mix_ffn(x_sample.reshape(1, rows, D_MODEL), [(attn_s.reshape(1, rows, D_ATTN),)],
                        yssm_s.reshape(1, rows, D_SSM), p_sample[0].reshape(1, rows, D_PLE),
                        fpre, ffn_weights, tm=rows, seg_in_tile=True)
    y_sample = y_s.reshape(nsamp, t_len, D_MODEL)
    k_sample = ks.reshape(1, nsamp, t_len, N_HEADS, HEAD_DIM)
    v_sample = vs.reshape(1, nsamp, t_len, N_HEADS, HEAD_DIM)
    conv_sample = s3(xbcs)[:, t_len - (SSM_CONV - 1):][None]
    ffn_conv_sample = u_s.reshape(nsamp, t_len, 2 * D_FF)[:, t_len - (FFN_CONV - 1):][None]

    return (y_prompt, y_sample, k_prompt, v_prompt, k_sample, v_sample,
            ssm_p[None], ssm_s[None], conv_prompt, conv_sample, ffn_conv_prompt, ffn_conv_sample)
```

```python
import functools
import math

import numpy as np
import jax
import jax.numpy as jnp
from jax import lax
from jax.experimental import pallas as pl
from jax.experimental.pallas import tpu as pltpu

F32 = jnp.float32
BF16 = jnp.bfloat16

D_MODEL = 1024
HEAD_DIM = 64
N_HEADS = 8
D_ATTN = N_HEADS * HEAD_DIM
N_PAIRS = N_HEADS // 2
DILATIONS = (1, 4, 16)
N_STEPS = 128
BLK = 128
N_BUCKETS = 32
MAX_DISTANCE = 2048
D_SSM = 512
N_SSM_HEADS = 8
SSM_HEAD_DIM = 64
D_STATE = 128
N_SSM_GROUPS = 2
SSM_CONV = 4
CONV_DIM = D_SSM + 2 * N_SSM_GROUPS * D_STATE
SSD_CHUNK = 128
D_FF = 2816
FFN_CONV = 3
D_PLE = 256
EPS = 1e-6
NEG = -1e30
LOG2E = math.log2(math.e)

LANE = 128
SUBLANE = 8
FF_CHUNK = 256
VMEM_LIMIT = 56 * 1024 * 1024

O_Q, O_K, O_V, O_Z, O_XBC, O_DT = 0, 512, 1024, 1536, 2048, 3072
IN_MAIN = O_DT + LANE


def _rel_bucket_np(dist):
    dist = np.asarray(dist, np.int32)
    max_exact = N_BUCKETS // 2
    d = np.maximum(dist, 1).astype(np.float32)
    large = max_exact + (np.log(d / np.float32(max_exact)) / np.float32(math.log(MAX_DISTANCE / max_exact))
                         * np.float32(N_BUCKETS - max_exact)).astype(np.int32)
    large = np.minimum(large, N_BUCKETS - 1)
    return np.where(dist < max_exact, dist, large)


def _nt_dot(a, b):
    return lax.dot_general(a, b, (((1,), (1,)), ((), ())), preferred_element_type=F32)


def _dot(a, b):
    return jnp.dot(a, b, preferred_element_type=F32)


def _silu(x):
    return x * jax.nn.sigmoid(x)


def _softplus(x):
    return jnp.maximum(x, 0.0) + jnp.log1p(jnp.exp(-jnp.abs(x)))


def _rms(x, g):
    return x * lax.rsqrt(jnp.mean(x * x, axis=-1, keepdims=True) + EPS) * g


def _lane_lt64(shape):
    return lax.broadcasted_iota(jnp.int32, shape, len(shape) - 1) < HEAD_DIM


def _params(*sem):
    return pltpu.CompilerParams(dimension_semantics=sem, vmem_limit_bytes=VMEM_LIMIT)


def _bias_kernel(rb_ref, rbt_ref, ig_ref, i16_ref, i4_ref, i1_ref, in_ref, tp_ref, t16_ref, t4_ref, t1_ref, tn_ref):
    def lookup(idx, h):
        def body(b, acc):
            return jnp.where(idx == b, rb_ref[b, h], acc)
        return lax.fori_loop(0, N_BUCKETS, body, jnp.full(idx.shape, NEG, F32), unroll=True)

    for br in range(len(DILATIONS)):
        idx = jnp.broadcast_to(ig_ref[br], (N_HEADS, 2 * BLK))
        gen = jnp.full((N_HEADS, 2 * BLK), NEG, F32)
        for b in range(N_BUCKETS):
            gen = jnp.where(idx == b, jnp.broadcast_to(rbt_ref[:, b:b + 1], (N_HEADS, 2 * BLK)), gen)
        for h in range(N_HEADS):
            rows = jnp.broadcast_to(gen[h:h + 1, :], (BLK, 2 * BLK))
            tp_ref[br, h] = (pltpu.roll(rows, 0, 1, stride=1, stride_axis=0) * LOG2E).T
    for h in range(N_HEADS):
        jp, half = divmod(h, 2)
        rs = slice(half * SUBLANE, (half + 1) * SUBLANE)
        t16_ref[jp, rs, :] = lookup(i16_ref[...], h)
        t4_ref[jp, rs, :] = lookup(i4_ref[...], h)
        t1_ref[jp, rs, :] = lookup(i1_ref[...], h)
        for br in range(len(DILATIONS)):
            tn_ref[br, jp, rs, :] = lookup(in_ref[br], h)


def _bucket_maps(t_len):
    j = BLK - np.arange(2 * BLK)[None, :]
    prompt = np.stack([np.where(j >= 0, _rel_bucket_np(np.clip(j, 0, N_STEPS) * d), -1) for d in DILATIONS])

    t = np.arange(t_len)[:, None]

    def sample_map(diff, dil):
        ok = (diff >= 0) & (diff % dil == 0) & (diff // dil <= N_STEPS)
        return np.where(ok, _rel_bucket_np(np.maximum(diff, 0)), -1).astype(np.int32)

    w = np.arange(MAX_DISTANCE)[None, :]
    cache = {d: sample_map(MAX_DISTANCE + t - w, d) for d in DILATIONS}
    g = np.arange(LANE)[None, :]
    new = np.stack([np.where(g < t_len, sample_map(t - g, d), -1) for d in DILATIONS])
    return (prompt.astype(np.int32), cache[16], cache[4][:, MAX_DISTANCE - 4 * BLK:],
            cache[1][:, MAX_DISTANCE - BLK:], new.astype(np.int32))


def _bias_tables(rel_bias, t_len):
    maps = _bucket_maps(t_len)
    nb = len(DILATIONS)
    shapes = [(nb, N_HEADS, 2 * BLK, BLK), (N_PAIRS, 2 * t_len, MAX_DISTANCE), (N_PAIRS, 2 * t_len, 4 * BLK),
              (N_PAIRS, 2 * t_len, BLK), (nb, N_PAIRS, 2 * t_len, LANE)]
    return pl.pallas_call(
        _bias_kernel,
        in_specs=[pl.BlockSpec(memory_space=pltpu.SMEM)] + [pl.BlockSpec(memory_space=pltpu.VMEM)] * 6,
        out_specs=[pl.BlockSpec(memory_space=pltpu.VMEM)] * 5,
        out_shape=[jax.ShapeDtypeStruct(s, F32) for s in shapes],
        compiler_params=pltpu.CompilerParams(vmem_limit_bytes=VMEM_LIMIT),
        name="bias_tables",
    )(rel_bias, rel_bias.T, *[jnp.asarray(m) for m in maps])


def _inproj_kernel(x_ref, g_ref, w_ref, q_ref, k_ref, v_ref, z_ref, xbc_ref, dt_ref):
    xn = _rms(x_ref[...], g_ref[...]).astype(BF16)

    def proj(lo, hi):
        return _dot(xn, w_ref[:, lo:hi])

    q_ref[...] = proj(O_Q, O_K) * (HEAD_DIM ** -0.5)
    k_ref[...] = proj(O_K, O_V)
    v_ref[...] = proj(O_V, O_Z)
    z_ref[...] = proj(O_Z, O_XBC)
    xbc_ref[...] = proj(O_XBC, O_DT)
    dt_ref[...] = proj(O_DT, IN_MAIN)


def _inproj(x, g_mix, w_main):
    rows = x.shape[0]
    full = lambda a: pl.BlockSpec(a.shape, lambda i: (0, 0))
    widths = (D_ATTN, D_ATTN, D_ATTN, D_SSM, CONV_DIM, LANE)
    return pl.pallas_call(
        _inproj_kernel,
        grid=(1,),
        in_specs=[full(x), full(g_mix), full(w_main)],
        out_specs=[pl.BlockSpec((rows, w), lambda i: (0, 0)) for w in widths],
        out_shape=[jax.ShapeDtypeStruct((rows, w), F32) for w in widths],
        compiler_params=_params("arbitrary"),
        name="inproj_sample",
    )(x, g_mix, w_main)


def _inproj_ssd_kernel(x_ref, g_ref, w_ref, prefix_ref, state0_ref, cw_ref, cb_ref, dtb_ref, alog_ref, dskip_ref,
                       gssm_ref, q1, k1, v1, q4, k4, v4, q16, k16, v16, kt_ref, vt_ref, y_ref, state_ref, ctail_ref,
                       perm_ref, mid_ref, xp_ref, z_scr, dt_scr, st_ref, *, keep_from, n_tiles):
    tm = x_ref.shape[0]
    L = SSD_CHUNK
    t = pl.program_id(1)

    @pl.when(t == 0)
    def _init():
        xp_ref[0:SUBLANE, :] = jnp.zeros((SUBLANE, CONV_DIM), F32)
        xp_ref[SUBLANE - (SSM_CONV - 1):SUBLANE, :] = prefix_ref[...]
        _load_state(state0_ref, st_ref)

    xn = _rms(x_ref[...], g_ref[...]).astype(BF16)

    def proj(lo, hi):
        return _dot(xn, w_ref[:, lo:hi])

    xp_ref[SUBLANE:SUBLANE + tm, :] = proj(O_XBC, O_DT)
    dt_scr[...] = proj(O_DT, IN_MAIN)
    z_scr[...] = proj(O_Z, O_XBC)

    def project_attn(i):
        lo, scale, nat, r4, r16 = ((O_Q, HEAD_DIM ** -0.5 * LOG2E, q1, q4, q16), (O_K, None, k1, k4, k16),
                                   (O_V, None, v1, v4, v16))[i]
        val = proj(lo, lo + D_ATTN)
        if scale is not None:
            val = val * scale
        nat[...] = val.astype(BF16)
        for jp in range(N_PAIRS):
            perm_ref[i, jp] = val[:, jp * LANE:(jp + 1) * LANE]
        for ra in range(4):
            for jp in range(N_PAIRS):
                mid_ref[i, ra, jp] = perm_ref[i, jp, pl.ds(ra, tm // 4, stride=4), :]
            r4[ra] = jnp.concatenate([mid_ref[i, ra, jp] for jp in range(N_PAIRS)], axis=1).astype(BF16)
            for rb in range(4):
                rows = [mid_ref[i, ra, jp, pl.ds(rb, tm // 16, stride=4), :] for jp in range(N_PAIRS)]
                r16[ra + 4 * rb] = jnp.concatenate(rows, axis=1).astype(BF16)

    prm = (cw_ref, cb_ref, dtb_ref, alog_ref, dskip_ref, gssm_ref)
    n_chunks = tm // L
    assert n_chunks >= 3
    for c in range(n_chunks):
        base = SUBLANE + c * L - (SSM_CONV - 1)
        y = _ssd_chunk(lambda k, base=base: xp_ref[base + k:base + k + L, :],
                       z_scr[c * L:(c + 1) * L, :], dt_scr[c * L:(c + 1) * L, :], prm, st_ref, L)
        y_ref[c * L:(c + 1) * L, :] = y.astype(y_ref.dtype)
        if c < 3:
            project_attn(c)
    xp_ref[0:SUBLANE, :] = xp_ref[tm:tm + SUBLANE, :]

    @pl.when(t == n_tiles - 1)
    def _fin():
        _store_state(st_ref, state_ref)
        ctail_ref[...] = xp_ref[0:SUBLANE, :]

    @pl.when(t >= keep_from)
    def _keep():
        for i, out_ref in ((1, kt_ref), (2, vt_ref)):
            for jp in range(N_PAIRS):
                out_ref[2 * jp:2 * jp + 2] = perm_ref[i, jp].T.reshape(2, HEAD_DIM, tm)


def _inproj_ssd(x, g_mix, w_main, prefix, state0, ssd_params, *, tm, n_keep):
    ns, l, _ = x.shape
    nt = l // tm
    keep_from = (l - n_keep) // tm
    tile = lambda s, t: (s, t, 0)
    seq = lambda s, t: (s, 0, 0)
    fixed = lambda s, t: (0, 0)
    state_spec = pl.BlockSpec((None, N_SSM_HEADS, SSM_HEAD_DIM, D_STATE), lambda s, t: (s, 0, 0, 0))
    nat = lambda w: (pl.BlockSpec((None, tm, w), tile), jax.ShapeDtypeStruct((ns, l, w), BF16))
    outs = [nat(D_ATTN)] * 3
    for dil in (4, 16):
        outs += [(pl.BlockSpec((None, dil, tm // dil, D_ATTN), lambda s, t: (s, 0, t, 0)),
                  jax.ShapeDtypeStruct((ns, dil, l // dil, D_ATTN), BF16))] * 3
    outs += [(pl.BlockSpec((None, N_HEADS, HEAD_DIM, tm), lambda s, t: (s, 0, 0, jnp.maximum(t - keep_from, 0))),
              jax.ShapeDtypeStruct((ns, N_HEADS, HEAD_DIM, n_keep), F32))] * 2
    outs += [nat(D_SSM),
             (state_spec, jax.ShapeDtypeStruct((ns, N_SSM_HEADS, SSM_HEAD_DIM, D_STATE), F32)),
             (pl.BlockSpec((None, SUBLANE, CONV_DIM), seq), jax.ShapeDtypeStruct((ns, SUBLANE, CONV_DIM), F32))]
    return pl.pallas_call(
        functools.partial(_inproj_ssd_kernel, keep_from=keep_from, n_tiles=nt),
        grid=(ns, nt),
        in_specs=[pl.BlockSpec((None, tm, D_MODEL), tile),
                  pl.BlockSpec((1, D_MODEL), fixed),
                  pl.BlockSpec((D_MODEL, IN_MAIN), fixed),
                  pl.BlockSpec((None, SSM_CONV - 1, CONV_DIM), seq),
                  state_spec] + [pl.BlockSpec(p.shape, fixed) for p in ssd_params],
        out_specs=[o[0] for o in outs],
        out_shape=[o[1] for o in outs],
        scratch_shapes=[pltpu.VMEM((3, N_PAIRS, tm, LANE), F32),
                        pltpu.VMEM((3, 4, N_PAIRS, tm // 4, LANE), F32),
                        pltpu.VMEM((SUBLANE + tm, CONV_DIM), F32),
                        pltpu.VMEM((tm, D_SSM), F32),
                        pltpu.VMEM((tm, LANE), F32),
                        pltpu.VMEM((N_PAIRS, D_STATE, LANE), F32)],
        compiler_params=_params("parallel", "arbitrary"),
        name="inproj_ssd",
    )(x, g_mix, w_main, prefix, state0, *ssd_params)


def _load_state(state0_ref, st_ref):
    for jp in range(N_PAIRS):
        st_ref[jp] = state0_ref[2 * jp:2 * jp + 2].reshape(2 * SSM_HEAD_DIM, D_STATE).T


def _store_state(st_ref, state_ref):
    for jp in range(N_PAIRS):
        state_ref[2 * jp:2 * jp + 2] = st_ref[jp].T.reshape(2, SSM_HEAD_DIM, D_STATE)


def _ssd_chunk(xwin, z, dt_raw, prm, st_ref, valid_len):
    cw_ref, cb_ref, dtb_ref, alog_ref, dskip_ref, gssm_ref = prm
    L = SSD_CHUNK
    conv = cb_ref[...]
    for k in range(SSM_CONV):
        conv = conv + cw_ref[k:k + 1, :] * xwin(k)
    xc = _silu(conv)
    xs = xc[:, :D_SSM]
    gn = N_SSM_GROUPS * D_STATE

    row = lax.broadcasted_iota(jnp.int32, (L, L), 0)
    col = lax.broadcasted_iota(jnp.int32, (L, L), 1)
    tri = row >= col
    lt64 = col < SSM_HEAD_DIM

    dt = _softplus(dt_raw + dtb_ref[...])
    if valid_len < L:
        dt = jnp.where(row < valid_len, dt, 0.0)
    a = dt * (-jnp.exp(alog_ref[...]))
    acum = a
    shift = 1
    while shift < L:
        acum = acum + jnp.where(row >= shift, pltpu.roll(acum, shift, 0), 0.0)
        shift *= 2
    acum_t = acum.T
    dt_t = dt.T
    e_slab = jnp.exp(acum)
    de_slab = jnp.exp(acum[L - 1:L, :] - acum) * dt

    def head_terms(h, cb):
        colb = jnp.broadcast_to(acum[:, h:h + 1], (L, L))
        rowb = jnp.broadcast_to(acum_t[h:h + 1, :], (L, L))
        decay = jnp.exp(jnp.where(tri, colb - rowb, NEG))
        m = (cb * decay * jnp.broadcast_to(dt_t[h:h + 1, :], (L, L))).astype(BF16)
        e = jnp.broadcast_to(e_slab[:, h:h + 1], (L, L))
        return m, e, jnp.broadcast_to(de_slab[:, h:h + 1], (L, L)), e[L - 1:L, :]

    pairs = []
    for g in range(N_SSM_GROUPS):
        bg = xc[:, D_SSM + g * D_STATE:D_SSM + (g + 1) * D_STATE]
        cg = xc[:, D_SSM + gn + g * D_STATE:D_SSM + gn + (g + 1) * D_STATE].astype(BF16)
        cb = _nt_dot(cg, bg.astype(BF16))
        bg_t = bg.T.astype(BF16)
        for i in range(N_PAIRS // N_SSM_GROUPS):
            jp = g * (N_PAIRS // N_SSM_GROUPS) + i
            xpair = xs[:, jp * LANE:(jp + 1) * LANE]
            xpair_b = xpair.astype(BF16)
            m_a, e_a, de_a, el_a = head_terms(2 * jp, cb)
            m_b, e_b, de_b, el_b = head_terms(2 * jp + 1, cb)
            st = st_ref[jp]
            y_diag = jnp.where(lt64, _dot(m_a, xpair_b), _dot(m_b, xpair_b))
            y_off = _dot(cg, st.astype(BF16)) * jnp.where(lt64, e_a, e_b)
            xd = (xpair * jnp.where(lt64, de_a, de_b)).astype(BF16)
            st_ref[jp] = st * jnp.where(lt64[0:1, :], el_a, el_b) + _dot(bg_t, xd)
            pairs.append(y_diag + y_off + dskip_ref[:, jp * LANE:(jp + 1) * LANE] * xpair)
    y = jnp.concatenate(pairs, axis=1)

    yf = y * _silu(z)
    gw = D_SSM // N_SSM_GROUPS
    normed = []
    for g in range(N_SSM_GROUPS):
        seg = yf[:, g * gw:(g + 1) * gw]
        normed.append(seg * lax.rsqrt(jnp.mean(seg * seg, axis=-1, keepdims=True) + EPS))
    return jnp.concatenate(normed, axis=1) * gssm_ref[...]


def _ssd_short_kernel(xbc_ref, z_ref, dt_ref, prefix_ref, state0_ref, cw_ref, cb_ref, dtb_ref, alog_ref,
                      dskip_ref, gssm_ref, y_ref, state_ref, xp_ref, st_ref):
    t_len = xbc_ref.shape[0]
    L = SSD_CHUNK
    lead = SUBLANE - (SSM_CONV - 1)
    xp_ref[0:lead, :] = jnp.zeros((lead, CONV_DIM), F32)
    xp_ref[lead:SUBLANE, :] = prefix_ref[...]
    xp_ref[SUBLANE:SUBLANE + t_len, :] = xbc_ref[...]
    xp_ref[SUBLANE + t_len:SUBLANE + L, :] = jnp.zeros((L - t_len, CONV_DIM), F32)
    _load_state(state0_ref, st_ref)
    pad = lambda a: jnp.concatenate([a, jnp.zeros((L - t_len, a.shape[1]), F32)], axis=0)
    y = _ssd_chunk(lambda k: xp_ref[lead + k:lead + k + L, :], pad(z_ref[...]), pad(dt_ref[...]),
                   (cw_ref, cb_ref, dtb_ref, alog_ref, dskip_ref, gssm_ref), st_ref, t_len)
    y_ref[...] = y[0:t_len].astype(y_ref.dtype)
    _store_state(st_ref, state_ref)


def _ssd_short(xbc, z, dt, prefix, state0, ssd_params):
    ns, t_len, _ = xbc.shape
    seq = lambda s: (s, 0, 0)
    state_spec = pl.BlockSpec((None, N_SSM_HEADS, SSM_HEAD_DIM, D_STATE), lambda s: (s, 0, 0, 0))
    return pl.pallas_call(
        _ssd_short_kernel,
        grid=(ns,),
        in_specs=[pl.BlockSpec((None, t_len, CONV_DIM), seq),
                  pl.BlockSpec((None, t_len, D_SSM), seq),
                  pl.BlockSpec((None, t_len, LANE), seq),
                  pl.BlockSpec((None, SSM_CONV - 1, CONV_DIM), seq),
                  state_spec] + [pl.BlockSpec(p.shape, lambda s: (0, 0)) for p in ssd_params],
        out_specs=[pl.BlockSpec((None, t_len, D_SSM), seq), state_spec],
        out_shape=[jax.ShapeDtypeStruct((ns, t_len, D_SSM), BF16),
                   jax.ShapeDtypeStruct((ns, N_SSM_HEADS, SSM_HEAD_DIM, D_STATE), F32)],
        scratch_shapes=[pltpu.VMEM((SUBLANE + SSD_CHUNK, CONV_DIM), F32),
                        pltpu.VMEM((N_PAIRS, D_STATE, LANE), F32)],
        compiler_params=_params("parallel"),
        name="ssd_short",
    )(xbc, z, dt, prefix, state0, *ssd_params)


ATTN_UNITS = 4


def _attn_prompt_kernel(q_ref, kp_ref, kc_ref, vp_ref, vc_ref, bias_ref, o_ref, max_ref, den_ref, s_ref, p_ref, *, dil):
    units = q_ref.shape[0]
    blk0 = pl.program_id(1) == 0
    lt64 = _lane_lt64((BLK, LANE))
    head_row = lax.broadcasted_iota(jnp.int32, (N_HEADS, BLK), 0)
    zero = jnp.zeros((BLK, LANE), BF16)

    def prev_of(g, p_ref_, c_ref_):
        if dil == 1:
            return (p_ref_.at[0], blk0) if g == 0 else (c_ref_.at[g - 1], None)
        return p_ref_.at[g], blk0

    def rows_of(g):
        if dil == 1:
            return slice(g * BLK, (g + 1) * BLK)
        return pl.ds(pl.program_id(2) * units + g, BLK, stride=dil)

    def scores(g, slot):
        kprev, masked = prev_of(g, kp_ref, kc_ref)
        for jp in range(N_PAIRS):
            sl = slice(jp * LANE, (jp + 1) * LANE)
            qp = q_ref[g, :, sl]
            kcat = jnp.concatenate([kprev[:, sl], kc_ref[g, :, sl]], axis=0)
            for half in range(2):
                h = 2 * jp + half
                qm = jnp.where(lt64 if half == 0 else ~lt64, qp, zero)
                bias = bias_ref[h]
                if masked is not None:
                    bias = jnp.concatenate([jnp.where(masked, NEG, bias[0:BLK]), bias[BLK:]], axis=0)
                s_ref[slot, h] = _nt_dot(kcat, qm) + bias

    def softmax(g, slot):
        s = s_ref[slot]
        m = jnp.max(s, axis=1, keepdims=True)
        p = jnp.exp2(s - m)
        den = jnp.sum(p, axis=1, keepdims=True)
        p_ref[slot] = p.astype(BF16)
        m_rows = jnp.zeros((N_HEADS, BLK), F32)
        den_rows = jnp.zeros((N_HEADS, BLK), F32)
        for h in range(N_HEADS):
            m_rows = jnp.where(head_row == h, jnp.broadcast_to(m[h], (N_HEADS, BLK)), m_rows)
            den_rows = jnp.where(head_row == h, jnp.broadcast_to(den[h], (N_HEADS, BLK)), den_rows)
        m_nat = m_rows * (1.0 / LOG2E)
        max_ref[rows_of(g), :] = jnp.concatenate([m_nat, jnp.zeros((BLK - N_HEADS, BLK), F32)], axis=0).T
        den_ref[rows_of(g), :] = jnp.concatenate([den_rows, jnp.ones((BLK - N_HEADS, BLK), F32)], axis=0).T

    def values(g, slot):
        vprev, _ = prev_of(g, vp_ref, vc_ref)
        for jp in range(N_PAIRS):
            sl = slice(jp * LANE, (jp + 1) * LANE)
            vcat = jnp.concatenate([vprev[:, sl], vc_ref[g, :, sl]], axis=0)
            pv = [lax.dot_general(p_ref[slot, 2 * jp + half], vcat, (((0,), (0,)), ((), ())),
                                  preferred_element_type=F32) for half in range(2)]
            o_ref[jp, rows_of(g), :] = jnp.where(lt64, pv[0], pv[1])

    scores(0, 0)
    for g in range(units):
        softmax(g, g % 2)
        if g + 1 < units:
            scores(g + 1, (g + 1) % 2)
        values(g, g % 2)


def _attn_prompt_branch(q, k, v, bias, dil):
    bsz, _, l, _ = q.shape
    s = l * dil
    u = ATTN_UNITS
    if dil == 1:
        q, k, v = (t.reshape(bsz, l // BLK, BLK, D_ATTN) for t in (q, k, v))
        grid = (bsz, l // BLK // u, 1)
        cur = pl.BlockSpec((None, u, BLK, D_ATTN), lambda b, n, r: (b, n, 0, 0))
        prev = pl.BlockSpec((None, 1, BLK, D_ATTN), lambda b, n, r: (b, jnp.maximum(n * u - 1, 0), 0, 0))
        rows = u * BLK
    else:
        grid = (bsz, l // BLK, dil // u)
        cur = pl.BlockSpec((None, u, BLK, D_ATTN), lambda b, n, r: (b, r, n, 0))
        prev = pl.BlockSpec((None, u, BLK, D_ATTN), lambda b, n, r: (b, r, jnp.maximum(n - 1, 0), 0))
        rows = dil * BLK
    return pl.pallas_call(
        functools.partial(_attn_prompt_kernel, dil=dil),
        grid=grid,
        in_specs=[cur, prev, cur, prev, cur,
                  pl.BlockSpec((N_HEADS, 2 * BLK, BLK), lambda b, n, r: (0, 0, 0))],
        out_specs=[pl.BlockSpec((None, N_PAIRS, rows, LANE), lambda b, n, r: (b, 0, n, 0)),
                   pl.BlockSpec((None, rows, LANE), lambda b, n, r: (b, n, 0)),
                   pl.BlockSpec((None, rows, LANE), lambda b, n, r: (b, n, 0))],
        out_shape=[jax.ShapeDtypeStruct((bsz, N_PAIRS, s, LANE), F32),
                   jax.ShapeDtypeStruct((bsz, s, LANE), F32),
                   jax.ShapeDtypeStruct((bsz, s, LANE), F32)],
        scratch_shapes=[pltpu.VMEM((2, N_HEADS, 2 * BLK, BLK), F32), pltpu.VMEM((2, N_HEADS, 2 * BLK, BLK), BF16)],
        compiler_params=_params("parallel", "parallel", "arbitrary"),
        name=f"attn_prompt_d{dil}",
    )(q, k, k, v, v, bias)


def _attn_sample_kernel(q_ref, kn_ref, vn_ref, kc_ref, vc_ref, b16_ref, b4_ref, b1_ref, bn_ref, o_ref):
    t_len = q_ref.shape[0]
    w = kc_ref.shape[-1]
    lt64 = _lane_lt64((t_len, LANE))
    pad = jnp.zeros((BLK - t_len, D_ATTN), F32)
    kn = jnp.concatenate([kn_ref[...], pad], axis=0).astype(BF16)
    vn = jnp.concatenate([vn_ref[...], pad], axis=0).astype(BF16)
    outs = []
    for jp in range(N_PAIRS):
        sl = slice(jp * LANE, (jp + 1) * LANE)
        qp = q_ref[:, sl]
        qm = jnp.concatenate([jnp.where(lt64, qp, 0.0), jnp.where(lt64, 0.0, qp)], axis=0).astype(BF16)
        kt = kc_ref[2 * jp:2 * jp + 2].reshape(2 * HEAD_DIM, w).astype(BF16)
        vt = vc_ref[2 * jp:2 * jp + 2].reshape(2 * HEAD_DIM, w).astype(BF16)
        s = _dot(qm, kt)
        s_new = _nt_dot(qm, kn[:, sl])
        w4, w1 = w - 4 * BLK, w - BLK
        tiles = [s + b16_ref[jp], s[:, w4:] + b4_ref[jp], s[:, w1:] + b1_ref[jp]]
        tiles += [s_new + bn_ref[br, jp] for br in range(len(DILATIONS))]
        m = functools.reduce(jnp.maximum, [jnp.max(t, axis=-1, keepdims=True) for t in tiles])
        ps = [jnp.exp(t - m) for t in tiles]
        den = functools.reduce(lambda a, b: a + b, [jnp.sum(p, axis=-1, keepdims=True) for p in ps])
        p16, p4, p1 = ps[0], ps[1], ps[2]
        p_cache = jnp.concatenate([p16[:, :w4], p16[:, w4:w1] + p4[:, :w1 - w4],
                                   p16[:, w1:] + p4[:, w1 - w4:] + p1], axis=1)
        p_new = ps[3] + ps[4] + ps[5]
        o2 = (_nt_dot(p_cache.astype(BF16), vt) + _dot(p_new.astype(BF16), vn[:, sl])) / den
        outs.append(jnp.where(lt64, o2[0:t_len], o2[t_len:2 * t_len]))
    o_ref[...] = jnp.concatenate(outs, axis=1).astype(o_ref.dtype)


def _attn_sample(q, kn, vn, kc, vc, b16, b4, b1, bn):
    ns, t_len, _ = q.shape
    w = kc.shape[-1]
    assert w == MAX_DISTANCE and t_len == SUBLANE
    tok = pl.BlockSpec((None, t_len, D_ATTN), lambda s: (s, 0, 0))
    cache = pl.BlockSpec((None, N_HEADS, HEAD_DIM, w), lambda s: (s, 0, 0, 0))
    full = lambda a: pl.BlockSpec(a.shape, lambda s: (0,) * a.ndim)
    return pl.pallas_call(
        _attn_sample_kernel,
        grid=(ns,),
        in_specs=[tok, tok, tok, cache, cache, full(b16), full(b4), full(b1), full(bn)],
        out_specs=tok,
        out_shape=jax.ShapeDtypeStruct((ns, t_len, D_ATTN), BF16),
        compiler_params=_params("parallel"),
        name="attn_sample",
    )(q, kn, vn, kc, vc, b16, b4, b1, bn)


def _mix_ffn_kernel(*refs, n_parts, seg_in_tile, n_tiles):
    it = iter(refs)
    x_ref = next(it)
    parts = [(next(it), next(it), next(it)) for _ in range(n_parts)] if n_parts > 1 else [(next(it),)]
    yssm_ref, p_ref, fpre_ref = next(it), next(it), next(it)
    (expand_ref, wout_ref, gffn_ref, wup_ref, fcw_ref, fcb_ref, wdown_ref, wple_ref, gple_ref,
     wgate_ref, gfin_ref) = (next(it) for _ in range(11))
    y_ref, uout_ref = next(it), next(it)
    act_ref, tail_ref, hn_ref = next(it), next(it), next(it)
    tm = x_ref.shape[0]

    if n_parts > 1:
        maxes = [max_ref[...] for _, max_ref, _ in parts]
        mx = functools.reduce(jnp.maximum, maxes)
        ws = [jnp.exp(m - mx) for m in maxes]
        den = functools.reduce(lambda a, b: a + b, [w * den_ref[...] for w, (_, _, den_ref) in zip(ws, parts)])
        attn = None
        for (o_ref, _, _), w in zip(parts, ws):
            wn = w / den
            hi = wn.astype(BF16)
            lo = (wn - hi.astype(F32)).astype(BF16)
            wexp = _dot(jnp.concatenate([hi, lo], axis=1), expand_ref[...])
            o = jnp.concatenate([o_ref[jp] for jp in range(N_PAIRS)], axis=1)
            attn = wexp * o if attn is None else attn + wexp * o
        attn = attn.astype(BF16)
    else:
        attn = parts[0][0][...]
    h1 = x_ref[...] + _dot(attn, wout_ref[0:D_ATTN, :]) + _dot(yssm_ref[...], wout_ref[D_ATTN:, :])
    y_ref[...] = h1
    hn_ref[...] = _rms(h1, gffn_ref[...]).astype(BF16)

    if not seg_in_tile:
        t = pl.program_id(1)

        @pl.when(t == 0)
        def _init():
            tail_ref[...] = jnp.zeros(tail_ref.shape, F32)
            tail_ref[SUBLANE - (FFN_CONV - 1):SUBLANE, :] = fpre_ref[...]

        rin8 = lax.broadcasted_iota(jnp.int32, (SUBLANE, FF_CHUNK), 0)
    else:
        rin = lax.broadcasted_iota(jnp.int32, (tm, FF_CHUNK), 0) & (SUBLANE - 1)

    def conv(u, c0):
        cs = slice(c0, c0 + FF_CHUNK)
        if seg_in_tile:
            pre = fpre_ref[:, cs]
            um1 = jnp.where(rin == 0, pltpu.roll(pre, tm - 1, 0), pltpu.roll(u, 1, 0))
            um2 = jnp.where(rin < 2, pre, pltpu.roll(u, 2, 0))
            uout_ref[:, cs] = u
        else:
            tail = tail_ref[:, cs]
            tail_ref[:, cs] = u[tm - SUBLANE:tm, :]
            r1, r2 = pltpu.roll(u, 1, 0), pltpu.roll(u, 2, 0)
            um1 = jnp.concatenate([jnp.where(rin8 < 1, pltpu.roll(tail, 1, 0), r1[0:SUBLANE]), r1[SUBLANE:]], axis=0)
            um2 = jnp.concatenate([jnp.where(rin8 < 2, pltpu.roll(tail, 2, 0), r2[0:SUBLANE]), r2[SUBLANE:]], axis=0)
        return fcb_ref[:, cs] + fcw_ref[0:1, cs] * um2 + fcw_ref[1:2, cs] * um1 + fcw_ref[2:3, cs] * u

    for j in range(D_FF // FF_CHUNK):
        c0 = j * FF_CHUNK
        u_gate = conv(_dot(hn_ref[...], wup_ref[:, c0:c0 + FF_CHUNK]), c0)
        u_lin = conv(_dot(hn_ref[...], wup_ref[:, D_FF + c0:D_FF + c0 + FF_CHUNK]), D_FF + c0)
        act_ref[:, c0:c0 + FF_CHUNK] = (_silu(u_gate) * u_lin).astype(BF16)
    h2 = y_ref[...] + _dot(act_ref[...], wdown_ref[...])

    e = _rms(_dot(p_ref[...].astype(BF16), wple_ref[...]), gple_ref[...])
    h3 = h2 + jax.nn.sigmoid(_dot(h2.astype(BF16), wgate_ref[...])) * e
    y_ref[...] = _rms(h3, gfin_ref[...])

    if not seg_in_tile:
        @pl.when(pl.program_id(1) == n_tiles - 1)
        def _fin():
            uout_ref[...] = tail_ref[...]


def _mix_ffn(x, parts, yssm, p, fpre, weights, *, tm, seg_in_tile):
    ns, l, _ = x.shape
    n_tiles = l // tm
    n_parts = len(parts)
    tile = lambda s, t: (s, t, 0)
    seq = lambda s, t: (s, 0, 0)
    fixed = lambda s, t: (0, 0)
    in_specs = [pl.BlockSpec((None, tm, D_MODEL), tile)]
    args = [x]
    for part in parts:
        if n_parts > 1:
            in_specs += [pl.BlockSpec((None, N_PAIRS, tm, LANE), lambda s, t: (s, 0, t, 0)),
                         pl.BlockSpec((None, tm, LANE), tile), pl.BlockSpec((None, tm, LANE), tile)]
            args += list(part)
        else:
            in_specs.append(pl.BlockSpec((None, tm, D_ATTN), tile))
            args.append(part[0])
    in_specs += [pl.BlockSpec((None, tm, D_SSM), tile), pl.BlockSpec((None, tm, D_PLE), tile)]
    args += [yssm, p]
    if seg_in_tile:
        in_specs.append(pl.BlockSpec((None, tm, 2 * D_FF), tile))
        uout_spec = pl.BlockSpec((None, tm, 2 * D_FF), tile)
        uout_shape = jax.ShapeDtypeStruct((ns, l, 2 * D_FF), F32)
    else:
        in_specs.append(pl.BlockSpec((None, FFN_CONV - 1, 2 * D_FF), seq))
        uout_spec = pl.BlockSpec((None, SUBLANE, 2 * D_FF), seq)
        uout_shape = jax.ShapeDtypeStruct((ns, SUBLANE, 2 * D_FF), F32)
    args.append(fpre)
    for w in weights:
        in_specs.append(pl.BlockSpec(w.shape, fixed))
        args.append(w)
    kern = functools.partial(_mix_ffn_kernel, n_parts=n_parts, seg_in_tile=seg_in_tile, n_tiles=n_tiles)
    return pl.pallas_call(
        kern,
        grid=(ns, n_tiles),
        in_specs=in_specs,
        out_specs=[pl.BlockSpec((None, tm, D_MODEL), tile), uout_spec],
        out_shape=[jax.ShapeDtypeStruct((ns, l, D_MODEL), F32), uout_shape],
        scratch_shapes=[pltpu.VMEM((tm, D_FF), BF16),
                        pltpu.VMEM((SUBLANE, 2 * D_FF), F32),
                        pltpu.VMEM((tm, D_MODEL), BF16)],
        compiler_params=_params("parallel", "arbitrary"),
        name="mix_ffn_seg" if seg_in_tile else "mix_ffn",
    )(*args)


def _pad_lanes(v, width=LANE):
    return jnp.pad(v.astype(F32), (0, width - v.shape[0]))[None, :]


def kernel(x_prompt, x_sample, p_prompt, p_sample, cache_k, cache_v, state_ssm, state_conv, state_ffn_conv,
           rel_bias, g_mix, w_in, conv_w, conv_b, dt_bias, a_log, d_skip, g_ssm, w_out, g_ffn, w_up,
           ffn_conv_w, ffn_conv_b, w_down, w_ple_proj, g_ple, w_ple_gate, g_final):
    assert w_in.shape[0] == 1, "one layer"
    bp, s, _ = x_prompt.shape
    nsamp, t_len, _ = x_sample.shape
    n_keep = min(MAX_DISTANCE, s)

    w_main = jnp.pad(w_in[0], ((0, 0), (0, IN_MAIN - w_in.shape[2]))).astype(BF16)
    gmix = g_mix[0][None, :]
    ssd_params = (conv_w[0], conv_b[0][None, :], _pad_lanes(dt_bias[0]), _pad_lanes(a_log[0]),
                  jnp.repeat(d_skip[0], SSM_HEAD_DIM)[None, :], g_ssm[0][None, :])
    expand = (np.arange(2 * LANE)[:, None] % LANE == (np.arange(D_ATTN)[None, :] // HEAD_DIM)).astype(np.float32)
    ffn_weights = (jnp.asarray(expand, BF16), w_out[0].astype(BF16), g_ffn[0][None, :], w_up[0].astype(BF16),
                   ffn_conv_w[0], ffn_conv_b[0][None, :], w_down[0].astype(BF16), w_ple_proj[0].astype(BF16),
                   g_ple[0][None, :], w_ple_gate[0].astype(BF16), g_final[None, :])
    bias_p, b16, b4, b1, bn = _bias_tables(rel_bias, t_len)

    (q1, k1, v1, q4, k4, v4, q16, k16, v16, kt, vt, yssm_p, ssm_p, ctail_p) = _inproj_ssd(
        x_prompt, gmix, w_main,
        jnp.zeros((bp, SSM_CONV - 1, CONV_DIM), F32),
        jnp.zeros((bp, N_SSM_HEADS, SSM_HEAD_DIM, D_STATE), F32),
        ssd_params, tm=512, n_keep=n_keep)
    parts = [_attn_prompt_branch(q1[:, None], k1[:, None], v1[:, None], bias_p[0], 1),
             _attn_prompt_branch(q4, k4, v4, bias_p[1], 4),
             _attn_prompt_branch(q16, k16, v16, bias_p[2], 16)]
    y_prompt, tail_p = _mix_ffn(x_prompt, parts, yssm_p, p_prompt[0],
                                jnp.zeros((bp, FFN_CONV - 1, 2 * D_FF), F32), ffn_weights,
                                tm=512, seg_in_tile=False)
    k_prompt = jnp.transpose(kt, (0, 3, 1, 2))[None]
    v_prompt = jnp.transpose(vt, (0, 3, 1, 2))[None]
    conv_prompt = ctail_p[:, SUBLANE - (SSM_CONV - 1):][None]
    ffn_conv_prompt = tail_p[:, SUBLANE - (FFN_CONV - 1):][None]

    rows = nsamp * t_len
    qs, ks, vs, zs, xbcs, dts = _inproj(x_sample.reshape(rows, D_MODEL), gmix, w_main)
    s3 = lambda a: a.reshape(nsamp, t_len, a.shape[-1])
    yssm_s, ssm_s = _ssd_short(s3(xbcs), s3(zs), s3(dts), state_conv[0], state_ssm[0], ssd_params)
    attn_s = _attn_sample(s3(qs), s3(ks), s3(vs),
                          jnp.transpose(cache_k[0], (0, 2, 3, 1)), jnp.transpose(cache_v[0], (0, 2, 3, 1)),
                          b16, b4, b1, bn)
    fpre = jnp.pad(state_ffn_conv[0], ((0, 0), (0, t_len - (FFN_CONV - 1)), (0, 0))).reshape(1, rows, 2 * D_FF)
    y_s, u_s = _mix_ffn(x_sample.reshape(1, rows, D_MODEL), [(attn_s.reshape(1, rows, D_ATTN),)],
                        yssm_s.reshape(1, rows, D_SSM), p_sample[0].reshape(1, rows, D_PLE),
                        fpre, ffn_weights, tm=rows, seg_in_tile=True)
    y_sample = y_s.reshape(nsamp, t_len, D_MODEL)
    k_sample = ks.reshape(1, nsamp, t_len, N_HEADS, HEAD_DIM)
    v_sample = vs.reshape(1, nsamp, t_len, N_HEADS, HEAD_DIM)
    conv_sample = s3(xbcs)[:, t_len - (SSM_CONV - 1):][None]
    ffn_conv_sample = u_s.reshape(nsamp, t_len, 2 * D_FF)[:, t_len - (FFN_CONV - 1):][None]

    return (y_prompt, y_sample, k_prompt, v_prompt, k_sample, v_sample,
            ssm_p[None], ssm_s[None], conv_prompt, conv_sample, ffn_conv_prompt, ffn_conv_sample)
```

```python
import functools
import math

import numpy as np
import jax
import jax.numpy as jnp
from jax import lax
from jax.experimental import pallas as pl
from jax.experimental.pallas import tpu as pltpu

F32 = jnp.float32
BF16 = jnp.bfloat16

D_MODEL = 1024
HEAD_DIM = 64
N_HEADS = 8
D_ATTN = N_HEADS * HEAD_DIM
N_PAIRS = N_HEADS // 2
DILATIONS = (1, 4, 16)
N_STEPS = 128
BLK = 128
N_BUCKETS = 32
MAX_DISTANCE = 2048
D_SSM = 512
N_SSM_HEADS = 8
SSM_HEAD_DIM = 64
D_STATE = 128
N_SSM_GROUPS = 2
SSM_CONV = 4
CONV_DIM = D_SSM + 2 * N_SSM_GROUPS * D_STATE
SSD_CHUNK = 128
D_FF = 2816
FFN_CONV = 3
D_PLE = 256
EPS = 1e-6
NEG = -1e30
LOG2E = math.log2(math.e)

LANE = 128
SUBLANE = 8
FF_CHUNK = 256
VMEM_LIMIT = 56 * 1024 * 1024

O_Q, O_K, O_V, O_Z, O_XBC, O_DT = 0, 512, 1024, 1536, 2048, 3072


def _rel_bucket_np(dist):
    dist = np.asarray(dist, np.int32)
    max_exact = N_BUCKETS // 2
    d = np.maximum(dist, 1).astype(np.float32)
    large = max_exact + (np.log(d / np.float32(max_exact)) / np.float32(math.log(MAX_DISTANCE / max_exact))
                         * np.float32(N_BUCKETS - max_exact)).astype(np.int32)
    large = np.minimum(large, N_BUCKETS - 1)
    return np.where(dist < max_exact, dist, large)


def _nt_dot(a, b):
    return lax.dot_general(a, b, (((1,), (1,)), ((), ())), preferred_element_type=F32)


def _dot(a, b):
    return jnp.dot(a, b, preferred_element_type=F32)


def _silu(x):
    return x * jax.nn.sigmoid(x)


def _softplus(x):
    return jnp.maximum(x, 0.0) + jnp.log1p(jnp.exp(-jnp.abs(x)))


def _rms(x, g):
    return x * lax.rsqrt(jnp.mean(x * x, axis=-1, keepdims=True) + EPS) * g


def _lane_lt64(shape):
    return lax.broadcasted_iota(jnp.int32, shape, len(shape) - 1) < HEAD_DIM


def _params(*sem):
    return pltpu.CompilerParams(dimension_semantics=sem, vmem_limit_bytes=VMEM_LIMIT)


def _bias_kernel(rb_ref, rbt_ref, ig_ref, i16_ref, i4_ref, i1_ref, in_ref, tp_ref, t16_ref, t4_ref, t1_ref, tn_ref):
    def lookup(idx, h):
        def body(b, acc):
            return jnp.where(idx == b, rb_ref[b, h], acc)
        return lax.fori_loop(0, N_BUCKETS, body, jnp.full(idx.shape, NEG, F32), unroll=True)

    for br in range(len(DILATIONS)):
        idx = jnp.broadcast_to(ig_ref[br], (N_HEADS, 2 * BLK))
        gen = jnp.full((N_HEADS, 2 * BLK), NEG, F32)
        for b in range(N_BUCKETS):
            gen = jnp.where(idx == b, jnp.broadcast_to(rbt_ref[:, b:b + 1], (N_HEADS, 2 * BLK)), gen)
        for h in range(N_HEADS):
            rows = jnp.broadcast_to(gen[h:h + 1, :], (BLK, 2 * BLK))
            tp_ref[br, h] = (pltpu.roll(rows, 0, 1, stride=1, stride_axis=0) * LOG2E).T
    for h in range(N_HEADS):
        jp, half = divmod(h, 2)
        rs = slice(half * SUBLANE, (half + 1) * SUBLANE)
        t16_ref[jp, rs, :] = lookup(i16_ref[...], h)
        t4_ref[jp, rs, :] = lookup(i4_ref[...], h)
        t1_ref[jp, rs, :] = lookup(i1_ref[...], h)
        for br in range(len(DILATIONS)):
            tn_ref[br, jp, rs, :] = lookup(in_ref[br], h)


def _bucket_maps(t_len):
    j = BLK - np.arange(2 * BLK)[None, :]
    prompt = np.stack([np.where(j >= 0, _rel_bucket_np(np.clip(j, 0, N_STEPS) * d), -1) for d in DILATIONS])

    t = np.arange(t_len)[:, None]

    def sample_map(diff, dil):
        ok = (diff >= 0) & (diff % dil == 0) & (diff // dil <= N_STEPS)
        return np.where(ok, _rel_bucket_np(np.maximum(diff, 0)), -1).astype(np.int32)

    w = np.arange(MAX_DISTANCE)[None, :]
    cache = {d: sample_map(MAX_DISTANCE + t - w, d) for d in DILATIONS}
    g = np.arange(LANE)[None, :]
    new = np.stack([np.where(g < t_len, sample_map(t - g, d), -1) for d in DILATIONS])
    return (prompt.astype(np.int32), cache[16], cache[4][:, MAX_DISTANCE - 4 * BLK:],
            cache[1][:, MAX_DISTANCE - BLK:], new.astype(np.int32))


def _bias_tables(rel_bias, t_len):
    maps = _bucket_maps(t_len)
    nb = len(DILATIONS)
    shapes = [(nb, N_HEADS, 2 * BLK, BLK), (N_PAIRS, 2 * t_len, MAX_DISTANCE), (N_PAIRS, 2 * t_len, 4 * BLK),
              (N_PAIRS, 2 * t_len, BLK), (nb, N_PAIRS, 2 * t_len, LANE)]
    return pl.pallas_call(
        _bias_kernel,
        in_specs=[pl.BlockSpec(memory_space=pltpu.SMEM)] + [pl.BlockSpec(memory_space=pltpu.VMEM)] * 6,
        out_specs=[pl.BlockSpec(memory_space=pltpu.VMEM)] * 5,
        out_shape=[jax.ShapeDtypeStruct(s, F32) for s in shapes],
        compiler_params=pltpu.CompilerParams(vmem_limit_bytes=VMEM_LIMIT),
        name="bias_tables",
    )(rel_bias, rel_bias.T, *[jnp.asarray(m) for m in maps])


W_CAST_ROWS = 128


def _cast_kernel(w_ref, o_ref):
    o_ref[...] = w_ref[...].astype(BF16)


def _cast_w_main(w_in):
    return pl.pallas_call(
        _cast_kernel,
        grid=(D_MODEL // W_CAST_ROWS,),
        in_specs=[pl.BlockSpec((None, W_CAST_ROWS, O_DT), lambda i: (0, i, 0))],
        out_specs=pl.BlockSpec((W_CAST_ROWS, O_DT), lambda i: (i, 0)),
        out_shape=jax.ShapeDtypeStruct((D_MODEL, O_DT), BF16),
        compiler_params=_params("parallel"),
        name="cast_w_in",
    )(w_in)


def _inproj_kernel(x_ref, g_ref, w_ref, wdt_ref, q_ref, k_ref, v_ref, z_ref, xbc_ref, dt_ref):
    xn = _rms(x_ref[...], g_ref[...]).astype(BF16)

    def proj(lo, hi):
        return _dot(xn, w_ref[:, lo:hi])

    q_ref[...] = proj(O_Q, O_K) * (HEAD_DIM ** -0.5)
    k_ref[...] = proj(O_K, O_V)
    v_ref[...] = proj(O_V, O_Z)
    z_ref[...] = proj(O_Z, O_XBC)
    xbc_ref[...] = proj(O_XBC, O_DT)
    dt_ref[...] = _dot(xn, wdt_ref[...])


def _inproj(x, g_mix, w_main, w_dt):
    rows = x.shape[0]
    full = lambda a: pl.BlockSpec(a.shape, lambda i: (0, 0))
    widths = (D_ATTN, D_ATTN, D_ATTN, D_SSM, CONV_DIM, LANE)
    return pl.pallas_call(
        _inproj_kernel,
        grid=(1,),
        in_specs=[full(x), full(g_mix), full(w_main), full(w_dt)],
        out_specs=[pl.BlockSpec((rows, w), lambda i: (0, 0)) for w in widths],
        out_shape=[jax.ShapeDtypeStruct((rows, w), F32) for w in widths],
        compiler_params=_params("arbitrary"),
        name="inproj_sample",
    )(x, g_mix, w_main, w_dt)


def _inproj_ssd_kernel(x_ref, g_ref, w_ref, wdt_ref, prefix_ref, state0_ref, cw_ref, cb_ref, dtb_ref, alog_ref, dskip_ref,
                       gssm_ref, q1, k1, v1, q4, k4, v4, q16, k16, v16, kt_ref, vt_ref, y_ref, state_ref, ctail_ref,
                       perm_ref, mid_ref, xp_ref, z_scr, dt_scr, st_ref, *, keep_from, n_tiles):
    tm = x_ref.shape[0]
    L = SSD_CHUNK
    t = pl.program_id(1)

    @pl.when(t == 0)
    def _init():
        xp_ref[0:SUBLANE, :] = jnp.zeros((SUBLANE, CONV_DIM), F32)
        xp_ref[SUBLANE - (SSM_CONV - 1):SUBLANE, :] = prefix_ref[...]
        _load_state(state0_ref, st_ref)

    xn = _rms(x_ref[...], g_ref[...]).astype(BF16)

    def proj(lo, hi):
        return _dot(xn, w_ref[:, lo:hi])

    xp_ref[SUBLANE:SUBLANE + tm, :] = proj(O_XBC, O_DT)
    dt_scr[...] = _dot(xn, wdt_ref[...])
    z_scr[...] = proj(O_Z, O_XBC)

    def project_attn(i):
        lo, scale, nat, r4, r16 = ((O_Q, HEAD_DIM ** -0.5 * LOG2E, q1, q4, q16), (O_K, None, k1, k4, k16),
                                   (O_V, None, v1, v4, v16))[i]
        val = proj(lo, lo + D_ATTN)
        if scale is not None:
            val = val * scale
        nat[...] = val.astype(BF16)
        for jp in range(N_PAIRS):
            perm_ref[i, jp] = val[:, jp * LANE:(jp + 1) * LANE]
        for ra in range(4):
            for jp in range(N_PAIRS):
                mid_ref[i, ra, jp] = perm_ref[i, jp, pl.ds(ra, tm // 4, stride=4), :]
            r4[ra] = jnp.concatenate([mid_ref[i, ra, jp] for jp in range(N_PAIRS)], axis=1).astype(BF16)
            for rb in range(4):
                rows = [mid_ref[i, ra, jp, pl.ds(rb, tm // 16, stride=4), :] for jp in range(N_PAIRS)]
                r16[ra + 4 * rb] = jnp.concatenate(rows, axis=1).astype(BF16)

    prm = (cw_ref, cb_ref, dtb_ref, alog_ref, dskip_ref, gssm_ref)
    n_chunks = tm // L
    assert n_chunks >= 3
    for c in range(n_chunks):
        base = SUBLANE + c * L - (SSM_CONV - 1)
        y = _ssd_chunk(lambda k, base=base: xp_ref[base + k:base + k + L, :],
                       z_scr[c * L:(c + 1) * L, :], dt_scr[c * L:(c + 1) * L, :], prm, st_ref, L)
        y_ref[c * L:(c + 1) * L, :] = y.astype(y_ref.dtype)
        if c < 3:
            project_attn(c)
    xp_ref[0:SUBLANE, :] = xp_ref[tm:tm + SUBLANE, :]

    @pl.when(t == n_tiles - 1)
    def _fin():
        _store_state(st_ref, state_ref)
        ctail_ref[...] = xp_ref[0:SUBLANE, :]

    @pl.when(t >= keep_from)
    def _keep():
        for i, out_ref in ((1, kt_ref), (2, vt_ref)):
            for jp in range(N_PAIRS):
                out_ref[2 * jp:2 * jp + 2] = perm_ref[i, jp].T.reshape(2, HEAD_DIM, tm)


def _inproj_ssd(x, g_mix, w_main, w_dt, prefix, state0, ssd_params, *, tm, n_keep):
    ns, l, _ = x.shape
    nt = l // tm
    keep_from = (l - n_keep) // tm
    tile = lambda s, t: (s, t, 0)
    seq = lambda s, t: (s, 0, 0)
    fixed = lambda s, t: (0, 0)
    state_spec = pl.BlockSpec((None, N_SSM_HEADS, SSM_HEAD_DIM, D_STATE), lambda s, t: (s, 0, 0, 0))
    nat = lambda w: (pl.BlockSpec((None, tm, w), tile), jax.ShapeDtypeStruct((ns, l, w), BF16))
    outs = [nat(D_ATTN)] * 3
    for dil in (4, 16):
        outs += [(pl.BlockSpec((None, dil, tm // dil, D_ATTN), lambda s, t: (s, 0, t, 0)),
                  jax.ShapeDtypeStruct((ns, dil, l // dil, D_ATTN), BF16))] * 3
    outs += [(pl.BlockSpec((None, N_HEADS, HEAD_DIM, tm), lambda s, t: (s, 0, 0, jnp.maximum(t - keep_from, 0))),
              jax.ShapeDtypeStruct((ns, N_HEADS, HEAD_DIM, n_keep), F32))] * 2
    outs += [nat(D_SSM),
             (state_spec, jax.ShapeDtypeStruct((ns, N_SSM_HEADS, SSM_HEAD_DIM, D_STATE), F32)),
             (pl.BlockSpec((None, SUBLANE, CONV_DIM), seq), jax.ShapeDtypeStruct((ns, SUBLANE, CONV_DIM), F32))]
    return pl.pallas_call(
        functools.partial(_inproj_ssd_kernel, keep_from=keep_from, n_tiles=nt),
        grid=(ns, nt),
        in_specs=[pl.BlockSpec((None, tm, D_MODEL), tile),
                  pl.BlockSpec((1, D_MODEL), fixed),
                  pl.BlockSpec((D_MODEL, O_DT), fixed),
                  pl.BlockSpec((D_MODEL, LANE), fixed),
                  pl.BlockSpec((None, SSM_CONV - 1, CONV_DIM), seq),
                  state_spec] + [pl.BlockSpec(p.shape, fixed) for p in ssd_params],
        out_specs=[o[0] for o in outs],
        out_shape=[o[1] for o in outs],
        scratch_shapes=[pltpu.VMEM((3, N_PAIRS, tm, LANE), F32),
                        pltpu.VMEM((3, 4, N_PAIRS, tm // 4, LANE), F32),
                        pltpu.VMEM((SUBLANE + tm, CONV_DIM), F32),
                        pltpu.VMEM((tm, D_SSM), F32),
                        pltpu.VMEM((tm, LANE), F32),
                        pltpu.VMEM((N_PAIRS, D_STATE, LANE), F32)],
        compiler_params=_params("parallel", "arbitrary"),
        name="inproj_ssd",
    )(x, g_mix, w_main, w_dt, prefix, state0, *ssd_params)


def _load_state(state0_ref, st_ref):
    for jp in range(N_PAIRS):
        st_ref[jp] = state0_ref[2 * jp:2 * jp + 2].reshape(2 * SSM_HEAD_DIM, D_STATE).T


def _store_state(st_ref, state_ref):
    for jp in range(N_PAIRS):
        state_ref[2 * jp:2 * jp + 2] = st_ref[jp].T.reshape(2, SSM_HEAD_DIM, D_STATE)


def _ssd_chunk(xwin, z, dt_raw, prm, st_ref, valid_len):
    cw_ref, cb_ref, dtb_ref, alog_ref, dskip_ref, gssm_ref = prm
    L = SSD_CHUNK
    conv = cb_ref[...]
    for k in range(SSM_CONV):
        conv = conv + cw_ref[k:k + 1, :] * xwin(k)
    xc = _silu(conv)
    xs = xc[:, :D_SSM]
    gn = N_SSM_GROUPS * D_STATE

    row = lax.broadcasted_iota(jnp.int32, (L, L), 0)
    col = lax.broadcasted_iota(jnp.int32, (L, L), 1)
    tri = row >= col
    lt64 = col < SSM_HEAD_DIM

    dt = _softplus(dt_raw + dtb_ref[...])
    if valid_len < L:
        dt = jnp.where(row < valid_len, dt, 0.0)
    a = dt * (-jnp.exp(alog_ref[...]))
    acum = a
    shift = 1
    while shift < L:
        acum = acum + jnp.where(row >= shift, pltpu.roll(acum, shift, 0), 0.0)
        shift *= 2
    acum_t = acum.T
    dt_t = dt.T
    e_slab = jnp.exp(acum)
    de_slab = jnp.exp(acum[L - 1:L, :] - acum) * dt

    def head_terms(h, cb):
        colb = jnp.broadcast_to(acum[:, h:h + 1], (L, L))
        rowb = jnp.broadcast_to(acum_t[h:h + 1, :], (L, L))
        decay = jnp.exp(jnp.where(tri, colb - rowb, NEG))
        m = (cb * decay * jnp.broadcast_to(dt_t[h:h + 1, :], (L, L))).astype(BF16)
        e = jnp.broadcast_to(e_slab[:, h:h + 1], (L, L))
        return m, e, jnp.broadcast_to(de_slab[:, h:h + 1], (L, L)), e[L - 1:L, :]

    pairs = []
    for g in range(N_SSM_GROUPS):
        bg = xc[:, D_SSM + g * D_STATE:D_SSM + (g + 1) * D_STATE]
        cg = xc[:, D_SSM + gn + g * D_STATE:D_SSM + gn + (g + 1) * D_STATE].astype(BF16)
        cb = _nt_dot(cg, bg.astype(BF16))
        bg_t = bg.T.astype(BF16)
        for i in range(N_PAIRS // N_SSM_GROUPS):
            jp = g * (N_PAIRS // N_SSM_GROUPS) + i
            xpair = xs[:, jp * LANE:(jp + 1) * LANE]
            xpair_b = xpair.astype(BF16)
            m_a, e_a, de_a, el_a = head_terms(2 * jp, cb)
            m_b, e_b, de_b, el_b = head_terms(2 * jp + 1, cb)
            st = st_ref[jp]
            y_diag = jnp.where(lt64, _dot(m_a, xpair_b), _dot(m_b, xpair_b))
            y_off = _dot(cg, st.astype(BF16)) * jnp.where(lt64, e_a, e_b)
            xd = (xpair * jnp.where(lt64, de_a, de_b)).astype(BF16)
            st_ref[jp] = st * jnp.where(lt64[0:1, :], el_a, el_b) + _dot(bg_t, xd)
            pairs.append(y_diag + y_off + dskip_ref[:, jp * LANE:(jp + 1) * LANE] * xpair)
    y = jnp.concatenate(pairs, axis=1)

    yf = y * _silu(z)
    gw = D_SSM // N_SSM_GROUPS
    normed = []
    for g in range(N_SSM_GROUPS):
        seg = yf[:, g * gw:(g + 1) * gw]
        normed.append(seg * lax.rsqrt(jnp.mean(seg * seg, axis=-1, keepdims=True) + EPS))
    return jnp.concatenate(normed, axis=1) * gssm_ref[...]


def _ssd_short_kernel(xbc_ref, z_ref, dt_ref, prefix_ref, state0_ref, cw_ref, cb_ref, dtb_ref, alog_ref,
                      dskip_ref, gssm_ref, y_ref, state_ref, xp_ref, st_ref):
    t_len = xbc_ref.shape[0]
    L = SSD_CHUNK
    lead = SUBLANE - (SSM_CONV - 1)
    xp_ref[0:lead, :] = jnp.zeros((lead, CONV_DIM), F32)
    xp_ref[lead:SUBLANE, :] = prefix_ref[...]
    xp_ref[SUBLANE:SUBLANE + t_len, :] = xbc_ref[...]
    xp_ref[SUBLANE + t_len:SUBLANE + L, :] = jnp.zeros((L - t_len, CONV_DIM), F32)
    _load_state(state0_ref, st_ref)
    pad = lambda a: jnp.concatenate([a, jnp.zeros((L - t_len, a.shape[1]), F32)], axis=0)
    y = _ssd_chunk(lambda k: xp_ref[lead + k:lead + k + L, :], pad(z_ref[...]), pad(dt_ref[...]),
                   (cw_ref, cb_ref, dtb_ref, alog_ref, dskip_ref, gssm_ref), st_ref, t_len)
    y_ref[...] = y[0:t_len].astype(y_ref.dtype)
    _store_state(st_ref, state_ref)


def _ssd_short(xbc, z, dt, prefix, state0, ssd_params):
    ns, t_len, _ = xbc.shape
    seq = lambda s: (s, 0, 0)
    state_spec = pl.BlockSpec((None, N_SSM_HEADS, SSM_HEAD_DIM, D_STATE), lambda s: (s, 0, 0, 0))
    return pl.pallas_call(
        _ssd_short_kernel,
        grid=(ns,),
        in_specs=[pl.BlockSpec((None, t_len, CONV_DIM), seq),
                  pl.BlockSpec((None, t_len, D_SSM), seq),
                  pl.BlockSpec((None, t_len, LANE), seq),
                  pl.BlockSpec((None, SSM_CONV - 1, CONV_DIM), seq),
                  state_spec] + [pl.BlockSpec(p.shape, lambda s: (0, 0)) for p in ssd_params],
        out_specs=[pl.BlockSpec((None, t_len, D_SSM), seq), state_spec],
        out_shape=[jax.ShapeDtypeStruct((ns, t_len, D_SSM), BF16),
                   jax.ShapeDtypeStruct((ns, N_SSM_HEADS, SSM_HEAD_DIM, D_STATE), F32)],
        scratch_shapes=[pltpu.VMEM((SUBLANE + SSD_CHUNK, CONV_DIM), F32),
                        pltpu.VMEM((N_PAIRS, D_STATE, LANE), F32)],
        compiler_params=_params("parallel"),
        name="ssd_short",
    )(xbc, z, dt, prefix, state0, *ssd_params)


ATTN_UNITS = 4


def _attn_stages(q_ref, kp_ref, kc_ref, vp_ref, vc_ref, bias_ref, o_ref, max_ref, den_ref, s_ref, p_ref, *,
                 dil, blk0, rows_of):
    lt64 = _lane_lt64((BLK, LANE))
    head_row = lax.broadcasted_iota(jnp.int32, (N_HEADS, BLK), 0)
    zero = jnp.zeros((BLK, LANE), BF16)

    def prev_of(g, p_ref_, c_ref_):
        if dil == 1:
            return (p_ref_.at[0], blk0) if g == 0 else (c_ref_.at[g - 1], None)
        return p_ref_.at[g], blk0

    def scores(g, slot):
        kprev, masked = prev_of(g, kp_ref, kc_ref)
        for jp in range(N_PAIRS):
            sl = slice(jp * LANE, (jp + 1) * LANE)
            qp = q_ref[g, :, sl]
            kcat = jnp.concatenate([kprev[:, sl], kc_ref[g, :, sl]], axis=0)
            for half in range(2):
                h = 2 * jp + half
                qm = jnp.where(lt64 if half == 0 else ~lt64, qp, zero)
                bias = bias_ref[h]
                if masked is not None:
                    bias = jnp.concatenate([jnp.where(masked, NEG, bias[0:BLK]), bias[BLK:]], axis=0)
                s_ref[slot, h] = _nt_dot(kcat, qm) + bias

    def softmax(g, slot):
        s = s_ref[slot]
        m = jnp.max(s, axis=1, keepdims=True)
        p = jnp.exp2(s - m)
        den = jnp.sum(p, axis=1, keepdims=True)
        p_ref[slot] = p.astype(BF16)
        m_rows = jnp.zeros((N_HEADS, BLK), F32)
        den_rows = jnp.zeros((N_HEADS, BLK), F32)
        for h in range(N_HEADS):
            m_rows = jnp.where(head_row == h, jnp.broadcast_to(m[h], (N_HEADS, BLK)), m_rows)
            den_rows = jnp.where(head_row == h, jnp.broadcast_to(den[h], (N_HEADS, BLK)), den_rows)
        m_nat = m_rows * (1.0 / LOG2E)
        max_ref[rows_of(g), :] = jnp.concatenate([m_nat, jnp.zeros((BLK - N_HEADS, BLK), F32)], axis=0).T
        den_ref[rows_of(g), :] = jnp.concatenate([den_rows, jnp.ones((BLK - N_HEADS, BLK), F32)], axis=0).T

    def values(g, slot):
        vprev, _ = prev_of(g, vp_ref, vc_ref)
        for jp in range(N_PAIRS):
            sl = slice(jp * LANE, (jp + 1) * LANE)
            vcat = jnp.concatenate([vprev[:, sl], vc_ref[g, :, sl]], axis=0)
            pv = [lax.dot_general(p_ref[slot, 2 * jp + half], vcat, (((0,), (0,)), ((), ())),
                                  preferred_element_type=F32) for half in range(2)]
            o_ref[jp, rows_of(g), :] = jnp.where(lt64, pv[0], pv[1])

    return scores, softmax, values


def _attn_prompt_kernel(q_ref, kp_ref, kc_ref, vp_ref, vc_ref, bias_ref, o_ref, max_ref, den_ref, s_ref, p_ref, *, dil):
    units = q_ref.shape[0]
    if dil == 1:
        rows_of = lambda g: slice(g * BLK, (g + 1) * BLK)
    else:
        rows_of = lambda g: pl.ds(pl.program_id(2) * units + g, BLK, stride=dil)
    scores, softmax, values = _attn_stages(q_ref, kp_ref, kc_ref, vp_ref, vc_ref, bias_ref, o_ref, max_ref, den_ref,
                                           s_ref, p_ref, dil=dil, blk0=pl.program_id(1) == 0, rows_of=rows_of)
    scores(0, 0)
    for g in range(units):
        softmax(g, g % 2)
        if g + 1 < units:
            scores(g + 1, (g + 1) % 2)
        values(g, g % 2)


def _attn_prompt_branch(q, k, v, bias, dil):
    bsz, _, l, _ = q.shape
    s = l * dil
    u = ATTN_UNITS
    if dil == 1:
        q, k, v = (t.reshape(bsz, l // BLK, BLK, D_ATTN) for t in (q, k, v))
        grid = (bsz, l // BLK // u, 1)
        cur = pl.BlockSpec((None, u, BLK, D_ATTN), lambda b, n, r: (b, n, 0, 0))
        prev = pl.BlockSpec((None, 1, BLK, D_ATTN), lambda b, n, r: (b, jnp.maximum(n * u - 1, 0), 0, 0))
        rows = u * BLK
    else:
        grid = (bsz, l // BLK, dil // u)
        cur = pl.BlockSpec((None, u, BLK, D_ATTN), lambda b, n, r: (b, r, n, 0))
        prev = pl.BlockSpec((None, u, BLK, D_ATTN), lambda b, n, r: (b, r, jnp.maximum(n - 1, 0), 0))
        rows = dil * BLK
    return pl.pallas_call(
        functools.partial(_attn_prompt_kernel, dil=dil),
        grid=grid,
        in_specs=[cur, prev, cur, prev, cur,
                  pl.BlockSpec((N_HEADS, 2 * BLK, BLK), lambda b, n, r: (0, 0, 0))],
        out_specs=[pl.BlockSpec((None, N_PAIRS, rows, LANE), lambda b, n, r: (b, 0, n, 0)),
                   pl.BlockSpec((None, rows, LANE), lambda b, n, r: (b, n, 0)),
                   pl.BlockSpec((None, rows, LANE), lambda b, n, r: (b, n, 0))],
        out_shape=[jax.ShapeDtypeStruct((bsz, N_PAIRS, s, LANE), F32),
                   jax.ShapeDtypeStruct((bsz, s, LANE), F32),
                   jax.ShapeDtypeStruct((bsz, s, LANE), F32)],
        scratch_shapes=[pltpu.VMEM((2, N_HEADS, 2 * BLK, BLK), F32), pltpu.VMEM((2, N_HEADS, 2 * BLK, BLK), BF16)],
        compiler_params=_params("parallel", "parallel", "arbitrary"),
        name=f"attn_prompt_d{dil}",
    )(q, k, k, v, v, bias)


def _attn_sample_kernel(q_ref, kn_ref, vn_ref, kc_ref, vc_ref, b16_ref, b4_ref, b1_ref, bn_ref, o_ref):
    t_len = q_ref.shape[0]
    w = kc_ref.shape[-1]
    lt64 = _lane_lt64((t_len, LANE))
    pad = jnp.zeros((BLK - t_len, D_ATTN), F32)
    kn = jnp.concatenate([kn_ref[...], pad], axis=0).astype(BF16)
    vn = jnp.concatenate([vn_ref[...], pad], axis=0).astype(BF16)
    outs = []
    for jp in range(N_PAIRS):
        sl = slice(jp * LANE, (jp + 1) * LANE)
        qp = q_ref[:, sl]
        qm = jnp.concatenate([jnp.where(lt64, qp, 0.0), jnp.where(lt64, 0.0, qp)], axis=0).astype(BF16)
        kt = kc_ref[2 * jp:2 * jp + 2].reshape(2 * HEAD_DIM, w).astype(BF16)
        vt = vc_ref[2 * jp:2 * jp + 2].reshape(2 * HEAD_DIM, w).astype(BF16)
        s = _dot(qm, kt)
        s_new = _nt_dot(qm, kn[:, sl])
        w4, w1 = w - 4 * BLK, w - BLK
        tiles = [s + b16_ref[jp], s[:, w4:] + b4_ref[jp], s[:, w1:] + b1_ref[jp]]
        tiles += [s_new + bn_ref[br, jp] for br in range(len(DILATIONS))]
        m = functools.reduce(jnp.maximum, [jnp.max(t, axis=-1, keepdims=True) for t in tiles])
        ps = [jnp.exp(t - m) for t in tiles]
        den = functools.reduce(lambda a, b: a + b, [jnp.sum(p, axis=-1, keepdims=True) for p in ps])
        p16, p4, p1 = ps[0], ps[1], ps[2]
        p_cache = jnp.concatenate([p16[:, :w4], p16[:, w4:w1] + p4[:, :w1 - w4],
                                   p16[:, w1:] + p4[:, w1 - w4:] + p1], axis=1)
        p_new = ps[3] + ps[4] + ps[5]
        o2 = (_nt_dot(p_cache.astype(BF16), vt) + _dot(p_new.astype(BF16), vn[:, sl])) / den
        outs.append(jnp.where(lt64, o2[0:t_len], o2[t_len:2 * t_len]))
    o_ref[...] = jnp.concatenate(outs, axis=1).astype(o_ref.dtype)


def _attn_sample(q, kn, vn, kc, vc, b16, b4, b1, bn):
    ns, t_len, _ = q.shape
    w = kc.shape[-1]
    assert w == MAX_DISTANCE and t_len == SUBLANE
    tok = pl.BlockSpec((None, t_len, D_ATTN), lambda s: (s, 0, 0))
    cache = pl.BlockSpec((None, N_HEADS, HEAD_DIM, w), lambda s: (s, 0, 0, 0))
    full = lambda a: pl.BlockSpec(a.shape, lambda s: (0,) * a.ndim)
    return pl.pallas_call(
        _attn_sample_kernel,
        grid=(ns,),
        in_specs=[tok, tok, tok, cache, cache, full(b16), full(b4), full(b1), full(bn)],
        out_specs=tok,
        out_shape=jax.ShapeDtypeStruct((ns, t_len, D_ATTN), BF16),
        compiler_params=_params("parallel"),
        name="attn_sample",
    )(q, kn, vn, kc, vc, b16, b4, b1, bn)


N_FFN_WEIGHTS = 11


def _merge_branches(parts, expand_ref):
    maxes = [max_ref[...] for _, max_ref, _ in parts]
    mx = functools.reduce(jnp.maximum, maxes)
    ws = [jnp.exp(m - mx) for m in maxes]
    den = functools.reduce(lambda a, b: a + b, [w * den_ref[...] for w, (_, _, den_ref) in zip(ws, parts)])
    attn = None
    for (o_ref, _, _), w in zip(parts, ws):
        wn = w / den
        hi = wn.astype(BF16)
        lo = (wn - hi.astype(F32)).astype(BF16)
        wexp = _dot(jnp.concatenate([hi, lo], axis=1), expand_ref[...])
        o = jnp.concatenate([o_ref[jp] for jp in range(N_PAIRS)], axis=1)
        attn = wexp * o if attn is None else attn + wexp * o
    return attn.astype(BF16)


def _ffn_core(x_ref, attn, yssm_ref, p_ref, w, y_ref, act_ref, conv):
    (_, wout_ref, gffn_ref, wup_ref, _, _, wdown_ref, wple_ref, gple_ref, wgate_ref, gfin_ref) = w
    h1 = x_ref[...] + _dot(attn, wout_ref[0:D_ATTN, :]) + _dot(yssm_ref[...], wout_ref[D_ATTN:, :])
    hn = _rms(h1, gffn_ref[...]).astype(BF16)
    for j in range(D_FF // FF_CHUNK):
        c0 = j * FF_CHUNK
        u_gate = conv(_dot(hn, wup_ref[:, c0:c0 + FF_CHUNK]), c0)
        u_lin = conv(_dot(hn, wup_ref[:, D_FF + c0:D_FF + c0 + FF_CHUNK]), D_FF + c0)
        act_ref[:, c0:c0 + FF_CHUNK] = (_silu(u_gate) * u_lin).astype(BF16)
    h2 = h1 + _dot(act_ref[...], wdown_ref[...])
    e = _rms(_dot(p_ref[...].astype(BF16), wple_ref[...]), gple_ref[...])
    h3 = h2 + jax.nn.sigmoid(_dot(h2.astype(BF16), wgate_ref[...])) * e
    y_ref[...] = _rms(h3, gfin_ref[...])


def _mix_ffn_seg_kernel(x_ref, attn_ref, yssm_ref, p_ref, fpre_ref, *rest):
    w = rest[:N_FFN_WEIGHTS]
    y_ref, uout_ref, act_ref = rest[N_FFN_WEIGHTS:]
    fcw_ref, fcb_ref = w[4], w[5]
    tm = x_ref.shape[0]
    rin = lax.broadcasted_iota(jnp.int32, (tm, FF_CHUNK), 0) & (SUBLANE - 1)

    def conv(u, c0):
        cs = slice(c0, c0 + FF_CHUNK)
        pre = fpre_ref[:, cs]
        um1 = jnp.where(rin == 0, pltpu.roll(pre, tm - 1, 0), pltpu.roll(u, 1, 0))
        um2 = jnp.where(rin < 2, pre, pltpu.roll(u, 2, 0))
        uout_ref[:, cs] = u
        return fcb_ref[:, cs] + fcw_ref[0:1, cs] * um2 + fcw_ref[1:2, cs] * um1 + fcw_ref[2:3, cs] * u

    _ffn_core(x_ref, attn_ref[...], yssm_ref, p_ref, w, y_ref, act_ref, conv)


def _mix_ffn_kernel(x_ref, *rest, n_parts, n_tiles):
    parts = [rest[3 * b:3 * b + 3] for b in range(n_parts)]
    yssm_ref, p_ref, fpre_ref = rest[3 * n_parts:3 * n_parts + 3]
    rest = rest[3 * n_parts + 3:]
    w = rest[:N_FFN_WEIGHTS]
    y_ref, uout_ref, act_ref, tail_ref = rest[N_FFN_WEIGHTS:]
    fcw_ref, fcb_ref = w[4], w[5]
    tm = x_ref.shape[0]
    t = pl.program_id(1)

    @pl.when(t == 0)
    def _init():
        tail_ref[...] = jnp.zeros(tail_ref.shape, F32)
        tail_ref[SUBLANE - (FFN_CONV - 1):SUBLANE, :] = fpre_ref[...]

    attn = _merge_branches(parts, w[0])

    rin8 = lax.broadcasted_iota(jnp.int32, (SUBLANE, FF_CHUNK), 0)

    def conv(u, c0):
        cs = slice(c0, c0 + FF_CHUNK)
        tail = tail_ref[:, cs]
        tail_ref[:, cs] = u[tm - SUBLANE:tm, :]
        r1, r2 = pltpu.roll(u, 1, 0), pltpu.roll(u, 2, 0)
        um1 = jnp.concatenate([jnp.where(rin8 < 1, pltpu.roll(tail, 1, 0), r1[0:SUBLANE]), r1[SUBLANE:]], axis=0)
        um2 = jnp.concatenate([jnp.where(rin8 < 2, pltpu.roll(tail, 2, 0), r2[0:SUBLANE]), r2[SUBLANE:]], axis=0)
        return fcb_ref[:, cs] + fcw_ref[0:1, cs] * um2 + fcw_ref[1:2, cs] * um1 + fcw_ref[2:3, cs] * u

    _ffn_core(x_ref, attn, yssm_ref, p_ref, w, y_ref, act_ref, conv)

    @pl.when(t == n_tiles - 1)
    def _fin():
        uout_ref[...] = tail_ref[...]


def _mix_ffn_seg(x, attn, yssm, p, fpre, weights):
    rows = x.shape[0]
    args = [x, attn, yssm, p, fpre, *weights]
    full = lambda a: pl.BlockSpec(a.shape, lambda i: (0, 0))
    return pl.pallas_call(
        _mix_ffn_seg_kernel,
        grid=(1,),
        in_specs=[full(a) for a in args],
        out_specs=[pl.BlockSpec((rows, D_MODEL), lambda i: (0, 0)), pl.BlockSpec((rows, 2 * D_FF), lambda i: (0, 0))],
        out_shape=[jax.ShapeDtypeStruct((rows, D_MODEL), F32), jax.ShapeDtypeStruct((rows, 2 * D_FF), F32)],
        scratch_shapes=[pltpu.VMEM((rows, D_FF), BF16)],
        compiler_params=_params("arbitrary"),
        name="mix_ffn_seg",
    )(*args)


def _mix_ffn(x, parts, yssm, p, fpre, weights, *, tm):
    bsz, s, _ = x.shape
    nt = s // tm
    tile = lambda b, t: (b, t, 0)
    seq = lambda b, t: (b, 0, 0)
    fixed = lambda b, t: (0, 0)
    in_specs = [pl.BlockSpec((None, tm, D_MODEL), tile)]
    args = [x]
    for part in parts:
        in_specs += [pl.BlockSpec((None, N_PAIRS, tm, LANE), lambda b, t: (b, 0, t, 0)),
                     pl.BlockSpec((None, tm, LANE), tile), pl.BlockSpec((None, tm, LANE), tile)]
        args += list(part)
    in_specs += [pl.BlockSpec((None, tm, D_SSM), tile), pl.BlockSpec((None, tm, D_PLE), tile),
                 pl.BlockSpec((None, FFN_CONV - 1, 2 * D_FF), seq)]
    args += [yssm, p, fpre]
    in_specs += [pl.BlockSpec(w.shape, fixed) for w in weights]
    args += list(weights)
    return pl.pallas_call(
        functools.partial(_mix_ffn_kernel, n_parts=len(parts), n_tiles=nt),
        grid=(bsz, nt),
        in_specs=in_specs,
        out_specs=[pl.BlockSpec((None, tm, D_MODEL), tile), pl.BlockSpec((None, SUBLANE, 2 * D_FF), seq)],
        out_shape=[jax.ShapeDtypeStruct((bsz, s, D_MODEL), F32), jax.ShapeDtypeStruct((bsz, SUBLANE, 2 * D_FF), F32)],
        scratch_shapes=[pltpu.VMEM((tm, D_FF), BF16), pltpu.VMEM((SUBLANE, 2 * D_FF), F32)],
        compiler_params=_params("parallel", "arbitrary"),
        name="mix_ffn",
    )(*args)


def _pad_lanes(v, width=LANE):
    return jnp.pad(v.astype(F32), (0, width - v.shape[0]))[None, :]


def kernel(x_prompt, x_sample, p_prompt, p_sample, cache_k, cache_v, state_ssm, state_conv, state_ffn_conv,
           rel_bias, g_mix, w_in, conv_w, conv_b, dt_bias, a_log, d_skip, g_ssm, w_out, g_ffn, w_up,
           ffn_conv_w, ffn_conv_b, w_down, w_ple_proj, g_ple, w_ple_gate, g_final):
    assert w_in.shape[0] == 1, "one layer"
    bp, s, _ = x_prompt.shape
    nsamp, t_len, _ = x_sample.shape
    n_keep = min(MAX_DISTANCE, s)

    w_main = _cast_w_main(w_in)
    w_dt = jnp.pad(w_in[0, :, O_DT:], ((0, 0), (0, LANE - (w_in.shape[2] - O_DT)))).astype(BF16)
    gmix = g_mix[0][None, :]
    ssd_params = (conv_w[0], conv_b[0][None, :], _pad_lanes(dt_bias[0]), _pad_lanes(a_log[0]),
                  jnp.repeat(d_skip[0], SSM_HEAD_DIM)[None, :], g_ssm[0][None, :])
    expand = (np.arange(2 * LANE)[:, None] % LANE == (np.arange(D_ATTN)[None, :] // HEAD_DIM)).astype(np.float32)
    ffn_weights = (jnp.asarray(expand, BF16), w_out[0].astype(BF16), g_ffn[0][None, :], w_up[0].astype(BF16),
                   ffn_conv_w[0], ffn_conv_b[0][None, :], w_down[0].astype(BF16), w_ple_proj[0].astype(BF16),
                   g_ple[0][None, :], w_ple_gate[0].astype(BF16), g_final[None, :])
    bias_p, b16, b4, b1, bn = _bias_tables(rel_bias, t_len)

    (q1, k1, v1, q4, k4, v4, q16, k16, v16, kt, vt, yssm_p, ssm_p, ctail_p) = _inproj_ssd(
        x_prompt, gmix, w_main, w_dt,
        jnp.zeros((bp, SSM_CONV - 1, CONV_DIM), F32),
        jnp.zeros((bp, N_SSM_HEADS, SSM_HEAD_DIM, D_STATE), F32),
        ssd_params, tm=512, n_keep=n_keep)
    parts = [_attn_prompt_branch(q1[:, None], k1[:, None], v1[:, None], bias_p[0], 1),
             _attn_prompt_branch(q4, k4, v4, bias_p[1], 4),
             _attn_prompt_branch(q16, k16, v16, bias_p[2], 16)]
    y_prompt, tail_p = _mix_ffn(x_prompt, parts, yssm_p, p_prompt[0],
                                jnp.zeros((bp, FFN_CONV - 1, 2 * D_FF), F32), ffn_weights, tm=512)
    k_prompt = jnp.transpose(kt, (0, 3, 1, 2))[None]
    v_prompt = jnp.transpose(vt, (0, 3, 1, 2))[None]
    conv_prompt = ctail_p[:, SUBLANE - (SSM_CONV - 1):][None]
    ffn_conv_prompt = tail_p[:, SUBLANE - (FFN_CONV - 1):][None]

    rows = nsamp * t_len
    qs, ks, vs, zs, xbcs, dts = _inproj(x_sample.reshape(rows, D_MODEL), gmix, w_main, w_dt)
    s3 = lambda a: a.reshape(nsamp, t_len, a.shape[-1])
    yssm_s, ssm_s = _ssd_short(s3(xbcs), s3(zs), s3(dts), state_conv[0], state_ssm[0], ssd_params)
    attn_s = _attn_sample(s3(qs), s3(ks), s3(vs),
                          jnp.transpose(cache_k[0], (0, 2, 3, 1)), jnp.transpose(cache_v[0], (0, 2, 3, 1)),
                          b16, b4, b1, bn)
    fpre = jnp.pad(state_ffn_conv[0], ((0, 0), (0, t_len - (FFN_CONV - 1)), (0, 0))).reshape(rows, 2 * D_FF)
    y_s, u_s = _mix_ffn_seg(x_sample.reshape(rows, D_MODEL), attn_s.reshape(rows, D_ATTN),
                            yssm_s.reshape(rows, D_SSM), p_sample[0].reshape(rows, D_PLE), fpre, ffn_weights)
    y_sample = y_s.reshape(nsamp, t_len, D_MODEL)
    k_sample = ks.reshape(1, nsamp, t_len, N_HEADS, HEAD_DIM)
    v_sample = vs.reshape(1, nsamp, t_len, N_HEADS, HEAD_DIM)
    conv_sample = s3(xbcs)[:, t_len - (SSM_CONV - 1):][None]
    ffn_conv_sample = u_s.reshape(nsamp, t_len, 2 * D_FF)[:, t_len - (FFN_CONV - 1):][None]

    return (y_prompt, y_sample, k_prompt, v_prompt, k_sample, v_sample,
            ssm_p[None], ssm_s[None], conv_prompt, conv_sample, ffn_conv_prompt, ffn_conv_sample)
```

```python
import functools
import math

import numpy as np
import jax
import jax.numpy as jnp
from jax import lax
from jax.experimental import pallas as pl
from jax.experimental.pallas import tpu as pltpu

F32 = jnp.float32
BF16 = jnp.bfloat16

D_MODEL = 1024
HEAD_DIM = 64
N_HEADS = 8
D_ATTN = N_HEADS * HEAD_DIM
N_PAIRS = N_HEADS // 2
DILATIONS = (1, 4, 16)
N_STEPS = 128
BLK = 128
N_BUCKETS = 32
MAX_DISTANCE = 2048
D_SSM = 512
N_SSM_HEADS = 8
SSM_HEAD_DIM = 64
D_STATE = 128
N_SSM_GROUPS = 2
SSM_CONV = 4
CONV_DIM = D_SSM + 2 * N_SSM_GROUPS * D_STATE
SSD_CHUNK = 128
D_FF = 2816
FFN_CONV = 3
D_PLE = 256
EPS = 1e-6
NEG = -1e30
LOG2E = math.log2(math.e)

LANE = 128
SUBLANE = 8
FF_CHUNK = 256
VMEM_LIMIT = 56 * 1024 * 1024

O_Q, O_K, O_V, O_Z, O_XBC, O_DT = 0, 512, 1024, 1536, 2048, 3072


def _rel_bucket_np(dist):
    dist = np.asarray(dist, np.int32)
    max_exact = N_BUCKETS // 2
    d = np.maximum(dist, 1).astype(np.float32)
    large = max_exact + (np.log(d / np.float32(max_exact)) / np.float32(math.log(MAX_DISTANCE / max_exact))
                         * np.float32(N_BUCKETS - max_exact)).astype(np.int32)
    large = np.minimum(large, N_BUCKETS - 1)
    return np.where(dist < max_exact, dist, large)


def _nt_dot(a, b):
    return lax.dot_general(a, b, (((1,), (1,)), ((), ())), preferred_element_type=F32)


def _dot(a, b):
    return jnp.dot(a, b, preferred_element_type=F32)


def _silu(x):
    return x * jax.nn.sigmoid(x)


def _softplus(x):
    return jnp.maximum(x, 0.0) + jnp.log1p(jnp.exp(-jnp.abs(x)))


def _rms(x, g):
    return x * lax.rsqrt(jnp.mean(x * x, axis=-1, keepdims=True) + EPS) * g


def _lane_lt64(shape):
    return lax.broadcasted_iota(jnp.int32, shape, len(shape) - 1) < HEAD_DIM


def _params(*sem):
    return pltpu.CompilerParams(dimension_semantics=sem, vmem_limit_bytes=VMEM_LIMIT)


def _bias_kernel(rb_ref, rbt_ref, ig_ref, i16_ref, i4_ref, i1_ref, in_ref, tp_ref, t16_ref, t4_ref, t1_ref, tn_ref):
    def lookup(idx, h):
        def body(b, acc):
            return jnp.where(idx == b, rb_ref[b, h], acc)
        return lax.fori_loop(0, N_BUCKETS, body, jnp.full(idx.shape, NEG, F32), unroll=True)

    for br in range(len(DILATIONS)):
        idx = jnp.broadcast_to(ig_ref[br], (N_HEADS, 2 * BLK))
        gen = jnp.full((N_HEADS, 2 * BLK), NEG, F32)
        for b in range(N_BUCKETS):
            gen = jnp.where(idx == b, jnp.broadcast_to(rbt_ref[:, b:b + 1], (N_HEADS, 2 * BLK)), gen)
        for h in range(N_HEADS):
            rows = jnp.broadcast_to(gen[h:h + 1, :], (BLK, 2 * BLK))
            tp_ref[br, h] = (pltpu.roll(rows, 0, 1, stride=1, stride_axis=0) * LOG2E).T
    for h in range(N_HEADS):
        jp, half = divmod(h, 2)
        rs = slice(half * SUBLANE, (half + 1) * SUBLANE)
        t16_ref[jp, rs, :] = lookup(i16_ref[...], h)
        t4_ref[jp, rs, :] = lookup(i4_ref[...], h)
        t1_ref[jp, rs, :] = lookup(i1_ref[...], h)
        for br in range(len(DILATIONS)):
            tn_ref[br, jp, rs, :] = lookup(in_ref[br], h)


def _bucket_maps(t_len):
    j = BLK - np.arange(2 * BLK)[None, :]
    prompt = np.stack([np.where(j >= 0, _rel_bucket_np(np.clip(j, 0, N_STEPS) * d), -1) for d in DILATIONS])

    t = np.arange(t_len)[:, None]

    def sample_map(diff, dil):
        ok = (diff >= 0) & (diff % dil == 0) & (diff // dil <= N_STEPS)
        return np.where(ok, _rel_bucket_np(np.maximum(diff, 0)), -1).astype(np.int32)

    w = np.arange(MAX_DISTANCE)[None, :]
    cache = {d: sample_map(MAX_DISTANCE + t - w, d) for d in DILATIONS}
    g = np.arange(LANE)[None, :]
    new = np.stack([np.where(g < t_len, sample_map(t - g, d), -1) for d in DILATIONS])
    return (prompt.astype(np.int32), cache[16], cache[4][:, MAX_DISTANCE - 4 * BLK:],
            cache[1][:, MAX_DISTANCE - BLK:], new.astype(np.int32))


def _bias_tables(rel_bias, t_len):
    maps = _bucket_maps(t_len)
    nb = len(DILATIONS)
    shapes = [(nb, N_HEADS, 2 * BLK, BLK), (N_PAIRS, 2 * t_len, MAX_DISTANCE), (N_PAIRS, 2 * t_len, 4 * BLK),
              (N_PAIRS, 2 * t_len, BLK), (nb, N_PAIRS, 2 * t_len, LANE)]
    return pl.pallas_call(
        _bias_kernel,
        in_specs=[pl.BlockSpec(memory_space=pltpu.SMEM)] + [pl.BlockSpec(memory_space=pltpu.VMEM)] * 6,
        out_specs=[pl.BlockSpec(memory_space=pltpu.VMEM)] * 5,
        out_shape=[jax.ShapeDtypeStruct(s, F32) for s in shapes],
        compiler_params=pltpu.CompilerParams(vmem_limit_bytes=VMEM_LIMIT),
        name="bias_tables",
    )(rel_bias, rel_bias.T, *[jnp.asarray(m) for m in maps])


W_CAST_COLS = 384


def _cast_kernel(wt_ref, o_ref):
    o_ref[...] = wt_ref[...].T.astype(BF16)


def _cast_w_main(w_in):
    return pl.pallas_call(
        _cast_kernel,
        grid=(O_DT // W_CAST_COLS,),
        in_specs=[pl.BlockSpec((W_CAST_COLS, D_MODEL), lambda i: (i, 0))],
        out_specs=pl.BlockSpec((D_MODEL, W_CAST_COLS), lambda i: (0, i)),
        out_shape=jax.ShapeDtypeStruct((D_MODEL, O_DT), BF16),
        compiler_params=_params("parallel"),
        name="cast_w_in",
    )(jnp.transpose(w_in[0]))


def _inproj_kernel(x_ref, g_ref, w_ref, wdt_ref, q_ref, k_ref, v_ref, z_ref, xbc_ref, dt_ref):
    xn = _rms(x_ref[...], g_ref[...]).astype(BF16)

    def proj(lo, hi):
        return _dot(xn, w_ref[:, lo:hi])

    q_ref[...] = proj(O_Q, O_K) * (HEAD_DIM ** -0.5)
    k_ref[...] = proj(O_K, O_V)
    v_ref[...] = proj(O_V, O_Z)
    z_ref[...] = proj(O_Z, O_XBC)
    xbc_ref[...] = proj(O_XBC, O_DT)
    dt_ref[...] = _dot(xn, wdt_ref[...])


def _inproj(x, g_mix, w_main, w_dt):
    rows = x.shape[0]
    full = lambda a: pl.BlockSpec(a.shape, lambda i: (0, 0))
    widths = (D_ATTN, D_ATTN, D_ATTN, D_SSM, CONV_DIM, LANE)
    return pl.pallas_call(
        _inproj_kernel,
        grid=(1,),
        in_specs=[full(x), full(g_mix), full(w_main), full(w_dt)],
        out_specs=[pl.BlockSpec((rows, w), lambda i: (0, 0)) for w in widths],
        out_shape=[jax.ShapeDtypeStruct((rows, w), F32) for w in widths],
        compiler_params=_params("arbitrary"),
        name="inproj_sample",
    )(x, g_mix, w_main, w_dt)


def _inproj_ssd_kernel(x_ref, g_ref, w_ref, wdt_ref, prefix_ref, state0_ref, cw_ref, cb_ref, dtb_ref, alog_ref, dskip_ref,
                       gssm_ref, q1, k1, v1, q4, k4, v4, q16, k16, v16, kt_ref, vt_ref, y_ref, state_ref, ctail_ref,
                       perm_ref, mid_ref, xp_ref, z_scr, dt_scr, st_ref, *, keep_from, n_tiles):
    tm = x_ref.shape[0]
    L = SSD_CHUNK
    t = pl.program_id(1)

    @pl.when(t == 0)
    def _init():
        xp_ref[0:SUBLANE, :] = jnp.zeros((SUBLANE, CONV_DIM), F32)
        xp_ref[SUBLANE - (SSM_CONV - 1):SUBLANE, :] = prefix_ref[...]
        _load_state(state0_ref, st_ref)

    xn = _rms(x_ref[...], g_ref[...]).astype(BF16)

    def proj(lo, hi):
        return _dot(xn, w_ref[:, lo:hi])

    xp_ref[SUBLANE:SUBLANE + tm, :] = proj(O_XBC, O_DT)
    dt_scr[...] = _dot(xn, wdt_ref[...])
    z_scr[...] = proj(O_Z, O_XBC)

    def project_attn(i):
        lo, scale, nat, r4, r16 = ((O_Q, HEAD_DIM ** -0.5 * LOG2E, q1, q4, q16), (O_K, None, k1, k4, k16),
                                   (O_V, None, v1, v4, v16))[i]
        val = proj(lo, lo + D_ATTN)
        if scale is not None:
            val = val * scale
        nat[...] = val.astype(BF16)
        for jp in range(N_PAIRS):
            perm_ref[i, jp] = val[:, jp * LANE:(jp + 1) * LANE]
        for ra in range(4):
            for jp in range(N_PAIRS):
                mid_ref[i, ra, jp] = perm_ref[i, jp, pl.ds(ra, tm // 4, stride=4), :]
            r4[ra] = jnp.concatenate([mid_ref[i, ra, jp] for jp in range(N_PAIRS)], axis=1).astype(BF16)
            for rb in range(4):
                rows = [mid_ref[i, ra, jp, pl.ds(rb, tm // 16, stride=4), :] for jp in range(N_PAIRS)]
                r16[ra + 4 * rb] = jnp.concatenate(rows, axis=1).astype(BF16)

    prm = (cw_ref, cb_ref, dtb_ref, alog_ref, dskip_ref, gssm_ref)
    n_chunks = tm // L
    assert n_chunks >= 3
    for c in range(n_chunks):
        base = SUBLANE + c * L - (SSM_CONV - 1)
        y = _ssd_chunk(lambda k, base=base: xp_ref[base + k:base + k + L, :],
                       z_scr[c * L:(c + 1) * L, :], dt_scr[c * L:(c + 1) * L, :], prm, st_ref, L)
        y_ref[c * L:(c + 1) * L, :] = y.astype(y_ref.dtype)
        if c < 3:
            project_attn(c)
    xp_ref[0:SUBLANE, :] = xp_ref[tm:tm + SUBLANE, :]

    @pl.when(t == n_tiles - 1)
    def _fin():
        _store_state(st_ref, state_ref)
        ctail_ref[...] = xp_ref[0:SUBLANE, :]

    @pl.when(t >= keep_from)
    def _keep():
        for i, out_ref in ((1, kt_ref), (2, vt_ref)):
            for jp in range(N_PAIRS):
                out_ref[2 * jp:2 * jp + 2] = perm_ref[i, jp].T.reshape(2, HEAD_DIM, tm)


def _inproj_ssd(x, g_mix, w_main, w_dt, prefix, state0, ssd_params, *, tm, n_keep):
    ns, l, _ = x.shape
    nt = l // tm
    keep_from = (l - n_keep) // tm
    tile = lambda s, t: (s, t, 0)
    seq = lambda s, t: (s, 0, 0)
    fixed = lambda s, t: (0, 0)
    state_spec = pl.BlockSpec((None, N_SSM_HEADS, SSM_HEAD_DIM, D_STATE), lambda s, t: (s, 0, 0, 0))
    nat = lambda w: (pl.BlockSpec((None, tm, w), tile), jax.ShapeDtypeStruct((ns, l, w), BF16))
    outs = [nat(D_ATTN)] * 3
    for dil in (4, 16):
        outs += [(pl.BlockSpec((None, dil, tm // dil, D_ATTN), lambda s, t: (s, 0, t, 0)),
                  jax.ShapeDtypeStruct((ns, dil, l // dil, D_ATTN), BF16))] * 3
    outs += [(pl.BlockSpec((None, N_HEADS, HEAD_DIM, tm), lambda s, t: (s, 0, 0, jnp.maximum(t - keep_from, 0))),
              jax.ShapeDtypeStruct((ns, N_HEADS, HEAD_DIM, n_keep), F32))] * 2
    outs += [nat(D_SSM),
             (state_spec, jax.ShapeDtypeStruct((ns, N_SSM_HEADS, SSM_HEAD_DIM, D_STATE), F32)),
             (pl.BlockSpec((None, SUBLANE, CONV_DIM), seq), jax.ShapeDtypeStruct((ns, SUBLANE, CONV_DIM), F32))]
    return pl.pallas_call(
        functools.partial(_inproj_ssd_kernel, keep_from=keep_from, n_tiles=nt),
        grid=(ns, nt),
        in_specs=[pl.BlockSpec((None, tm, D_MODEL), tile),
                  pl.BlockSpec((1, D_MODEL), fixed),
                  pl.BlockSpec((D_MODEL, O_DT), fixed),
                  pl.BlockSpec((D_MODEL, LANE), fixed),
                  pl.BlockSpec((None, SSM_CONV - 1, CONV_DIM), seq),
                  state_spec] + [pl.BlockSpec(p.shape, fixed) for p in ssd_params],
        out_specs=[o[0] for o in outs],
        out_shape=[o[1] for o in outs],
        scratch_shapes=[pltpu.VMEM((3, N_PAIRS, tm, LANE), F32),
                        pltpu.VMEM((3, 4, N_PAIRS, tm // 4, LANE), F32),
                        pltpu.VMEM((SUBLANE + tm, CONV_DIM), F32),
                        pltpu.VMEM((tm, D_SSM), F32),
                        pltpu.VMEM((tm, LANE), F32),
                        pltpu.VMEM((N_PAIRS, D_STATE, LANE), F32)],
        compiler_params=_params("parallel", "arbitrary"),
        name="inproj_ssd",
    )(x, g_mix, w_main, w_dt, prefix, state0, *ssd_params)


def _load_state(state0_ref, st_ref):
    for jp in range(N_PAIRS):
        st_ref[jp] = state0_ref[2 * jp:2 * jp + 2].reshape(2 * SSM_HEAD_DIM, D_STATE).T


def _store_state(st_ref, state_ref):
    for jp in range(N_PAIRS):
        state_ref[2 * jp:2 * jp + 2] = st_ref[jp].T.reshape(2, SSM_HEAD_DIM, D_STATE)


def _ssd_chunk(xwin, z, dt_raw, prm, st_ref, valid_len):
    cw_ref, cb_ref, dtb_ref, alog_ref, dskip_ref, gssm_ref = prm
    L = SSD_CHUNK
    conv = cb_ref[...]
    for k in range(SSM_CONV):
        conv = conv + cw_ref[k:k + 1, :] * xwin(k)
    xc = _silu(conv)
    xs = xc[:, :D_SSM]
    gn = N_SSM_GROUPS * D_STATE

    row = lax.broadcasted_iota(jnp.int32, (L, L), 0)
    col = lax.broadcasted_iota(jnp.int32, (L, L), 1)
    tri = row >= col
    lt64 = col < SSM_HEAD_DIM

    dt = _softplus(dt_raw + dtb_ref[...])
    if valid_len < L:
        dt = jnp.where(row < valid_len, dt, 0.0)
    a = dt * (-jnp.exp(alog_ref[...]))
    acum = a
    shift = 1
    while shift < L:
        acum = acum + jnp.where(row >= shift, pltpu.roll(acum, shift, 0), 0.0)
        shift *= 2
    acum_t = acum.T
    dt_t = dt.T
    e_slab = jnp.exp(acum)
    de_slab = jnp.exp(acum[L - 1:L, :] - acum) * dt

    def head_terms(h, cb):
        colb = jnp.broadcast_to(acum[:, h:h + 1], (L, L))
        rowb = jnp.broadcast_to(acum_t[h:h + 1, :], (L, L))
        decay = jnp.exp(jnp.where(tri, colb - rowb, NEG))
        m = (cb * decay * jnp.broadcast_to(dt_t[h:h + 1, :], (L, L))).astype(BF16)
        e = jnp.broadcast_to(e_slab[:, h:h + 1], (L, L))
        return m, e, jnp.broadcast_to(de_slab[:, h:h + 1], (L, L)), e[L - 1:L, :]

    pairs = []
    for g in range(N_SSM_GROUPS):
        bg = xc[:, D_SSM + g * D_STATE:D_SSM + (g + 1) * D_STATE]
        cg = xc[:, D_SSM + gn + g * D_STATE:D_SSM + gn + (g + 1) * D_STATE].astype(BF16)
        cb = _nt_dot(cg, bg.astype(BF16))
        bg_t = bg.T.astype(BF16)
        for i in range(N_PAIRS // N_SSM_GROUPS):
            jp = g * (N_PAIRS // N_SSM_GROUPS) + i
            xpair = xs[:, jp * LANE:(jp + 1) * LANE]
            xpair_b = xpair.astype(BF16)
            m_a, e_a, de_a, el_a = head_terms(2 * jp, cb)
            m_b, e_b, de_b, el_b = head_terms(2 * jp + 1, cb)
            st = st_ref[jp]
            y_diag = jnp.where(lt64, _dot(m_a, xpair_b), _dot(m_b, xpair_b))
            y_off = _dot(cg, st.astype(BF16)) * jnp.where(lt64, e_a, e_b)
            xd = (xpair * jnp.where(lt64, de_a, de_b)).astype(BF16)
            st_ref[jp] = st * jnp.where(lt64[0:1, :], el_a, el_b) + _dot(bg_t, xd)
            pairs.append(y_diag + y_off + dskip_ref[:, jp * LANE:(jp + 1) * LANE] * xpair)
    y = jnp.concatenate(pairs, axis=1)

    yf = y * _silu(z)
    gw = D_SSM // N_SSM_GROUPS
    normed = []
    for g in range(N_SSM_GROUPS):
        seg = yf[:, g * gw:(g + 1) * gw]
        normed.append(seg * lax.rsqrt(jnp.mean(seg * seg, axis=-1, keepdims=True) + EPS))
    return jnp.concatenate(normed, axis=1) * gssm_ref[...]


def _ssd_short_kernel(xbc_ref, z_ref, dt_ref, prefix_ref, state0_ref, cw_ref, cb_ref, dtb_ref, alog_ref,
                      dskip_ref, gssm_ref, y_ref, state_ref, xp_ref, st_ref):
    t_len = xbc_ref.shape[0]
    L = SSD_CHUNK
    lead = SUBLANE - (SSM_CONV - 1)
    xp_ref[0:lead, :] = jnp.zeros((lead, CONV_DIM), F32)
    xp_ref[lead:SUBLANE, :] = prefix_ref[...]
    xp_ref[SUBLANE:SUBLANE + t_len, :] = xbc_ref[...]
    xp_ref[SUBLANE + t_len:SUBLANE + L, :] = jnp.zeros((L - t_len, CONV_DIM), F32)
    _load_state(state0_ref, st_ref)
    pad = lambda a: jnp.concatenate([a, jnp.zeros((L - t_len, a.shape[1]), F32)], axis=0)
    y = _ssd_chunk(lambda k: xp_ref[lead + k:lead + k + L, :], pad(z_ref[...]), pad(dt_ref[...]),
                   (cw_ref, cb_ref, dtb_ref, alog_ref, dskip_ref, gssm_ref), st_ref, t_len)
    y_ref[...] = y[0:t_len].astype(y_ref.dtype)
    _store_state(st_ref, state_ref)


def _ssd_short(xbc, z, dt, prefix, state0, ssd_params):
    ns, t_len, _ = xbc.shape
    seq = lambda s: (s, 0, 0)
    state_spec = pl.BlockSpec((None, N_SSM_HEADS, SSM_HEAD_DIM, D_STATE), lambda s: (s, 0, 0, 0))
    return pl.pallas_call(
        _ssd_short_kernel,
        grid=(ns,),
        in_specs=[pl.BlockSpec((None, t_len, CONV_DIM), seq),
                  pl.BlockSpec((None, t_len, D_SSM), seq),
                  pl.BlockSpec((None, t_len, LANE), seq),
                  pl.BlockSpec((None, SSM_CONV - 1, CONV_DIM), seq),
                  state_spec] + [pl.BlockSpec(p.shape, lambda s: (0, 0)) for p in ssd_params],
        out_specs=[pl.BlockSpec((None, t_len, D_SSM), seq), state_spec],
        out_shape=[jax.ShapeDtypeStruct((ns, t_len, D_SSM), BF16),
                   jax.ShapeDtypeStruct((ns, N_SSM_HEADS, SSM_HEAD_DIM, D_STATE), F32)],
        scratch_shapes=[pltpu.VMEM((SUBLANE + SSD_CHUNK, CONV_DIM), F32),
                        pltpu.VMEM((N_PAIRS, D_STATE, LANE), F32)],
        compiler_params=_params("parallel"),
        name="ssd_short",
    )(xbc, z, dt, prefix, state0, *ssd_params)


ATTN_UNITS = 4


def _attn_stages(q_ref, kp_ref, kc_ref, vp_ref, vc_ref, bias_ref, o_ref, max_ref, den_ref, s_ref, p_ref, *,
                 dil, blk0, rows_of):
    lt64 = _lane_lt64((BLK, LANE))
    head_row = lax.broadcasted_iota(jnp.int32, (N_HEADS, BLK), 0)
    zero = jnp.zeros((BLK, LANE), BF16)

    def prev_of(g, p_ref_, c_ref_):
        if dil == 1:
            return (p_ref_.at[0], blk0) if g == 0 else (c_ref_.at[g - 1], None)
        return p_ref_.at[g], blk0

    def scores(g, slot):
        kprev, masked = prev_of(g, kp_ref, kc_ref)
        for jp in range(N_PAIRS):
            sl = slice(jp * LANE, (jp + 1) * LANE)
            qp = q_ref[g, :, sl]
            kcat = jnp.concatenate([kprev[:, sl], kc_ref[g, :, sl]], axis=0)
            for half in range(2):
                h = 2 * jp + half
                qm = jnp.where(lt64 if half == 0 else ~lt64, qp, zero)
                bias = bias_ref[h]
                if masked is not None:
                    bias = jnp.concatenate([jnp.where(masked, NEG, bias[0:BLK]), bias[BLK:]], axis=0)
                s_ref[slot, h] = _nt_dot(kcat, qm) + bias

    def softmax(g, slot):
        s = s_ref[slot]
        m = jnp.max(s, axis=1, keepdims=True)
        p = jnp.exp2(s - m)
        den = jnp.sum(p, axis=1, keepdims=True)
        p_ref[slot] = p.astype(BF16)
        m_rows = jnp.zeros((N_HEADS, BLK), F32)
        den_rows = jnp.zeros((N_HEADS, BLK), F32)
        for h in range(N_HEADS):
            m_rows = jnp.where(head_row == h, jnp.broadcast_to(m[h], (N_HEADS, BLK)), m_rows)
            den_rows = jnp.where(head_row == h, jnp.broadcast_to(den[h], (N_HEADS, BLK)), den_rows)
        m_nat = m_rows * (1.0 / LOG2E)
        max_ref[rows_of(g), :] = jnp.concatenate([m_nat, jnp.zeros((BLK - N_HEADS, BLK), F32)], axis=0).T
        den_ref[rows_of(g), :] = jnp.concatenate([den_rows, jnp.ones((BLK - N_HEADS, BLK), F32)], axis=0).T

    def values(g, slot):
        vprev, _ = prev_of(g, vp_ref, vc_ref)
        for jp in range(N_PAIRS):
            sl = slice(jp * LANE, (jp + 1) * LANE)
            vcat = jnp.concatenate([vprev[:, sl], vc_ref[g, :, sl]], axis=0)
            pv = [lax.dot_general(p_ref[slot, 2 * jp + half], vcat, (((0,), (0,)), ((), ())),
                                  preferred_element_type=F32) for half in range(2)]
            o_ref[jp, rows_of(g), :] = jnp.where(lt64, pv[0], pv[1])

    return scores, softmax, values


def _attn_prompt_kernel(q_ref, kp_ref, kc_ref, vp_ref, vc_ref, bias_ref, o_ref, max_ref, den_ref, s_ref, p_ref, *, dil):
    units = q_ref.shape[0]
    if dil == 1:
        rows_of = lambda g: slice(g * BLK, (g + 1) * BLK)
    else:
        rows_of = lambda g: pl.ds(pl.program_id(2) * units + g, BLK, stride=dil)
    scores, softmax, values = _attn_stages(q_ref, kp_ref, kc_ref, vp_ref, vc_ref, bias_ref, o_ref, max_ref, den_ref,
                                           s_ref, p_ref, dil=dil, blk0=pl.program_id(1) == 0, rows_of=rows_of)
    scores(0, 0)
    for g in range(units):
        softmax(g, g % 2)
        if g + 1 < units:
            scores(g + 1, (g + 1) % 2)
        values(g, g % 2)


def _attn_prompt_branch(q, k, v, bias, dil):
    bsz, _, l, _ = q.shape
    s = l * dil
    u = ATTN_UNITS
    if dil == 1:
        q, k, v = (t.reshape(bsz, l // BLK, BLK, D_ATTN) for t in (q, k, v))
        grid = (bsz, l // BLK // u, 1)
        cur = pl.BlockSpec((None, u, BLK, D_ATTN), lambda b, n, r: (b, n, 0, 0))
        prev = pl.BlockSpec((None, 1, BLK, D_ATTN), lambda b, n, r: (b, jnp.maximum(n * u - 1, 0), 0, 0))
        rows = u * BLK
    else:
        grid = (bsz, l // BLK, dil // u)
        cur = pl.BlockSpec((None, u, BLK, D_ATTN), lambda b, n, r: (b, r, n, 0))
        prev = pl.BlockSpec((None, u, BLK, D_ATTN), lambda b, n, r: (b, r, jnp.maximum(n - 1, 0), 0))
        rows = dil * BLK
    return pl.pallas_call(
        functools.partial(_attn_prompt_kernel, dil=dil),
        grid=grid,
        in_specs=[cur, prev, cur, prev, cur,
                  pl.BlockSpec((N_HEADS, 2 * BLK, BLK), lambda b, n, r: (0, 0, 0))],
        out_specs=[pl.BlockSpec((None, N_PAIRS, rows, LANE), lambda b, n, r: (b, 0, n, 0)),
                   pl.BlockSpec((None, rows, LANE), lambda b, n, r: (b, n, 0)),
                   pl.BlockSpec((None, rows, LANE), lambda b, n, r: (b, n, 0))],
        out_shape=[jax.ShapeDtypeStruct((bsz, N_PAIRS, s, LANE), F32),
                   jax.ShapeDtypeStruct((bsz, s, LANE), F32),
                   jax.ShapeDtypeStruct((bsz, s, LANE), F32)],
        scratch_shapes=[pltpu.VMEM((2, N_HEADS, 2 * BLK, BLK), F32), pltpu.VMEM((2, N_HEADS, 2 * BLK, BLK), BF16)],
        compiler_params=_params("parallel", "parallel", "arbitrary"),
        name=f"attn_prompt_d{dil}",
    )(q, k, k, v, v, bias)


def _attn_sample_kernel(q_ref, kn_ref, vn_ref, kc_ref, vc_ref, b16_ref, b4_ref, b1_ref, bn_ref, o_ref):
    t_len = q_ref.shape[0]
    w = kc_ref.shape[-1]
    lt64 = _lane_lt64((t_len, LANE))
    pad = jnp.zeros((BLK - t_len, D_ATTN), F32)
    kn = jnp.concatenate([kn_ref[...], pad], axis=0).astype(BF16)
    vn = jnp.concatenate([vn_ref[...], pad], axis=0).astype(BF16)
    outs = []
    for jp in range(N_PAIRS):
        sl = slice(jp * LANE, (jp + 1) * LANE)
        qp = q_ref[:, sl]
        qm = jnp.concatenate([jnp.where(lt64, qp, 0.0), jnp.where(lt64, 0.0, qp)], axis=0).astype(BF16)
        kt = kc_ref[2 * jp:2 * jp + 2].reshape(2 * HEAD_DIM, w).astype(BF16)
        vt = vc_ref[2 * jp:2 * jp + 2].reshape(2 * HEAD_DIM, w).astype(BF16)
        s = _dot(qm, kt)
        s_new = _nt_dot(qm, kn[:, sl])
        w4, w1 = w - 4 * BLK, w - BLK
        tiles = [s + b16_ref[jp], s[:, w4:] + b4_ref[jp], s[:, w1:] + b1_ref[jp]]
        tiles += [s_new + bn_ref[br, jp] for br in range(len(DILATIONS))]
        m = functools.reduce(jnp.maximum, [jnp.max(t, axis=-1, keepdims=True) for t in tiles])
        ps = [jnp.exp(t - m) for t in tiles]
        den = functools.reduce(lambda a, b: a + b, [jnp.sum(p, axis=-1, keepdims=True) for p in ps])
        p16, p4, p1 = ps[0], ps[1], ps[2]
        p_cache = jnp.concatenate([p16[:, :w4], p16[:, w4:w1] + p4[:, :w1 - w4],
                                   p16[:, w1:] + p4[:, w1 - w4:] + p1], axis=1)
        p_new = ps[3] + ps[4] + ps[5]
        o2 = (_nt_dot(p_cache.astype(BF16), vt) + _dot(p_new.astype(BF16), vn[:, sl])) / den
        outs.append(jnp.where(lt64, o2[0:t_len], o2[t_len:2 * t_len]))
    o_ref[...] = jnp.concatenate(outs, axis=1).astype(o_ref.dtype)


def _attn_sample(q, kn, vn, kc, vc, b16, b4, b1, bn):
    ns, t_len, _ = q.shape
    w = kc.shape[-1]
    assert w == MAX_DISTANCE and t_len == SUBLANE
    tok = pl.BlockSpec((None, t_len, D_ATTN), lambda s: (s, 0, 0))
    cache = pl.BlockSpec((None, N_HEADS, HEAD_DIM, w), lambda s: (s, 0, 0, 0))
    full = lambda a: pl.BlockSpec(a.shape, lambda s: (0,) * a.ndim)
    return pl.pallas_call(
        _attn_sample_kernel,
        grid=(ns,),
        in_specs=[tok, tok, tok, cache, cache, full(b16), full(b4), full(b1), full(bn)],
        out_specs=tok,
        out_shape=jax.ShapeDtypeStruct((ns, t_len, D_ATTN), BF16),
        compiler_params=_params("parallel"),
        name="attn_sample",
    )(q, kn, vn, kc, vc, b16, b4, b1, bn)


FFN_ROW_PARTS = 4
N_FFN_WEIGHTS = 11


def _merge_branches(parts, expand_ref):
    maxes = [max_ref[...] for _, max_ref, _ in parts]
    mx = functools.reduce(jnp.maximum, maxes)
    ws = [jnp.exp(m - mx) for m in maxes]
    den = functools.reduce(lambda a, b: a + b, [w * den_ref[...] for w, (_, _, den_ref) in zip(ws, parts)])
    attn = None
    for (o_ref, _, _), w in zip(parts, ws):
        wn = w / den
        hi = wn.astype(BF16)
        lo = (wn - hi.astype(F32)).astype(BF16)
        wexp = _dot(jnp.concatenate([hi, lo], axis=1), expand_ref[...])
        o = jnp.concatenate([o_ref[jp] for jp in range(N_PAIRS)], axis=1)
        attn = wexp * o if attn is None else attn + wexp * o
    return attn.astype(BF16)


def _ffn_core(x_ref, attn, yssm_ref, p_ref, w, y_ref, act_ref, conv, row_parts=1):
    (_, wout_ref, gffn_ref, wup_ref, _, _, wdown_ref, wple_ref, gple_ref, wgate_ref, gfin_ref) = w
    h1 = x_ref[...] + _dot(attn, wout_ref[0:D_ATTN, :]) + _dot(yssm_ref[...], wout_ref[D_ATTN:, :])
    hn = _rms(h1, gffn_ref[...]).astype(BF16)
    rp = x_ref.shape[0] // row_parts
    hn_parts = [hn[r * rp:(r + 1) * rp] for r in range(row_parts)]
    for j in range(D_FF // FF_CHUNK):
        c0 = j * FF_CHUNK
        u_gate = conv([_dot(h, wup_ref[:, c0:c0 + FF_CHUNK]) for h in hn_parts], c0)
        u_lin = conv([_dot(h, wup_ref[:, D_FF + c0:D_FF + c0 + FF_CHUNK]) for h in hn_parts], D_FF + c0)
        for r in range(row_parts):
            act_ref[r * rp:(r + 1) * rp, c0:c0 + FF_CHUNK] = (_silu(u_gate[r]) * u_lin[r]).astype(BF16)
    h2 = h1 + _dot(act_ref[...], wdown_ref[...])
    e = _rms(_dot(p_ref[...].astype(BF16), wple_ref[...]), gple_ref[...])
    h3 = h2 + jax.nn.sigmoid(_dot(h2.astype(BF16), wgate_ref[...])) * e
    y_ref[...] = _rms(h3, gfin_ref[...])


def _mix_ffn_seg_kernel(x_ref, attn_ref, yssm_ref, p_ref, fpre_ref, *rest):
    w = rest[:N_FFN_WEIGHTS]
    y_ref, uout_ref, act_ref = rest[N_FFN_WEIGHTS:]
    fcw_ref, fcb_ref = w[4], w[5]
    tm = x_ref.shape[0]
    rin = lax.broadcasted_iota(jnp.int32, (tm, FF_CHUNK), 0) & (SUBLANE - 1)

    def conv(parts, c0):
        (u,) = parts
        cs = slice(c0, c0 + FF_CHUNK)
        pre = fpre_ref[:, cs]
        um1 = jnp.where(rin == 0, pltpu.roll(pre, tm - 1, 0), pltpu.roll(u, 1, 0))
        um2 = jnp.where(rin < 2, pre, pltpu.roll(u, 2, 0))
        uout_ref[:, cs] = u
        return [fcb_ref[:, cs] + fcw_ref[0:1, cs] * um2 + fcw_ref[1:2, cs] * um1 + fcw_ref[2:3, cs] * u]

    _ffn_core(x_ref, attn_ref[...], yssm_ref, p_ref, w, y_ref, act_ref, conv)


def _mix_ffn_kernel(x_ref, *rest, n_parts, n_tiles):
    parts = [rest[3 * b:3 * b + 3] for b in range(n_parts)]
    yssm_ref, p_ref, fpre_ref = rest[3 * n_parts:3 * n_parts + 3]
    rest = rest[3 * n_parts + 3:]
    w = rest[:N_FFN_WEIGHTS]
    y_ref, uout_ref, act_ref, tail_ref = rest[N_FFN_WEIGHTS:]
    fcw_ref, fcb_ref = w[4], w[5]
    tm = x_ref.shape[0]
    t = pl.program_id(1)

    @pl.when(t == 0)
    def _init():
        tail_ref[...] = jnp.zeros(tail_ref.shape, F32)
        tail_ref[SUBLANE - (FFN_CONV - 1):SUBLANE, :] = fpre_ref[...]

    attn = _merge_branches(parts, w[0])

    rin8 = lax.broadcasted_iota(jnp.int32, (SUBLANE, FF_CHUNK), 0)

    def conv(parts, c0):
        cs = slice(c0, c0 + FF_CHUNK)
        prev = tail_ref[:, cs]
        tail_ref[:, cs] = parts[-1][-SUBLANE:, :]
        outs = []
        for u in parts:
            r1, r2 = pltpu.roll(u, 1, 0), pltpu.roll(u, 2, 0)
            um1 = jnp.concatenate([jnp.where(rin8 < 1, pltpu.roll(prev, 1, 0), r1[0:SUBLANE]), r1[SUBLANE:]], axis=0)
            um2 = jnp.concatenate([jnp.where(rin8 < 2, pltpu.roll(prev, 2, 0), r2[0:SUBLANE]), r2[SUBLANE:]], axis=0)
            outs.append(fcb_ref[:, cs] + fcw_ref[0:1, cs] * um2 + fcw_ref[1:2, cs] * um1 + fcw_ref[2:3, cs] * u)
            prev = u[-SUBLANE:, :]
        return outs

    _ffn_core(x_ref, attn, yssm_ref, p_ref, w, y_ref, act_ref, conv, row_parts=FFN_ROW_PARTS)

    @pl.when(t == n_tiles - 1)
    def _fin():
        uout_ref[...] = tail_ref[...]


def _mix_ffn_seg(x, attn, yssm, p, fpre, weights):
    rows = x.shape[0]
    args = [x, attn, yssm, p, fpre, *weights]
    full = lambda a: pl.BlockSpec(a.shape, lambda i: (0, 0))
    return pl.pallas_call(
        _mix_ffn_seg_kernel,
        grid=(1,),
        in_specs=[full(a) for a in args],
        out_specs=[pl.BlockSpec((rows, D_MODEL), lambda i: (0, 0)), pl.BlockSpec((rows, 2 * D_FF), lambda i: (0, 0))],
        out_shape=[jax.ShapeDtypeStruct((rows, D_MODEL), F32), jax.ShapeDtypeStruct((rows, 2 * D_FF), F32)],
        scratch_shapes=[pltpu.VMEM((rows, D_FF), BF16)],
        compiler_params=_params("arbitrary"),
        name="mix_ffn_seg",
    )(*args)


def _mix_ffn(x, parts, yssm, p, fpre, weights, *, tm):
    bsz, s, _ = x.shape
    nt = s // tm
    tile = lambda b, t: (b, t, 0)
    seq = lambda b, t: (b, 0, 0)
    fixed = lambda b, t: (0, 0)
    in_specs = [pl.BlockSpec((None, tm, D_MODEL), tile)]
    args = [x]
    for part in parts:
        in_specs += [pl.BlockSpec((None, N_PAIRS, tm, LANE), lambda b, t: (b, 0, t, 0)),
                     pl.BlockSpec((None, tm, LANE), tile), pl.BlockSpec((None, tm, LANE), tile)]
        args += list(part)
    in_specs += [pl.BlockSpec((None, tm, D_SSM), tile), pl.BlockSpec((None, tm, D_PLE), tile),
                 pl.BlockSpec((None, FFN_CONV - 1, 2 * D_FF), seq)]
    args += [yssm, p, fpre]
    in_specs += [pl.BlockSpec(w.shape, fixed) for w in weights]
    args += list(weights)
    return pl.pallas_call(
        functools.partial(_mix_ffn_kernel, n_parts=len(parts), n_tiles=nt),
        grid=(bsz, nt),
        in_specs=in_specs,
        out_specs=[pl.BlockSpec((None, tm, D_MODEL), tile), pl.BlockSpec((None, SUBLANE, 2 * D_FF), seq)],
        out_shape=[jax.ShapeDtypeStruct((bsz, s, D_MODEL), F32), jax.ShapeDtypeStruct((bsz, SUBLANE, 2 * D_FF), F32)],
        scratch_shapes=[pltpu.VMEM((tm, D_FF), BF16), pltpu.VMEM((SUBLANE, 2 * D_FF), F32)],
        compiler_params=_params("parallel", "arbitrary"),
        name="mix_ffn",
    )(*args)


def _pad_lanes(v, width=LANE):
    return jnp.pad(v.astype(F32), (0, width - v.shape[0]))[None, :]


def kernel(x_prompt, x_sample, p_prompt, p_sample, cache_k, cache_v, state_ssm, state_conv, state_ffn_conv,
           rel_bias, g_mix, w_in, conv_w, conv_b, dt_bias, a_log, d_skip, g_ssm, w_out, g_ffn, w_up,
           ffn_conv_w, ffn_conv_b, w_down, w_ple_proj, g_ple, w_ple_gate, g_final):
    assert w_in.shape[0] == 1, "one layer"
    bp, s, _ = x_prompt.shape
    nsamp, t_len, _ = x_sample.shape
    n_keep = min(MAX_DISTANCE, s)

    w_main = _cast_w_main(w_in)
    w_dt = jnp.pad(w_in[0, :, O_DT:], ((0, 0), (0, LANE - (w_in.shape[2] - O_DT)))).astype(BF16)
    gmix = g_mix[0][None, :]
    ssd_params = (conv_w[0], conv_b[0][None, :], _pad_lanes(dt_bias[0]), _pad_lanes(a_log[0]),
                  jnp.repeat(d_skip[0], SSM_HEAD_DIM)[None, :], g_ssm[0][None, :])
    expand = (np.arange(2 * LANE)[:, None] % LANE == (np.arange(D_ATTN)[None, :] // HEAD_DIM)).astype(np.float32)
    ffn_weights = (jnp.asarray(expand, BF16), w_out[0].astype(BF16), g_ffn[0][None, :], w_up[0].astype(BF16),
                   ffn_conv_w[0], ffn_conv_b[0][None, :], w_down[0].astype(BF16), w_ple_proj[0].astype(BF16),
                   g_ple[0][None, :], w_ple_gate[0].astype(BF16), g_final[None, :])
    bias_p, b16, b4, b1, bn = _bias_tables(rel_bias, t_len)

    (q1, k1, v1, q4, k4, v4, q16, k16, v16, kt, vt, yssm_p, ssm_p, ctail_p) = _inproj_ssd(
        x_prompt, gmix, w_main, w_dt,
        jnp.zeros((bp, SSM_CONV - 1, CONV_DIM), F32),
        jnp.zeros((bp, N_SSM_HEADS, SSM_HEAD_DIM, D_STATE), F32),
        ssd_params, tm=512, n_keep=n_keep)
    parts = [_attn_prompt_branch(q1[:, None], k1[:, None], v1[:, None], bias_p[0], 1),
             _attn_prompt_branch(q4, k4, v4, bias_p[1], 4),
             _attn_prompt_branch(q16, k16, v16, bias_p[2], 16)]
    y_prompt, tail_p = _mix_ffn(x_prompt, parts, yssm_p, p_prompt[0],
                                jnp.zeros((bp, FFN_CONV - 1, 2 * D_FF), F32), ffn_weights, tm=512)
    k_prompt = jnp.transpose(kt, (0, 3, 1, 2))[None]
    v_prompt = jnp.transpose(vt, (0, 3, 1, 2))[None]
    conv_prompt = ctail_p[:, SUBLANE - (SSM_CONV - 1):][None]
    ffn_conv_prompt = tail_p[:, SUBLANE - (FFN_CONV - 1):][None]

    rows = nsamp * t_len
    qs, ks, vs, zs, xbcs, dts = _inproj(x_sample.reshape(rows, D_MODEL), gmix, w_main, w_dt)
    s3 = lambda a: a.reshape(nsamp, t_len, a.shape[-1])
    yssm_s, ssm_s = _ssd_short(s3(xbcs), s3(zs), s3(dts), state_conv[0], state_ssm[0], ssd_params)
    attn_s = _attn_sample(s3(qs), s3(ks), s3(vs),
                          jnp.transpose(cache_k[0], (0, 2, 3, 1)), jnp.transpose(cache_v[0], (0, 2, 3, 1)),
                          b16, b4, b1, bn)
    fpre = jnp.pad(state_ffn_conv[0], ((0, 0), (0, t_len - (FFN_CONV - 1)), (0, 0))).reshape(rows, 2 * D_FF)
    y_s, u_s = _mix_ffn_seg(x_sample.reshape(rows, D_MODEL), attn_s.reshape(rows, D_ATTN),
                            yssm_s.reshape(rows, D_SSM), p_sample[0].reshape(rows, D_PLE), fpre, ffn_weights)
    y_sample = y_s.reshape(nsamp, t_len, D_MODEL)
    k_sample = ks.reshape(1, nsamp, t_len, N_HEADS, HEAD_DIM)
    v_sample = vs.reshape(1, nsamp, t_len, N_HEADS, HEAD_DIM)
    conv_sample = s3(xbcs)[:, t_len - (SSM_CONV - 1):][None]
    ffn_conv_sample = u_s.reshape(nsamp, t_len, 2 * D_FF)[:, t_len - (FFN_CONV - 1):][None]

    return (y_prompt, y_sample, k_prompt, v_prompt, k_sample, v_sample,
            ssm_p[None], ssm_s[None], conv_prompt, conv_sample, ffn_conv_prompt, ffn_conv_sample)
```

```python
import functools
import math

import numpy as np
import jax
import jax.numpy as jnp
from jax import lax
from jax.experimental import pallas as pl
from jax.experimental.pallas import tpu as pltpu

F32 = jnp.float32
BF16 = jnp.bfloat16

D_MODEL = 1024
HEAD_DIM = 64
N_HEADS = 8
D_ATTN = N_HEADS * HEAD_DIM
N_PAIRS = N_HEADS // 2
DILATIONS = (1, 4, 16)
N_STEPS = 128
BLK = 128
N_BUCKETS = 32
MAX_DISTANCE = 2048
D_SSM = 512
N_SSM_HEADS = 8
SSM_HEAD_DIM = 64
D_STATE = 128
N_SSM_GROUPS = 2
SSM_CONV = 4
CONV_DIM = D_SSM + 2 * N_SSM_GROUPS * D_STATE
SSD_CHUNK = 128
D_FF = 2816
FFN_CONV = 3
D_PLE = 256
EPS = 1e-6
NEG = -1e30
LOG2E = math.log2(math.e)

LANE = 128
SUBLANE = 8
FF_CHUNK = 256
VMEM_LIMIT = 56 * 1024 * 1024

O_Q, O_K, O_V, O_Z, O_XBC, O_DT = 0, 512, 1024, 1536, 2048, 3072


def _rel_bucket_np(dist):
    dist = np.asarray(dist, np.int32)
    max_exact = N_BUCKETS // 2
    d = np.maximum(dist, 1).astype(np.float32)
    large = max_exact + (np.log(d / np.float32(max_exact)) / np.float32(math.log(MAX_DISTANCE / max_exact))
                         * np.float32(N_BUCKETS - max_exact)).astype(np.int32)
    large = np.minimum(large, N_BUCKETS - 1)
    return np.where(dist < max_exact, dist, large)


def _nt_dot(a, b):
    return lax.dot_general(a, b, (((1,), (1,)), ((), ())), preferred_element_type=F32)


def _dot(a, b):
    return jnp.dot(a, b, preferred_element_type=F32)


def _silu(x):
    return x * jax.nn.sigmoid(x)


def _softplus(x):
    return jnp.maximum(x, 0.0) + jnp.log1p(jnp.exp(-jnp.abs(x)))


def _rms(x, g):
    return x * lax.rsqrt(jnp.mean(x * x, axis=-1, keepdims=True) + EPS) * g


def _lane_lt64(shape):
    return lax.broadcasted_iota(jnp.int32, shape, len(shape) - 1) < HEAD_DIM


def _params(*sem):
    return pltpu.CompilerParams(dimension_semantics=sem, vmem_limit_bytes=VMEM_LIMIT)


def _bias_kernel(rb_ref, rbt_ref, ig_ref, i16_ref, i4_ref, i1_ref, in_ref, tp_ref, t16_ref, t4_ref, t1_ref, tn_ref):
    def lookup(idx, h):
        def body(b, acc):
            return jnp.where(idx == b, rb_ref[b, h], acc)
        return lax.fori_loop(0, N_BUCKETS, body, jnp.full(idx.shape, NEG, F32), unroll=True)

    for br in range(len(DILATIONS)):
        idx = jnp.broadcast_to(ig_ref[br], (N_HEADS, 2 * BLK))
        gen = jnp.full((N_HEADS, 2 * BLK), NEG, F32)
        for b in range(N_BUCKETS):
            gen = jnp.where(idx == b, jnp.broadcast_to(rbt_ref[:, b:b + 1], (N_HEADS, 2 * BLK)), gen)
        for h in range(N_HEADS):
            rows = jnp.broadcast_to(gen[h:h + 1, :], (BLK, 2 * BLK))
            tp_ref[br, h] = (pltpu.roll(rows, 0, 1, stride=1, stride_axis=0) * LOG2E).T
    for h in range(N_HEADS):
        jp, half = divmod(h, 2)
        rs = slice(half * SUBLANE, (half + 1) * SUBLANE)
        t16_ref[jp, rs, :] = lookup(i16_ref[...], h)
        t4_ref[jp, rs, :] = lookup(i4_ref[...], h)
        t1_ref[jp, rs, :] = lookup(i1_ref[...], h)
        for br in range(len(DILATIONS)):
            tn_ref[br, jp, rs, :] = lookup(in_ref[br], h)


def _bucket_maps(t_len):
    j = BLK - np.arange(2 * BLK)[None, :]
    prompt = np.stack([np.where(j >= 0, _rel_bucket_np(np.clip(j, 0, N_STEPS) * d), -1) for d in DILATIONS])

    t = np.arange(t_len)[:, None]

    def sample_map(diff, dil):
        ok = (diff >= 0) & (diff % dil == 0) & (diff // dil <= N_STEPS)
        return np.where(ok, _rel_bucket_np(np.maximum(diff, 0)), -1).astype(np.int32)

    w = np.arange(MAX_DISTANCE)[None, :]
    cache = {d: sample_map(MAX_DISTANCE + t - w, d) for d in DILATIONS}
    g = np.arange(LANE)[None, :]
    new = np.stack([np.where(g < t_len, sample_map(t - g, d), -1) for d in DILATIONS])
    return (prompt.astype(np.int32), cache[16], cache[4][:, MAX_DISTANCE - 4 * BLK:],
            cache[1][:, MAX_DISTANCE - BLK:], new.astype(np.int32))


def _bias_tables(rel_bias, t_len):
    maps = _bucket_maps(t_len)
    nb = len(DILATIONS)
    shapes = [(nb, N_HEADS, 2 * BLK, BLK), (N_PAIRS, 2 * t_len, MAX_DISTANCE), (N_PAIRS, 2 * t_len, 4 * BLK),
              (N_PAIRS, 2 * t_len, BLK), (nb, N_PAIRS, 2 * t_len, LANE)]
    return pl.pallas_call(
        _bias_kernel,
        in_specs=[pl.BlockSpec(memory_space=pltpu.SMEM)] + [pl.BlockSpec(memory_space=pltpu.VMEM)] * 6,
        out_specs=[pl.BlockSpec(memory_space=pltpu.VMEM)] * 5,
        out_shape=[jax.ShapeDtypeStruct(s, F32) for s in shapes],
        compiler_params=pltpu.CompilerParams(vmem_limit_bytes=VMEM_LIMIT),
        name="bias_tables",
    )(rel_bias, rel_bias.T, *[jnp.asarray(m) for m in maps])


W_CAST_COLS = 384


def _cast_kernel(wt_ref, o_ref):
    o_ref[...] = wt_ref[...].T.astype(BF16)


def _cast_w_main(w_in):
    return pl.pallas_call(
        _cast_kernel,
        grid=(O_DT // W_CAST_COLS,),
        in_specs=[pl.BlockSpec((W_CAST_COLS, D_MODEL), lambda i: (i, 0))],
        out_specs=pl.BlockSpec((D_MODEL, W_CAST_COLS), lambda i: (0, i)),
        out_shape=jax.ShapeDtypeStruct((D_MODEL, O_DT), BF16),
        compiler_params=_params("parallel"),
        name="cast_w_in",
    )(jnp.transpose(w_in[0]))


def _inproj_kernel(x_ref, g_ref, w_ref, wdt_ref, q_ref, k_ref, v_ref, z_ref, xbc_ref, dt_ref):
    xn = _rms(x_ref[...], g_ref[...]).astype(BF16)

    def proj(lo, hi):
        return _dot(xn, w_ref[:, lo:hi])

    q_ref[...] = proj(O_Q, O_K) * (HEAD_DIM ** -0.5)
    k_ref[...] = proj(O_K, O_V)
    v_ref[...] = proj(O_V, O_Z)
    z_ref[...] = proj(O_Z, O_XBC)
    xbc_ref[...] = proj(O_XBC, O_DT)
    dt_ref[...] = _dot(xn, wdt_ref[...])


def _inproj(x, g_mix, w_main, w_dt):
    rows = x.shape[0]
    full = lambda a: pl.BlockSpec(a.shape, lambda i: (0, 0))
    widths = (D_ATTN, D_ATTN, D_ATTN, D_SSM, CONV_DIM, LANE)
    return pl.pallas_call(
        _inproj_kernel,
        grid=(1,),
        in_specs=[full(x), full(g_mix), full(w_main), full(w_dt)],
        out_specs=[pl.BlockSpec((rows, w), lambda i: (0, 0)) for w in widths],
        out_shape=[jax.ShapeDtypeStruct((rows, w), F32) for w in widths],
        compiler_params=_params("arbitrary"),
        name="inproj_sample",
    )(x, g_mix, w_main, w_dt)


def _inproj_ssd_kernel(x_ref, g_ref, w_ref, wdt_ref, prefix_ref, state0_ref, cw_ref, cb_ref, dtb_ref, alog_ref, dskip_ref,
                       gssm_ref, q1, k1, v1, q4, k4, v4, q16, k16, v16, kt_ref, vt_ref, y_ref, state_ref, ctail_ref,
                       perm_ref, mid_ref, xp_ref, z_scr, dt_scr, st_ref, *, keep_from, n_tiles):
    tm = x_ref.shape[0]
    L = SSD_CHUNK
    t = pl.program_id(1)

    @pl.when(t == 0)
    def _init():
        xp_ref[0:SUBLANE, :] = jnp.zeros((SUBLANE, CONV_DIM), F32)
        xp_ref[SUBLANE - (SSM_CONV - 1):SUBLANE, :] = prefix_ref[...]
        _load_state(state0_ref, st_ref)

    xn = _rms(x_ref[...], g_ref[...]).astype(BF16)

    def proj(lo, hi):
        return _dot(xn, w_ref[:, lo:hi])

    xp_ref[SUBLANE:SUBLANE + tm, :] = proj(O_XBC, O_DT)
    dt_scr[...] = _dot(xn, wdt_ref[...])
    z_scr[...] = proj(O_Z, O_XBC)

    def project_attn(i):
        lo, scale, nat, r4, r16 = ((O_Q, HEAD_DIM ** -0.5 * LOG2E, q1, q4, q16), (O_K, None, k1, k4, k16),
                                   (O_V, None, v1, v4, v16))[i]
        val = proj(lo, lo + D_ATTN)
        if scale is not None:
            val = val * scale
        nat[...] = val.astype(BF16)
        for jp in range(N_PAIRS):
            perm_ref[i, jp] = val[:, jp * LANE:(jp + 1) * LANE]
        for ra in range(4):
            for jp in range(N_PAIRS):
                mid_ref[i, ra, jp] = perm_ref[i, jp, pl.ds(ra, tm // 4, stride=4), :]
            r4[ra] = jnp.concatenate([mid_ref[i, ra, jp] for jp in range(N_PAIRS)], axis=1).astype(BF16)
            for rb in range(4):
                rows = [mid_ref[i, ra, jp, pl.ds(rb, tm // 16, stride=4), :] for jp in range(N_PAIRS)]
                r16[ra + 4 * rb] = jnp.concatenate(rows, axis=1).astype(BF16)

    prm = (cw_ref, cb_ref, dtb_ref, alog_ref, dskip_ref, gssm_ref)
    n_chunks = tm // L
    assert n_chunks >= 3
    for c in range(n_chunks):
        base = SUBLANE + c * L - (SSM_CONV - 1)
        y = _ssd_chunk(lambda k, base=base: xp_ref[base + k:base + k + L, :],
                       z_scr[c * L:(c + 1) * L, :], dt_scr[c * L:(c + 1) * L, :], prm, st_ref, L)
        y_ref[c * L:(c + 1) * L, :] = y.astype(y_ref.dtype)
        if c < 3:
            project_attn(c)
    xp_ref[0:SUBLANE, :] = xp_ref[tm:tm + SUBLANE, :]

    @pl.when(t == n_tiles - 1)
    def _fin():
        _store_state(st_ref, state_ref)
        ctail_ref[...] = xp_ref[0:SUBLANE, :]

    @pl.when(t >= keep_from)
    def _keep():
        for i, out_ref in ((1, kt_ref), (2, vt_ref)):
            for jp in range(N_PAIRS):
                out_ref[2 * jp:2 * jp + 2] = perm_ref[i, jp].T.reshape(2, HEAD_DIM, tm)


def _inproj_ssd(x, g_mix, w_main, w_dt, prefix, state0, ssd_params, *, tm, n_keep):
    ns, l, _ = x.shape
    nt = l // tm
    keep_from = (l - n_keep) // tm
    tile = lambda s, t: (s, t, 0)
    seq = lambda s, t: (s, 0, 0)
    fixed = lambda s, t: (0, 0)
    state_spec = pl.BlockSpec((None, N_SSM_HEADS, SSM_HEAD_DIM, D_STATE), lambda s, t: (s, 0, 0, 0))
    nat = lambda w: (pl.BlockSpec((None, tm, w), tile), jax.ShapeDtypeStruct((ns, l, w), BF16))
    outs = [nat(D_ATTN)] * 3
    for dil in (4, 16):
        outs += [(pl.BlockSpec((None, dil, tm // dil, D_ATTN), lambda s, t: (s, 0, t, 0)),
                  jax.ShapeDtypeStruct((ns, dil, l // dil, D_ATTN), BF16))] * 3
    outs += [(pl.BlockSpec((None, N_HEADS, HEAD_DIM, tm), lambda s, t: (s, 0, 0, jnp.maximum(t - keep_from, 0))),
              jax.ShapeDtypeStruct((ns, N_HEADS, HEAD_DIM, n_keep), F32))] * 2
    outs += [nat(D_SSM),
             (state_spec, jax.ShapeDtypeStruct((ns, N_SSM_HEADS, SSM_HEAD_DIM, D_STATE), F32)),
             (pl.BlockSpec((None, SUBLANE, CONV_DIM), seq), jax.ShapeDtypeStruct((ns, SUBLANE, CONV_DIM), F32))]
    return pl.pallas_call(
        functools.partial(_inproj_ssd_kernel, keep_from=keep_from, n_tiles=nt),
        grid=(ns, nt),
        in_specs=[pl.BlockSpec((None, tm, D_MODEL), tile),
                  pl.BlockSpec((1, D_MODEL), fixed),
                  pl.BlockSpec((D_MODEL, O_DT), fixed),
                  pl.BlockSpec((D_MODEL, LANE), fixed),
                  pl.BlockSpec((None, SSM_CONV - 1, CONV_DIM), seq),
                  state_spec] + [pl.BlockSpec(p.shape, fixed) for p in ssd_params],
        out_specs=[o[0] for o in outs],
        out_shape=[o[1] for o in outs],
        scratch_shapes=[pltpu.VMEM((3, N_PAIRS, tm, LANE), F32),
                        pltpu.VMEM((3, 4, N_PAIRS, tm // 4, LANE), F32),
                        pltpu.VMEM((SUBLANE + tm, CONV_DIM), F32),
                        pltpu.VMEM((tm, D_SSM), F32),
                        pltpu.VMEM((tm, LANE), F32),
                        pltpu.VMEM((N_PAIRS, D_STATE, LANE), F32)],
        compiler_params=_params("parallel", "arbitrary"),
        name="inproj_ssd",
    )(x, g_mix, w_main, w_dt, prefix, state0, *ssd_params)


def _load_state(state0_ref, st_ref):
    for jp in range(N_PAIRS):
        st_ref[jp] = state0_ref[2 * jp:2 * jp + 2].reshape(2 * SSM_HEAD_DIM, D_STATE).T


def _store_state(st_ref, state_ref):
    for jp in range(N_PAIRS):
        state_ref[2 * jp:2 * jp + 2] = st_ref[jp].T.reshape(2, SSM_HEAD_DIM, D_STATE)


def _ssd_chunk(xwin, z, dt_raw, prm, st_ref, valid_len):
    cw_ref, cb_ref, dtb_ref, alog_ref, dskip_ref, gssm_ref = prm
    L = SSD_CHUNK
    conv = cb_ref[...]
    for k in range(SSM_CONV):
        conv = conv + cw_ref[k:k + 1, :] * xwin(k)
    xc = _silu(conv)
    xs = xc[:, :D_SSM]
    gn = N_SSM_GROUPS * D_STATE

    row = lax.broadcasted_iota(jnp.int32, (L, L), 0)
    col = lax.broadcasted_iota(jnp.int32, (L, L), 1)
    tri = row >= col
    lt64 = col < SSM_HEAD_DIM

    dt = _softplus(dt_raw + dtb_ref[...])
    if valid_len < L:
        dt = jnp.where(row < valid_len, dt, 0.0)
    a = dt * (-jnp.exp(alog_ref[...]))
    acum = a
    shift = 1
    while shift < L:
        acum = acum + jnp.where(row >= shift, pltpu.roll(acum, shift, 0), 0.0)
        shift *= 2
    acum_t = acum.T
    dt_t = dt.T
    e_slab = jnp.exp(acum)
    de_slab = jnp.exp(acum[L - 1:L, :] - acum) * dt

    def head_terms(h, cb):
        colb = jnp.broadcast_to(acum[:, h:h + 1], (L, L))
        rowb = jnp.broadcast_to(acum_t[h:h + 1, :], (L, L))
        decay = jnp.exp(jnp.where(tri, colb - rowb, NEG))
        m = (cb * decay * jnp.broadcast_to(dt_t[h:h + 1, :], (L, L))).astype(BF16)
        e = jnp.broadcast_to(e_slab[:, h:h + 1], (L, L))
        return m, e, jnp.broadcast_to(de_slab[:, h:h + 1], (L, L)), e[L - 1:L, :]

    pairs = []
    for g in range(N_SSM_GROUPS):
        bg = xc[:, D_SSM + g * D_STATE:D_SSM + (g + 1) * D_STATE]
        cg = xc[:, D_SSM + gn + g * D_STATE:D_SSM + gn + (g + 1) * D_STATE].astype(BF16)
        cb = _nt_dot(cg, bg.astype(BF16))
        bg_t = bg.T.astype(BF16)
        for i in range(N_PAIRS // N_SSM_GROUPS):
            jp = g * (N_PAIRS // N_SSM_GROUPS) + i
            xpair = xs[:, jp * LANE:(jp + 1) * LANE]
            xpair_b = xpair.astype(BF16)
            m_a, e_a, de_a, el_a = head_terms(2 * jp, cb)
            m_b, e_b, de_b, el_b = head_terms(2 * jp + 1, cb)
            st = st_ref[jp]
            y_diag = jnp.where(lt64, _dot(m_a, xpair_b), _dot(m_b, xpair_b))
            y_off = _dot(cg, st.astype(BF16)) * jnp.where(lt64, e_a, e_b)
            xd = (xpair * jnp.where(lt64, de_a, de_b)).astype(BF16)
            st_ref[jp] = st * jnp.where(lt64[0:1, :], el_a, el_b) + _dot(bg_t, xd)
            pairs.append(y_diag + y_off + dskip_ref[:, jp * LANE:(jp + 1) * LANE] * xpair)
    y = jnp.concatenate(pairs, axis=1)

    yf = y * _silu(z)
    gw = D_SSM // N_SSM_GROUPS
    normed = []
    for g in range(N_SSM_GROUPS):
        seg = yf[:, g * gw:(g + 1) * gw]
        normed.append(seg * lax.rsqrt(jnp.mean(seg * seg, axis=-1, keepdims=True) + EPS))
    return jnp.concatenate(normed, axis=1) * gssm_ref[...]


def _ssd_short_kernel(xbc_ref, z_ref, dt_ref, prefix_ref, state0_ref, cw_ref, cb_ref, dtb_ref, alog_ref,
                      dskip_ref, gssm_ref, y_ref, state_ref, xp_ref, st_ref):
    t_len = xbc_ref.shape[0]
    L = SSD_CHUNK
    lead = SUBLANE - (SSM_CONV - 1)
    xp_ref[0:lead, :] = jnp.zeros((lead, CONV_DIM), F32)
    xp_ref[lead:SUBLANE, :] = prefix_ref[...]
    xp_ref[SUBLANE:SUBLANE + t_len, :] = xbc_ref[...]
    xp_ref[SUBLANE + t_len:SUBLANE + L, :] = jnp.zeros((L - t_len, CONV_DIM), F32)
    _load_state(state0_ref, st_ref)
    pad = lambda a: jnp.concatenate([a, jnp.zeros((L - t_len, a.shape[1]), F32)], axis=0)
    y = _ssd_chunk(lambda k: xp_ref[lead + k:lead + k + L, :], pad(z_ref[...]), pad(dt_ref[...]),
                   (cw_ref, cb_ref, dtb_ref, alog_ref, dskip_ref, gssm_ref), st_ref, t_len)
    y_ref[...] = y[0:t_len].astype(y_ref.dtype)
    _store_state(st_ref, state_ref)


def _ssd_short(xbc, z, dt, prefix, state0, ssd_params):
    ns, t_len, _ = xbc.shape
    seq = lambda s: (s, 0, 0)
    state_spec = pl.BlockSpec((None, N_SSM_HEADS, SSM_HEAD_DIM, D_STATE), lambda s: (s, 0, 0, 0))
    return pl.pallas_call(
        _ssd_short_kernel,
        grid=(ns,),
        in_specs=[pl.BlockSpec((None, t_len, CONV_DIM), seq),
                  pl.BlockSpec((None, t_len, D_SSM), seq),
                  pl.BlockSpec((None, t_len, LANE), seq),
                  pl.BlockSpec((None, SSM_CONV - 1, CONV_DIM), seq),
                  state_spec] + [pl.BlockSpec(p.shape, lambda s: (0, 0)) for p in ssd_params],
        out_specs=[pl.BlockSpec((None, t_len, D_SSM), seq), state_spec],
        out_shape=[jax.ShapeDtypeStruct((ns, t_len, D_SSM), BF16),
                   jax.ShapeDtypeStruct((ns, N_SSM_HEADS, SSM_HEAD_DIM, D_STATE), F32)],
        scratch_shapes=[pltpu.VMEM((SUBLANE + SSD_CHUNK, CONV_DIM), F32),
                        pltpu.VMEM((N_PAIRS, D_STATE, LANE), F32)],
        compiler_params=_params("parallel"),
        name="ssd_short",
    )(xbc, z, dt, prefix, state0, *ssd_params)


ATTN_UNITS = 8


def _attn_stages(q_ref, kp_ref, kc_ref, vp_ref, vc_ref, bias_ref, o_ref, max_ref, den_ref, s_ref, p_ref, *,
                 dil, blk0, rows_of):
    lt64 = _lane_lt64((BLK, LANE))
    head_row = lax.broadcasted_iota(jnp.int32, (N_HEADS, BLK), 0)
    zero = jnp.zeros((BLK, LANE), BF16)

    def prev_of(g, p_ref_, c_ref_):
        if dil == 1:
            return (p_ref_.at[0], blk0) if g == 0 else (c_ref_.at[g - 1], None)
        return p_ref_.at[g], blk0

    def scores(g, slot):
        kprev, masked = prev_of(g, kp_ref, kc_ref)
        for jp in range(N_PAIRS):
            sl = slice(jp * LANE, (jp + 1) * LANE)
            qp = q_ref[g, :, sl]
            kcat = jnp.concatenate([kprev[:, sl], kc_ref[g, :, sl]], axis=0)
            for half in range(2):
                h = 2 * jp + half
                qm = jnp.where(lt64 if half == 0 else ~lt64, qp, zero)
                bias = bias_ref[h]
                if masked is not None:
                    bias = jnp.concatenate([jnp.where(masked, NEG, bias[0:BLK]), bias[BLK:]], axis=0)
                s_ref[slot, h] = _nt_dot(kcat, qm) + bias

    def softmax(g, slot):
        s = s_ref[slot]
        m = jnp.max(s, axis=1, keepdims=True)
        p = jnp.exp2(s - m)
        den = jnp.sum(p, axis=1, keepdims=True)
        p_ref[slot] = p.astype(BF16)
        m_rows = jnp.zeros((N_HEADS, BLK), F32)
        den_rows = jnp.zeros((N_HEADS, BLK), F32)
        for h in range(N_HEADS):
            m_rows = jnp.where(head_row == h, jnp.broadcast_to(m[h], (N_HEADS, BLK)), m_rows)
            den_rows = jnp.where(head_row == h, jnp.broadcast_to(den[h], (N_HEADS, BLK)), den_rows)
        m_nat = m_rows * (1.0 / LOG2E)
        max_ref[rows_of(g), :] = jnp.concatenate([m_nat, jnp.zeros((BLK - N_HEADS, BLK), F32)], axis=0).T
        den_ref[rows_of(g), :] = jnp.concatenate([den_rows, jnp.ones((BLK - N_HEADS, BLK), F32)], axis=0).T

    def values(g, slot):
        vprev, _ = prev_of(g, vp_ref, vc_ref)
        for jp in range(N_PAIRS):
            sl = slice(jp * LANE, (jp + 1) * LANE)
            vcat = jnp.concatenate([vprev[:, sl], vc_ref[g, :, sl]], axis=0)
            pv = [lax.dot_general(p_ref[slot, 2 * jp + half], vcat, (((0,), (0,)), ((), ())),
                                  preferred_element_type=F32) for half in range(2)]
            o_ref[jp, rows_of(g), :] = jnp.where(lt64, pv[0], pv[1])

    return scores, softmax, values


def _attn_prompt_kernel(q_ref, kp_ref, kc_ref, vp_ref, vc_ref, bias_ref, o_ref, max_ref, den_ref, s_ref, p_ref, *, dil):
    units = q_ref.shape[0]
    if dil == 1:
        rows_of = lambda g: slice(g * BLK, (g + 1) * BLK)
    else:
        rows_of = lambda g: pl.ds(pl.program_id(2) * units + g, BLK, stride=dil)
    scores, softmax, values = _attn_stages(q_ref, kp_ref, kc_ref, vp_ref, vc_ref, bias_ref, o_ref, max_ref, den_ref,
                                           s_ref, p_ref, dil=dil, blk0=pl.program_id(1) == 0, rows_of=rows_of)
    scores(0, 0)
    for g in range(units):
        softmax(g, g % 2)
        if g + 1 < units:
            scores(g + 1, (g + 1) % 2)
        values(g, g % 2)


def _attn_prompt_branch(q, k, v, bias, dil):
    bsz, _, l, _ = q.shape
    s = l * dil
    u = ATTN_UNITS if dil == 1 else min(ATTN_UNITS, dil)
    if dil == 1:
        q, k, v = (t.reshape(bsz, l // BLK, BLK, D_ATTN) for t in (q, k, v))
        grid = (bsz, l // BLK // u, 1)
        cur = pl.BlockSpec((None, u, BLK, D_ATTN), lambda b, n, r: (b, n, 0, 0))
        prev = pl.BlockSpec((None, 1, BLK, D_ATTN), lambda b, n, r: (b, jnp.maximum(n * u - 1, 0), 0, 0))
        rows = u * BLK
    else:
        grid = (bsz, l // BLK, dil // u)
        cur = pl.BlockSpec((None, u, BLK, D_ATTN), lambda b, n, r: (b, r, n, 0))
        prev = pl.BlockSpec((None, u, BLK, D_ATTN), lambda b, n, r: (b, r, jnp.maximum(n - 1, 0), 0))
        rows = dil * BLK
    return pl.pallas_call(
        functools.partial(_attn_prompt_kernel, dil=dil),
        grid=grid,
        in_specs=[cur, prev, cur, prev, cur,
                  pl.BlockSpec((N_HEADS, 2 * BLK, BLK), lambda b, n, r: (0, 0, 0))],
        out_specs=[pl.BlockSpec((None, N_PAIRS, rows, LANE), lambda b, n, r: (b, 0, n, 0)),
                   pl.BlockSpec((None, rows, LANE), lambda b, n, r: (b, n, 0)),
                   pl.BlockSpec((None, rows, LANE), lambda b, n, r: (b, n, 0))],
        out_shape=[jax.ShapeDtypeStruct((bsz, N_PAIRS, s, LANE), F32),
                   jax.ShapeDtypeStruct((bsz, s, LANE), F32),
                   jax.ShapeDtypeStruct((bsz, s, LANE), F32)],
        scratch_shapes=[pltpu.VMEM((2, N_HEADS, 2 * BLK, BLK), F32), pltpu.VMEM((2, N_HEADS, 2 * BLK, BLK), BF16)],
        compiler_params=_params("parallel", "parallel", "arbitrary"),
        name=f"attn_prompt_d{dil}",
    )(q, k, k, v, v, bias)


def _attn_sample_kernel(q_ref, kn_ref, vn_ref, kc_ref, vc_ref, b16_ref, b4_ref, b1_ref, bn_ref, o_ref):
    t_len = q_ref.shape[0]
    w = kc_ref.shape[-1]
    lt64 = _lane_lt64((t_len, LANE))
    pad = jnp.zeros((BLK - t_len, D_ATTN), F32)
    kn = jnp.concatenate([kn_ref[...], pad], axis=0).astype(BF16)
    vn = jnp.concatenate([vn_ref[...], pad], axis=0).astype(BF16)
    outs = []
    for jp in range(N_PAIRS):
        sl = slice(jp * LANE, (jp + 1) * LANE)
        qp = q_ref[:, sl]
        qm = jnp.concatenate([jnp.where(lt64, qp, 0.0), jnp.where(lt64, 0.0, qp)], axis=0).astype(BF16)
        kt = kc_ref[2 * jp:2 * jp + 2].reshape(2 * HEAD_DIM, w).astype(BF16)
        vt = vc_ref[2 * jp:2 * jp + 2].reshape(2 * HEAD_DIM, w).astype(BF16)
        s = _dot(qm, kt)
        s_new = _nt_dot(qm, kn[:, sl])
        w4, w1 = w - 4 * BLK, w - BLK
        tiles = [s + b16_ref[jp], s[:, w4:] + b4_ref[jp], s[:, w1:] + b1_ref[jp]]
        tiles += [s_new + bn_ref[br, jp] for br in range(len(DILATIONS))]
        m = functools.reduce(jnp.maximum, [jnp.max(t, axis=-1, keepdims=True) for t in tiles])
        ps = [jnp.exp(t - m) for t in tiles]
        den = functools.reduce(lambda a, b: a + b, [jnp.sum(p, axis=-1, keepdims=True) for p in ps])
        p16, p4, p1 = ps[0], ps[1], ps[2]
        p_cache = jnp.concatenate([p16[:, :w4], p16[:, w4:w1] + p4[:, :w1 - w4],
                                   p16[:, w1:] + p4[:, w1 - w4:] + p1], axis=1)
        p_new = ps[3] + ps[4] + ps[5]
        o2 = (_nt_dot(p_cache.astype(BF16), vt) + _dot(p_new.astype(BF16), vn[:, sl])) / den
        outs.append(jnp.where(lt64, o2[0:t_len], o2[t_len:2 * t_len]))
    o_ref[...] = jnp.concatenate(outs, axis=1).astype(o_ref.dtype)


def _attn_sample(q, kn, vn, kc, vc, b16, b4, b1, bn):
    ns, t_len, _ = q.shape
    w = kc.shape[-1]
    assert w == MAX_DISTANCE and t_len == SUBLANE
    tok = pl.BlockSpec((None, t_len, D_ATTN), lambda s: (s, 0, 0))
    cache = pl.BlockSpec((None, N_HEADS, HEAD_DIM, w), lambda s: (s, 0, 0, 0))
    full = lambda a: pl.BlockSpec(a.shape, lambda s: (0,) * a.ndim)
    return pl.pallas_call(
        _attn_sample_kernel,
        grid=(ns,),
        in_specs=[tok, tok, tok, cache, cache, full(b16), full(b4), full(b1), full(bn)],
        out_specs=tok,
        out_shape=jax.ShapeDtypeStruct((ns, t_len, D_ATTN), BF16),
        compiler_params=_params("parallel"),
        name="attn_sample",
    )(q, kn, vn, kc, vc, b16, b4, b1, bn)


FFN_ROW_PARTS = 4
N_FFN_WEIGHTS = 11


def _merge_branches(parts, expand_ref):
    maxes = [max_ref[...] for _, max_ref, _ in parts]
    mx = functools.reduce(jnp.maximum, maxes)
    ws = [jnp.exp(m - mx) for m in maxes]
    den = functools.reduce(lambda a, b: a + b, [w * den_ref[...] for w, (_, _, den_ref) in zip(ws, parts)])
    attn = None
    for (o_ref, _, _), w in zip(parts, ws):
        wn = w / den
        hi = wn.astype(BF16)
        lo = (wn - hi.astype(F32)).astype(BF16)
        wexp = _dot(jnp.concatenate([hi, lo], axis=1), expand_ref[...])
        o = jnp.concatenate([o_ref[jp] for jp in range(N_PAIRS)], axis=1)
        attn = wexp * o if attn is None else attn + wexp * o
    return attn.astype(BF16)


def _ffn_core(x_ref, attn, yssm_ref, p_ref, w, y_ref, act_ref, conv, row_parts=1):
    (_, wout_ref, gffn_ref, wup_ref, _, _, wdown_ref, wple_ref, gple_ref, wgate_ref, gfin_ref) = w
    rp = x_ref.shape[0] // row_parts
    blocks = [slice(r * rp, (r + 1) * rp) for r in range(row_parts)]
    h1 = [x_ref[rs, :] + _dot(attn[rs], wout_ref[0:D_ATTN, :]) + _dot(yssm_ref[rs, :], wout_ref[D_ATTN:, :])
          for rs in blocks]
    hn = [_rms(h, gffn_ref[...]).astype(BF16) for h in h1]
    for j in range(D_FF // FF_CHUNK):
        c0 = j * FF_CHUNK
        u_gate = conv([_dot(h, wup_ref[:, c0:c0 + FF_CHUNK]) for h in hn], c0)
        u_lin = conv([_dot(h, wup_ref[:, D_FF + c0:D_FF + c0 + FF_CHUNK]) for h in hn], D_FF + c0)
        for r, rs in enumerate(blocks):
            act_ref[rs, c0:c0 + FF_CHUNK] = (_silu(u_gate[r]) * u_lin[r]).astype(BF16)
    for r, rs in enumerate(blocks):
        h2 = h1[r] + _dot(act_ref[rs, :], wdown_ref[...])
        e = _rms(_dot(p_ref[rs, :].astype(BF16), wple_ref[...]), gple_ref[...])
        h3 = h2 + jax.nn.sigmoid(_dot(h2.astype(BF16), wgate_ref[...])) * e
        y_ref[rs, :] = _rms(h3, gfin_ref[...])


def _mix_ffn_seg_kernel(x_ref, attn_ref, yssm_ref, p_ref, fpre_ref, *rest):
    w = rest[:N_FFN_WEIGHTS]
    y_ref, uout_ref, act_ref = rest[N_FFN_WEIGHTS:]
    fcw_ref, fcb_ref = w[4], w[5]
    tm = x_ref.shape[0]
    rin = lax.broadcasted_iota(jnp.int32, (tm, FF_CHUNK), 0) & (SUBLANE - 1)

    def conv(parts, c0):
        (u,) = parts
        cs = slice(c0, c0 + FF_CHUNK)
        pre = fpre_ref[:, cs]
        um1 = jnp.where(rin == 0, pltpu.roll(pre, tm - 1, 0), pltpu.roll(u, 1, 0))
        um2 = jnp.where(rin < 2, pre, pltpu.roll(u, 2, 0))
        uout_ref[:, cs] = u
        return [fcb_ref[:, cs] + fcw_ref[0:1, cs] * um2 + fcw_ref[1:2, cs] * um1 + fcw_ref[2:3, cs] * u]

    _ffn_core(x_ref, attn_ref[...], yssm_ref, p_ref, w, y_ref, act_ref, conv)


def _mix_ffn_kernel(x_ref, *rest, n_parts, n_tiles):
    parts = [rest[3 * b:3 * b + 3] for b in range(n_parts)]
    yssm_ref, p_ref, fpre_ref = rest[3 * n_parts:3 * n_parts + 3]
    rest = rest[3 * n_parts + 3:]
    w = rest[:N_FFN_WEIGHTS]
    y_ref, uout_ref, act_ref, tail_ref = rest[N_FFN_WEIGHTS:]
    fcw_ref, fcb_ref = w[4], w[5]
    tm = x_ref.shape[0]
    t = pl.program_id(1)

    @pl.when(t == 0)
    def _init():
        tail_ref[...] = jnp.zeros(tail_ref.shape, F32)
        tail_ref[SUBLANE - (FFN_CONV - 1):SUBLANE, :] = fpre_ref[...]

    attn = _merge_branches(parts, w[0])

    rin8 = lax.broadcasted_iota(jnp.int32, (SUBLANE, FF_CHUNK), 0)

    def conv(parts, c0):
        cs = slice(c0, c0 + FF_CHUNK)
        prev = tail_ref[:, cs]
        tail_ref[:, cs] = parts[-1][-SUBLANE:, :]
        outs = []
        for u in parts:
            r1, r2 = pltpu.roll(u, 1, 0), pltpu.roll(u, 2, 0)
            um1 = jnp.concatenate([jnp.where(rin8 < 1, pltpu.roll(prev, 1, 0), r1[0:SUBLANE]), r1[SUBLANE:]], axis=0)
            um2 = jnp.concatenate([jnp.where(rin8 < 2, pltpu.roll(prev, 2, 0), r2[0:SUBLANE]), r2[SUBLANE:]], axis=0)
            outs.append(fcb_ref[:, cs] + fcw_ref[0:1, cs] * um2 + fcw_ref[1:2, cs] * um1 + fcw_ref[2:3, cs] * u)
            prev = u[-SUBLANE:, :]
        return outs

    _ffn_core(x_ref, attn, yssm_ref, p_ref, w, y_ref, act_ref, conv, row_parts=FFN_ROW_PARTS)

    @pl.when(t == n_tiles - 1)
    def _fin():
        uout_ref[...] = tail_ref[...]


def _mix_ffn_seg(x, attn, yssm, p, fpre, weights):
    rows = x.shape[0]
    args = [x, attn, yssm, p, fpre, *weights]
    full = lambda a: pl.BlockSpec(a.shape, lambda i: (0, 0))
    return pl.pallas_call(
        _mix_ffn_seg_kernel,
        grid=(1,),
        in_specs=[full(a) for a in args],
        out_specs=[pl.BlockSpec((rows, D_MODEL), lambda i: (0, 0)), pl.BlockSpec((rows, 2 * D_FF), lambda i: (0, 0))],
        out_shape=[jax.ShapeDtypeStruct((rows, D_MODEL), F32), jax.ShapeDtypeStruct((rows, 2 * D_FF), F32)],
        scratch_shapes=[pltpu.VMEM((rows, D_FF), BF16)],
        compiler_params=_params("arbitrary"),
        name="mix_ffn_seg",
    )(*args)


def _mix_ffn(x, parts, yssm, p, fpre, weights, *, tm):
    bsz, s, _ = x.shape
    nt = s // tm
    tile = lambda b, t: (b, t, 0)
    seq = lambda b, t: (b, 0, 0)
    fixed = lambda b, t: (0, 0)
    in_specs = [pl.BlockSpec((None, tm, D_MODEL), tile)]
    args = [x]
    for part in parts:
        in_specs += [pl.BlockSpec((None, N_PAIRS, tm, LANE), lambda b, t: (b, 0, t, 0)),
                     pl.BlockSpec((None, tm, LANE), tile), pl.BlockSpec((None, tm, LANE), tile)]
        args += list(part)
    in_specs += [pl.BlockSpec((None, tm, D_SSM), tile), pl.BlockSpec((None, tm, D_PLE), tile),
                 pl.BlockSpec((None, FFN_CONV - 1, 2 * D_FF), seq)]
    args += [yssm, p, fpre]
    in_specs += [pl.BlockSpec(w.shape, fixed) for w in weights]
    args += list(weights)
    return pl.pallas_call(
        functools.partial(_mix_ffn_kernel, n_parts=len(parts), n_tiles=nt),
        grid=(bsz, nt),
        in_specs=in_specs,
        out_specs=[pl.BlockSpec((None, tm, D_MODEL), tile), pl.BlockSpec((None, SUBLANE, 2 * D_FF), seq)],
        out_shape=[jax.ShapeDtypeStruct((bsz, s, D_MODEL), F32), jax.ShapeDtypeStruct((bsz, SUBLANE, 2 * D_FF), F32)],
        scratch_shapes=[pltpu.VMEM((tm, D_FF), BF16), pltpu.VMEM((SUBLANE, 2 * D_FF), F32)],
        compiler_params=_params("parallel", "arbitrary"),
        name="mix_ffn",
    )(*args)


def _pad_lanes(v, width=LANE):
    return jnp.pad(v.astype(F32), (0, width - v.shape[0]))[None, :]


def kernel(x_prompt, x_sample, p_prompt, p_sample, cache_k, cache_v, state_ssm, state_conv, state_ffn_conv,
           rel_bias, g_mix, w_in, conv_w, conv_b, dt_bias, a_log, d_skip, g_ssm, w_out, g_ffn, w_up,
           ffn_conv_w, ffn_conv_b, w_down, w_ple_proj, g_ple, w_ple_gate, g_final):
    assert w_in.shape[0] == 1, "one layer"
    bp, s, _ = x_prompt.shape
    nsamp, t_len, _ = x_sample.shape
    n_keep = min(MAX_DISTANCE, s)

    w_main = _cast_w_main(w_in)
    w_dt = jnp.pad(w_in[0, :, O_DT:], ((0, 0), (0, LANE - (w_in.shape[2] - O_DT)))).astype(BF16)
    gmix = g_mix[0][None, :]
    ssd_params = (conv_w[0], conv_b[0][None, :], _pad_lanes(dt_bias[0]), _pad_lanes(a_log[0]),
                  jnp.repeat(d_skip[0], SSM_HEAD_DIM)[None, :], g_ssm[0][None, :])
    expand = (np.arange(2 * LANE)[:, None] % LANE == (np.arange(D_ATTN)[None, :] // HEAD_DIM)).astype(np.float32)
    ffn_weights = (jnp.asarray(expand, BF16), w_out[0].astype(BF16), g_ffn[0][None, :], w_up[0].astype(BF16),
                   ffn_conv_w[0], ffn_conv_b[0][None, :], w_down[0].astype(BF16), w_ple_proj[0].astype(BF16),
                   g_ple[0][None, :], w_ple_gate[0].astype(BF16), g_final[None, :])
    bias_p, b16, b4, b1, bn = _bias_tables(rel_bias, t_len)

    (q1, k1, v1, q4, k4, v4, q16, k16, v16, kt, vt, yssm_p, ssm_p, ctail_p) = _inproj_ssd(
        x_prompt, gmix, w_main, w_dt,
        jnp.zeros((bp, SSM_CONV - 1, CONV_DIM), F32),
        jnp.zeros((bp, N_SSM_HEADS, SSM_HEAD_DIM, D_STATE), F32),
        ssd_params, tm=512, n_keep=n_keep)
    parts = [_attn_prompt_branch(q1[:, None], k1[:, None], v1[:, None], bias_p[0], 1),
             _attn_prompt_branch(q4, k4, v4, bias_p[1], 4),
             _attn_prompt_branch(q16, k16, v16, bias_p[2], 16)]
    y_prompt, tail_p = _mix_ffn(x_prompt, parts, yssm_p, p_prompt[0],
                                jnp.zeros((bp, FFN_CONV - 1, 2 * D_FF), F32), ffn_weights, tm=512)
    k_prompt = jnp.transpose(kt, (0, 3, 1, 2))[None]
    v_prompt = jnp.transpose(vt, (0, 3, 1, 2))[None]
    conv_prompt = ctail_p[:, SUBLANE - (SSM_CONV - 1):][None]
    ffn_conv_prompt = tail_p[:, SUBLANE - (FFN_CONV - 1):][None]

    rows = nsamp * t_len
    qs, ks, vs, zs, xbcs, dts = _inproj(x_sample.reshape(rows, D_MODEL), gmix, w_main, w_dt)
    s3 = lambda a: a.reshape(nsamp, t_len, a.shape[-1])
    yssm_s, ssm_s = _ssd_short(s3(xbcs), s3(zs), s3(dts), state_conv[0], state_ssm[0], ssd_params)
    attn_s = _attn_sample(s3(qs), s3(ks), s3(vs),
                          jnp.transpose(cache_k[0], (0, 2, 3, 1)), jnp.transpose(cache_v[0], (0, 2, 3, 1)),
                          b16, b4, b1, bn)
    fpre = jnp.pad(state_ffn_conv[0], ((0, 0), (0, t_len - (FFN_CONV - 1)), (0, 0))).reshape(rows, 2 * D_FF)
    y_s, u_s = _mix_ffn_seg(x_sample.reshape(rows, D_MODEL), attn_s.reshape(rows, D_ATTN),
                            yssm_s.reshape(rows, D_SSM), p_sample[0].reshape(rows, D_PLE), fpre, ffn_weights)
    y_sample = y_s.reshape(nsamp, t_len, D_MODEL)
    k_sample = ks.reshape(1, nsamp, t_len, N_HEADS, HEAD_DIM)
    v_sample = vs.reshape(1, nsamp, t_len, N_HEADS, HEAD_DIM)
    conv_sample = s3(xbcs)[:, t_len - (SSM_CONV - 1):][None]
    ffn_conv_sample = u_s.reshape(nsamp, t_len, 2 * D_FF)[:, t_len - (FFN_CONV - 1):][None]

    return (y_prompt, y_sample, k_prompt, v_prompt, k_sample, v_sample,
            ssm_p[None], ssm_s[None], conv_prompt, conv_sample, ffn_conv_prompt, ffn_conv_sample)
```

```python
import functools
import math

import numpy as np
import jax
import jax.numpy as jnp
from jax import lax
from jax.experimental import pallas as pl
from jax.experimental.pallas import tpu as pltpu

F32 = jnp.float32
BF16 = jnp.bfloat16

D_MODEL = 1024
HEAD_DIM = 64
N_HEADS = 8
D_ATTN = N_HEADS * HEAD_DIM
N_PAIRS = N_HEADS // 2
DILATIONS = (1, 4, 16)
N_STEPS = 128
BLK = 128
N_BUCKETS = 32
MAX_DISTANCE = 2048
D_SSM = 512
N_SSM_HEADS = 8
SSM_HEAD_DIM = 64
D_STATE = 128
N_SSM_GROUPS = 2
SSM_CONV = 4
CONV_DIM = D_SSM + 2 * N_SSM_GROUPS * D_STATE
SSD_CHUNK = 128
D_FF = 2816
FFN_CONV = 3
D_PLE = 256
EPS = 1e-6
NEG = -1e30
LOG2E = math.log2(math.e)

LANE = 128
SUBLANE = 8
FF_CHUNK = 256
VMEM_LIMIT = 56 * 1024 * 1024

O_Q, O_K, O_V, O_Z, O_XBC, O_DT = 0, 512, 1024, 1536, 2048, 3072


def _rel_bucket_np(dist):
    dist = np.asarray(dist, np.int32)
    max_exact = N_BUCKETS // 2
    d = np.maximum(dist, 1).astype(np.float32)
    large = max_exact + (np.log(d / np.float32(max_exact)) / np.float32(math.log(MAX_DISTANCE / max_exact))
                         * np.float32(N_BUCKETS - max_exact)).astype(np.int32)
    large = np.minimum(large, N_BUCKETS - 1)
    return np.where(dist < max_exact, dist, large)


def _nt_dot(a, b):
    return lax.dot_general(a, b, (((1,), (1,)), ((), ())), preferred_element_type=F32)


def _dot(a, b):
    return jnp.dot(a, b, preferred_element_type=F32)


def _silu(x):
    return x * jax.nn.sigmoid(x)


def _softplus(x):
    return jnp.maximum(x, 0.0) + jnp.log1p(jnp.exp(-jnp.abs(x)))


def _rms(x, g):
    return x * lax.rsqrt(jnp.mean(x * x, axis=-1, keepdims=True) + EPS) * g


def _lane_lt64(shape):
    return lax.broadcasted_iota(jnp.int32, shape, len(shape) - 1) < HEAD_DIM


def _params(*sem):
    return pltpu.CompilerParams(dimension_semantics=sem, vmem_limit_bytes=VMEM_LIMIT)


def _bias_kernel(rb_ref, rbt_ref, ig_ref, i16_ref, i4_ref, i1_ref, in_ref, tp_ref, t16_ref, t4_ref, t1_ref, tn_ref):
    def lookup(idx, h):
        def body(b, acc):
            return jnp.where(idx == b, rb_ref[b, h], acc)
        return lax.fori_loop(0, N_BUCKETS, body, jnp.full(idx.shape, NEG, F32), unroll=True)

    for br in range(len(DILATIONS)):
        idx = jnp.broadcast_to(ig_ref[br], (N_HEADS, 2 * BLK))
        gen = jnp.full((N_HEADS, 2 * BLK), NEG, F32)
        for b in range(N_BUCKETS):
            gen = jnp.where(idx == b, jnp.broadcast_to(rbt_ref[:, b:b + 1], (N_HEADS, 2 * BLK)), gen)
        for h in range(N_HEADS):
            rows = jnp.broadcast_to(gen[h:h + 1, :], (BLK, 2 * BLK))
            tp_ref[br, h] = (pltpu.roll(rows, 0, 1, stride=1, stride_axis=0) * LOG2E).T
    for h in range(N_HEADS):
        jp, half = divmod(h, 2)
        rs = slice(half * SUBLANE, (half + 1) * SUBLANE)
        t16_ref[jp, rs, :] = lookup(i16_ref[...], h)
        t4_ref[jp, rs, :] = lookup(i4_ref[...], h)
        t1_ref[jp, rs, :] = lookup(i1_ref[...], h)
        for br in range(len(DILATIONS)):
            tn_ref[br, jp, rs, :] = lookup(in_ref[br], h)


def _bucket_maps(t_len):
    j = BLK - np.arange(2 * BLK)[None, :]
    prompt = np.stack([np.where(j >= 0, _rel_bucket_np(np.clip(j, 0, N_STEPS) * d), -1) for d in DILATIONS])

    t = np.arange(t_len)[:, None]

    def sample_map(diff, dil):
        ok = (diff >= 0) & (diff % dil == 0) & (diff // dil <= N_STEPS)
        return np.where(ok, _rel_bucket_np(np.maximum(diff, 0)), -1).astype(np.int32)

    w = np.arange(MAX_DISTANCE)[None, :]
    cache = {d: sample_map(MAX_DISTANCE + t - w, d) for d in DILATIONS}
    g = np.arange(LANE)[None, :]
    new = np.stack([np.where(g < t_len, sample_map(t - g, d), -1) for d in DILATIONS])
    return (prompt.astype(np.int32), cache[16], cache[4][:, MAX_DISTANCE - 4 * BLK:],
            cache[1][:, MAX_DISTANCE - BLK:], new.astype(np.int32))


def _bias_tables(rel_bias, t_len):
    maps = _bucket_maps(t_len)
    nb = len(DILATIONS)
    shapes = [(nb, N_HEADS, 2 * BLK, BLK), (N_PAIRS, 2 * t_len, MAX_DISTANCE), (N_PAIRS, 2 * t_len, 4 * BLK),
              (N_PAIRS, 2 * t_len, BLK), (nb, N_PAIRS, 2 * t_len, LANE)]
    return pl.pallas_call(
        _bias_kernel,
        in_specs=[pl.BlockSpec(memory_space=pltpu.SMEM)] + [pl.BlockSpec(memory_space=pltpu.VMEM)] * 6,
        out_specs=[pl.BlockSpec(memory_space=pltpu.VMEM)] * 5,
        out_shape=[jax.ShapeDtypeStruct(s, F32) for s in shapes],
        compiler_params=pltpu.CompilerParams(vmem_limit_bytes=VMEM_LIMIT),
        name="bias_tables",
    )(rel_bias, rel_bias.T, *[jnp.asarray(m) for m in maps])


W_CAST_COLS = 384


def _cast_kernel(wt_ref, o_ref):
    o_ref[...] = wt_ref[...].T.astype(BF16)


def _cast_w_main(w_in):
    return pl.pallas_call(
        _cast_kernel,
        grid=(O_DT // W_CAST_COLS,),
        in_specs=[pl.BlockSpec((W_CAST_COLS, D_MODEL), lambda i: (i, 0))],
        out_specs=pl.BlockSpec((D_MODEL, W_CAST_COLS), lambda i: (0, i)),
        out_shape=jax.ShapeDtypeStruct((D_MODEL, O_DT), BF16),
        compiler_params=_params("parallel"),
        name="cast_w_in",
    )(jnp.transpose(w_in[0]))


def _inproj_kernel(x_ref, g_ref, w_ref, wdt_ref, q_ref, k_ref, v_ref, z_ref, xbc_ref, dt_ref):
    xn = _rms(x_ref[...], g_ref[...]).astype(BF16)

    def proj(lo, hi):
        return _dot(xn, w_ref[:, lo:hi])

    q_ref[...] = proj(O_Q, O_K) * (HEAD_DIM ** -0.5)
    k_ref[...] = proj(O_K, O_V)
    v_ref[...] = proj(O_V, O_Z)
    z_ref[...] = proj(O_Z, O_XBC)
    xbc_ref[...] = proj(O_XBC, O_DT)
    dt_ref[...] = _dot(xn, wdt_ref[...])


def _inproj(x, g_mix, w_main, w_dt):
    rows = x.shape[0]
    full = lambda a: pl.BlockSpec(a.shape, lambda i: (0, 0))
    widths = (D_ATTN, D_ATTN, D_ATTN, D_SSM, CONV_DIM, LANE)
    return pl.pallas_call(
        _inproj_kernel,
        grid=(1,),
        in_specs=[full(x), full(g_mix), full(w_main), full(w_dt)],
        out_specs=[pl.BlockSpec((rows, w), lambda i: (0, 0)) for w in widths],
        out_shape=[jax.ShapeDtypeStruct((rows, w), F32) for w in widths],
        compiler_params=_params("arbitrary"),
        name="inproj_sample",
    )(x, g_mix, w_main, w_dt)


def _inproj_ssd_kernel(x_ref, g_ref, w_ref, wdt_ref, prefix_ref, state0_ref, cw_ref, cb_ref, dtb_ref, alog_ref, dskip_ref,
                       gssm_ref, q1, k1, v1, q4, k4, v4, q16, k16, v16, kt_ref, vt_ref, y_ref, state_ref, ctail_ref,
                       perm_ref, mid_ref, xp_ref, z_scr, dt_scr, st_ref, *, keep_from, n_tiles):
    tm = x_ref.shape[0]
    L = SSD_CHUNK
    t = pl.program_id(1)

    @pl.when(t == 0)
    def _init():
        xp_ref[0:SUBLANE, :] = jnp.zeros((SUBLANE, CONV_DIM), F32)
        xp_ref[SUBLANE - (SSM_CONV - 1):SUBLANE, :] = prefix_ref[...]
        _load_state(state0_ref, st_ref)

    xn = _rms(x_ref[...], g_ref[...]).astype(BF16)

    def proj(lo, hi):
        return _dot(xn, w_ref[:, lo:hi])

    xp_ref[SUBLANE:SUBLANE + tm, :] = proj(O_XBC, O_DT)
    dt_scr[...] = _dot(xn, wdt_ref[...])
    z_scr[...] = proj(O_Z, O_XBC)

    def project_attn(i):
        lo, scale, nat, r4, r16 = ((O_Q, HEAD_DIM ** -0.5 * LOG2E, q1, q4, q16), (O_K, None, k1, k4, k16),
                                   (O_V, None, v1, v4, v16))[i]
        val = proj(lo, lo + D_ATTN)
        if scale is not None:
            val = val * scale
        nat[...] = val.astype(BF16)
        for jp in range(N_PAIRS):
            perm_ref[i, jp] = val[:, jp * LANE:(jp + 1) * LANE]
        for ra in range(4):
            for jp in range(N_PAIRS):
                mid_ref[i, ra, jp] = perm_ref[i, jp, pl.ds(ra, tm // 4, stride=4), :]
            r4[ra] = jnp.concatenate([mid_ref[i, ra, jp] for jp in range(N_PAIRS)], axis=1).astype(BF16)
            for rb in range(4):
                rows = [mid_ref[i, ra, jp, pl.ds(rb, tm // 16, stride=4), :] for jp in range(N_PAIRS)]
                r16[ra + 4 * rb] = jnp.concatenate(rows, axis=1).astype(BF16)

    prm = (cw_ref, cb_ref, dtb_ref, alog_ref, dskip_ref, gssm_ref)
    n_chunks = tm // L
    assert n_chunks >= 3
    for c in range(n_chunks):
        base = SUBLANE + c * L - (SSM_CONV - 1)
        y = _ssd_chunk(lambda k, base=base: xp_ref[base + k:base + k + L, :],
                       z_scr[c * L:(c + 1) * L, :], dt_scr[c * L:(c + 1) * L, :], prm, st_ref, L)
        y_ref[c * L:(c + 1) * L, :] = y.astype(y_ref.dtype)
        if c < 3:
            project_attn(c)
    xp_ref[0:SUBLANE, :] = xp_ref[tm:tm + SUBLANE, :]

    @pl.when(t == n_tiles - 1)
    def _fin():
        _store_state(st_ref, state_ref)
        ctail_ref[...] = xp_ref[0:SUBLANE, :]

    @pl.when(t >= keep_from)
    def _keep():
        for i, out_ref in ((1, kt_ref), (2, vt_ref)):
            for jp in range(N_PAIRS):
                out_ref[2 * jp:2 * jp + 2] = perm_ref[i, jp].T.reshape(2, HEAD_DIM, tm)


def _inproj_ssd(x, g_mix, w_main, w_dt, prefix, state0, ssd_params, *, tm, n_keep):
    ns, l, _ = x.shape
    nt = l // tm
    keep_from = (l - n_keep) // tm
    tile = lambda s, t: (s, t, 0)
    seq = lambda s, t: (s, 0, 0)
    fixed = lambda s, t: (0, 0)
    state_spec = pl.BlockSpec((None, N_SSM_HEADS, SSM_HEAD_DIM, D_STATE), lambda s, t: (s, 0, 0, 0))
    nat = lambda w: (pl.BlockSpec((None, tm, w), tile), jax.ShapeDtypeStruct((ns, l, w), BF16))
    outs = [nat(D_ATTN)] * 3
    for dil in (4, 16):
        outs += [(pl.BlockSpec((None, dil, tm // dil, D_ATTN), lambda s, t: (s, 0, t, 0)),
                  jax.ShapeDtypeStruct((ns, dil, l // dil, D_ATTN), BF16))] * 3
    outs += [(pl.BlockSpec((None, N_HEADS, HEAD_DIM, tm), lambda s, t: (s, 0, 0, jnp.maximum(t - keep_from, 0))),
              jax.ShapeDtypeStruct((ns, N_HEADS, HEAD_DIM, n_keep), F32))] * 2
    outs += [nat(D_SSM),
             (state_spec, jax.ShapeDtypeStruct((ns, N_SSM_HEADS, SSM_HEAD_DIM, D_STATE), F32)),
             (pl.BlockSpec((None, SUBLANE, CONV_DIM), seq), jax.ShapeDtypeStruct((ns, SUBLANE, CONV_DIM), F32))]
    return pl.pallas_call(
        functools.partial(_inproj_ssd_kernel, keep_from=keep_from, n_tiles=nt),
        grid=(ns, nt),
        in_specs=[pl.BlockSpec((None, tm, D_MODEL), tile),
                  pl.BlockSpec((1, D_MODEL), fixed),
                  pl.BlockSpec((D_MODEL, O_DT), fixed),
                  pl.BlockSpec((D_MODEL, LANE), fixed),
                  pl.BlockSpec((None, SSM_CONV - 1, CONV_DIM), seq),
                  state_spec] + [pl.BlockSpec(p.shape, fixed) for p in ssd_params],
        out_specs=[o[0] for o in outs],
        out_shape=[o[1] for o in outs],
        scratch_shapes=[pltpu.VMEM((3, N_PAIRS, tm, LANE), F32),
                        pltpu.VMEM((3, 4, N_PAIRS, tm // 4, LANE), F32),
                        pltpu.VMEM((SUBLANE + tm, CONV_DIM), F32),
                        pltpu.VMEM((tm, D_SSM), F32),
                        pltpu.VMEM((tm, LANE), F32),
                        pltpu.VMEM((N_PAIRS, D_STATE, LANE), F32)],
        compiler_params=_params("parallel", "arbitrary"),
        name="inproj_ssd",
    )(x, g_mix, w_main, w_dt, prefix, state0, *ssd_params)


def _load_state(state0_ref, st_ref):
    for jp in range(N_PAIRS):
        st_ref[jp] = state0_ref[2 * jp:2 * jp + 2].reshape(2 * SSM_HEAD_DIM, D_STATE).T


def _store_state(st_ref, state_ref):
    for jp in range(N_PAIRS):
        state_ref[2 * jp:2 * jp + 2] = st_ref[jp].T.reshape(2, SSM_HEAD_DIM, D_STATE)


def _ssd_chunk(xwin, z, dt_raw, prm, st_ref, valid_len):
    cw_ref, cb_ref, dtb_ref, alog_ref, dskip_ref, gssm_ref = prm
    L = SSD_CHUNK
    conv = cb_ref[...]
    for k in range(SSM_CONV):
        conv = conv + cw_ref[k:k + 1, :] * xwin(k)
    xc = _silu(conv)
    xs = xc[:, :D_SSM]
    gn = N_SSM_GROUPS * D_STATE

    row = lax.broadcasted_iota(jnp.int32, (L, L), 0)
    col = lax.broadcasted_iota(jnp.int32, (L, L), 1)
    tri = row >= col
    lt64 = col < SSM_HEAD_DIM

    dt = _softplus(dt_raw + dtb_ref[...])
    if valid_len < L:
        dt = jnp.where(row < valid_len, dt, 0.0)
    a = dt * (-jnp.exp(alog_ref[...]))
    acum = a
    shift = 1
    while shift < L:
        acum = acum + jnp.where(row >= shift, pltpu.roll(acum, shift, 0), 0.0)
        shift *= 2
    acum_t = acum.T
    dt_t = dt.T
    e_slab = jnp.exp(acum)
    de_slab = jnp.exp(acum[L - 1:L, :] - acum) * dt

    def head_terms(h, cb):
        colb = jnp.broadcast_to(acum[:, h:h + 1], (L, L))
        rowb = jnp.broadcast_to(acum_t[h:h + 1, :], (L, L))
        decay = jnp.exp(jnp.where(tri, colb - rowb, NEG))
        m = (cb * decay * jnp.broadcast_to(dt_t[h:h + 1, :], (L, L))).astype(BF16)
        e = jnp.broadcast_to(e_slab[:, h:h + 1], (L, L))
        return m, e, jnp.broadcast_to(de_slab[:, h:h + 1], (L, L)), e[L - 1:L, :]

    pairs = []
    for g in range(N_SSM_GROUPS):
        bg = xc[:, D_SSM + g * D_STATE:D_SSM + (g + 1) * D_STATE]
        cg = xc[:, D_SSM + gn + g * D_STATE:D_SSM + gn + (g + 1) * D_STATE].astype(BF16)
        cb = _nt_dot(cg, bg.astype(BF16))
        bg_t = bg.T.astype(BF16)
        for i in range(N_PAIRS // N_SSM_GROUPS):
            jp = g * (N_PAIRS // N_SSM_GROUPS) + i
            xpair = xs[:, jp * LANE:(jp + 1) * LANE]
            xpair_b = xpair.astype(BF16)
            m_a, e_a, de_a, el_a = head_terms(2 * jp, cb)
            m_b, e_b, de_b, el_b = head_terms(2 * jp + 1, cb)
            st = st_ref[jp]
            y_diag = jnp.where(lt64, _dot(m_a, xpair_b), _dot(m_b, xpair_b))
            y_off = _dot(cg, st.astype(BF16)) * jnp.where(lt64, e_a, e_b)
            xd = (xpair * jnp.where(lt64, de_a, de_b)).astype(BF16)
            st_ref[jp] = st * jnp.where(lt64[0:1, :], el_a, el_b) + _dot(bg_t, xd)
            pairs.append(y_diag + y_off + dskip_ref[:, jp * LANE:(jp + 1) * LANE] * xpair)
    y = jnp.concatenate(pairs, axis=1)

    yf = y * _silu(z)
    gw = D_SSM // N_SSM_GROUPS
    normed = []
    for g in range(N_SSM_GROUPS):
        seg = yf[:, g * gw:(g + 1) * gw]
        normed.append(seg * lax.rsqrt(jnp.mean(seg * seg, axis=-1, keepdims=True) + EPS))
    return jnp.concatenate(normed, axis=1) * gssm_ref[...]


def _ssd_short_kernel(xbc_ref, z_ref, dt_ref, prefix_ref, state0_ref, cw_ref, cb_ref, dtb_ref, alog_ref,
                      dskip_ref, gssm_ref, y_ref, state_ref, xp_ref, st_ref):
    t_len = xbc_ref.shape[0]
    L = SSD_CHUNK
    lead = SUBLANE - (SSM_CONV - 1)
    xp_ref[0:lead, :] = jnp.zeros((lead, CONV_DIM), F32)
    xp_ref[lead:SUBLANE, :] = prefix_ref[...]
    xp_ref[SUBLANE:SUBLANE + t_len, :] = xbc_ref[...]
    xp_ref[SUBLANE + t_len:SUBLANE + L, :] = jnp.zeros((L - t_len, CONV_DIM), F32)
    _load_state(state0_ref, st_ref)
    pad = lambda a: jnp.concatenate([a, jnp.zeros((L - t_len, a.shape[1]), F32)], axis=0)
    y = _ssd_chunk(lambda k: xp_ref[lead + k:lead + k + L, :], pad(z_ref[...]), pad(dt_ref[...]),
                   (cw_ref, cb_ref, dtb_ref, alog_ref, dskip_ref, gssm_ref), st_ref, t_len)
    y_ref[...] = y[0:t_len].astype(y_ref.dtype)
    _store_state(st_ref, state_ref)


def _ssd_short(xbc, z, dt, prefix, state0, ssd_params):
    ns, t_len, _ = xbc.shape
    seq = lambda s: (s, 0, 0)
    state_spec = pl.BlockSpec((None, N_SSM_HEADS, SSM_HEAD_DIM, D_STATE), lambda s: (s, 0, 0, 0))
    return pl.pallas_call(
        _ssd_short_kernel,
        grid=(ns,),
        in_specs=[pl.BlockSpec((None, t_len, CONV_DIM), seq),
                  pl.BlockSpec((None, t_len, D_SSM), seq),
                  pl.BlockSpec((None, t_len, LANE), seq),
                  pl.BlockSpec((None, SSM_CONV - 1, CONV_DIM), seq),
                  state_spec] + [pl.BlockSpec(p.shape, lambda s: (0, 0)) for p in ssd_params],
        out_specs=[pl.BlockSpec((None, t_len, D_SSM), seq), state_spec],
        out_shape=[jax.ShapeDtypeStruct((ns, t_len, D_SSM), BF16),
                   jax.ShapeDtypeStruct((ns, N_SSM_HEADS, SSM_HEAD_DIM, D_STATE), F32)],
        scratch_shapes=[pltpu.VMEM((SUBLANE + SSD_CHUNK, CONV_DIM), F32),
                        pltpu.VMEM((N_PAIRS, D_STATE, LANE), F32)],
        compiler_params=_params("parallel"),
        name="ssd_short",
    )(xbc, z, dt, prefix, state0, *ssd_params)


ATTN_UNITS = 8


def _attn_stages(q_ref, kp_ref, kc_ref, vp_ref, vc_ref, bias_ref, o_ref, max_ref, den_ref, s_ref, p_ref, *,
                 n_res, blk0, rows_of):
    lt64 = _lane_lt64((BLK, LANE))
    head_row = lax.broadcasted_iota(jnp.int32, (N_HEADS, BLK), 0)
    zero = jnp.zeros((BLK, LANE), BF16)

    def cur(ref, g):
        j, r = divmod(g, n_res)
        return ref.at[r, pl.ds(j * BLK, BLK)]

    def prev_of(g, p_ref_, c_ref_):
        j, r = divmod(g, n_res)
        if j == 0:
            return p_ref_.at[r], blk0
        return c_ref_.at[r, pl.ds((j - 1) * BLK, BLK)], None

    def scores(g, slot):
        kprev, masked = prev_of(g, kp_ref, kc_ref)
        for jp in range(N_PAIRS):
            sl = slice(jp * LANE, (jp + 1) * LANE)
            qp = cur(q_ref, g)[:, sl]
            kcat = jnp.concatenate([kprev[:, sl], cur(kc_ref, g)[:, sl]], axis=0)
            for half in range(2):
                h = 2 * jp + half
                qm = jnp.where(lt64 if half == 0 else ~lt64, qp, zero)
                bias = bias_ref[h]
                if masked is not None:
                    bias = jnp.concatenate([jnp.where(masked, NEG, bias[0:BLK]), bias[BLK:]], axis=0)
                s_ref[slot, h] = _nt_dot(kcat, qm) + bias

    def softmax(g, slot):
        s = s_ref[slot]
        m = jnp.max(s, axis=1, keepdims=True)
        p = jnp.exp2(s - m)
        den = jnp.sum(p, axis=1, keepdims=True)
        p_ref[slot] = p.astype(BF16)
        m_rows = jnp.zeros((N_HEADS, BLK), F32)
        den_rows = jnp.zeros((N_HEADS, BLK), F32)
        for h in range(N_HEADS):
            m_rows = jnp.where(head_row == h, jnp.broadcast_to(m[h], (N_HEADS, BLK)), m_rows)
            den_rows = jnp.where(head_row == h, jnp.broadcast_to(den[h], (N_HEADS, BLK)), den_rows)
        m_nat = m_rows * (1.0 / LOG2E)
        max_ref[rows_of(g), :] = jnp.concatenate([m_nat, jnp.zeros((BLK - N_HEADS, BLK), F32)], axis=0).T
        den_ref[rows_of(g), :] = jnp.concatenate([den_rows, jnp.ones((BLK - N_HEADS, BLK), F32)], axis=0).T

    def values(g, slot):
        vprev, _ = prev_of(g, vp_ref, vc_ref)
        for jp in range(N_PAIRS):
            sl = slice(jp * LANE, (jp + 1) * LANE)
            vcat = jnp.concatenate([vprev[:, sl], cur(vc_ref, g)[:, sl]], axis=0)
            pv = [lax.dot_general(p_ref[slot, 2 * jp + half], vcat, (((0,), (0,)), ((), ())),
                                  preferred_element_type=F32) for half in range(2)]
            o_ref[jp, rows_of(g), :] = jnp.where(lt64, pv[0], pv[1])

    return scores, softmax, values


def _attn_prompt_kernel(q_ref, kp_ref, kc_ref, vp_ref, vc_ref, bias_ref, o_ref, max_ref, den_ref, s_ref, p_ref, *, dil):
    n_res, rows, _ = q_ref.shape
    units = n_res * (rows // BLK)

    def rows_of(g):
        j, r = divmod(g, n_res)
        if dil == 1:
            return slice(j * BLK, (j + 1) * BLK)
        return pl.ds(pl.program_id(2) * n_res + r + j * BLK * dil, BLK, stride=dil)

    scores, softmax, values = _attn_stages(q_ref, kp_ref, kc_ref, vp_ref, vc_ref, bias_ref, o_ref, max_ref, den_ref,
                                           s_ref, p_ref, n_res=n_res, blk0=pl.program_id(1) == 0, rows_of=rows_of)
    scores(0, 0)
    for g in range(units):
        softmax(g, g % 2)
        if g + 1 < units:
            scores(g + 1, (g + 1) % 2)
        values(g, g % 2)


def _attn_prompt_branch(q, k, v, bias, dil):
    bsz, _, l, _ = q.shape
    s = l * dil
    n_res = min(ATTN_UNITS, dil)
    n_blk = ATTN_UNITS // n_res
    cur = pl.BlockSpec((None, n_res, n_blk * BLK, D_ATTN), lambda b, n, r: (b, r, n, 0))
    prev = pl.BlockSpec((None, n_res, BLK, D_ATTN), lambda b, n, r: (b, r, jnp.maximum(n * n_blk - 1, 0), 0))
    rows = n_blk * BLK * dil
    return pl.pallas_call(
        functools.partial(_attn_prompt_kernel, dil=dil),
        grid=(bsz, l // (n_blk * BLK), dil // n_res),
        in_specs=[cur, prev, cur, prev, cur,
                  pl.BlockSpec((N_HEADS, 2 * BLK, BLK), lambda b, n, r: (0, 0, 0))],
        out_specs=[pl.BlockSpec((None, N_PAIRS, rows, LANE), lambda b, n, r: (b, 0, n, 0)),
                   pl.BlockSpec((None, rows, LANE), lambda b, n, r: (b, n, 0)),
                   pl.BlockSpec((None, rows, LANE), lambda b, n, r: (b, n, 0))],
        out_shape=[jax.ShapeDtypeStruct((bsz, N_PAIRS, s, LANE), F32),
                   jax.ShapeDtypeStruct((bsz, s, LANE), F32),
                   jax.ShapeDtypeStruct((bsz, s, LANE), F32)],
        scratch_shapes=[pltpu.VMEM((2, N_HEADS, 2 * BLK, BLK), F32), pltpu.VMEM((2, N_HEADS, 2 * BLK, BLK), BF16)],
        compiler_params=_params("parallel", "parallel", "arbitrary"),
        name=f"attn_prompt_d{dil}",
    )(q, k, k, v, v, bias)


def _attn_sample_kernel(q_ref, kn_ref, vn_ref, kc_ref, vc_ref, b16_ref, b4_ref, b1_ref, bn_ref, o_ref):
    t_len = q_ref.shape[0]
    w = kc_ref.shape[-1]
    lt64 = _lane_lt64((t_len, LANE))
    pad = jnp.zeros((BLK - t_len, D_ATTN), F32)
    kn = jnp.concatenate([kn_ref[...], pad], axis=0).astype(BF16)
    vn = jnp.concatenate([vn_ref[...], pad], axis=0).astype(BF16)
    outs = []
    for jp in range(N_PAIRS):
        sl = slice(jp * LANE, (jp + 1) * LANE)
        qp = q_ref[:, sl]
        qm = jnp.concatenate([jnp.where(lt64, qp, 0.0), jnp.where(lt64, 0.0, qp)], axis=0).astype(BF16)
        kt = kc_ref[2 * jp:2 * jp + 2].reshape(2 * HEAD_DIM, w).astype(BF16)
        vt = vc_ref[2 * jp:2 * jp + 2].reshape(2 * HEAD_DIM, w).astype(BF16)
        s = _dot(qm, kt)
        s_new = _nt_dot(qm, kn[:, sl])
        w4, w1 = w - 4 * BLK, w - BLK
        tiles = [s + b16_ref[jp], s[:, w4:] + b4_ref[jp], s[:, w1:] + b1_ref[jp]]
        tiles += [s_new + bn_ref[br, jp] for br in range(len(DILATIONS))]
        m = functools.reduce(jnp.maximum, [jnp.max(t, axis=-1, keepdims=True) for t in tiles])
        ps = [jnp.exp(t - m) for t in tiles]
        den = functools.reduce(lambda a, b: a + b, [jnp.sum(p, axis=-1, keepdims=True) for p in ps])
        p16, p4, p1 = ps[0], ps[1], ps[2]
        p_cache = jnp.concatenate([p16[:, :w4], p16[:, w4:w1] + p4[:, :w1 - w4],
                                   p16[:, w1:] + p4[:, w1 - w4:] + p1], axis=1)
        p_new = ps[3] + ps[4] + ps[5]
        o2 = (_nt_dot(p_cache.astype(BF16), vt) + _dot(p_new.astype(BF16), vn[:, sl])) / den
        outs.append(jnp.where(lt64, o2[0:t_len], o2[t_len:2 * t_len]))
    o_ref[...] = jnp.concatenate(outs, axis=1).astype(o_ref.dtype)


def _attn_sample(q, kn, vn, kc, vc, b16, b4, b1, bn):
    ns, t_len, _ = q.shape
    w = kc.shape[-1]
    assert w == MAX_DISTANCE and t_len == SUBLANE
    tok = pl.BlockSpec((None, t_len, D_ATTN), lambda s: (s, 0, 0))
    cache = pl.BlockSpec((None, N_HEADS, HEAD_DIM, w), lambda s: (s, 0, 0, 0))
    full = lambda a: pl.BlockSpec(a.shape, lambda s: (0,) * a.ndim)
    return pl.pallas_call(
        _attn_sample_kernel,
        grid=(ns,),
        in_specs=[tok, tok, tok, cache, cache, full(b16), full(b4), full(b1), full(bn)],
        out_specs=tok,
        out_shape=jax.ShapeDtypeStruct((ns, t_len, D_ATTN), BF16),
        compiler_params=_params("parallel"),
        name="attn_sample",
    )(q, kn, vn, kc, vc, b16, b4, b1, bn)


FFN_ROW_PARTS = 4
N_FFN_WEIGHTS = 11


def _merge_branches(parts, expand_ref):
    maxes = [max_ref[...] for _, max_ref, _ in parts]
    mx = functools.reduce(jnp.maximum, maxes)
    ws = [jnp.exp(m - mx) for m in maxes]
    den = functools.reduce(lambda a, b: a + b, [w * den_ref[...] for w, (_, _, den_ref) in zip(ws, parts)])
    attn = None
    for (o_ref, _, _), w in zip(parts, ws):
        wn = w / den
        hi = wn.astype(BF16)
        lo = (wn - hi.astype(F32)).astype(BF16)
        wexp = _dot(jnp.concatenate([hi, lo], axis=1), expand_ref[...])
        o = jnp.concatenate([o_ref[jp] for jp in range(N_PAIRS)], axis=1)
        attn = wexp * o if attn is None else attn + wexp * o
    return attn.astype(BF16)


def _ffn_core(x_ref, attn, yssm_ref, p_ref, w, y_ref, act_ref, conv, row_parts=1):
    (_, wout_ref, gffn_ref, wup_ref, _, _, wdown_ref, wple_ref, gple_ref, wgate_ref, gfin_ref) = w
    h1 = x_ref[...] + _dot(attn, wout_ref[0:D_ATTN, :]) + _dot(yssm_ref[...], wout_ref[D_ATTN:, :])
    hn = _rms(h1, gffn_ref[...]).astype(BF16)
    rp = x_ref.shape[0] // row_parts
    blocks = [slice(r * rp, (r + 1) * rp) for r in range(row_parts)]
    hn_parts = [hn[rs] for rs in blocks]
    for j in range(D_FF // FF_CHUNK):
        c0 = j * FF_CHUNK
        u_gate = conv([_dot(h, wup_ref[:, c0:c0 + FF_CHUNK]) for h in hn_parts], c0)
        u_lin = conv([_dot(h, wup_ref[:, D_FF + c0:D_FF + c0 + FF_CHUNK]) for h in hn_parts], D_FF + c0)
        for r, rs in enumerate(blocks):
            act_ref[rs, c0:c0 + FF_CHUNK] = (_silu(u_gate[r]) * u_lin[r]).astype(BF16)
    h2 = h1 + _dot(act_ref[...], wdown_ref[...])
    e = _rms(_dot(p_ref[...].astype(BF16), wple_ref[...]), gple_ref[...])
    h3 = h2 + jax.nn.sigmoid(_dot(h2.astype(BF16), wgate_ref[...])) * e
    y_ref[...] = _rms(h3, gfin_ref[...])


def _mix_ffn_seg_kernel(x_ref, attn_ref, yssm_ref, p_ref, fpre_ref, *rest):
    w = rest[:N_FFN_WEIGHTS]
    y_ref, uout_ref, act_ref = rest[N_FFN_WEIGHTS:]
    fcw_ref, fcb_ref = w[4], w[5]
    tm = x_ref.shape[0]
    rin = lax.broadcasted_iota(jnp.int32, (tm, FF_CHUNK), 0) & (SUBLANE - 1)

    def conv(parts, c0):
        (u,) = parts
        cs = slice(c0, c0 + FF_CHUNK)
        pre = fpre_ref[:, cs]
        um1 = jnp.where(rin == 0, pltpu.roll(pre, tm - 1, 0), pltpu.roll(u, 1, 0))
        um2 = jnp.where(rin < 2, pre, pltpu.roll(u, 2, 0))
        uout_ref[:, cs] = u
        return [fcb_ref[:, cs] + fcw_ref[0:1, cs] * um2 + fcw_ref[1:2, cs] * um1 + fcw_ref[2:3, cs] * u]

    _ffn_core(x_ref, attn_ref[...], yssm_ref, p_ref, w, y_ref, act_ref, conv)


def _mix_ffn_kernel(x_ref, *rest, n_parts, n_tiles):
    parts = [rest[3 * b:3 * b + 3] for b in range(n_parts)]
    yssm_ref, p_ref, fpre_ref = rest[3 * n_parts:3 * n_parts + 3]
    rest = rest[3 * n_parts + 3:]
    w = rest[:N_FFN_WEIGHTS]
    y_ref, uout_ref, act_ref, tail_ref = rest[N_FFN_WEIGHTS:]
    fcw_ref, fcb_ref = w[4], w[5]
    tm = x_ref.shape[0]
    t = pl.program_id(1)

    @pl.when(t == 0)
    def _init():
        tail_ref[...] = jnp.zeros(tail_ref.shape, F32)
        tail_ref[SUBLANE - (FFN_CONV - 1):SUBLANE, :] = fpre_ref[...]

    attn = _merge_branches(parts, w[0])

    rin8 = lax.broadcasted_iota(jnp.int32, (SUBLANE, FF_CHUNK), 0)

    def conv(parts, c0):
        cs = slice(c0, c0 + FF_CHUNK)
        prev = tail_ref[:, cs]
        tail_ref[:, cs] = parts[-1][-SUBLANE:, :]
        outs = []
        for u in parts:
            r1, r2 = pltpu.roll(u, 1, 0), pltpu.roll(u, 2, 0)
            um1 = jnp.concatenate([jnp.where(rin8 < 1, pltpu.roll(prev, 1, 0), r1[0:SUBLANE]), r1[SUBLANE:]], axis=0)
            um2 = jnp.concatenate([jnp.where(rin8 < 2, pltpu.roll(prev, 2, 0), r2[0:SUBLANE]), r2[SUBLANE:]], axis=0)
            outs.append(fcb_ref[:, cs] + fcw_ref[0:1, cs] * um2 + fcw_ref[1:2, cs] * um1 + fcw_ref[2:3, cs] * u)
            prev = u[-SUBLANE:, :]
        return outs

    _ffn_core(x_ref, attn, yssm_ref, p_ref, w, y_ref, act_ref, conv, row_parts=FFN_ROW_PARTS)

    @pl.when(t == n_tiles - 1)
    def _fin():
        uout_ref[...] = tail_ref[...]


def _mix_ffn_seg(x, attn, yssm, p, fpre, weights):
    rows = x.shape[0]
    args = [x, attn, yssm, p, fpre, *weights]
    full = lambda a: pl.BlockSpec(a.shape, lambda i: (0, 0))
    return pl.pallas_call(
        _mix_ffn_seg_kernel,
        grid=(1,),
        in_specs=[full(a) for a in args],
        out_specs=[pl.BlockSpec((rows, D_MODEL), lambda i: (0, 0)), pl.BlockSpec((rows, 2 * D_FF), lambda i: (0, 0))],
        out_shape=[jax.ShapeDtypeStruct((rows, D_MODEL), F32), jax.ShapeDtypeStruct((rows, 2 * D_FF), F32)],
        scratch_shapes=[pltpu.VMEM((rows, D_FF), BF16)],
        compiler_params=_params("arbitrary"),
        name="mix_ffn_seg",
    )(*args)


def _mix_ffn(x, parts, yssm, p, fpre, weights, *, tm):
    bsz, s, _ = x.shape
    nt = s // tm
    tile = lambda b, t: (b, t, 0)
    seq = lambda b, t: (b, 0, 0)
    fixed = lambda b, t: (0, 0)
    in_specs = [pl.BlockSpec((None, tm, D_MODEL), tile)]
    args = [x]
    for part in parts:
        in_specs += [pl.BlockSpec((None, N_PAIRS, tm, LANE), lambda b, t: (b, 0, t, 0)),
                     pl.BlockSpec((None, tm, LANE), tile), pl.BlockSpec((None, tm, LANE), tile)]
        args += list(part)
    in_specs += [pl.BlockSpec((None, tm, D_SSM), tile), pl.BlockSpec((None, tm, D_PLE), tile),
                 pl.BlockSpec((None, FFN_CONV - 1, 2 * D_FF), seq)]
    args += [yssm, p, fpre]
    in_specs += [pl.BlockSpec(w.shape, fixed) for w in weights]
    args += list(weights)
    return pl.pallas_call(
        functools.partial(_mix_ffn_kernel, n_parts=len(parts), n_tiles=nt),
        grid=(bsz, nt),
        in_specs=in_specs,
        out_specs=[pl.BlockSpec((None, tm, D_MODEL), tile), pl.BlockSpec((None, SUBLANE, 2 * D_FF), seq)],
        out_shape=[jax.ShapeDtypeStruct((bsz, s, D_MODEL), F32), jax.ShapeDtypeStruct((bsz, SUBLANE, 2 * D_FF), F32)],
        scratch_shapes=[pltpu.VMEM((tm, D_FF), BF16), pltpu.VMEM((SUBLANE, 2 * D_FF), F32)],
        compiler_params=_params("parallel", "arbitrary"),
        name="mix_ffn",
    )(*args)


def _pad_lanes(v, width=LANE):
    return jnp.pad(v.astype(F32), (0, width - v.shape[0]))[None, :]


def kernel(x_prompt, x_sample, p_prompt, p_sample, cache_k, cache_v, state_ssm, state_conv, state_ffn_conv,
           rel_bias, g_mix, w_in, conv_w, conv_b, dt_bias, a_log, d_skip, g_ssm, w_out, g_ffn, w_up,
           ffn_conv_w, ffn_conv_b, w_down, w_ple_proj, g_ple, w_ple_gate, g_final):
    assert w_in.shape[0] == 1, "one layer"
    bp, s, _ = x_prompt.shape
    nsamp, t_len, _ = x_sample.shape
    n_keep = min(MAX_DISTANCE, s)

    w_main = _cast_w_main(w_in)
    w_dt = jnp.pad(w_in[0, :, O_DT:], ((0, 0), (0, LANE - (w_in.shape[2] - O_DT)))).astype(BF16)
    gmix = g_mix[0][None, :]
    ssd_params = (conv_w[0], conv_b[0][None, :], _pad_lanes(dt_bias[0]), _pad_lanes(a_log[0]),
                  jnp.repeat(d_skip[0], SSM_HEAD_DIM)[None, :], g_ssm[0][None, :])
    expand = (np.arange(2 * LANE)[:, None] % LANE == (np.arange(D_ATTN)[None, :] // HEAD_DIM)).astype(np.float32)
    ffn_weights = (jnp.asarray(expand, BF16), w_out[0].astype(BF16), g_ffn[0][None, :], w_up[0].astype(BF16),
                   ffn_conv_w[0], ffn_conv_b[0][None, :], w_down[0].astype(BF16), w_ple_proj[0].astype(BF16),
                   g_ple[0][None, :], w_ple_gate[0].astype(BF16), g_final[None, :])
    bias_p, b16, b4, b1, bn = _bias_tables(rel_bias, t_len)

    (q1, k1, v1, q4, k4, v4, q16, k16, v16, kt, vt, yssm_p, ssm_p, ctail_p) = _inproj_ssd(
        x_prompt, gmix, w_main, w_dt,
        jnp.zeros((bp, SSM_CONV - 1, CONV_DIM), F32),
        jnp.zeros((bp, N_SSM_HEADS, SSM_HEAD_DIM, D_STATE), F32),
        ssd_params, tm=512, n_keep=n_keep)
    parts = [_attn_prompt_branch(q1[:, None], k1[:, None], v1[:, None], bias_p[0], 1),
             _attn_prompt_branch(q4, k4, v4, bias_p[1], 4),
             _attn_prompt_branch(q16, k16, v16, bias_p[2], 16)]
    y_prompt, tail_p = _mix_ffn(x_prompt, parts, yssm_p, p_prompt[0],
                                jnp.zeros((bp, FFN_CONV - 1, 2 * D_FF), F32), ffn_weights, tm=512)
    k_prompt = jnp.transpose(kt, (0, 3, 1, 2))[None]
    v_prompt = jnp.transpose(vt, (0, 3, 1, 2))[None]
    conv_prompt = ctail_p[:, SUBLANE - (SSM_CONV - 1):][None]
    ffn_conv_prompt = tail_p[:, SUBLANE - (FFN_CONV - 1):][None]

    rows = nsamp * t_len
    qs, ks, vs, zs, xbcs, dts = _inproj(x_sample.reshape(rows, D_MODEL), gmix, w_main, w_dt)
    s3 = lambda a: a.reshape(nsamp, t_len, a.shape[-1])
    yssm_s, ssm_s = _ssd_short(s3(xbcs), s3(zs), s3(dts), state_conv[0], state_ssm[0], ssd_params)
    attn_s = _attn_sample(s3(qs), s3(ks), s3(vs),
                          jnp.transpose(cache_k[0], (0, 2, 3, 1)), jnp.transpose(cache_v[0], (0, 2, 3, 1)),
                          b16, b4, b1, bn)
    fpre = jnp.pad(state_ffn_conv[0], ((0, 0), (0, t_len - (FFN_CONV - 1)), (0, 0))).reshape(rows, 2 * D_FF)
    y_s, u_s = _mix_ffn_seg(x_sample.reshape(rows, D_MODEL), attn_s.reshape(rows, D_ATTN),
                            yssm_s.reshape(rows, D_SSM), p_sample[0].reshape(rows, D_PLE), fpre, ffn_weights)
    y_sample = y_s.reshape(nsamp, t_len, D_MODEL)
    k_sample = ks.reshape(1, nsamp, t_len, N_HEADS, HEAD_DIM)
    v_sample = vs.reshape(1, nsamp, t_len, N_HEADS, HEAD_DIM)
    conv_sample = s3(xbcs)[:, t_len - (SSM_CONV - 1):][None]
    ffn_conv_sample = u_s.reshape(nsamp, t_len, 2 * D_FF)[:, t_len - (FFN_CONV - 1):][None]

    return (y_prompt, y_sample, k_prompt, v_prompt, k_sample, v_sample,
            ssm_p[None], ssm_s[None], conv_prompt, conv_sample, ffn_conv_prompt, ffn_conv_sample)
```

```python
import functools
import math

import numpy as np
import jax
import jax.numpy as jnp
from jax import lax
from jax.experimental import pallas as pl
from jax.experimental.pallas import tpu as pltpu

F32 = jnp.float32
BF16 = jnp.bfloat16

D_MODEL = 1024
HEAD_DIM = 64
N_HEADS = 8
D_ATTN = N_HEADS * HEAD_DIM
N_PAIRS = N_HEADS // 2
DILATIONS = (1, 4, 16)
N_STEPS = 128
BLK = 128
N_BUCKETS = 32
MAX_DISTANCE = 2048
D_SSM = 512
N_SSM_HEADS = 8
SSM_HEAD_DIM = 64
D_STATE = 128
N_SSM_GROUPS = 2
SSM_CONV = 4
CONV_DIM = D_SSM + 2 * N_SSM_GROUPS * D_STATE
SSD_CHUNK = 128
D_FF = 2816
FFN_CONV = 3
D_PLE = 256
EPS = 1e-6
NEG = -1e30
LOG2E = math.log2(math.e)

LANE = 128
SUBLANE = 8
FF_CHUNK = 256
VMEM_LIMIT = 56 * 1024 * 1024

O_Q, O_K, O_V, O_Z, O_XBC, O_DT = 0, 512, 1024, 1536, 2048, 3072


def _rel_bucket_np(dist):
    dist = np.asarray(dist, np.int32)
    max_exact = N_BUCKETS // 2
    d = np.maximum(dist, 1).astype(np.float32)
    large = max_exact + (np.log(d / np.float32(max_exact)) / np.float32(math.log(MAX_DISTANCE / max_exact))
                         * np.float32(N_BUCKETS - max_exact)).astype(np.int32)
    large = np.minimum(large, N_BUCKETS - 1)
    return np.where(dist < max_exact, dist, large)


def _nt_dot(a, b):
    return lax.dot_general(a, b, (((1,), (1,)), ((), ())), preferred_element_type=F32)


def _dot(a, b):
    return jnp.dot(a, b, preferred_element_type=F32)


def _silu(x):
    return x * jax.nn.sigmoid(x)


def _softplus(x):
    return jnp.maximum(x, 0.0) + jnp.log1p(jnp.exp(-jnp.abs(x)))


def _rms(x, g):
    return x * lax.rsqrt(jnp.mean(x * x, axis=-1, keepdims=True) + EPS) * g


def _lane_lt64(shape):
    return lax.broadcasted_iota(jnp.int32, shape, len(shape) - 1) < HEAD_DIM


def _params(*sem):
    return pltpu.CompilerParams(dimension_semantics=sem, vmem_limit_bytes=VMEM_LIMIT)


def _bias_kernel(rb_ref, rbt_ref, ig_ref, i16_ref, i4_ref, i1_ref, in_ref, tp_ref, t16_ref, t4_ref, t1_ref, tn_ref):
    def lookup(idx, h):
        def body(b, acc):
            return jnp.where(idx == b, rb_ref[b, h], acc)
        return lax.fori_loop(0, N_BUCKETS, body, jnp.full(idx.shape, NEG, F32), unroll=True)

    for br in range(len(DILATIONS)):
        idx = jnp.broadcast_to(ig_ref[br], (N_HEADS, 2 * BLK))
        gen = jnp.full((N_HEADS, 2 * BLK), NEG, F32)
        for b in range(N_BUCKETS):
            gen = jnp.where(idx == b, jnp.broadcast_to(rbt_ref[:, b:b + 1], (N_HEADS, 2 * BLK)), gen)
        for h in range(N_HEADS):
            rows = jnp.broadcast_to(gen[h:h + 1, :], (BLK, 2 * BLK))
            tp_ref[br, h] = (pltpu.roll(rows, 0, 1, stride=1, stride_axis=0) * LOG2E).T
    for h in range(N_HEADS):
        jp, half = divmod(h, 2)
        rs = slice(half * SUBLANE, (half + 1) * SUBLANE)
        t16_ref[jp, rs, :] = lookup(i16_ref[...], h)
        t4_ref[jp, rs, :] = lookup(i4_ref[...], h)
        t1_ref[jp, rs, :] = lookup(i1_ref[...], h)
        for br in range(len(DILATIONS)):
            tn_ref[br, jp, rs, :] = lookup(in_ref[br], h)


def _bucket_maps(t_len):
    j = BLK - np.arange(2 * BLK)[None, :]
    prompt = np.stack([np.where(j >= 0, _rel_bucket_np(np.clip(j, 0, N_STEPS) * d), -1) for d in DILATIONS])

    t = np.arange(t_len)[:, None]

    def sample_map(diff, dil):
        ok = (diff >= 0) & (diff % dil == 0) & (diff // dil <= N_STEPS)
        return np.where(ok, _rel_bucket_np(np.maximum(diff, 0)), -1).astype(np.int32)

    w = np.arange(MAX_DISTANCE)[None, :]
    cache = {d: sample_map(MAX_DISTANCE + t - w, d) for d in DILATIONS}
    g = np.arange(LANE)[None, :]
    new = np.stack([np.where(g < t_len, sample_map(t - g, d), -1) for d in DILATIONS])
    return (prompt.astype(np.int32), cache[16], cache[4][:, MAX_DISTANCE - 4 * BLK:],
            cache[1][:, MAX_DISTANCE - BLK:], new.astype(np.int32))


def _bias_tables(rel_bias, t_len):
    maps = _bucket_maps(t_len)
    nb = len(DILATIONS)
    shapes = [(nb, N_HEADS, 2 * BLK, BLK), (N_PAIRS, 2 * t_len, MAX_DISTANCE), (N_PAIRS, 2 * t_len, 4 * BLK),
              (N_PAIRS, 2 * t_len, BLK), (nb, N_PAIRS, 2 * t_len, LANE)]
    return pl.pallas_call(
        _bias_kernel,
        in_specs=[pl.BlockSpec(memory_space=pltpu.SMEM)] + [pl.BlockSpec(memory_space=pltpu.VMEM)] * 6,
        out_specs=[pl.BlockSpec(memory_space=pltpu.VMEM)] * 5,
        out_shape=[jax.ShapeDtypeStruct(s, F32) for s in shapes],
        compiler_params=pltpu.CompilerParams(vmem_limit_bytes=VMEM_LIMIT),
        name="bias_tables",
    )(rel_bias, rel_bias.T, *[jnp.asarray(m) for m in maps])


W_CAST_COLS = 384


def _cast_kernel(wt_ref, o_ref):
    o_ref[...] = wt_ref[...].T.astype(BF16)


def _cast_w_main(w_in):
    return pl.pallas_call(
        _cast_kernel,
        grid=(O_DT // W_CAST_COLS,),
        in_specs=[pl.BlockSpec((W_CAST_COLS, D_MODEL), lambda i: (i, 0))],
        out_specs=pl.BlockSpec((D_MODEL, W_CAST_COLS), lambda i: (0, i)),
        out_shape=jax.ShapeDtypeStruct((D_MODEL, O_DT), BF16),
        compiler_params=_params("parallel"),
        name="cast_w_in",
    )(jnp.transpose(w_in[0]))


def _inproj_kernel(x_ref, g_ref, w_ref, wdt_ref, q_ref, k_ref, v_ref, z_ref, xbc_ref, dt_ref):
    xn = _rms(x_ref[...], g_ref[...]).astype(BF16)

    def proj(lo, hi):
        return _dot(xn, w_ref[:, lo:hi])

    q_ref[...] = proj(O_Q, O_K) * (HEAD_DIM ** -0.5)
    k_ref[...] = proj(O_K, O_V)
    v_ref[...] = proj(O_V, O_Z)
    z_ref[...] = proj(O_Z, O_XBC)
    xbc_ref[...] = proj(O_XBC, O_DT)
    dt_ref[...] = _dot(xn, wdt_ref[...])


def _inproj(x, g_mix, w_main, w_dt):
    rows = x.shape[0]
    full = lambda a: pl.BlockSpec(a.shape, lambda i: (0, 0))
    widths = (D_ATTN, D_ATTN, D_ATTN, D_SSM, CONV_DIM, LANE)
    return pl.pallas_call(
        _inproj_kernel,
        grid=(1,),
        in_specs=[full(x), full(g_mix), full(w_main), full(w_dt)],
        out_specs=[pl.BlockSpec((rows, w), lambda i: (0, 0)) for w in widths],
        out_shape=[jax.ShapeDtypeStruct((rows, w), F32) for w in widths],
        compiler_params=_params("arbitrary"),
        name="inproj_sample",
    )(x, g_mix, w_main, w_dt)


def _inproj_ssd_kernel(x_ref, g_ref, w_ref, wdt_ref, prefix_ref, state0_ref, cw_ref, cb_ref, dtb_ref, alog_ref, dskip_ref,
                       gssm_ref, q1, k1, v1, q4, k4, v4, q16, k16, v16, kt_ref, vt_ref, y_ref, state_ref, ctail_ref,
                       perm_ref, mid_ref, xp_ref, z_scr, dt_scr, st_ref, *, keep_from, n_tiles):
    tm = x_ref.shape[0]
    L = SSD_CHUNK
    t = pl.program_id(1)

    @pl.when(t == 0)
    def _init():
        xp_ref[0:SUBLANE, :] = jnp.zeros((SUBLANE, CONV_DIM), F32)
        xp_ref[SUBLANE - (SSM_CONV - 1):SUBLANE, :] = prefix_ref[...]
        _load_state(state0_ref, st_ref)

    xn = _rms(x_ref[...], g_ref[...]).astype(BF16)

    def proj(lo, hi):
        return _dot(xn, w_ref[:, lo:hi])

    xp_ref[SUBLANE:SUBLANE + tm, :] = proj(O_XBC, O_DT)
    dt_scr[...] = _dot(xn, wdt_ref[...])
    z_scr[...] = proj(O_Z, O_XBC)

    def project_attn(i):
        lo, scale, nat, r4, r16 = ((O_Q, HEAD_DIM ** -0.5 * LOG2E, q1, q4, q16), (O_K, None, k1, k4, k16),
                                   (O_V, None, v1, v4, v16))[i]
        val = proj(lo, lo + D_ATTN)
        if scale is not None:
            val = val * scale
        nat[...] = val.astype(BF16)
        for jp in range(N_PAIRS):
            perm_ref[i, jp] = val[:, jp * LANE:(jp + 1) * LANE]
        for ra in range(4):
            for jp in range(N_PAIRS):
                mid_ref[i, ra, jp] = perm_ref[i, jp, pl.ds(ra, tm // 4, stride=4), :]
            r4[ra] = jnp.concatenate([mid_ref[i, ra, jp] for jp in range(N_PAIRS)], axis=1).astype(BF16)
            for rb in range(4):
                rows = [mid_ref[i, ra, jp, pl.ds(rb, tm // 16, stride=4), :] for jp in range(N_PAIRS)]
                r16[ra + 4 * rb] = jnp.concatenate(rows, axis=1).astype(BF16)

    prm = (cw_ref, cb_ref, dtb_ref, alog_ref, dskip_ref, gssm_ref)
    n_chunks = tm // L
    assert n_chunks >= 3
    for c in range(n_chunks):
        base = SUBLANE + c * L - (SSM_CONV - 1)
        y = _ssd_chunk(lambda k, base=base: xp_ref[base + k:base + k + L, :],
                       z_scr[c * L:(c + 1) * L, :], dt_scr[c * L:(c + 1) * L, :], prm, st_ref, L)
        y_ref[c * L:(c + 1) * L, :] = y.astype(y_ref.dtype)
        if c < 3:
            project_attn(c)
    xp_ref[0:SUBLANE, :] = xp_ref[tm:tm + SUBLANE, :]

    @pl.when(t == n_tiles - 1)
    def _fin():
        _store_state(st_ref, state_ref)
        ctail_ref[...] = xp_ref[0:SUBLANE, :]

    @pl.when(t >= keep_from)
    def _keep():
        for i, out_ref in ((1, kt_ref), (2, vt_ref)):
            for jp in range(N_PAIRS):
                out_ref[2 * jp:2 * jp + 2] = perm_ref[i, jp].T.reshape(2, HEAD_DIM, tm)


def _inproj_ssd(x, g_mix, w_main, w_dt, prefix, state0, ssd_params, *, tm, n_keep):
    ns, l, _ = x.shape
    nt = l // tm
    keep_from = (l - n_keep) // tm
    tile = lambda s, t: (s, t, 0)
    seq = lambda s, t: (s, 0, 0)
    fixed = lambda s, t: (0, 0)
    state_spec = pl.BlockSpec((None, N_SSM_HEADS, SSM_HEAD_DIM, D_STATE), lambda s, t: (s, 0, 0, 0))
    nat = lambda w: (pl.BlockSpec((None, tm, w), tile), jax.ShapeDtypeStruct((ns, l, w), BF16))
    outs = [nat(D_ATTN)] * 3
    for dil in (4, 16):
        outs += [(pl.BlockSpec((None, dil, tm // dil, D_ATTN), lambda s, t: (s, 0, t, 0)),
                  jax.ShapeDtypeStruct((ns, dil, l // dil, D_ATTN), BF16))] * 3
    outs += [(pl.BlockSpec((None, N_HEADS, HEAD_DIM, tm), lambda s, t: (s, 0, 0, jnp.maximum(t - keep_from, 0))),
              jax.ShapeDtypeStruct((ns, N_HEADS, HEAD_DIM, n_keep), F32))] * 2
    outs += [nat(D_SSM),
             (state_spec, jax.ShapeDtypeStruct((ns, N_SSM_HEADS, SSM_HEAD_DIM, D_STATE), F32)),
             (pl.BlockSpec((None, SUBLANE, CONV_DIM), seq), jax.ShapeDtypeStruct((ns, SUBLANE, CONV_DIM), F32))]
    return pl.pallas_call(
        functools.partial(_inproj_ssd_kernel, keep_from=keep_from, n_tiles=nt),
        grid=(ns, nt),
        in_specs=[pl.BlockSpec((None, tm, D_MODEL), tile),
                  pl.BlockSpec((1, D_MODEL), fixed),
                  pl.BlockSpec((D_MODEL, O_DT), fixed),
                  pl.BlockSpec((D_MODEL, LANE), fixed),
                  pl.BlockSpec((None, SSM_CONV - 1, CONV_DIM), seq),
                  state_spec] + [pl.BlockSpec(p.shape, fixed) for p in ssd_params],
        out_specs=[o[0] for o in outs],
        out_shape=[o[1] for o in outs],
        scratch_shapes=[pltpu.VMEM((3, N_PAIRS, tm, LANE), F32),
                        pltpu.VMEM((3, 4, N_PAIRS, tm // 4, LANE), F32),
                        pltpu.VMEM((SUBLANE + tm, CONV_DIM), F32),
                        pltpu.VMEM((tm, D_SSM), F32),
                        pltpu.VMEM((tm, LANE), F32),
                        pltpu.VMEM((N_PAIRS, D_STATE, LANE), F32)],
        compiler_params=_params("parallel", "arbitrary"),
        name="inproj_ssd",
    )(x, g_mix, w_main, w_dt, prefix, state0, *ssd_params)


def _load_state(state0_ref, st_ref):
    for jp in range(N_PAIRS):
        st_ref[jp] = state0_ref[2 * jp:2 * jp + 2].reshape(2 * SSM_HEAD_DIM, D_STATE).T


def _store_state(st_ref, state_ref):
    for jp in range(N_PAIRS):
        state_ref[2 * jp:2 * jp + 2] = st_ref[jp].T.reshape(2, SSM_HEAD_DIM, D_STATE)


def _ssd_chunk(xwin, z, dt_raw, prm, st_ref, valid_len):
    cw_ref, cb_ref, dtb_ref, alog_ref, dskip_ref, gssm_ref = prm
    L = SSD_CHUNK
    conv = cb_ref[...]
    for k in range(SSM_CONV):
        conv = conv + cw_ref[k:k + 1, :] * xwin(k)
    xc = _silu(conv)
    xs = xc[:, :D_SSM]
    gn = N_SSM_GROUPS * D_STATE

    row = lax.broadcasted_iota(jnp.int32, (L, L), 0)
    col = lax.broadcasted_iota(jnp.int32, (L, L), 1)
    tri = row >= col
    lt64 = col < SSM_HEAD_DIM

    dt = _softplus(dt_raw + dtb_ref[...])
    if valid_len < L:
        dt = jnp.where(row < valid_len, dt, 0.0)
    a = dt * (-jnp.exp(alog_ref[...]))
    acum = a
    shift = 1
    while shift < L:
        acum = acum + jnp.where(row >= shift, pltpu.roll(acum, shift, 0), 0.0)
        shift *= 2
    acum_t = acum.T
    dt_t = dt.T
    e_slab = jnp.exp(acum)
    de_slab = jnp.exp(acum[L - 1:L, :] - acum) * dt

    def head_terms(h, cb):
        colb = jnp.broadcast_to(acum[:, h:h + 1], (L, L))
        rowb = jnp.broadcast_to(acum_t[h:h + 1, :], (L, L))
        decay = jnp.exp(jnp.where(tri, colb - rowb, NEG))
        m = (cb * decay * jnp.broadcast_to(dt_t[h:h + 1, :], (L, L))).astype(BF16)
        e = jnp.broadcast_to(e_slab[:, h:h + 1], (L, L))
        return m, e, jnp.broadcast_to(de_slab[:, h:h + 1], (L, L)), e[L - 1:L, :]

    pairs = []
    for g in range(N_SSM_GROUPS):
        bg = xc[:, D_SSM + g * D_STATE:D_SSM + (g + 1) * D_STATE]
        cg = xc[:, D_SSM + gn + g * D_STATE:D_SSM + gn + (g + 1) * D_STATE].astype(BF16)
        cb = _nt_dot(cg, bg.astype(BF16))
        bg_t = bg.T.astype(BF16)
        for i in range(N_PAIRS // N_SSM_GROUPS):
            jp = g * (N_PAIRS // N_SSM_GROUPS) + i
            xpair = xs[:, jp * LANE:(jp + 1) * LANE]
            xpair_b = xpair.astype(BF16)
            m_a, e_a, de_a, el_a = head_terms(2 * jp, cb)
            m_b, e_b, de_b, el_b = head_terms(2 * jp + 1, cb)
            st = st_ref[jp]
            y_diag = jnp.where(lt64, _dot(m_a, xpair_b), _dot(m_b, xpair_b))
            y_off = _dot(cg, st.astype(BF16)) * jnp.where(lt64, e_a, e_b)
            xd = (xpair * jnp.where(lt64, de_a, de_b)).astype(BF16)
            st_ref[jp] = st * jnp.where(lt64[0:1, :], el_a, el_b) + _dot(bg_t, xd)
            pairs.append(y_diag + y_off + dskip_ref[:, jp * LANE:(jp + 1) * LANE] * xpair)
    y = jnp.concatenate(pairs, axis=1)

    yf = y * _silu(z)
    gw = D_SSM // N_SSM_GROUPS
    normed = []
    for g in range(N_SSM_GROUPS):
        seg = yf[:, g * gw:(g + 1) * gw]
        normed.append(seg * lax.rsqrt(jnp.mean(seg * seg, axis=-1, keepdims=True) + EPS))
    return jnp.concatenate(normed, axis=1) * gssm_ref[...]


def _ssd_short_kernel(xbc_ref, z_ref, dt_ref, prefix_ref, state0_ref, cw_ref, cb_ref, dtb_ref, alog_ref,
                      dskip_ref, gssm_ref, y_ref, state_ref, xp_ref, st_ref):
    n_seq, t_len, _ = xbc_ref.shape
    L = SSD_CHUNK
    lead = SUBLANE - (SSM_CONV - 1)
    pad = lambda a: jnp.concatenate([a, jnp.zeros((L - t_len, a.shape[1]), F32)], axis=0)
    for i in range(n_seq):
        xp_ref[0:lead, :] = jnp.zeros((lead, CONV_DIM), F32)
        xp_ref[lead:SUBLANE, :] = prefix_ref[i]
        xp_ref[SUBLANE:SUBLANE + t_len, :] = xbc_ref[i]
        xp_ref[SUBLANE + t_len:SUBLANE + L, :] = jnp.zeros((L - t_len, CONV_DIM), F32)
        _load_state(state0_ref.at[i], st_ref)
        y = _ssd_chunk(lambda k: xp_ref[lead + k:lead + k + L, :], pad(z_ref[i]), pad(dt_ref[i]),
                       (cw_ref, cb_ref, dtb_ref, alog_ref, dskip_ref, gssm_ref), st_ref, t_len)
        y_ref[i] = y[0:t_len].astype(y_ref.dtype)
        _store_state(st_ref, state_ref.at[i])


def _ssd_short(xbc, z, dt, prefix, state0, ssd_params):
    ns, t_len, _ = xbc.shape
    g = SAMPLE_SEQS
    seq = lambda s: (s, 0, 0)
    state_spec = pl.BlockSpec((g, N_SSM_HEADS, SSM_HEAD_DIM, D_STATE), lambda s: (s, 0, 0, 0))
    return pl.pallas_call(
        _ssd_short_kernel,
        grid=(ns // g,),
        in_specs=[pl.BlockSpec((g, t_len, CONV_DIM), seq),
                  pl.BlockSpec((g, t_len, D_SSM), seq),
                  pl.BlockSpec((g, t_len, LANE), seq),
                  pl.BlockSpec((g, SSM_CONV - 1, CONV_DIM), seq),
                  state_spec] + [pl.BlockSpec(p.shape, lambda s: (0, 0)) for p in ssd_params],
        out_specs=[pl.BlockSpec((g, t_len, D_SSM), seq), state_spec],
        out_shape=[jax.ShapeDtypeStruct((ns, t_len, D_SSM), BF16),
                   jax.ShapeDtypeStruct((ns, N_SSM_HEADS, SSM_HEAD_DIM, D_STATE), F32)],
        scratch_shapes=[pltpu.VMEM((SUBLANE + SSD_CHUNK, CONV_DIM), F32),
                        pltpu.VMEM((N_PAIRS, D_STATE, LANE), F32)],
        compiler_params=_params("parallel"),
        name="ssd_short",
    )(xbc, z, dt, prefix, state0, *ssd_params)


ATTN_UNITS = 8


def _attn_stages(q_ref, kp_ref, kc_ref, vp_ref, vc_ref, bias_ref, o_ref, max_ref, den_ref, s_ref, p_ref, *,
                 n_res, blk0, rows_of):
    lt64 = _lane_lt64((BLK, LANE))
    head_row = lax.broadcasted_iota(jnp.int32, (N_HEADS, BLK), 0)
    zero = jnp.zeros((BLK, LANE), BF16)

    def cur(ref, g):
        j, r = divmod(g, n_res)
        return ref.at[r, pl.ds(j * BLK, BLK)]

    def prev_of(g, p_ref_, c_ref_):
        j, r = divmod(g, n_res)
        if j == 0:
            return p_ref_.at[r], blk0
        return c_ref_.at[r, pl.ds((j - 1) * BLK, BLK)], None

    def scores(g, slot):
        kprev, masked = prev_of(g, kp_ref, kc_ref)
        for jp in range(N_PAIRS):
            sl = slice(jp * LANE, (jp + 1) * LANE)
            qp = cur(q_ref, g)[:, sl]
            kcat = jnp.concatenate([kprev[:, sl], cur(kc_ref, g)[:, sl]], axis=0)
            for half in range(2):
                h = 2 * jp + half
                qm = jnp.where(lt64 if half == 0 else ~lt64, qp, zero)
                bias = bias_ref[h]
                if masked is not None:
                    bias = jnp.concatenate([jnp.where(masked, NEG, bias[0:BLK]), bias[BLK:]], axis=0)
                s_ref[slot, h] = _nt_dot(kcat, qm) + bias

    def softmax(g, slot):
        s = s_ref[slot]
        m = jnp.max(s, axis=1, keepdims=True)
        p = jnp.exp2(s - m)
        den = jnp.sum(p, axis=1, keepdims=True)
        p_ref[slot] = p.astype(BF16)
        m_rows = jnp.zeros((N_HEADS, BLK), F32)
        den_rows = jnp.zeros((N_HEADS, BLK), F32)
        for h in range(N_HEADS):
            m_rows = jnp.where(head_row == h, jnp.broadcast_to(m[h], (N_HEADS, BLK)), m_rows)
            den_rows = jnp.where(head_row == h, jnp.broadcast_to(den[h], (N_HEADS, BLK)), den_rows)
        m_nat = m_rows * (1.0 / LOG2E)
        max_ref[rows_of(g), :] = jnp.concatenate([m_nat, jnp.zeros((BLK - N_HEADS, BLK), F32)], axis=0).T
        den_ref[rows_of(g), :] = jnp.concatenate([den_rows, jnp.ones((BLK - N_HEADS, BLK), F32)], axis=0).T

    def values(g, slot):
        vprev, _ = prev_of(g, vp_ref, vc_ref)
        for jp in range(N_PAIRS):
            sl = slice(jp * LANE, (jp + 1) * LANE)
            vcat = jnp.concatenate([vprev[:, sl], cur(vc_ref, g)[:, sl]], axis=0)
            pv = [lax.dot_general(p_ref[slot, 2 * jp + half], vcat, (((0,), (0,)), ((), ())),
                                  preferred_element_type=F32) for half in range(2)]
            o_ref[jp, rows_of(g), :] = jnp.where(lt64, pv[0], pv[1])

    return scores, softmax, values


def _attn_prompt_kernel(q_ref, kp_ref, kc_ref, vp_ref, vc_ref, bias_ref, o_ref, max_ref, den_ref, s_ref, p_ref, *, dil):
    n_res, rows, _ = q_ref.shape
    units = n_res * (rows // BLK)

    def rows_of(g):
        j, r = divmod(g, n_res)
        if dil == 1:
            return slice(j * BLK, (j + 1) * BLK)
        return pl.ds(pl.program_id(2) * n_res + r + j * BLK * dil, BLK, stride=dil)

    scores, softmax, values = _attn_stages(q_ref, kp_ref, kc_ref, vp_ref, vc_ref, bias_ref, o_ref, max_ref, den_ref,
                                           s_ref, p_ref, n_res=n_res, blk0=pl.program_id(1) == 0, rows_of=rows_of)
    scores(0, 0)
    for g in range(units):
        softmax(g, g % 2)
        if g + 1 < units:
            scores(g + 1, (g + 1) % 2)
        values(g, g % 2)


def _attn_prompt_branch(q, k, v, bias, dil):
    bsz, _, l, _ = q.shape
    s = l * dil
    n_res = min(ATTN_UNITS, dil)
    n_blk = ATTN_UNITS // n_res
    cur = pl.BlockSpec((None, n_res, n_blk * BLK, D_ATTN), lambda b, n, r: (b, r, n, 0))
    prev = pl.BlockSpec((None, n_res, BLK, D_ATTN), lambda b, n, r: (b, r, jnp.maximum(n * n_blk - 1, 0), 0))
    rows = n_blk * BLK * dil
    return pl.pallas_call(
        functools.partial(_attn_prompt_kernel, dil=dil),
        grid=(bsz, l // (n_blk * BLK), dil // n_res),
        in_specs=[cur, prev, cur, prev, cur,
                  pl.BlockSpec((N_HEADS, 2 * BLK, BLK), lambda b, n, r: (0, 0, 0))],
        out_specs=[pl.BlockSpec((None, N_PAIRS, rows, LANE), lambda b, n, r: (b, 0, n, 0)),
                   pl.BlockSpec((None, rows, LANE), lambda b, n, r: (b, n, 0)),
                   pl.BlockSpec((None, rows, LANE), lambda b, n, r: (b, n, 0))],
        out_shape=[jax.ShapeDtypeStruct((bsz, N_PAIRS, s, LANE), F32),
                   jax.ShapeDtypeStruct((bsz, s, LANE), F32),
                   jax.ShapeDtypeStruct((bsz, s, LANE), F32)],
        scratch_shapes=[pltpu.VMEM((2, N_HEADS, 2 * BLK, BLK), F32), pltpu.VMEM((2, N_HEADS, 2 * BLK, BLK), BF16)],
        compiler_params=_params("parallel", "parallel", "arbitrary"),
        name=f"attn_prompt_d{dil}",
    )(q, k, k, v, v, bias)


SAMPLE_SEQS = 2


def _attn_sample_kernel(q_ref, kn_ref, vn_ref, kc_ref, vc_ref, b16_ref, b4_ref, b1_ref, bn_ref, o_ref):
    for i in range(q_ref.shape[0]):
        _attn_sample_one(q_ref.at[i], kn_ref.at[i], vn_ref.at[i], kc_ref.at[i], vc_ref.at[i],
                         b16_ref, b4_ref, b1_ref, bn_ref, o_ref.at[i])


def _attn_sample_one(q_ref, kn_ref, vn_ref, kc_ref, vc_ref, b16_ref, b4_ref, b1_ref, bn_ref, o_ref):
    t_len = q_ref.shape[0]
    w = kc_ref.shape[-1]
    lt64 = _lane_lt64((t_len, LANE))
    pad = jnp.zeros((BLK - t_len, D_ATTN), F32)
    kn = jnp.concatenate([kn_ref[...], pad], axis=0).astype(BF16)
    vn = jnp.concatenate([vn_ref[...], pad], axis=0).astype(BF16)
    outs = []
    for jp in range(N_PAIRS):
        sl = slice(jp * LANE, (jp + 1) * LANE)
        qp = q_ref[:, sl]
        qm = jnp.concatenate([jnp.where(lt64, qp, 0.0), jnp.where(lt64, 0.0, qp)], axis=0).astype(BF16)
        kt = kc_ref[2 * jp:2 * jp + 2].reshape(2 * HEAD_DIM, w).astype(BF16)
        vt = vc_ref[2 * jp:2 * jp + 2].reshape(2 * HEAD_DIM, w).astype(BF16)
        s = _dot(qm, kt)
        s_new = _nt_dot(qm, kn[:, sl])
        w4, w1 = w - 4 * BLK, w - BLK
        tiles = [s + b16_ref[jp], s[:, w4:] + b4_ref[jp], s[:, w1:] + b1_ref[jp]]
        tiles += [s_new + bn_ref[br, jp] for br in range(len(DILATIONS))]
        m = functools.reduce(jnp.maximum, [jnp.max(t, axis=-1, keepdims=True) for t in tiles])
        ps = [jnp.exp(t - m) for t in tiles]
        den = functools.reduce(lambda a, b: a + b, [jnp.sum(p, axis=-1, keepdims=True) for p in ps])
        p16, p4, p1 = ps[0], ps[1], ps[2]
        p_cache = jnp.concatenate([p16[:, :w4], p16[:, w4:w1] + p4[:, :w1 - w4],
                                   p16[:, w1:] + p4[:, w1 - w4:] + p1], axis=1)
        p_new = ps[3] + ps[4] + ps[5]
        o2 = (_nt_dot(p_cache.astype(BF16), vt) + _dot(p_new.astype(BF16), vn[:, sl])) / den
        outs.append(jnp.where(lt64, o2[0:t_len], o2[t_len:2 * t_len]))
    o_ref[...] = jnp.concatenate(outs, axis=1).astype(o_ref.dtype)


def _attn_sample(q, kn, vn, kc, vc, b16, b4, b1, bn):
    ns, t_len, _ = q.shape
    w = kc.shape[-1]
    assert w == MAX_DISTANCE and t_len == SUBLANE
    tok = pl.BlockSpec((SAMPLE_SEQS, t_len, D_ATTN), lambda s: (s, 0, 0))
    cache = pl.BlockSpec((SAMPLE_SEQS, N_HEADS, HEAD_DIM, w), lambda s: (s, 0, 0, 0))
    full = lambda a: pl.BlockSpec(a.shape, lambda s: (0,) * a.ndim)
    return pl.pallas_call(
        _attn_sample_kernel,
        grid=(ns // SAMPLE_SEQS,),
        in_specs=[tok, tok, tok, cache, cache, full(b16), full(b4), full(b1), full(bn)],
        out_specs=tok,
        out_shape=jax.ShapeDtypeStruct((ns, t_len, D_ATTN), BF16),
        compiler_params=_params("parallel"),
        name="attn_sample",
    )(q, kn, vn, kc, vc, b16, b4, b1, bn)


FFN_ROW_PARTS = 4
N_FFN_WEIGHTS = 11


def _merge_branches(parts, expand_ref):
    maxes = [max_ref[...] for _, max_ref, _ in parts]
    mx = functools.reduce(jnp.maximum, maxes)
    ws = [jnp.exp(m - mx) for m in maxes]
    den = functools.reduce(lambda a, b: a + b, [w * den_ref[...] for w, (_, _, den_ref) in zip(ws, parts)])
    attn = None
    for (o_ref, _, _), w in zip(parts, ws):
        wn = w / den
        hi = wn.astype(BF16)
        lo = (wn - hi.astype(F32)).astype(BF16)
        wexp = _dot(jnp.concatenate([hi, lo], axis=1), expand_ref[...])
        o = jnp.concatenate([o_ref[jp] for jp in range(N_PAIRS)], axis=1)
        attn = wexp * o if attn is None else attn + wexp * o
    return attn.astype(BF16)


def _ffn_core(x_ref, attn, yssm_ref, p_ref, w, y_ref, act_ref, conv, row_parts=1):
    (_, wout_ref, gffn_ref, wup_ref, _, _, wdown_ref, wple_ref, gple_ref, wgate_ref, gfin_ref) = w
    h1 = x_ref[...] + _dot(attn, wout_ref[0:D_ATTN, :]) + _dot(yssm_ref[...], wout_ref[D_ATTN:, :])
    hn = _rms(h1, gffn_ref[...]).astype(BF16)
    rp = x_ref.shape[0] // row_parts
    blocks = [slice(r * rp, (r + 1) * rp) for r in range(row_parts)]
    hn_parts = [hn[rs] for rs in blocks]
    for j in range(D_FF // FF_CHUNK):
        c0 = j * FF_CHUNK
        u_gate = conv([_dot(h, wup_ref[:, c0:c0 + FF_CHUNK]) for h in hn_parts], c0)
        u_lin = conv([_dot(h, wup_ref[:, D_FF + c0:D_FF + c0 + FF_CHUNK]) for h in hn_parts], D_FF + c0)
        for r, rs in enumerate(blocks):
            act_ref[rs, c0:c0 + FF_CHUNK] = (_silu(u_gate[r]) * u_lin[r]).astype(BF16)
    h2 = h1 + _dot(act_ref[...], wdown_ref[...])
    e = _rms(_dot(p_ref[...].astype(BF16), wple_ref[...]), gple_ref[...])
    h3 = h2 + jax.nn.sigmoid(_dot(h2.astype(BF16), wgate_ref[...])) * e
    y_ref[...] = _rms(h3, gfin_ref[...])


def _mix_ffn_seg_kernel(x_ref, attn_ref, yssm_ref, p_ref, fpre_ref, *rest):
    w = rest[:N_FFN_WEIGHTS]
    y_ref, uout_ref, act_ref = rest[N_FFN_WEIGHTS:]
    fcw_ref, fcb_ref = w[4], w[5]
    tm = x_ref.shape[0]
    rin = lax.broadcasted_iota(jnp.int32, (tm, FF_CHUNK), 0) & (SUBLANE - 1)

    def conv(parts, c0):
        (u,) = parts
        cs = slice(c0, c0 + FF_CHUNK)
        pre = fpre_ref[:, cs]
        um1 = jnp.where(rin == 0, pltpu.roll(pre, tm - 1, 0), pltpu.roll(u, 1, 0))
        um2 = jnp.where(rin < 2, pre, pltpu.roll(u, 2, 0))
        uout_ref[:, cs] = u
        return [fcb_ref[:, cs] + fcw_ref[0:1, cs] * um2 + fcw_ref[1:2, cs] * um1 + fcw_ref[2:3, cs] * u]

    _ffn_core(x_ref, attn_ref[...], yssm_ref, p_ref, w, y_ref, act_ref, conv)


def _mix_ffn_kernel(x_ref, *rest, n_parts, n_tiles):
    parts = [rest[3 * b:3 * b + 3] for b in range(n_parts)]
    yssm_ref, p_ref, fpre_ref = rest[3 * n_parts:3 * n_parts + 3]
    rest = rest[3 * n_parts + 3:]
    w = rest[:N_FFN_WEIGHTS]
    y_ref, uout_ref, act_ref, tail_ref = rest[N_FFN_WEIGHTS:]
    fcw_ref, fcb_ref = w[4], w[5]
    tm = x_ref.shape[0]
    t = pl.program_id(1)

    @pl.when(t == 0)
    def _init():
        tail_ref[...] = jnp.zeros(tail_ref.shape, F32)
        tail_ref[SUBLANE - (FFN_CONV - 1):SUBLANE, :] = fpre_ref[...]

    attn = _merge_branches(parts, w[0])

    rin8 = lax.broadcasted_iota(jnp.int32, (SUBLANE, FF_CHUNK), 0)

    def conv(parts, c0):
        cs = slice(c0, c0 + FF_CHUNK)
        prev = tail_ref[:, cs]
        tail_ref[:, cs] = parts[-1][-SUBLANE:, :]
        outs = []
        for u in parts:
            r1, r2 = pltpu.roll(u, 1, 0), pltpu.roll(u, 2, 0)
            um1 = jnp.concatenate([jnp.where(rin8 < 1, pltpu.roll(prev, 1, 0), r1[0:SUBLANE]), r1[SUBLANE:]], axis=0)
            um2 = jnp.concatenate([jnp.where(rin8 < 2, pltpu.roll(prev, 2, 0), r2[0:SUBLANE]), r2[SUBLANE:]], axis=0)
            outs.append(fcb_ref[:, cs] + fcw_ref[0:1, cs] * um2 + fcw_ref[1:2, cs] * um1 + fcw_ref[2:3, cs] * u)
            prev = u[-SUBLANE:, :]
        return outs

    _ffn_core(x_ref, attn, yssm_ref, p_ref, w, y_ref, act_ref, conv, row_parts=FFN_ROW_PARTS)

    @pl.when(t == n_tiles - 1)
    def _fin():
        uout_ref[...] = tail_ref[...]


def _mix_ffn_seg(x, attn, yssm, p, fpre, weights):
    rows = x.shape[0]
    args = [x, attn, yssm, p, fpre, *weights]
    full = lambda a: pl.BlockSpec(a.shape, lambda i: (0, 0))
    return pl.pallas_call(
        _mix_ffn_seg_kernel,
        grid=(1,),
        in_specs=[full(a) for a in args],
        out_specs=[pl.BlockSpec((rows, D_MODEL), lambda i: (0, 0)), pl.BlockSpec((rows, 2 * D_FF), lambda i: (0, 0))],
        out_shape=[jax.ShapeDtypeStruct((rows, D_MODEL), F32), jax.ShapeDtypeStruct((rows, 2 * D_FF), F32)],
        scratch_shapes=[pltpu.VMEM((rows, D_FF), BF16)],
        compiler_params=_params("arbitrary"),
        name="mix_ffn_seg",
    )(*args)


def _mix_ffn(x, parts, yssm, p, fpre, weights, *, tm):
    bsz, s, _ = x.shape
    nt = s // tm
    tile = lambda b, t: (b, t, 0)
    seq = lambda b, t: (b, 0, 0)
    fixed = lambda b, t: (0, 0)
    in_specs = [pl.BlockSpec((None, tm, D_MODEL), tile)]
    args = [x]
    for part in parts:
        in_specs += [pl.BlockSpec((None, N_PAIRS, tm, LANE), lambda b, t: (b, 0, t, 0)),
                     pl.BlockSpec((None, tm, LANE), tile), pl.BlockSpec((None, tm, LANE), tile)]
        args += list(part)
    in_specs += [pl.BlockSpec((None, tm, D_SSM), tile), pl.BlockSpec((None, tm, D_PLE), tile),
                 pl.BlockSpec((None, FFN_CONV - 1, 2 * D_FF), seq)]
    args += [yssm, p, fpre]
    in_specs += [pl.BlockSpec(w.shape, fixed) for w in weights]
    args += list(weights)
    return pl.pallas_call(
        functools.partial(_mix_ffn_kernel, n_parts=len(parts), n_tiles=nt),
        grid=(bsz, nt),
        in_specs=in_specs,
        out_specs=[pl.BlockSpec((None, tm, D_MODEL), tile), pl.BlockSpec((None, SUBLANE, 2 * D_FF), seq)],
        out_shape=[jax.ShapeDtypeStruct((bsz, s, D_MODEL), F32), jax.ShapeDtypeStruct((bsz, SUBLANE, 2 * D_FF), F32)],
        scratch_shapes=[pltpu.VMEM((tm, D_FF), BF16), pltpu.VMEM((SUBLANE, 2 * D_FF), F32)],
        compiler_params=_params("parallel", "arbitrary"),
        name="mix_ffn",
    )(*args)


def _pad_lanes(v, width=LANE):
    return jnp.pad(v.astype(F32), (0, width - v.shape[0]))[None, :]


def kernel(x_prompt, x_sample, p_prompt, p_sample, cache_k, cache_v, state_ssm, state_conv, state_ffn_conv,
           rel_bias, g_mix, w_in, conv_w, conv_b, dt_bias, a_log, d_skip, g_ssm, w_out, g_ffn, w_up,
           ffn_conv_w, ffn_conv_b, w_down, w_ple_proj, g_ple, w_ple_gate, g_final):
    assert w_in.shape[0] == 1, "one layer"
    bp, s, _ = x_prompt.shape
    nsamp, t_len, _ = x_sample.shape
    n_keep = min(MAX_DISTANCE, s)

    w_main = _cast_w_main(w_in)
    w_dt = jnp.pad(w_in[0, :, O_DT:], ((0, 0), (0, LANE - (w_in.shape[2] - O_DT)))).astype(BF16)
    gmix = g_mix[0][None, :]
    ssd_params = (conv_w[0], conv_b[0][None, :], _pad_lanes(dt_bias[0]), _pad_lanes(a_log[0]),
                  jnp.repeat(d_skip[0], SSM_HEAD_DIM)[None, :], g_ssm[0][None, :])
    expand = (np.arange(2 * LANE)[:, None] % LANE == (np.arange(D_ATTN)[None, :] // HEAD_DIM)).astype(np.float32)
    ffn_weights = (jnp.asarray(expand, BF16), w_out[0].astype(BF16), g_ffn[0][None, :], w_up[0].astype(BF16),
                   ffn_conv_w[0], ffn_conv_b[0][None, :], w_down[0].astype(BF16), w_ple_proj[0].astype(BF16),
                   g_ple[0][None, :], w_ple_gate[0].astype(BF16), g_final[None, :])
    bias_p, b16, b4, b1, bn = _bias_tables(rel_bias, t_len)

    (q1, k1, v1, q4, k4, v4, q16, k16, v16, kt, vt, yssm_p, ssm_p, ctail_p) = _inproj_ssd(
        x_prompt, gmix, w_main, w_dt,
        jnp.zeros((bp, SSM_CONV - 1, CONV_DIM), F32),
        jnp.zeros((bp, N_SSM_HEADS, SSM_HEAD_DIM, D_STATE), F32),
        ssd_params, tm=512, n_keep=n_keep)
    parts = [_attn_prompt_branch(q1[:, None], k1[:, None], v1[:, None], bias_p[0], 1),
             _attn_prompt_branch(q4, k4, v4, bias_p[1], 4),
             _attn_prompt_branch(q16, k16, v16, bias_p[2], 16)]
    y_prompt, tail_p = _mix_ffn(x_prompt, parts, yssm_p, p_prompt[0],
                                jnp.zeros((bp, FFN_CONV - 1, 2 * D_FF), F32), ffn_weights, tm=512)
    k_prompt = jnp.transpose(kt, (0, 3, 1, 2))[None]
    v_prompt = jnp.transpose(vt, (0, 3, 1, 2))[None]
    conv_prompt = ctail_p[:, SUBLANE - (SSM_CONV - 1):][None]
    ffn_conv_prompt = tail_p[:, SUBLANE - (FFN_CONV - 1):][None]

    rows = nsamp * t_len
    qs, ks, vs, zs, xbcs, dts = _inproj(x_sample.reshape(rows, D_MODEL), gmix, w_main, w_dt)
    s3 = lambda a: a.reshape(nsamp, t_len, a.shape[-1])
    yssm_s, ssm_s = _ssd_short(s3(xbcs), s3(zs), s3(dts), state_conv[0], state_ssm[0], ssd_params)
    attn_s = _attn_sample(s3(qs), s3(ks), s3(vs),
                          jnp.transpose(cache_k[0], (0, 2, 3, 1)), jnp.transpose(cache_v[0], (0, 2, 3, 1)),
                          b16, b4, b1, bn)
    fpre = jnp.pad(state_ffn_conv[0], ((0, 0), (0, t_len - (FFN_CONV - 1)), (0, 0))).reshape(rows, 2 * D_FF)
    y_s, u_s = _mix_ffn_seg(x_sample.reshape(rows, D_MODEL), attn_s.reshape(rows, D_ATTN),
                            yssm_s.reshape(rows, D_SSM), p_sample[0].reshape(rows, D_PLE), fpre, ffn_weights)
    y_sample = y_s.reshape(nsamp, t_len, D_MODEL)
    k_sample = ks.reshape(1, nsamp, t_len, N_HEADS, HEAD_DIM)
    v_sample = vs.reshape(1, nsamp, t_len, N_HEADS, HEAD_DIM)
    conv_sample = s3(xbcs)[:, t_len - (SSM_CONV - 1):][None]
    ffn_conv_sample = u_s.reshape(nsamp, t_len, 2 * D_FF)[:, t_len - (FFN_CONV - 1):][None]

    return (y_prompt, y_sample, k_prompt, v_prompt, k_sample, v_sample,
            ssm_p[None], ssm_s[None], conv_prompt, conv_sample, ffn_conv_prompt, ffn_conv_sample)
```

```python
import functools
import math

import numpy as np
import jax
import jax.numpy as jnp
from jax import lax
from jax.experimental import pallas as pl
from jax.experimental.pallas import tpu as pltpu

F32 = jnp.float32
BF16 = jnp.bfloat16

D_MODEL = 1024
HEAD_DIM = 64
N_HEADS = 8
D_ATTN = N_HEADS * HEAD_DIM
N_PAIRS = N_HEADS // 2
DILATIONS = (1, 4, 16)
N_STEPS = 128
BLK = 128
N_BUCKETS = 32
MAX_DISTANCE = 2048
D_SSM = 512
N_SSM_HEADS = 8
SSM_HEAD_DIM = 64
D_STATE = 128
N_SSM_GROUPS = 2
SSM_CONV = 4
CONV_DIM = D_SSM + 2 * N_SSM_GROUPS * D_STATE
SSD_CHUNK = 128
D_FF = 2816
FFN_CONV = 3
D_PLE = 256
EPS = 1e-6
NEG = -1e30
LOG2E = math.log2(math.e)

LANE = 128
SUBLANE = 8
FF_CHUNK = 256
VMEM_LIMIT = 56 * 1024 * 1024

O_Q, O_K, O_V, O_Z, O_XBC, O_DT = 0, 512, 1024, 1536, 2048, 3072


def _rel_bucket_np(dist):
    dist = np.asarray(dist, np.int32)
    max_exact = N_BUCKETS // 2
    d = np.maximum(dist, 1).astype(np.float32)
    large = max_exact + (np.log(d / np.float32(max_exact)) / np.float32(math.log(MAX_DISTANCE / max_exact))
                         * np.float32(N_BUCKETS - max_exact)).astype(np.int32)
    large = np.minimum(large, N_BUCKETS - 1)
    return np.where(dist < max_exact, dist, large)


def _nt_dot(a, b):
    return lax.dot_general(a, b, (((1,), (1,)), ((), ())), preferred_element_type=F32)


def _dot(a, b):
    return jnp.dot(a, b, preferred_element_type=F32)


def _silu(x):
    return x * jax.nn.sigmoid(x)


def _softplus(x):
    return jnp.maximum(x, 0.0) + jnp.log1p(jnp.exp(-jnp.abs(x)))


def _rms(x, g):
    return x * lax.rsqrt(jnp.mean(x * x, axis=-1, keepdims=True) + EPS) * g


def _lane_lt64(shape):
    return lax.broadcasted_iota(jnp.int32, shape, len(shape) - 1) < HEAD_DIM


def _params(*sem):
    return pltpu.CompilerParams(dimension_semantics=sem, vmem_limit_bytes=VMEM_LIMIT)


def _bias_kernel(rb_ref, rbt_ref, ig_ref, i16_ref, i4_ref, i1_ref, in_ref, tp_ref, t16_ref, t4_ref, t1_ref, tn_ref):
    def lookup(idx, h):
        def body(b, acc):
            return jnp.where(idx == b, rb_ref[b, h], acc)
        return lax.fori_loop(0, N_BUCKETS, body, jnp.full(idx.shape, NEG, F32), unroll=True)

    for br in range(len(DILATIONS)):
        idx = jnp.broadcast_to(ig_ref[br], (N_HEADS, 2 * BLK))
        gen = jnp.full((N_HEADS, 2 * BLK), NEG, F32)
        for b in range(N_BUCKETS):
            gen = jnp.where(idx == b, jnp.broadcast_to(rbt_ref[:, b:b + 1], (N_HEADS, 2 * BLK)), gen)
        for h in range(N_HEADS):
            rows = jnp.broadcast_to(gen[h:h + 1, :], (BLK, 2 * BLK))
            tp_ref[br, h] = (pltpu.roll(rows, 0, 1, stride=1, stride_axis=0) * LOG2E).T
    for h in range(N_HEADS):
        jp, half = divmod(h, 2)
        rs = slice(half * SUBLANE, (half + 1) * SUBLANE)
        t16_ref[jp, rs, :] = lookup(i16_ref[...], h)
        t4_ref[jp, rs, :] = lookup(i4_ref[...], h)
        t1_ref[jp, rs, :] = lookup(i1_ref[...], h)
        for br in range(len(DILATIONS)):
            tn_ref[br, jp, rs, :] = lookup(in_ref[br], h)


def _bucket_maps(t_len):
    j = BLK - np.arange(2 * BLK)[None, :]
    prompt = np.stack([np.where(j >= 0, _rel_bucket_np(np.clip(j, 0, N_STEPS) * d), -1) for d in DILATIONS])

    t = np.arange(t_len)[:, None]

    def sample_map(diff, dil):
        ok = (diff >= 0) & (diff % dil == 0) & (diff // dil <= N_STEPS)
        return np.where(ok, _rel_bucket_np(np.maximum(diff, 0)), -1).astype(np.int32)

    w = np.arange(MAX_DISTANCE)[None, :]
    cache = {d: sample_map(MAX_DISTANCE + t - w, d) for d in DILATIONS}
    g = np.arange(LANE)[None, :]
    new = np.stack([np.where(g < t_len, sample_map(t - g, d), -1) for d in DILATIONS])
    return (prompt.astype(np.int32), cache[16], cache[4][:, MAX_DISTANCE - 4 * BLK:],
            cache[1][:, MAX_DISTANCE - BLK:], new.astype(np.int32))


def _bias_tables(rel_bias, t_len):
    maps = _bucket_maps(t_len)
    nb = len(DILATIONS)
    shapes = [(nb, N_HEADS, 2 * BLK, BLK), (N_PAIRS, 2 * t_len, MAX_DISTANCE), (N_PAIRS, 2 * t_len, 4 * BLK),
              (N_PAIRS, 2 * t_len, BLK), (nb, N_PAIRS, 2 * t_len, LANE)]
    return pl.pallas_call(
        _bias_kernel,
        in_specs=[pl.BlockSpec(memory_space=pltpu.SMEM)] + [pl.BlockSpec(memory_space=pltpu.VMEM)] * 6,
        out_specs=[pl.BlockSpec(memory_space=pltpu.VMEM)] * 5,
        out_shape=[jax.ShapeDtypeStruct(s, F32) for s in shapes],
        compiler_params=pltpu.CompilerParams(vmem_limit_bytes=VMEM_LIMIT),
        name="bias_tables",
    )(rel_bias, rel_bias.T, *[jnp.asarray(m) for m in maps])


W_CAST_COLS = 384


def _cast_kernel(wt_ref, o_ref):
    o_ref[...] = wt_ref[...].T.astype(BF16)


def _cast_w_main(w_in):
    return pl.pallas_call(
        _cast_kernel,
        grid=(O_DT // W_CAST_COLS,),
        in_specs=[pl.BlockSpec((W_CAST_COLS, D_MODEL), lambda i: (i, 0))],
        out_specs=pl.BlockSpec((D_MODEL, W_CAST_COLS), lambda i: (0, i)),
        out_shape=jax.ShapeDtypeStruct((D_MODEL, O_DT), BF16),
        compiler_params=_params("parallel"),
        name="cast_w_in",
    )(jnp.transpose(w_in[0]))


def _inproj_kernel(x_ref, g_ref, w_ref, wdt_ref, q_ref, k_ref, v_ref, z_ref, xbc_ref, dt_ref):
    xn = _rms(x_ref[...], g_ref[...]).astype(BF16)

    def proj(lo, hi):
        return _dot(xn, w_ref[:, lo:hi])

    q_ref[...] = proj(O_Q, O_K) * (HEAD_DIM ** -0.5)
    k_ref[...] = proj(O_K, O_V)
    v_ref[...] = proj(O_V, O_Z)
    z_ref[...] = proj(O_Z, O_XBC)
    xbc_ref[...] = proj(O_XBC, O_DT)
    dt_ref[...] = _dot(xn, wdt_ref[...])


def _inproj(x, g_mix, w_main, w_dt):
    rows = x.shape[0]
    full = lambda a: pl.BlockSpec(a.shape, lambda i: (0, 0))
    widths = (D_ATTN, D_ATTN, D_ATTN, D_SSM, CONV_DIM, LANE)
    return pl.pallas_call(
        _inproj_kernel,
        grid=(1,),
        in_specs=[full(x), full(g_mix), full(w_main), full(w_dt)],
        out_specs=[pl.BlockSpec((rows, w), lambda i: (0, 0)) for w in widths],
        out_shape=[jax.ShapeDtypeStruct((rows, w), F32) for w in widths],
        compiler_params=_params("arbitrary"),
        name="inproj_sample",
    )(x, g_mix, w_main, w_dt)


def _inproj_ssd_kernel(x_ref, g_ref, w_ref, wdt_ref, prefix_ref, state0_ref, cw_ref, cb_ref, dtb_ref, alog_ref, dskip_ref,
                       gssm_ref, q1, k1, v1, q4, k4, v4, q16, k16, v16, kt_ref, vt_ref, y_ref, state_ref, ctail_ref,
                       perm_ref, mid_ref, xp_ref, z_scr, dt_scr, st_ref, *, keep_from, n_tiles):
    tm = x_ref.shape[0]
    L = SSD_CHUNK
    t = pl.program_id(1)

    @pl.when(t == 0)
    def _init():
        xp_ref[0:SUBLANE, :] = jnp.zeros((SUBLANE, CONV_DIM), F32)
        xp_ref[SUBLANE - (SSM_CONV - 1):SUBLANE, :] = prefix_ref[...]
        _load_state(state0_ref, st_ref)

    xn = _rms(x_ref[...], g_ref[...]).astype(BF16)

    def proj(lo, hi):
        return _dot(xn, w_ref[:, lo:hi])

    xp_ref[SUBLANE:SUBLANE + tm, :] = proj(O_XBC, O_DT)
    dt_scr[...] = _dot(xn, wdt_ref[...])
    z_scr[...] = proj(O_Z, O_XBC)

    def project_attn(i):
        lo, scale, nat, r4, r16 = ((O_Q, HEAD_DIM ** -0.5 * LOG2E, q1, q4, q16), (O_K, None, k1, k4, k16),
                                   (O_V, None, v1, v4, v16))[i]
        val = proj(lo, lo + D_ATTN)
        if scale is not None:
            val = val * scale
        nat[...] = val.astype(BF16)
        for jp in range(N_PAIRS):
            perm_ref[i, jp] = val[:, jp * LANE:(jp + 1) * LANE]
        for ra in range(4):
            for jp in range(N_PAIRS):
                mid_ref[i, ra, jp] = perm_ref[i, jp, pl.ds(ra, tm // 4, stride=4), :]
            r4[ra] = jnp.concatenate([mid_ref[i, ra, jp] for jp in range(N_PAIRS)], axis=1).astype(BF16)
            for rb in range(4):
                rows = [mid_ref[i, ra, jp, pl.ds(rb, tm // 16, stride=4), :] for jp in range(N_PAIRS)]
                r16[ra + 4 * rb] = jnp.concatenate(rows, axis=1).astype(BF16)

    prm = (cw_ref, cb_ref, dtb_ref, alog_ref, dskip_ref, gssm_ref)
    n_chunks = tm // L
    assert n_chunks >= 3
    for c in range(n_chunks):
        base = SUBLANE + c * L - (SSM_CONV - 1)
        y = _ssd_chunk(lambda k, base=base: xp_ref[base + k:base + k + L, :],
                       z_scr[c * L:(c + 1) * L, :], dt_scr[c * L:(c + 1) * L, :], prm, st_ref, L)
        y_ref[c * L:(c + 1) * L, :] = y.astype(y_ref.dtype)
        if c < 3:
            project_attn(c)
    xp_ref[0:SUBLANE, :] = xp_ref[tm:tm + SUBLANE, :]

    @pl.when(t == n_tiles - 1)
    def _fin():
        _store_state(st_ref, state_ref)
        ctail_ref[...] = xp_ref[0:SUBLANE, :]

    @pl.when(t >= keep_from)
    def _keep():
        for i, out_ref in ((1, kt_ref), (2, vt_ref)):
            for jp in range(N_PAIRS):
                out_ref[2 * jp:2 * jp + 2] = perm_ref[i, jp].T.reshape(2, HEAD_DIM, tm)


def _inproj_ssd(x, g_mix, w_main, w_dt, prefix, state0, ssd_params, *, tm, n_keep):
    ns, l, _ = x.shape
    nt = l // tm
    keep_from = (l - n_keep) // tm
    tile = lambda s, t: (s, t, 0)
    seq = lambda s, t: (s, 0, 0)
    fixed = lambda s, t: (0, 0)
    state_spec = pl.BlockSpec((None, N_SSM_HEADS, SSM_HEAD_DIM, D_STATE), lambda s, t: (s, 0, 0, 0))
    nat = lambda w: (pl.BlockSpec((None, tm, w), tile), jax.ShapeDtypeStruct((ns, l, w), BF16))
    outs = [nat(D_ATTN)] * 3
    for dil in (4, 16):
        outs += [(pl.BlockSpec((None, dil, tm // dil, D_ATTN), lambda s, t: (s, 0, t, 0)),
                  jax.ShapeDtypeStruct((ns, dil, l // dil, D_ATTN), BF16))] * 3
    outs += [(pl.BlockSpec((None, N_HEADS, HEAD_DIM, tm), lambda s, t: (s, 0, 0, jnp.maximum(t - keep_from, 0))),
              jax.ShapeDtypeStruct((ns, N_HEADS, HEAD_DIM, n_keep), F32))] * 2
    outs += [nat(D_SSM),
             (state_spec, jax.ShapeDtypeStruct((ns, N_SSM_HEADS, SSM_HEAD_DIM, D_STATE), F32)),
             (pl.BlockSpec((None, SUBLANE, CONV_DIM), seq), jax.ShapeDtypeStruct((ns, SUBLANE, CONV_DIM), F32))]
    return pl.pallas_call(
        functools.partial(_inproj_ssd_kernel, keep_from=keep_from, n_tiles=nt),
        grid=(ns, nt),
        in_specs=[pl.BlockSpec((None, tm, D_MODEL), tile),
                  pl.BlockSpec((1, D_MODEL), fixed),
                  pl.BlockSpec((D_MODEL, O_DT), fixed),
                  pl.BlockSpec((D_MODEL, LANE), fixed),
                  pl.BlockSpec((None, SSM_CONV - 1, CONV_DIM), seq),
                  state_spec] + [pl.BlockSpec(p.shape, fixed) for p in ssd_params],
        out_specs=[o[0] for o in outs],
        out_shape=[o[1] for o in outs],
        scratch_shapes=[pltpu.VMEM((3, N_PAIRS, tm, LANE), F32),
                        pltpu.VMEM((3, 4, N_PAIRS, tm // 4, LANE), F32),
                        pltpu.VMEM((SUBLANE + tm, CONV_DIM), F32),
                        pltpu.VMEM((tm, D_SSM), F32),
                        pltpu.VMEM((tm, LANE), F32),
                        pltpu.VMEM((N_PAIRS, D_STATE, LANE), F32)],
        compiler_params=_params("parallel", "arbitrary"),
        name="inproj_ssd",
    )(x, g_mix, w_main, w_dt, prefix, state0, *ssd_params)


def _load_state(state0_ref, st_ref):
    for jp in range(N_PAIRS):
        st_ref[jp] = state0_ref[2 * jp:2 * jp + 2].reshape(2 * SSM_HEAD_DIM, D_STATE).T


def _store_state(st_ref, state_ref):
    for jp in range(N_PAIRS):
        state_ref[2 * jp:2 * jp + 2] = st_ref[jp].T.reshape(2, SSM_HEAD_DIM, D_STATE)


def _ssd_chunk(xwin, z, dt_raw, prm, st_ref, valid_len):
    cw_ref, cb_ref, dtb_ref, alog_ref, dskip_ref, gssm_ref = prm
    L = SSD_CHUNK
    conv = cb_ref[...]
    for k in range(SSM_CONV):
        conv = conv + cw_ref[k:k + 1, :] * xwin(k)
    xc = _silu(conv)
    xs = xc[:, :D_SSM]
    gn = N_SSM_GROUPS * D_STATE

    row = lax.broadcasted_iota(jnp.int32, (L, L), 0)
    col = lax.broadcasted_iota(jnp.int32, (L, L), 1)
    tri = row >= col
    lt64 = col < SSM_HEAD_DIM

    dt = _softplus(dt_raw + dtb_ref[...])
    if valid_len < L:
        dt = jnp.where(row < valid_len, dt, 0.0)
    a = dt * (-jnp.exp(alog_ref[...]))
    acum = a
    shift = 1
    while shift < L:
        acum = acum + jnp.where(row >= shift, pltpu.roll(acum, shift, 0), 0.0)
        shift *= 2
    acum_t = acum.T
    dt_t = dt.T
    e_slab = jnp.exp(acum)
    de_slab = jnp.exp(acum[L - 1:L, :] - acum) * dt

    def head_terms(h, cb):
        colb = jnp.broadcast_to(acum[:, h:h + 1], (L, L))
        rowb = jnp.broadcast_to(acum_t[h:h + 1, :], (L, L))
        decay = jnp.exp(jnp.where(tri, colb - rowb, NEG))
        m = (cb * decay * jnp.broadcast_to(dt_t[h:h + 1, :], (L, L))).astype(BF16)
        e = jnp.broadcast_to(e_slab[:, h:h + 1], (L, L))
        return m, e, jnp.broadcast_to(de_slab[:, h:h + 1], (L, L)), e[L - 1:L, :]

    pairs = []
    for g in range(N_SSM_GROUPS):
        bg = xc[:, D_SSM + g * D_STATE:D_SSM + (g + 1) * D_STATE]
        cg = xc[:, D_SSM + gn + g * D_STATE:D_SSM + gn + (g + 1) * D_STATE].astype(BF16)
        cb = _nt_dot(cg, bg.astype(BF16))
        bg_t = bg.T.astype(BF16)
        for i in range(N_PAIRS // N_SSM_GROUPS):
            jp = g * (N_PAIRS // N_SSM_GROUPS) + i
            xpair = xs[:, jp * LANE:(jp + 1) * LANE]
            xpair_b = xpair.astype(BF16)
            m_a, e_a, de_a, el_a = head_terms(2 * jp, cb)
            m_b, e_b, de_b, el_b = head_terms(2 * jp + 1, cb)
            st = st_ref[jp]
            y_diag = jnp.where(lt64, _dot(m_a, xpair_b), _dot(m_b, xpair_b))
            y_off = _dot(cg, st.astype(BF16)) * jnp.where(lt64, e_a, e_b)
            xd = (xpair * jnp.where(lt64, de_a, de_b)).astype(BF16)
            st_ref[jp] = st * jnp.where(lt64[0:1, :], el_a, el_b) + _dot(bg_t, xd)
            pairs.append(y_diag + y_off + dskip_ref[:, jp * LANE:(jp + 1) * LANE] * xpair)
    y = jnp.concatenate(pairs, axis=1)

    yf = y * _silu(z)
    gw = D_SSM // N_SSM_GROUPS
    normed = []
    for g in range(N_SSM_GROUPS):
        seg = yf[:, g * gw:(g + 1) * gw]
        normed.append(seg * lax.rsqrt(jnp.mean(seg * seg, axis=-1, keepdims=True) + EPS))
    return jnp.concatenate(normed, axis=1) * gssm_ref[...]


def _ssd_short_kernel(xbc_ref, z_ref, dt_ref, prefix_ref, state0_ref, cw_ref, cb_ref, dtb_ref, alog_ref,
                      dskip_ref, gssm_ref, y_ref, state_ref, xp_ref, st_ref):
    n_seq, t_len, _ = xbc_ref.shape
    L = SSD_CHUNK
    lead = SUBLANE - (SSM_CONV - 1)
    pad = lambda a: jnp.concatenate([a, jnp.zeros((L - t_len, a.shape[1]), F32)], axis=0)
    for i in range(n_seq):
        xp_ref[0:lead, :] = jnp.zeros((lead, CONV_DIM), F32)
        xp_ref[lead:SUBLANE, :] = prefix_ref[i]
        xp_ref[SUBLANE:SUBLANE + t_len, :] = xbc_ref[i]
        xp_ref[SUBLANE + t_len:SUBLANE + L, :] = jnp.zeros((L - t_len, CONV_DIM), F32)
        _load_state(state0_ref.at[i], st_ref)
        y = _ssd_chunk(lambda k: xp_ref[lead + k:lead + k + L, :], pad(z_ref[i]), pad(dt_ref[i]),
                       (cw_ref, cb_ref, dtb_ref, alog_ref, dskip_ref, gssm_ref), st_ref, t_len)
        y_ref[i] = y[0:t_len].astype(y_ref.dtype)
        _store_state(st_ref, state_ref.at[i])


def _ssd_short(xbc, z, dt, prefix, state0, ssd_params):
    ns, t_len, _ = xbc.shape
    g = SAMPLE_SEQS
    seq = lambda s: (s, 0, 0)
    state_spec = pl.BlockSpec((g, N_SSM_HEADS, SSM_HEAD_DIM, D_STATE), lambda s: (s, 0, 0, 0))
    return pl.pallas_call(
        _ssd_short_kernel,
        grid=(ns // g,),
        in_specs=[pl.BlockSpec((g, t_len, CONV_DIM), seq),
                  pl.BlockSpec((g, t_len, D_SSM), seq),
                  pl.BlockSpec((g, t_len, LANE), seq),
                  pl.BlockSpec((g, SSM_CONV - 1, CONV_DIM), seq),
                  state_spec] + [pl.BlockSpec(p.shape, lambda s: (0, 0)) for p in ssd_params],
        out_specs=[pl.BlockSpec((g, t_len, D_SSM), seq), state_spec],
        out_shape=[jax.ShapeDtypeStruct((ns, t_len, D_SSM), BF16),
                   jax.ShapeDtypeStruct((ns, N_SSM_HEADS, SSM_HEAD_DIM, D_STATE), F32)],
        scratch_shapes=[pltpu.VMEM((SUBLANE + SSD_CHUNK, CONV_DIM), F32),
                        pltpu.VMEM((N_PAIRS, D_STATE, LANE), F32)],
        compiler_params=_params("parallel"),
        name="ssd_short",
    )(xbc, z, dt, prefix, state0, *ssd_params)


ATTN_UNITS = 8


def _attn_stages(q_ref, kp_ref, kc_ref, vp_ref, vc_ref, bias_ref, o_ref, max_ref, den_ref, s_ref, p_ref, *,
                 n_res, blk0, rows_of):
    lt64 = _lane_lt64((BLK, LANE))
    head_row = lax.broadcasted_iota(jnp.int32, (N_HEADS, BLK), 0)
    zero = jnp.zeros((BLK, LANE), BF16)

    def cur(ref, g):
        j, r = divmod(g, n_res)
        return ref.at[r, pl.ds(j * BLK, BLK)]

    def prev_of(g, p_ref_, c_ref_):
        j, r = divmod(g, n_res)
        if j == 0:
            return p_ref_.at[r], blk0
        return c_ref_.at[r, pl.ds((j - 1) * BLK, BLK)], None

    def scores(g, slot):
        kprev, masked = prev_of(g, kp_ref, kc_ref)
        for jp in range(N_PAIRS):
            sl = slice(jp * LANE, (jp + 1) * LANE)
            qp = cur(q_ref, g)[:, sl]
            kcat = jnp.concatenate([kprev[:, sl], cur(kc_ref, g)[:, sl]], axis=0)
            for half in range(2):
                h = 2 * jp + half
                qm = jnp.where(lt64 if half == 0 else ~lt64, qp, zero)
                bias = bias_ref[h]
                if masked is not None:
                    bias = jnp.concatenate([jnp.where(masked, NEG, bias[0:BLK]), bias[BLK:]], axis=0)
                s_ref[slot, h] = _nt_dot(kcat, qm) + bias

    def softmax(g, slot):
        s = s_ref[slot]
        m = jnp.max(s, axis=1, keepdims=True)
        p = jnp.exp2(s - m)
        den = jnp.sum(p, axis=1, keepdims=True)
        p_ref[slot] = p.astype(BF16)
        m_rows = jnp.zeros((N_HEADS, BLK), F32)
        den_rows = jnp.zeros((N_HEADS, BLK), F32)
        for h in range(N_HEADS):
            m_rows = jnp.where(head_row == h, jnp.broadcast_to(m[h], (N_HEADS, BLK)), m_rows)
            den_rows = jnp.where(head_row == h, jnp.broadcast_to(den[h], (N_HEADS, BLK)), den_rows)
        m_nat = m_rows * (1.0 / LOG2E)
        max_ref[rows_of(g), :] = jnp.concatenate([m_nat, jnp.zeros((BLK - N_HEADS, BLK), F32)], axis=0).T
        den_ref[rows_of(g), :] = jnp.concatenate([den_rows, jnp.ones((BLK - N_HEADS, BLK), F32)], axis=0).T

    def values(g, slot):
        vprev, _ = prev_of(g, vp_ref, vc_ref)
        for jp in range(N_PAIRS):
            sl = slice(jp * LANE, (jp + 1) * LANE)
            vcat = jnp.concatenate([vprev[:, sl], cur(vc_ref, g)[:, sl]], axis=0)
            pv = [lax.dot_general(p_ref[slot, 2 * jp + half], vcat, (((0,), (0,)), ((), ())),
                                  preferred_element_type=F32) for half in range(2)]
            o_ref[jp, rows_of(g), :] = jnp.where(lt64, pv[0], pv[1])

    return scores, softmax, values


def _attn_prompt_kernel(q_ref, kp_ref, kc_ref, vp_ref, vc_ref, bias_ref, o_ref, max_ref, den_ref, s_ref, p_ref, *, dil):
    n_res, rows, _ = q_ref.shape
    units = n_res * (rows // BLK)

    def rows_of(g):
        j, r = divmod(g, n_res)
        if dil == 1:
            return slice(j * BLK, (j + 1) * BLK)
        return pl.ds(pl.program_id(2) * n_res + r + j * BLK * dil, BLK, stride=dil)

    scores, softmax, values = _attn_stages(q_ref, kp_ref, kc_ref, vp_ref, vc_ref, bias_ref, o_ref, max_ref, den_ref,
                                           s_ref, p_ref, n_res=n_res, blk0=pl.program_id(1) == 0, rows_of=rows_of)
    scores(0, 0)
    for g in range(units):
        softmax(g, g % 2)
        if g + 1 < units:
            scores(g + 1, (g + 1) % 2)
        values(g, g % 2)


def _attn_prompt_branch(q, k, v, bias, dil):
    bsz, _, l, _ = q.shape
    s = l * dil
    n_res = min(ATTN_UNITS, dil)
    n_blk = ATTN_UNITS // n_res
    cur = pl.BlockSpec((None, n_res, n_blk * BLK, D_ATTN), lambda b, n, r: (b, r, n, 0))
    prev = pl.BlockSpec((None, n_res, BLK, D_ATTN), lambda b, n, r: (b, r, jnp.maximum(n * n_blk - 1, 0), 0))
    rows = n_blk * BLK * dil
    return pl.pallas_call(
        functools.partial(_attn_prompt_kernel, dil=dil),
        grid=(bsz, l // (n_blk * BLK), dil // n_res),
        in_specs=[cur, prev, cur, prev, cur,
                  pl.BlockSpec((N_HEADS, 2 * BLK, BLK), lambda b, n, r: (0, 0, 0))],
        out_specs=[pl.BlockSpec((None, N_PAIRS, rows, LANE), lambda b, n, r: (b, 0, n, 0)),
                   pl.BlockSpec((None, rows, LANE), lambda b, n, r: (b, n, 0)),
                   pl.BlockSpec((None, rows, LANE), lambda b, n, r: (b, n, 0))],
        out_shape=[jax.ShapeDtypeStruct((bsz, N_PAIRS, s, LANE), F32),
                   jax.ShapeDtypeStruct((bsz, s, LANE), F32),
                   jax.ShapeDtypeStruct((bsz, s, LANE), F32)],
        scratch_shapes=[pltpu.VMEM((2, N_HEADS, 2 * BLK, BLK), F32), pltpu.VMEM((2, N_HEADS, 2 * BLK, BLK), BF16)],
        compiler_params=_params("parallel", "parallel", "arbitrary"),
        name=f"attn_prompt_d{dil}",
    )(q, k, k, v, v, bias)


SAMPLE_SEQS = 2


def _attn_sample_kernel(q_ref, kn_ref, vn_ref, kc_ref, vc_ref, b16_ref, b4_ref, b1_ref, bn_ref, o_ref):
    for i in range(q_ref.shape[0]):
        _attn_sample_one(q_ref.at[i], kn_ref.at[i], vn_ref.at[i], kc_ref.at[i], vc_ref.at[i],
                         b16_ref, b4_ref, b1_ref, bn_ref, o_ref.at[i])


def _attn_sample_one(q_ref, kn_ref, vn_ref, kc_ref, vc_ref, b16_ref, b4_ref, b1_ref, bn_ref, o_ref):
    t_len = q_ref.shape[0]
    w = kc_ref.shape[-1]
    lt64 = _lane_lt64((t_len, LANE))
    pad = jnp.zeros((BLK - t_len, D_ATTN), F32)
    kn = jnp.concatenate([kn_ref[...], pad], axis=0).astype(BF16)
    vn = jnp.concatenate([vn_ref[...], pad], axis=0).astype(BF16)
    outs = []
    for jp in range(N_PAIRS):
        sl = slice(jp * LANE, (jp + 1) * LANE)
        qp = q_ref[:, sl]
        qm = jnp.concatenate([jnp.where(lt64, qp, 0.0), jnp.where(lt64, 0.0, qp)], axis=0).astype(BF16)
        kt = kc_ref[2 * jp:2 * jp + 2].reshape(2 * HEAD_DIM, w).astype(BF16)
        vt = vc_ref[2 * jp:2 * jp + 2].reshape(2 * HEAD_DIM, w).astype(BF16)
        s = _dot(qm, kt)
        s_new = _nt_dot(qm, kn[:, sl])
        w4, w1 = w - 4 * BLK, w - BLK
        tiles = [s + b16_ref[jp], s[:, w4:] + b4_ref[jp], s[:, w1:] + b1_ref[jp]]
        tiles += [s_new + bn_ref[br, jp] for br in range(len(DILATIONS))]
        m = functools.reduce(jnp.maximum, [jnp.max(t, axis=-1, keepdims=True) for t in tiles])
        ps = [jnp.exp(t - m) for t in tiles]
        den = functools.reduce(lambda a, b: a + b, [jnp.sum(p, axis=-1, keepdims=True) for p in ps])
        p16, p4, p1 = ps[0], ps[1], ps[2]
        p_cache = jnp.concatenate([p16[:, :w4], p16[:, w4:w1] + p4[:, :w1 - w4],
                                   p16[:, w1:] + p4[:, w1 - w4:] + p1], axis=1)
        p_new = ps[3] + ps[4] + ps[5]
        o2 = (_nt_dot(p_cache.astype(BF16), vt) + _dot(p_new.astype(BF16), vn[:, sl])) / den
        outs.append(jnp.where(lt64, o2[0:t_len], o2[t_len:2 * t_len]))
    o_ref[...] = jnp.concatenate(outs, axis=1).astype(o_ref.dtype)


def _attn_sample(q, kn, vn, kc, vc, b16, b4, b1, bn):
    ns, t_len, _ = q.shape
    w = kc.shape[-1]
    assert w == MAX_DISTANCE and t_len == SUBLANE
    tok = pl.BlockSpec((SAMPLE_SEQS, t_len, D_ATTN), lambda s: (s, 0, 0))
    cache = pl.BlockSpec((SAMPLE_SEQS, N_HEADS, HEAD_DIM, w), lambda s: (s, 0, 0, 0))
    full = lambda a: pl.BlockSpec(a.shape, lambda s: (0,) * a.ndim)
    return pl.pallas_call(
        _attn_sample_kernel,
        grid=(ns // SAMPLE_SEQS,),
        in_specs=[tok, tok, tok, cache, cache, full(b16), full(b4), full(b1), full(bn)],
        out_specs=tok,
        out_shape=jax.ShapeDtypeStruct((ns, t_len, D_ATTN), BF16),
        compiler_params=_params("parallel"),
        name="attn_sample",
    )(q, kn, vn, kc, vc, b16, b4, b1, bn)


SEG_ROW_PARTS = 2
FFN_ROW_PARTS = 4
N_FFN_WEIGHTS = 11


def _merge_branches(parts, expand_ref):
    maxes = [max_ref[...] for _, max_ref, _ in parts]
    mx = functools.reduce(jnp.maximum, maxes)
    ws = [jnp.exp(m - mx) for m in maxes]
    den = functools.reduce(lambda a, b: a + b, [w * den_ref[...] for w, (_, _, den_ref) in zip(ws, parts)])
    attn = None
    for (o_ref, _, _), w in zip(parts, ws):
        wn = w / den
        hi = wn.astype(BF16)
        lo = (wn - hi.astype(F32)).astype(BF16)
        wexp = _dot(jnp.concatenate([hi, lo], axis=1), expand_ref[...])
        o = jnp.concatenate([o_ref[jp] for jp in range(N_PAIRS)], axis=1)
        attn = wexp * o if attn is None else attn + wexp * o
    return attn.astype(BF16)


def _ffn_core(x_ref, attn, yssm_ref, p_ref, w, y_ref, act_ref, conv, row_parts=1):
    (_, wout_ref, gffn_ref, wup_ref, _, _, wdown_ref, wple_ref, gple_ref, wgate_ref, gfin_ref) = w
    h1 = x_ref[...] + _dot(attn, wout_ref[0:D_ATTN, :]) + _dot(yssm_ref[...], wout_ref[D_ATTN:, :])
    hn = _rms(h1, gffn_ref[...]).astype(BF16)
    rp = x_ref.shape[0] // row_parts
    blocks = [slice(r * rp, (r + 1) * rp) for r in range(row_parts)]
    hn_parts = [hn[rs] for rs in blocks]
    for j in range(D_FF // FF_CHUNK):
        c0 = j * FF_CHUNK
        u_gate = conv([_dot(h, wup_ref[:, c0:c0 + FF_CHUNK]) for h in hn_parts], c0)
        u_lin = conv([_dot(h, wup_ref[:, D_FF + c0:D_FF + c0 + FF_CHUNK]) for h in hn_parts], D_FF + c0)
        for r, rs in enumerate(blocks):
            act_ref[rs, c0:c0 + FF_CHUNK] = (_silu(u_gate[r]) * u_lin[r]).astype(BF16)
    h2 = h1 + _dot(act_ref[...], wdown_ref[...])
    e = _rms(_dot(p_ref[...].astype(BF16), wple_ref[...]), gple_ref[...])
    h3 = h2 + jax.nn.sigmoid(_dot(h2.astype(BF16), wgate_ref[...])) * e
    y_ref[...] = _rms(h3, gfin_ref[...])


def _mix_ffn_seg_kernel(x_ref, attn_ref, yssm_ref, p_ref, fpre_ref, *rest):
    w = rest[:N_FFN_WEIGHTS]
    y_ref, uout_ref, act_ref = rest[N_FFN_WEIGHTS:]
    fcw_ref, fcb_ref = w[4], w[5]
    t_len = SUBLANE
    rp = x_ref.shape[0] // SEG_ROW_PARTS
    n_seq = rp // t_len
    rin = lax.broadcasted_iota(jnp.int32, (rp, FF_CHUNK), 0) & (t_len - 1)

    def conv(parts, c0):
        cs = slice(c0, c0 + FF_CHUNK)
        outs = []
        for r, u in enumerate(parts):
            seqs = slice(r * n_seq, (r + 1) * n_seq)
            pre = fpre_ref[seqs, :, cs]
            p0, p1 = (jnp.broadcast_to(pre[:, i:i + 1, :], (n_seq, t_len, FF_CHUNK)).reshape(rp, FF_CHUNK)
                      for i in range(FFN_CONV - 1))
            um1 = jnp.where(rin == 0, p1, pltpu.roll(u, 1, 0))
            um2 = jnp.where(rin == 0, p0, jnp.where(rin == 1, p1, pltpu.roll(u, 2, 0)))
            uout_ref[seqs, :, cs] = u.reshape(n_seq, t_len, FF_CHUNK)[:, t_len - (FFN_CONV - 1):, :]
            outs.append(fcb_ref[:, cs] + fcw_ref[0:1, cs] * um2 + fcw_ref[1:2, cs] * um1 + fcw_ref[2:3, cs] * u)
        return outs

    _ffn_core(x_ref, attn_ref[...], yssm_ref, p_ref, w, y_ref, act_ref, conv, row_parts=SEG_ROW_PARTS)


def _mix_ffn_kernel(x_ref, *rest, n_parts, n_tiles):
    parts = [rest[3 * b:3 * b + 3] for b in range(n_parts)]
    yssm_ref, p_ref, fpre_ref = rest[3 * n_parts:3 * n_parts + 3]
    rest = rest[3 * n_parts + 3:]
    w = rest[:N_FFN_WEIGHTS]
    y_ref, uout_ref, act_ref, tail_ref = rest[N_FFN_WEIGHTS:]
    fcw_ref, fcb_ref = w[4], w[5]
    tm = x_ref.shape[0]
    t = pl.program_id(1)

    @pl.when(t == 0)
    def _init():
        tail_ref[...] = jnp.zeros(tail_ref.shape, F32)
        tail_ref[SUBLANE - (FFN_CONV - 1):SUBLANE, :] = fpre_ref[...]

    attn = _merge_branches(parts, w[0])

    rin8 = lax.broadcasted_iota(jnp.int32, (SUBLANE, FF_CHUNK), 0)

    def conv(parts, c0):
        cs = slice(c0, c0 + FF_CHUNK)
        prev = tail_ref[:, cs]
        tail_ref[:, cs] = parts[-1][-SUBLANE:, :]
        outs = []
        for u in parts:
            r1, r2 = pltpu.roll(u, 1, 0), pltpu.roll(u, 2, 0)
            um1 = jnp.concatenate([jnp.where(rin8 < 1, pltpu.roll(prev, 1, 0), r1[0:SUBLANE]), r1[SUBLANE:]], axis=0)
            um2 = jnp.concatenate([jnp.where(rin8 < 2, pltpu.roll(prev, 2, 0), r2[0:SUBLANE]), r2[SUBLANE:]], axis=0)
            outs.append(fcb_ref[:, cs] + fcw_ref[0:1, cs] * um2 + fcw_ref[1:2, cs] * um1 + fcw_ref[2:3, cs] * u)
            prev = u[-SUBLANE:, :]
        return outs

    _ffn_core(x_ref, attn, yssm_ref, p_ref, w, y_ref, act_ref, conv, row_parts=FFN_ROW_PARTS)

    @pl.when(t == n_tiles - 1)
    def _fin():
        uout_ref[...] = tail_ref[...]


def _mix_ffn_seg(x, attn, yssm, p, fpre, weights):
    rows = x.shape[0]
    args = [x, attn, yssm, p, fpre, *weights]
    full = lambda a: pl.BlockSpec(a.shape, lambda i: (0,) * a.ndim)
    return pl.pallas_call(
        _mix_ffn_seg_kernel,
        grid=(1,),
        in_specs=[full(a) for a in args],
        out_specs=[pl.BlockSpec((rows, D_MODEL), lambda i: (0, 0)), pl.BlockSpec(fpre.shape, lambda i: (0, 0, 0))],
        out_shape=[jax.ShapeDtypeStruct((rows, D_MODEL), F32), jax.ShapeDtypeStruct(fpre.shape, F32)],
        scratch_shapes=[pltpu.VMEM((rows, D_FF), BF16)],
        compiler_params=_params("arbitrary"),
        name="mix_ffn_seg",
    )(*args)


def _mix_ffn(x, parts, yssm, p, fpre, weights, *, tm):
    bsz, s, _ = x.shape
    nt = s // tm
    tile = lambda b, t: (b, t, 0)
    seq = lambda b, t: (b, 0, 0)
    fixed = lambda b, t: (0, 0)
    in_specs = [pl.BlockSpec((None, tm, D_MODEL), tile)]
    args = [x]
    for part in parts:
        in_specs += [pl.BlockSpec((None, N_PAIRS, tm, LANE), lambda b, t: (b, 0, t, 0)),
                     pl.BlockSpec((None, tm, LANE), tile), pl.BlockSpec((None, tm, LANE), tile)]
        args += list(part)
    in_specs += [pl.BlockSpec((None, tm, D_SSM), tile), pl.BlockSpec((None, tm, D_PLE), tile),
                 pl.BlockSpec((None, FFN_CONV - 1, 2 * D_FF), seq)]
    args += [yssm, p, fpre]
    in_specs += [pl.BlockSpec(w.shape, fixed) for w in weights]
    args += list(weights)
    return pl.pallas_call(
        functools.partial(_mix_ffn_kernel, n_parts=len(parts), n_tiles=nt),
        grid=(bsz, nt),
        in_specs=in_specs,
        out_specs=[pl.BlockSpec((None, tm, D_MODEL), tile), pl.BlockSpec((None, SUBLANE, 2 * D_FF), seq)],
        out_shape=[jax.ShapeDtypeStruct((bsz, s, D_MODEL), F32), jax.ShapeDtypeStruct((bsz, SUBLANE, 2 * D_FF), F32)],
        scratch_shapes=[pltpu.VMEM((tm, D_FF), BF16), pltpu.VMEM((SUBLANE, 2 * D_FF), F32)],
        compiler_params=_params("parallel", "arbitrary"),
        name="mix_ffn",
    )(*args)


def _pad_lanes(v, width=LANE):
    return jnp.pad(v.astype(F32), (0, width - v.shape[0]))[None, :]


def kernel(x_prompt, x_sample, p_prompt, p_sample, cache_k, cache_v, state_ssm, state_conv, state_ffn_conv,
           rel_bias, g_mix, w_in, conv_w, conv_b, dt_bias, a_log, d_skip, g_ssm, w_out, g_ffn, w_up,
           ffn_conv_w, ffn_conv_b, w_down, w_ple_proj, g_ple, w_ple_gate, g_final):
    assert w_in.shape[0] == 1, "one layer"
    bp, s, _ = x_prompt.shape
    nsamp, t_len, _ = x_sample.shape
    n_keep = min(MAX_DISTANCE, s)

    w_main = _cast_w_main(w_in)
    w_dt = jnp.pad(w_in[0, :, O_DT:], ((0, 0), (0, LANE - (w_in.shape[2] - O_DT)))).astype(BF16)
    gmix = g_mix[0][None, :]
    ssd_params = (conv_w[0], conv_b[0][None, :], _pad_lanes(dt_bias[0]), _pad_lanes(a_log[0]),
                  jnp.repeat(d_skip[0], SSM_HEAD_DIM)[None, :], g_ssm[0][None, :])
    expand = (np.arange(2 * LANE)[:, None] % LANE == (np.arange(D_ATTN)[None, :] // HEAD_DIM)).astype(np.float32)
    ffn_weights = (jnp.asarray(expand, BF16), w_out[0].astype(BF16), g_ffn[0][None, :], w_up[0].astype(BF16),
                   ffn_conv_w[0], ffn_conv_b[0][None, :], w_down[0].astype(BF16), w_ple_proj[0].astype(BF16),
                   g_ple[0][None, :], w_ple_gate[0].astype(BF16), g_final[None, :])
    bias_p, b16, b4, b1, bn = _bias_tables(rel_bias, t_len)

    (q1, k1, v1, q4, k4, v4, q16, k16, v16, kt, vt, yssm_p, ssm_p, ctail_p) = _inproj_ssd(
        x_prompt, gmix, w_main, w_dt,
        jnp.zeros((bp, SSM_CONV - 1, CONV_DIM), F32),
        jnp.zeros((bp, N_SSM_HEADS, SSM_HEAD_DIM, D_STATE), F32),
        ssd_params, tm=512, n_keep=n_keep)
    parts = [_attn_prompt_branch(q1[:, None], k1[:, None], v1[:, None], bias_p[0], 1),
             _attn_prompt_branch(q4, k4, v4, bias_p[1], 4),
             _attn_prompt_branch(q16, k16, v16, bias_p[2], 16)]
    y_prompt, tail_p = _mix_ffn(x_prompt, parts, yssm_p, p_prompt[0],
                                jnp.zeros((bp, FFN_CONV - 1, 2 * D_FF), F32), ffn_weights, tm=512)
    k_prompt = jnp.transpose(kt, (0, 3, 1, 2))[None]
    v_prompt = jnp.transpose(vt, (0, 3, 1, 2))[None]
    conv_prompt = ctail_p[:, SUBLANE - (SSM_CONV - 1):][None]
    ffn_conv_prompt = tail_p[:, SUBLANE - (FFN_CONV - 1):][None]

    rows = nsamp * t_len
    qs, ks, vs, zs, xbcs, dts = _inproj(x_sample.reshape(rows, D_MODEL), gmix, w_main, w_dt)
    s3 = lambda a: a.reshape(nsamp, t_len, a.shape[-1])
    yssm_s, ssm_s = _ssd_short(s3(xbcs), s3(zs), s3(dts), state_conv[0], state_ssm[0], ssd_params)
    attn_s = _attn_sample(s3(qs), s3(ks), s3(vs),
                          jnp.transpose(cache_k[0], (0, 2, 3, 1)), jnp.transpose(cache_v[0], (0, 2, 3, 1)),
                          b16, b4, b1, bn)
    y_s, u_s = _mix_ffn_seg(x_sample.reshape(rows, D_MODEL), attn_s.reshape(rows, D_ATTN),
                            yssm_s.reshape(rows, D_SSM), p_sample[0].reshape(rows, D_PLE),
                            state_ffn_conv[0], ffn_weights)
    y_sample = y_s.reshape(nsamp, t_len, D_MODEL)
    k_sample = ks.reshape(1, nsamp, t_len, N_HEADS, HEAD_DIM)
    v_sample = vs.reshape(1, nsamp, t_len, N_HEADS, HEAD_DIM)
    conv_sample = s3(xbcs)[:, t_len - (SSM_CONV - 1):][None]
    ffn_conv_sample = u_s[None]

    return (y_prompt, y_sample, k_prompt, v_prompt, k_sample, v_sample,
            ssm_p[None], ssm_s[None], conv_prompt, conv_sample, ffn_conv_prompt, ffn_conv_sample)
```

```python
import functools
import math

import numpy as np
import jax
import jax.numpy as jnp
from jax import lax
from jax.experimental import pallas as pl
from jax.experimental.pallas import tpu as pltpu

F32 = jnp.float32
BF16 = jnp.bfloat16

D_MODEL = 1024
HEAD_DIM = 64
N_HEADS = 8
D_ATTN = N_HEADS * HEAD_DIM
N_PAIRS = N_HEADS // 2
DILATIONS = (1, 4, 16)
N_STEPS = 128
BLK = 128
N_BUCKETS = 32
MAX_DISTANCE = 2048
D_SSM = 512
N_SSM_HEADS = 8
SSM_HEAD_DIM = 64
D_STATE = 128
N_SSM_GROUPS = 2
SSM_CONV = 4
CONV_DIM = D_SSM + 2 * N_SSM_GROUPS * D_STATE
SSD_CHUNK = 128
D_FF = 2816
FFN_CONV = 3
D_PLE = 256
EPS = 1e-6
NEG = -1e30
LOG2E = math.log2(math.e)

LANE = 128
SUBLANE = 8
FF_CHUNK = 256
VMEM_LIMIT = 56 * 1024 * 1024

O_Q, O_K, O_V, O_Z, O_XBC, O_DT = 0, 512, 1024, 1536, 2048, 3072


def _rel_bucket_np(dist):
    dist = np.asarray(dist, np.int32)
    max_exact = N_BUCKETS // 2
    d = np.maximum(dist, 1).astype(np.float32)
    large = max_exact + (np.log(d / np.float32(max_exact)) / np.float32(math.log(MAX_DISTANCE / max_exact))
                         * np.float32(N_BUCKETS - max_exact)).astype(np.int32)
    large = np.minimum(large, N_BUCKETS - 1)
    return np.where(dist < max_exact, dist, large)


def _nt_dot(a, b):
    return lax.dot_general(a, b, (((1,), (1,)), ((), ())), preferred_element_type=F32)


def _dot(a, b):
    return jnp.dot(a, b, preferred_element_type=F32)


def _silu(x):
    return x * jax.nn.sigmoid(x)


def _softplus(x):
    return jnp.maximum(x, 0.0) + jnp.log1p(jnp.exp(-jnp.abs(x)))


def _rms(x, g):
    return x * lax.rsqrt(jnp.mean(x * x, axis=-1, keepdims=True) + EPS) * g


def _lane_lt64(shape):
    return lax.broadcasted_iota(jnp.int32, shape, len(shape) - 1) < HEAD_DIM


def _params(*sem):
    return pltpu.CompilerParams(dimension_semantics=sem, vmem_limit_bytes=VMEM_LIMIT)


def _bias_kernel(rb_ref, rbt_ref, ig_ref, i16_ref, i4_ref, i1_ref, in_ref, tp_ref, t16_ref, t4_ref, t1_ref, tn_ref):
    def lookup(idx, h):
        def body(b, acc):
            return jnp.where(idx == b, rb_ref[b, h], acc)
        return lax.fori_loop(0, N_BUCKETS, body, jnp.full(idx.shape, NEG, F32), unroll=True)

    for br in range(len(DILATIONS)):
        idx = jnp.broadcast_to(ig_ref[br], (N_HEADS, 2 * BLK))
        gen = jnp.full((N_HEADS, 2 * BLK), NEG, F32)
        for b in range(N_BUCKETS):
            gen = jnp.where(idx == b, jnp.broadcast_to(rbt_ref[:, b:b + 1], (N_HEADS, 2 * BLK)), gen)
        for h in range(N_HEADS):
            rows = jnp.broadcast_to(gen[h:h + 1, :], (BLK, 2 * BLK))
            tp_ref[br, h] = (pltpu.roll(rows, 0, 1, stride=1, stride_axis=0) * LOG2E).T
    for h in range(N_HEADS):
        jp, half = divmod(h, 2)
        rs = slice(half * SUBLANE, (half + 1) * SUBLANE)
        t16_ref[jp, rs, :] = lookup(i16_ref[...], h)
        t4_ref[jp, rs, :] = lookup(i4_ref[...], h)
        t1_ref[jp, rs, :] = lookup(i1_ref[...], h)
        for br in range(len(DILATIONS)):
            tn_ref[br, jp, rs, :] = lookup(in_ref[br], h)


def _bucket_maps(t_len):
    j = BLK - np.arange(2 * BLK)[None, :]
    prompt = np.stack([np.where(j >= 0, _rel_bucket_np(np.clip(j, 0, N_STEPS) * d), -1) for d in DILATIONS])

    t = np.arange(t_len)[:, None]

    def sample_map(diff, dil):
        ok = (diff >= 0) & (diff % dil == 0) & (diff // dil <= N_STEPS)
        return np.where(ok, _rel_bucket_np(np.maximum(diff, 0)), -1).astype(np.int32)

    w = np.arange(MAX_DISTANCE)[None, :]
    cache = {d: sample_map(MAX_DISTANCE + t - w, d) for d in DILATIONS}
    g = np.arange(LANE)[None, :]
    new = np.stack([np.where(g < t_len, sample_map(t - g, d), -1) for d in DILATIONS])
    return (prompt.astype(np.int32), cache[16], cache[4][:, MAX_DISTANCE - 4 * BLK:],
            cache[1][:, MAX_DISTANCE - BLK:], new.astype(np.int32))


def _bias_tables(rel_bias, t_len):
    maps = _bucket_maps(t_len)
    nb = len(DILATIONS)
    shapes = [(nb, N_HEADS, 2 * BLK, BLK), (N_PAIRS, 2 * t_len, MAX_DISTANCE), (N_PAIRS, 2 * t_len, 4 * BLK),
              (N_PAIRS, 2 * t_len, BLK), (nb, N_PAIRS, 2 * t_len, LANE)]
    return pl.pallas_call(
        _bias_kernel,
        in_specs=[pl.BlockSpec(memory_space=pltpu.SMEM)] + [pl.BlockSpec(memory_space=pltpu.VMEM)] * 6,
        out_specs=[pl.BlockSpec(memory_space=pltpu.VMEM)] * 5,
        out_shape=[jax.ShapeDtypeStruct(s, F32) for s in shapes],
        compiler_params=pltpu.CompilerParams(vmem_limit_bytes=VMEM_LIMIT),
        name="bias_tables",
    )(rel_bias, rel_bias.T, *[jnp.asarray(m) for m in maps])


W_CAST_COLS = 384


def _cast_kernel(wt_ref, o_ref):
    o_ref[...] = wt_ref[...].T.astype(BF16)


def _cast_w_main(w_in):
    return pl.pallas_call(
        _cast_kernel,
        grid=(O_DT // W_CAST_COLS,),
        in_specs=[pl.BlockSpec((W_CAST_COLS, D_MODEL), lambda i: (i, 0))],
        out_specs=pl.BlockSpec((D_MODEL, W_CAST_COLS), lambda i: (0, i)),
        out_shape=jax.ShapeDtypeStruct((D_MODEL, O_DT), BF16),
        compiler_params=_params("parallel"),
        name="cast_w_in",
    )(jnp.transpose(w_in[0]))


def _inproj_kernel(x_ref, g_ref, w_ref, wdt_ref, q_ref, k_ref, v_ref, z_ref, xbc_ref, dt_ref):
    xn = _rms(x_ref[...], g_ref[...]).astype(BF16)

    def proj(lo, hi):
        return _dot(xn, w_ref[:, lo:hi])

    q_ref[...] = proj(O_Q, O_K) * (HEAD_DIM ** -0.5)
    k_ref[...] = proj(O_K, O_V)
    v_ref[...] = proj(O_V, O_Z)
    z_ref[...] = proj(O_Z, O_XBC)
    xbc_ref[...] = proj(O_XBC, O_DT)
    dt_ref[...] = _dot(xn, wdt_ref[...])


def _inproj(x, g_mix, w_main, w_dt):
    rows = x.shape[0]
    full = lambda a: pl.BlockSpec(a.shape, lambda i: (0, 0))
    widths = (D_ATTN, D_ATTN, D_ATTN, D_SSM, CONV_DIM, LANE)
    return pl.pallas_call(
        _inproj_kernel,
        grid=(1,),
        in_specs=[full(x), full(g_mix), full(w_main), full(w_dt)],
        out_specs=[pl.BlockSpec((rows, w), lambda i: (0, 0)) for w in widths],
        out_shape=[jax.ShapeDtypeStruct((rows, w), F32) for w in widths],
        compiler_params=_params("arbitrary"),
        name="inproj_sample",
    )(x, g_mix, w_main, w_dt)


def _inproj_ssd_kernel(x_ref, g_ref, w_ref, wdt_ref, prefix_ref, state0_ref, cw_ref, cb_ref, dtb_ref, alog_ref, dskip_ref,
                       gssm_ref, q1, k1, v1, q4, k4, v4, q16, k16, v16, kt_ref, vt_ref, y_ref, state_ref, ctail_ref,
                       perm_ref, mid_ref, xp_ref, z_scr, dt_scr, st_ref, *, keep_from, n_tiles):
    tm = x_ref.shape[0]
    L = SSD_CHUNK
    t = pl.program_id(1)

    @pl.when(t == 0)
    def _init():
        xp_ref[0:SUBLANE, :] = jnp.zeros((SUBLANE, CONV_DIM), F32)
        xp_ref[SUBLANE - (SSM_CONV - 1):SUBLANE, :] = prefix_ref[...]
        _load_state(state0_ref, st_ref)

    xn = _rms(x_ref[...], g_ref[...]).astype(BF16)

    def proj(lo, hi):
        return _dot(xn, w_ref[:, lo:hi])

    xp_ref[SUBLANE:SUBLANE + tm, :] = proj(O_XBC, O_DT)
    dt_scr[...] = _dot(xn, wdt_ref[...])
    z_scr[...] = proj(O_Z, O_XBC)

    def project_attn(i):
        lo, scale, nat, r4, r16 = ((O_Q, HEAD_DIM ** -0.5 * LOG2E, q1, q4, q16), (O_K, None, k1, k4, k16),
                                   (O_V, None, v1, v4, v16))[i]
        val = proj(lo, lo + D_ATTN)
        if scale is not None:
            val = val * scale
        nat[...] = val.astype(BF16)
        for jp in range(N_PAIRS):
            perm_ref[i, jp] = val[:, jp * LANE:(jp + 1) * LANE]
        for ra in range(4):
            for jp in range(N_PAIRS):
                mid_ref[i, ra, jp] = perm_ref[i, jp, pl.ds(ra, tm // 4, stride=4), :]
            r4[ra] = jnp.concatenate([mid_ref[i, ra, jp] for jp in range(N_PAIRS)], axis=1).astype(BF16)
            for rb in range(4):
                rows = [mid_ref[i, ra, jp, pl.ds(rb, tm // 16, stride=4), :] for jp in range(N_PAIRS)]
                r16[ra + 4 * rb] = jnp.concatenate(rows, axis=1).astype(BF16)

    prm = (cw_ref, cb_ref, dtb_ref, alog_ref, dskip_ref, gssm_ref)
    n_chunks = tm // L
    assert n_chunks >= 3
    for c in range(n_chunks):
        base = SUBLANE + c * L - (SSM_CONV - 1)
        y = _ssd_chunk(lambda k, base=base: xp_ref[base + k:base + k + L, :],
                       z_scr[c * L:(c + 1) * L, :], dt_scr[c * L:(c + 1) * L, :], prm, st_ref, L)
        y_ref[c * L:(c + 1) * L, :] = y.astype(y_ref.dtype)
        if c < 3:
            project_attn(c)
    xp_ref[0:SUBLANE, :] = xp_ref[tm:tm + SUBLANE, :]

    @pl.when(t == n_tiles - 1)
    def _fin():
        _store_state(st_ref, state_ref)
        ctail_ref[...] = xp_ref[0:SUBLANE, :]

    @pl.when(t >= keep_from)
    def _keep():
        for i, out_ref in ((1, kt_ref), (2, vt_ref)):
            for jp in range(N_PAIRS):
                out_ref[2 * jp:2 * jp + 2] = perm_ref[i, jp].T.reshape(2, HEAD_DIM, tm)


def _inproj_ssd(x, g_mix, w_main, w_dt, prefix, state0, ssd_params, *, tm, n_keep):
    ns, l, _ = x.shape
    nt = l // tm
    keep_from = (l - n_keep) // tm
    tile = lambda s, t: (s, t, 0)
    seq = lambda s, t: (s, 0, 0)
    fixed = lambda s, t: (0, 0)
    state_spec = pl.BlockSpec((None, N_SSM_HEADS, SSM_HEAD_DIM, D_STATE), lambda s, t: (s, 0, 0, 0))
    nat = lambda w: (pl.BlockSpec((None, tm, w), tile), jax.ShapeDtypeStruct((ns, l, w), BF16))
    outs = [nat(D_ATTN)] * 3
    for dil in (4, 16):
        outs += [(pl.BlockSpec((None, dil, tm // dil, D_ATTN), lambda s, t: (s, 0, t, 0)),
                  jax.ShapeDtypeStruct((ns, dil, l // dil, D_ATTN), BF16))] * 3
    outs += [(pl.BlockSpec((None, N_HEADS, HEAD_DIM, tm), lambda s, t: (s, 0, 0, jnp.maximum(t - keep_from, 0))),
              jax.ShapeDtypeStruct((ns, N_HEADS, HEAD_DIM, n_keep), F32))] * 2
    outs += [nat(D_SSM),
             (state_spec, jax.ShapeDtypeStruct((ns, N_SSM_HEADS, SSM_HEAD_DIM, D_STATE), F32)),
             (pl.BlockSpec((None, SUBLANE, CONV_DIM), seq), jax.ShapeDtypeStruct((ns, SUBLANE, CONV_DIM), F32))]
    return pl.pallas_call(
        functools.partial(_inproj_ssd_kernel, keep_from=keep_from, n_tiles=nt),
        grid=(ns, nt),
        in_specs=[pl.BlockSpec((None, tm, D_MODEL), tile),
                  pl.BlockSpec((1, D_MODEL), fixed),
                  pl.BlockSpec((D_MODEL, O_DT), fixed),
                  pl.BlockSpec((D_MODEL, LANE), fixed),
                  pl.BlockSpec((None, SSM_CONV - 1, CONV_DIM), seq),
                  state_spec] + [pl.BlockSpec(p.shape, fixed) for p in ssd_params],
        out_specs=[o[0] for o in outs],
        out_shape=[o[1] for o in outs],
        scratch_shapes=[pltpu.VMEM((3, N_PAIRS, tm, LANE), F32),
                        pltpu.VMEM((3, 4, N_PAIRS, tm // 4, LANE), F32),
                        pltpu.VMEM((SUBLANE + tm, CONV_DIM), F32),
                        pltpu.VMEM((tm, D_SSM), F32),
                        pltpu.VMEM((tm, LANE), F32),
                        pltpu.VMEM((N_PAIRS, D_STATE, LANE), F32)],
        compiler_params=_params("parallel", "arbitrary"),
        name="inproj_ssd",
    )(x, g_mix, w_main, w_dt, prefix, state0, *ssd_params)


def _load_state(state0_ref, st_ref):
    for jp in range(N_PAIRS):
        st_ref[jp] = state0_ref[2 * jp:2 * jp + 2].reshape(2 * SSM_HEAD_DIM, D_STATE).T


def _store_state(st_ref, state_ref):
    for jp in range(N_PAIRS):
        state_ref[2 * jp:2 * jp + 2] = st_ref[jp].T.reshape(2, SSM_HEAD_DIM, D_STATE)


def _ssd_chunk(xwin, z, dt_raw, prm, st_ref, valid_len):
    cw_ref, cb_ref, dtb_ref, alog_ref, dskip_ref, gssm_ref = prm
    L = SSD_CHUNK
    conv = cb_ref[...]
    for k in range(SSM_CONV):
        conv = conv + cw_ref[k:k + 1, :] * xwin(k)
    xc = _silu(conv)
    xs = xc[:, :D_SSM]
    gn = N_SSM_GROUPS * D_STATE

    row = lax.broadcasted_iota(jnp.int32, (L, L), 0)
    col = lax.broadcasted_iota(jnp.int32, (L, L), 1)
    tri = row >= col
    lt64 = col < SSM_HEAD_DIM

    dt = _softplus(dt_raw + dtb_ref[...])
    if valid_len < L:
        dt = jnp.where(row < valid_len, dt, 0.0)
    a = dt * (-jnp.exp(alog_ref[...]))
    acum = a
    shift = 1
    while shift < L:
        acum = acum + jnp.where(row >= shift, pltpu.roll(acum, shift, 0), 0.0)
        shift *= 2
    acum_t = acum.T
    dt_t = dt.T
    e_slab = jnp.exp(acum)
    de_slab = jnp.exp(acum[L - 1:L, :] - acum) * dt

    def head_terms(h, cb):
        colb = jnp.broadcast_to(acum[:, h:h + 1], (L, L))
        rowb = jnp.broadcast_to(acum_t[h:h + 1, :], (L, L))
        decay = jnp.exp(jnp.where(tri, colb - rowb, NEG))
        m = (cb * decay * jnp.broadcast_to(dt_t[h:h + 1, :], (L, L))).astype(BF16)
        e = jnp.broadcast_to(e_slab[:, h:h + 1], (L, L))
        return m, e, jnp.broadcast_to(de_slab[:, h:h + 1], (L, L)), e[L - 1:L, :]

    pairs = []
    for g in range(N_SSM_GROUPS):
        bg = xc[:, D_SSM + g * D_STATE:D_SSM + (g + 1) * D_STATE]
        cg = xc[:, D_SSM + gn + g * D_STATE:D_SSM + gn + (g + 1) * D_STATE].astype(BF16)
        cb = _nt_dot(cg, bg.astype(BF16))
        bg_t = bg.T.astype(BF16)
        for i in range(N_PAIRS // N_SSM_GROUPS):
            jp = g * (N_PAIRS // N_SSM_GROUPS) + i
            xpair = xs[:, jp * LANE:(jp + 1) * LANE]
            xpair_b = xpair.astype(BF16)
            m_a, e_a, de_a, el_a = head_terms(2 * jp, cb)
            m_b, e_b, de_b, el_b = head_terms(2 * jp + 1, cb)
            st = st_ref[jp]
            y_diag = jnp.where(lt64, _dot(m_a, xpair_b), _dot(m_b, xpair_b))
            y_off = _dot(cg, st.astype(BF16)) * jnp.where(lt64, e_a, e_b)
            xd = (xpair * jnp.where(lt64, de_a, de_b)).astype(BF16)
            st_ref[jp] = st * jnp.where(lt64[0:1, :], el_a, el_b) + _dot(bg_t, xd)
            pairs.append(y_diag + y_off + dskip_ref[:, jp * LANE:(jp + 1) * LANE] * xpair)
    y = jnp.concatenate(pairs, axis=1)

    yf = y * _silu(z)
    gw = D_SSM // N_SSM_GROUPS
    normed = []
    for g in range(N_SSM_GROUPS):
        seg = yf[:, g * gw:(g + 1) * gw]
        normed.append(seg * lax.rsqrt(jnp.mean(seg * seg, axis=-1, keepdims=True) + EPS))
    return jnp.concatenate(normed, axis=1) * gssm_ref[...]


def _ssd_short_kernel(xbc_ref, z_ref, dt_ref, prefix_ref, state0_ref, cw_ref, cb_ref, dtb_ref, alog_ref,
                      dskip_ref, gssm_ref, y_ref, state_ref, xp_ref, st_ref):
    n_seq, t_len, _ = xbc_ref.shape
    L = SSD_CHUNK
    lead = SUBLANE - (SSM_CONV - 1)
    pad = lambda a: jnp.concatenate([a, jnp.zeros((L - t_len, a.shape[1]), F32)], axis=0)
    for i in range(n_seq):
        xp_ref[0:lead, :] = jnp.zeros((lead, CONV_DIM), F32)
        xp_ref[lead:SUBLANE, :] = prefix_ref[i]
        xp_ref[SUBLANE:SUBLANE + t_len, :] = xbc_ref[i]
        xp_ref[SUBLANE + t_len:SUBLANE + L, :] = jnp.zeros((L - t_len, CONV_DIM), F32)
        _load_state(state0_ref.at[i], st_ref)
        y = _ssd_chunk(lambda k: xp_ref[lead + k:lead + k + L, :], pad(z_ref[i]), pad(dt_ref[i]),
                       (cw_ref, cb_ref, dtb_ref, alog_ref, dskip_ref, gssm_ref), st_ref, t_len)
        y_ref[i] = y[0:t_len].astype(y_ref.dtype)
        _store_state(st_ref, state_ref.at[i])


def _ssd_short(xbc, z, dt, prefix, state0, ssd_params):
    ns, t_len, _ = xbc.shape
    g = SAMPLE_SEQS
    seq = lambda s: (s, 0, 0)
    state_spec = pl.BlockSpec((g, N_SSM_HEADS, SSM_HEAD_DIM, D_STATE), lambda s: (s, 0, 0, 0))
    return pl.pallas_call(
        _ssd_short_kernel,
        grid=(ns // g,),
        in_specs=[pl.BlockSpec((g, t_len, CONV_DIM), seq),
                  pl.BlockSpec((g, t_len, D_SSM), seq),
                  pl.BlockSpec((g, t_len, LANE), seq),
                  pl.BlockSpec((g, SSM_CONV - 1, CONV_DIM), seq),
                  state_spec] + [pl.BlockSpec(p.shape, lambda s: (0, 0)) for p in ssd_params],
        out_specs=[pl.BlockSpec((g, t_len, D_SSM), seq), state_spec],
        out_shape=[jax.ShapeDtypeStruct((ns, t_len, D_SSM), BF16),
                   jax.ShapeDtypeStruct((ns, N_SSM_HEADS, SSM_HEAD_DIM, D_STATE), F32)],
        scratch_shapes=[pltpu.VMEM((SUBLANE + SSD_CHUNK, CONV_DIM), F32),
                        pltpu.VMEM((N_PAIRS, D_STATE, LANE), F32)],
        compiler_params=_params("parallel"),
        name="ssd_short",
    )(xbc, z, dt, prefix, state0, *ssd_params)


ATTN_UNITS = 16


def _attn_stages(q_ref, kp_ref, kc_ref, vp_ref, vc_ref, bias_ref, o_ref, max_ref, den_ref, s_ref, p_ref, *,
                 n_res, blk0, rows_of):
    lt64 = _lane_lt64((BLK, LANE))
    head_row = lax.broadcasted_iota(jnp.int32, (N_HEADS, BLK), 0)
    zero = jnp.zeros((BLK, LANE), BF16)

    def cur(ref, g):
        j, r = divmod(g, n_res)
        return ref.at[r, pl.ds(j * BLK, BLK)]

    def prev_of(g, p_ref_, c_ref_):
        j, r = divmod(g, n_res)
        if j == 0:
            return p_ref_.at[r], blk0
        return c_ref_.at[r, pl.ds((j - 1) * BLK, BLK)], None

    def scores(g, slot):
        kprev, masked = prev_of(g, kp_ref, kc_ref)
        for jp in range(N_PAIRS):
            sl = slice(jp * LANE, (jp + 1) * LANE)
            qp = cur(q_ref, g)[:, sl]
            kcat = jnp.concatenate([kprev[:, sl], cur(kc_ref, g)[:, sl]], axis=0)
            for half in range(2):
                h = 2 * jp + half
                qm = jnp.where(lt64 if half == 0 else ~lt64, qp, zero)
                bias = bias_ref[h]
                if masked is not None:
                    bias = jnp.concatenate([jnp.where(masked, NEG, bias[0:BLK]), bias[BLK:]], axis=0)
                s_ref[slot, h] = _nt_dot(kcat, qm) + bias

    def softmax(g, slot):
        s = s_ref[slot]
        m = jnp.max(s, axis=1, keepdims=True)
        p = jnp.exp2(s - m)
        den = jnp.sum(p, axis=1, keepdims=True)
        p_ref[slot] = p.astype(BF16)
        m_rows = jnp.zeros((N_HEADS, BLK), F32)
        den_rows = jnp.zeros((N_HEADS, BLK), F32)
        for h in range(N_HEADS):
            m_rows = jnp.where(head_row == h, jnp.broadcast_to(m[h], (N_HEADS, BLK)), m_rows)
            den_rows = jnp.where(head_row == h, jnp.broadcast_to(den[h], (N_HEADS, BLK)), den_rows)
        m_nat = m_rows * (1.0 / LOG2E)
        max_ref[rows_of(g), :] = jnp.concatenate([m_nat, jnp.zeros((BLK - N_HEADS, BLK), F32)], axis=0).T
        den_ref[rows_of(g), :] = jnp.concatenate([den_rows, jnp.ones((BLK - N_HEADS, BLK), F32)], axis=0).T

    def values(g, slot):
        vprev, _ = prev_of(g, vp_ref, vc_ref)
        for jp in range(N_PAIRS):
            sl = slice(jp * LANE, (jp + 1) * LANE)
            vcat = jnp.concatenate([vprev[:, sl], cur(vc_ref, g)[:, sl]], axis=0)
            pv = [lax.dot_general(p_ref[slot, 2 * jp + half], vcat, (((0,), (0,)), ((), ())),
                                  preferred_element_type=F32) for half in range(2)]
            o_ref[jp, rows_of(g), :] = jnp.where(lt64, pv[0], pv[1])

    return scores, softmax, values


def _attn_prompt_kernel(q_ref, kp_ref, kc_ref, vp_ref, vc_ref, bias_ref, o_ref, max_ref, den_ref, s_ref, p_ref, *, dil):
    n_res, rows, _ = q_ref.shape
    units = n_res * (rows // BLK)

    def rows_of(g):
        j, r = divmod(g, n_res)
        if dil == 1:
            return slice(j * BLK, (j + 1) * BLK)
        return pl.ds(pl.program_id(2) * n_res + r + j * BLK * dil, BLK, stride=dil)

    scores, softmax, values = _attn_stages(q_ref, kp_ref, kc_ref, vp_ref, vc_ref, bias_ref, o_ref, max_ref, den_ref,
                                           s_ref, p_ref, n_res=n_res, blk0=pl.program_id(1) == 0, rows_of=rows_of)
    scores(0, 0)
    for g in range(units):
        softmax(g, g % 2)
        if g + 1 < units:
            scores(g + 1, (g + 1) % 2)
        values(g, g % 2)


def _attn_prompt_branch(q, k, v, bias, dil):
    bsz, _, l, _ = q.shape
    s = l * dil
    n_res = min(ATTN_UNITS, dil)
    n_blk = ATTN_UNITS // n_res
    cur = pl.BlockSpec((None, n_res, n_blk * BLK, D_ATTN), lambda b, n, r: (b, r, n, 0))
    prev = pl.BlockSpec((None, n_res, BLK, D_ATTN), lambda b, n, r: (b, r, jnp.maximum(n * n_blk - 1, 0), 0))
    rows = n_blk * BLK * dil
    return pl.pallas_call(
        functools.partial(_attn_prompt_kernel, dil=dil),
        grid=(bsz, l // (n_blk * BLK), dil // n_res),
        in_specs=[cur, prev, cur, prev, cur,
                  pl.BlockSpec((N_HEADS, 2 * BLK, BLK), lambda b, n, r: (0, 0, 0))],
        out_specs=[pl.BlockSpec((None, N_PAIRS, rows, LANE), lambda b, n, r: (b, 0, n, 0)),
                   pl.BlockSpec((None, rows, LANE), lambda b, n, r: (b, n, 0)),
                   pl.BlockSpec((None, rows, LANE), lambda b, n, r: (b, n, 0))],
        out_shape=[jax.ShapeDtypeStruct((bsz, N_PAIRS, s, LANE), F32),
                   jax.ShapeDtypeStruct((bsz, s, LANE), F32),
                   jax.ShapeDtypeStruct((bsz, s, LANE), F32)],
        scratch_shapes=[pltpu.VMEM((2, N_HEADS, 2 * BLK, BLK), F32), pltpu.VMEM((2, N_HEADS, 2 * BLK, BLK), BF16)],
        compiler_params=_params("parallel", "parallel", "arbitrary"),
        name=f"attn_prompt_d{dil}",
    )(q, k, k, v, v, bias)


SAMPLE_SEQS = 2


def _attn_sample_kernel(q_ref, kn_ref, vn_ref, kc_ref, vc_ref, b16_ref, b4_ref, b1_ref, bn_ref, o_ref):
    for i in range(q_ref.shape[0]):
        _attn_sample_one(q_ref.at[i], kn_ref.at[i], vn_ref.at[i], kc_ref.at[i], vc_ref.at[i],
                         b16_ref, b4_ref, b1_ref, bn_ref, o_ref.at[i])


def _attn_sample_one(q_ref, kn_ref, vn_ref, kc_ref, vc_ref, b16_ref, b4_ref, b1_ref, bn_ref, o_ref):
    t_len = q_ref.shape[0]
    w = kc_ref.shape[-1]
    lt64 = _lane_lt64((t_len, LANE))
    pad = jnp.zeros((BLK - t_len, D_ATTN), F32)
    kn = jnp.concatenate([kn_ref[...], pad], axis=0).astype(BF16)
    vn = jnp.concatenate([vn_ref[...], pad], axis=0).astype(BF16)
    outs = []
    for jp in range(N_PAIRS):
        sl = slice(jp * LANE, (jp + 1) * LANE)
        qp = q_ref[:, sl]
        qm = jnp.concatenate([jnp.where(lt64, qp, 0.0), jnp.where(lt64, 0.0, qp)], axis=0).astype(BF16)
        kt = kc_ref[2 * jp:2 * jp + 2].reshape(2 * HEAD_DIM, w).astype(BF16)
        vt = vc_ref[2 * jp:2 * jp + 2].reshape(2 * HEAD_DIM, w).astype(BF16)
        s = _dot(qm, kt)
        s_new = _nt_dot(qm, kn[:, sl])
        w4, w1 = w - 4 * BLK, w - BLK
        tiles = [s + b16_ref[jp], s[:, w4:] + b4_ref[jp], s[:, w1:] + b1_ref[jp]]
        tiles += [s_new + bn_ref[br, jp] for br in range(len(DILATIONS))]
        m = functools.reduce(jnp.maximum, [jnp.max(t, axis=-1, keepdims=True) for t in tiles])
        ps = [jnp.exp(t - m) for t in tiles]
        den = functools.reduce(lambda a, b: a + b, [jnp.sum(p, axis=-1, keepdims=True) for p in ps])
        p16, p4, p1 = ps[0], ps[1], ps[2]
        p_cache = jnp.concatenate([p16[:, :w4], p16[:, w4:w1] + p4[:, :w1 - w4],
                                   p16[:, w1:] + p4[:, w1 - w4:] + p1], axis=1)
        p_new = ps[3] + ps[4] + ps[5]
        o2 = (_nt_dot(p_cache.astype(BF16), vt) + _dot(p_new.astype(BF16), vn[:, sl])) / den
        outs.append(jnp.where(lt64, o2[0:t_len], o2[t_len:2 * t_len]))
    o_ref[...] = jnp.concatenate(outs, axis=1).astype(o_ref.dtype)


def _attn_sample(q, kn, vn, kc, vc, b16, b4, b1, bn):
    ns, t_len, _ = q.shape
    w = kc.shape[-1]
    assert w == MAX_DISTANCE and t_len == SUBLANE
    tok = pl.BlockSpec((SAMPLE_SEQS, t_len, D_ATTN), lambda s: (s, 0, 0))
    cache = pl.BlockSpec((SAMPLE_SEQS, N_HEADS, HEAD_DIM, w), lambda s: (s, 0, 0, 0))
    full = lambda a: pl.BlockSpec(a.shape, lambda s: (0,) * a.ndim)
    return pl.pallas_call(
        _attn_sample_kernel,
        grid=(ns // SAMPLE_SEQS,),
        in_specs=[tok, tok, tok, cache, cache, full(b16), full(b4), full(b1), full(bn)],
        out_specs=tok,
        out_shape=jax.ShapeDtypeStruct((ns, t_len, D_ATTN), BF16),
        compiler_params=_params("parallel"),
        name="attn_sample",
    )(q, kn, vn, kc, vc, b16, b4, b1, bn)


SEG_ROW_PARTS = 2
FFN_ROW_PARTS = 4
N_FFN_WEIGHTS = 11


def _merge_branches(parts, expand_ref):
    maxes = [max_ref[...] for _, max_ref, _ in parts]
    mx = functools.reduce(jnp.maximum, maxes)
    ws = [jnp.exp(m - mx) for m in maxes]
    den = functools.reduce(lambda a, b: a + b, [w * den_ref[...] for w, (_, _, den_ref) in zip(ws, parts)])
    attn = None
    for (o_ref, _, _), w in zip(parts, ws):
        wn = w / den
        hi = wn.astype(BF16)
        lo = (wn - hi.astype(F32)).astype(BF16)
        wexp = _dot(jnp.concatenate([hi, lo], axis=1), expand_ref[...])
        o = jnp.concatenate([o_ref[jp] for jp in range(N_PAIRS)], axis=1)
        attn = wexp * o if attn is None else attn + wexp * o
    return attn.astype(BF16)


def _ffn_core(x_ref, attn, yssm_ref, p_ref, w, y_ref, act_ref, conv, row_parts=1):
    (_, wout_ref, gffn_ref, wup_ref, _, _, wdown_ref, wple_ref, gple_ref, wgate_ref, gfin_ref) = w
    h1 = x_ref[...] + _dot(attn, wout_ref[0:D_ATTN, :]) + _dot(yssm_ref[...], wout_ref[D_ATTN:, :])
    hn = _rms(h1, gffn_ref[...]).astype(BF16)
    rp = x_ref.shape[0] // row_parts
    blocks = [slice(r * rp, (r + 1) * rp) for r in range(row_parts)]
    hn_parts = [hn[rs] for rs in blocks]
    for j in range(D_FF // FF_CHUNK):
        c0 = j * FF_CHUNK
        u_gate = conv([_dot(h, wup_ref[:, c0:c0 + FF_CHUNK]) for h in hn_parts], c0)
        u_lin = conv([_dot(h, wup_ref[:, D_FF + c0:D_FF + c0 + FF_CHUNK]) for h in hn_parts], D_FF + c0)
        for r, rs in enumerate(blocks):
            act_ref[rs, c0:c0 + FF_CHUNK] = (_silu(u_gate[r]) * u_lin[r]).astype(BF16)
    h2 = h1 + _dot(act_ref[...], wdown_ref[...])
    e = _rms(_dot(p_ref[...].astype(BF16), wple_ref[...]), gple_ref[...])
    h3 = h2 + jax.nn.sigmoid(_dot(h2.astype(BF16), wgate_ref[...])) * e
    y_ref[...] = _rms(h3, gfin_ref[...])


def _mix_ffn_seg_kernel(x_ref, attn_ref, yssm_ref, p_ref, fpre_ref, *rest):
    w = rest[:N_FFN_WEIGHTS]
    y_ref, uout_ref, act_ref = rest[N_FFN_WEIGHTS:]
    fcw_ref, fcb_ref = w[4], w[5]
    t_len = SUBLANE
    rp = x_ref.shape[0] // SEG_ROW_PARTS
    n_seq = rp // t_len
    rin = lax.broadcasted_iota(jnp.int32, (rp, FF_CHUNK), 0) & (t_len - 1)

    def conv(parts, c0):
        cs = slice(c0, c0 + FF_CHUNK)
        outs = []
        for r, u in enumerate(parts):
            seqs = slice(r * n_seq, (r + 1) * n_seq)
            pre = fpre_ref[seqs, :, cs]
            p0, p1 = (jnp.broadcast_to(pre[:, i:i + 1, :], (n_seq, t_len, FF_CHUNK)).reshape(rp, FF_CHUNK)
                      for i in range(FFN_CONV - 1))
            um1 = jnp.where(rin == 0, p1, pltpu.roll(u, 1, 0))
            um2 = jnp.where(rin == 0, p0, jnp.where(rin == 1, p1, pltpu.roll(u, 2, 0)))
            uout_ref[seqs, :, cs] = u.reshape(n_seq, t_len, FF_CHUNK)[:, t_len - (FFN_CONV - 1):, :]
            outs.append(fcb_ref[:, cs] + fcw_ref[0:1, cs] * um2 + fcw_ref[1:2, cs] * um1 + fcw_ref[2:3, cs] * u)
        return outs

    _ffn_core(x_ref, attn_ref[...], yssm_ref, p_ref, w, y_ref, act_ref, conv, row_parts=SEG_ROW_PARTS)


def _mix_ffn_kernel(x_ref, *rest, n_parts, n_tiles):
    parts = [rest[3 * b:3 * b + 3] for b in range(n_parts)]
    yssm_ref, p_ref, fpre_ref = rest[3 * n_parts:3 * n_parts + 3]
    rest = rest[3 * n_parts + 3:]
    w = rest[:N_FFN_WEIGHTS]
    y_ref, uout_ref, act_ref, tail_ref = rest[N_FFN_WEIGHTS:]
    fcw_ref, fcb_ref = w[4], w[5]
    tm = x_ref.shape[0]
    t = pl.program_id(1)

    @pl.when(t == 0)
    def _init():
        tail_ref[...] = jnp.zeros(tail_ref.shape, F32)
        tail_ref[SUBLANE - (FFN_CONV - 1):SUBLANE, :] = fpre_ref[...]

    attn = _merge_branches(parts, w[0])

    rin8 = lax.broadcasted_iota(jnp.int32, (SUBLANE, FF_CHUNK), 0)

    def conv(parts, c0):
        cs = slice(c0, c0 + FF_CHUNK)
        prev = tail_ref[:, cs]
        tail_ref[:, cs] = parts[-1][-SUBLANE:, :]
        outs = []
        for u in parts:
            r1, r2 = pltpu.roll(u, 1, 0), pltpu.roll(u, 2, 0)
            um1 = jnp.concatenate([jnp.where(rin8 < 1, pltpu.roll(prev, 1, 0), r1[0:SUBLANE]), r1[SUBLANE:]], axis=0)
            um2 = jnp.concatenate([jnp.where(rin8 < 2, pltpu.roll(prev, 2, 0), r2[0:SUBLANE]), r2[SUBLANE:]], axis=0)
            outs.append(fcb_ref[:, cs] + fcw_ref[0:1, cs] * um2 + fcw_ref[1:2, cs] * um1 + fcw_ref[2:3, cs] * u)
            prev = u[-SUBLANE:, :]
        return outs

    _ffn_core(x_ref, attn, yssm_ref, p_ref, w, y_ref, act_ref, conv, row_parts=FFN_ROW_PARTS)

    @pl.when(t == n_tiles - 1)
    def _fin():
        uout_ref[...] = tail_ref[...]


def _mix_ffn_seg(x, attn, yssm, p, fpre, weights):
    rows = x.shape[0]
    args = [x, attn, yssm, p, fpre, *weights]
    full = lambda a: pl.BlockSpec(a.shape, lambda i: (0,) * a.ndim)
    return pl.pallas_call(
        _mix_ffn_seg_kernel,
        grid=(1,),
        in_specs=[full(a) for a in args],
        out_specs=[pl.BlockSpec((rows, D_MODEL), lambda i: (0, 0)), pl.BlockSpec(fpre.shape, lambda i: (0, 0, 0))],
        out_shape=[jax.ShapeDtypeStruct((rows, D_MODEL), F32), jax.ShapeDtypeStruct(fpre.shape, F32)],
        scratch_shapes=[pltpu.VMEM((rows, D_FF), BF16)],
        compiler_params=_params("arbitrary"),
        name="mix_ffn_seg",
    )(*args)


def _mix_ffn(x, parts, yssm, p, fpre, weights, *, tm):
    bsz, s, _ = x.shape
    nt = s // tm
    tile = lambda b, t: (b, t, 0)
    seq = lambda b, t: (b, 0, 0)
    fixed = lambda b, t: (0, 0)
    in_specs = [pl.BlockSpec((None, tm, D_MODEL), tile)]
    args = [x]
    for part in parts:
        in_specs += [pl.BlockSpec((None, N_PAIRS, tm, LANE), lambda b, t: (b, 0, t, 0)),
                     pl.BlockSpec((None, tm, LANE), tile), pl.BlockSpec((None, tm, LANE), tile)]
        args += list(part)
    in_specs += [pl.BlockSpec((None, tm, D_SSM), tile), pl.BlockSpec((None, tm, D_PLE), tile),
                 pl.BlockSpec((None, FFN_CONV - 1, 2 * D_FF), seq)]
    args += [yssm, p, fpre]
    in_specs += [pl.BlockSpec(w.shape, fixed) for w in weights]
    args += list(weights)
    return pl.pallas_call(
        functools.partial(_mix_ffn_kernel, n_parts=len(parts), n_tiles=nt),
        grid=(bsz, nt),
        in_specs=in_specs,
        out_specs=[pl.BlockSpec((None, tm, D_MODEL), tile), pl.BlockSpec((None, SUBLANE, 2 * D_FF), seq)],
        out_shape=[jax.ShapeDtypeStruct((bsz, s, D_MODEL), F32), jax.ShapeDtypeStruct((bsz, SUBLANE, 2 * D_FF), F32)],
        scratch_shapes=[pltpu.VMEM((tm, D_FF), BF16), pltpu.VMEM((SUBLANE, 2 * D_FF), F32)],
        compiler_params=_params("parallel", "arbitrary"),
        name="mix_ffn",
    )(*args)


def _pad_lanes(v, width=LANE):
    return jnp.pad(v.astype(F32), (0, width - v.shape[0]))[None, :]


def kernel(x_prompt, x_sample, p_prompt, p_sample, cache_k, cache_v, state_ssm, state_conv, state_ffn_conv,
           rel_bias, g_mix, w_in, conv_w, conv_b, dt_bias, a_log, d_skip, g_ssm, w_out, g_ffn, w_up,
           ffn_conv_w, ffn_conv_b, w_down, w_ple_proj, g_ple, w_ple_gate, g_final):
    assert w_in.shape[0] == 1, "one layer"
    bp, s, _ = x_prompt.shape
    nsamp, t_len, _ = x_sample.shape
    n_keep = min(MAX_DISTANCE, s)

    w_main = _cast_w_main(w_in)
    w_dt = jnp.pad(w_in[0, :, O_DT:], ((0, 0), (0, LANE - (w_in.shape[2] - O_DT)))).astype(BF16)
    gmix = g_mix[0][None, :]
    ssd_params = (conv_w[0], conv_b[0][None, :], _pad_lanes(dt_bias[0]), _pad_lanes(a_log[0]),
                  jnp.repeat(d_skip[0], SSM_HEAD_DIM)[None, :], g_ssm[0][None, :])
    expand = (np.arange(2 * LANE)[:, None] % LANE == (np.arange(D_ATTN)[None, :] // HEAD_DIM)).astype(np.float32)
    ffn_weights = (jnp.asarray(expand, BF16), w_out[0].astype(BF16), g_ffn[0][None, :], w_up[0].astype(BF16),
                   ffn_conv_w[0], ffn_conv_b[0][None, :], w_down[0].astype(BF16), w_ple_proj[0].astype(BF16),
                   g_ple[0][None, :], w_ple_gate[0].astype(BF16), g_final[None, :])
    bias_p, b16, b4, b1, bn = _bias_tables(rel_bias, t_len)

    (q1, k1, v1, q4, k4, v4, q16, k16, v16, kt, vt, yssm_p, ssm_p, ctail_p) = _inproj_ssd(
        x_prompt, gmix, w_main, w_dt,
        jnp.zeros((bp, SSM_CONV - 1, CONV_DIM), F32),
        jnp.zeros((bp, N_SSM_HEADS, SSM_HEAD_DIM, D_STATE), F32),
        ssd_params, tm=512, n_keep=n_keep)
    parts = [_attn_prompt_branch(q1[:, None], k1[:, None], v1[:, None], bias_p[0], 1),
             _attn_prompt_branch(q4, k4, v4, bias_p[1], 4),
             _attn_prompt_branch(q16, k16, v16, bias_p[2], 16)]
    y_prompt, tail_p = _mix_ffn(x_prompt, parts, yssm_p, p_prompt[0],
                                jnp.zeros((bp, FFN_CONV - 1, 2 * D_FF), F32), ffn_weights, tm=512)
    k_prompt = jnp.transpose(kt, (0, 3, 1, 2))[None]
    v_prompt = jnp.transpose(vt, (0, 3, 1, 2))[None]
    conv_prompt = ctail_p[:, SUBLANE - (SSM_CONV - 1):][None]
    ffn_conv_prompt = tail_p[:, SUBLANE - (FFN_CONV - 1):][None]

    rows = nsamp * t_len
    qs, ks, vs, zs, xbcs, dts = _inproj(x_sample.reshape(rows, D_MODEL), gmix, w_main, w_dt)
    s3 = lambda a: a.reshape(nsamp, t_len, a.shape[-1])
    yssm_s, ssm_s = _ssd_short(s3(xbcs), s3(zs), s3(dts), state_conv[0], state_ssm[0], ssd_params)
    attn_s = _attn_sample(s3(qs), s3(ks), s3(vs),
                          jnp.transpose(cache_k[0], (0, 2, 3, 1)), jnp.transpose(cache_v[0], (0, 2, 3, 1)),
                          b16, b4, b1, bn)
    y_s, u_s = _mix_ffn_seg(x_sample.reshape(rows, D_MODEL), attn_s.reshape(rows, D_ATTN),
                            yssm_s.reshape(rows, D_SSM), p_sample[0].reshape(rows, D_PLE),
                            state_ffn_conv[0], ffn_weights)
    y_sample = y_s.reshape(nsamp, t_len, D_MODEL)
    k_sample = ks.reshape(1, nsamp, t_len, N_HEADS, HEAD_DIM)
    v_sample = vs.reshape(1, nsamp, t_len, N_HEADS, HEAD_DIM)
    conv_sample = s3(xbcs)[:, t_len - (SSM_CONV - 1):][None]
    ffn_conv_sample = u_s[None]

    return (y_prompt, y_sample, k_prompt, v_prompt, k_sample, v_sample,
            ssm_p[None], ssm_s[None], conv_prompt, conv_sample, ffn_conv_prompt, ffn_conv_sample)
```
